```python
import math
import jax
import jax.numpy as jnp
from jax import lax
import numpy as np

D_MODEL = 1024
BATCH = 8
SEQ = 2048
DEPTH = 2
DEC_BATCH = 32
DEC_SEQ = 1
PAST_LEN = 16384
PAGE_SIZE = 128

GROUP_WIDTH = 256
N_MIX_GROUPS = 4
MIX_WIDTH = GROUP_WIDTH * N_MIX_GROUPS
HEAD_DIM = 64

A_HEADS = GROUP_WIDTH // HEAD_DIM
MOBA_BLOCK = 256
MOBA_TOPK = 3
MOBA_QBLK = 64
ROPE_THETA = 10000.0

LRU_WIDTH = GROUP_WIDTH
LRU_BLOCKS = 4
LRU_BDIM = LRU_WIDTH // LRU_BLOCKS
LRU_C = 8.0
CONV_W = 4

SSD_HEADS = 4
SSD_HEADDIM = GROUP_WIDTH // SSD_HEADS
SSD_GROUPS = 2
SSD_STATE = 64
SSD_CHUNK = 128
SSD_CONV_CH = GROUP_WIDTH + 2 * SSD_GROUPS * SSD_STATE

RWKV_HEADS = 4
RWKV_HEADDIM = GROUP_WIDTH // RWKV_HEADS
RWKV_W_RANK = 32
RWKV_A_RANK = 32
RWKV_G_RANK = 64
RWKV_COLS = 3 * GROUP_WIDTH + RWKV_W_RANK + RWKV_A_RANK + RWKV_G_RANK
RWKV_LN_EPS = 64e-5

A_OFF = 0
A_COLS = 3 * GROUP_WIDTH
B_OFF = A_OFF + A_COLS
B_COLS = 2 * LRU_WIDTH
C_OFF = B_OFF + B_COLS
C_COLS = GROUP_WIDTH + SSD_CONV_CH + SSD_HEADS
D_OFF = C_OFF + C_COLS
IN_WIDTH = D_OFF + RWKV_COLS

MEM_LEN = 256
X_HEADS = 4
X_HEADDIM = D_MODEL // X_HEADS

PEER_HEADS = 8
PEER_NKEYS = 128
PEER_EXPERTS = PEER_NKEYS * PEER_NKEYS
PEER_QDIM = 256
PEER_HALF = PEER_QDIM // 2
PEER_TOPK = 16
PEER_TOKBLK = 128

NORM_EPS = 1e-6
NEG_INF = -1e30

kernel_name = 'hybrid_moba_lru_ssd_rwkv_peer_step'


def rmsnorm(x, g):
    xf = x.astype(jnp.float32)
    y = xf * lax.rsqrt(jnp.mean(xf * xf, axis=-1, keepdims=True) + NORM_EPS)
    return (y * g.astype(jnp.float32)).astype(x.dtype)


def rope(x, pos):
    half = x.shape[-1] // 2
    freq = 1.0 / (ROPE_THETA ** (jnp.arange(half, dtype=jnp.float32) / half))
    ang = pos.astype(jnp.float32)[:, None] * freq[None, :]
    cos = jnp.cos(ang)[None, :, None, :]
    sin = jnp.sin(ang)[None, :, None, :]
    xf = x.astype(jnp.float32)
    x1, x2 = xf[..., :half], xf[..., half:]
    return jnp.concatenate([x1 * cos - x2 * sin, x2 * cos + x1 * sin], axis=-1).astype(x.dtype)


def causal_conv(x, buf, w, b):
    T = x.shape[1]
    xp = jnp.concatenate([buf.astype(x.dtype), x], axis=1)
    y = b + xp[:, 0:T] * w[0]
    for j in range(1, CONV_W):
        y = y + xp[:, j:j + T] * w[j]
    return y, xp[:, -(CONV_W - 1):]


def linear_scan(a, b, h0):
    b = b.at[:, 0].add(a[:, 0] * h0)
    def comb(l, r):
        al, bl = l
        ar, br = r
        return al * ar, ar * bl + br
    _, h = lax.associative_scan(comb, (a, b), axis=1)
    return h


def moba_attend(q, k_all, v_all, q_pos0):
    bsz, T, H, hd = q.shape
    L = k_all.shape[1]
    nb = -(-L // MOBA_BLOCK)
    pad = nb * MOBA_BLOCK - L
    def to_blocks(t):
        t = jnp.pad(t, ((0, 0), (0, pad), (0, 0), (0, 0)))
        return t.reshape(bsz, nb, MOBA_BLOCK, H, hd).transpose(0, 3, 1, 2, 4)
    kb = to_blocks(k_all)
    vb = to_blocks(v_all)
    kmean = jnp.mean(kb.astype(jnp.float32), axis=3)
    n_sel = min(MOBA_TOPK, nb)
    qblk = math.gcd(T, MOBA_QBLK)
    nq = T // qblk
    qs = q.reshape(bsz, nq, qblk, H, hd).transpose(1, 0, 3, 2, 4)
    pos = (q_pos0 + jnp.arange(T, dtype=jnp.int32)).reshape(nq, qblk)
    bi = jnp.arange(bsz)[:, None, None, None]
    hi = jnp.arange(H)[None, :, None, None]
    offs = jnp.arange(MOBA_BLOCK, dtype=jnp.int32)
    scale = hd ** -0.5

    def one_block(args):
        qb, pb = args
        cur = pb // MOBA_BLOCK
        gate = jnp.einsum('bhqd,bhnd->bhqn', qb.astype(jnp.float32), kmean)
        past = jnp.arange(nb, dtype=jnp.int32)[None, :] < cur[:, None]
        gate = jnp.where(past, gate, NEG_INF)
        _, top = lax.top_k(gate, n_sel)
        own = jnp.broadcast_to(cur[:, None], (bsz, H, qblk, 1)).astype(top.dtype)
        idx = jnp.concatenate([top, own], axis=-1)
        ok = jnp.concatenate([top < cur[:, None], jnp.ones(own.shape, dtype=bool)], axis=-1)
        kg = kb[bi, hi, idx]
        vg = vb[bi, hi, idx]
        s = jnp.einsum('bhqd,bhqnkd->bhqnk', qb, kg).astype(jnp.float32) * scale
        kpos = idx[..., None] * MOBA_BLOCK + offs
        mask = ok[..., None] & (kpos <= pb[:, None, None])
        s = jnp.where(mask, s, NEG_INF)
        p = jax.nn.softmax(s.reshape(bsz, H, qblk, -1), axis=-1).reshape(s.shape)
        return jnp.einsum('bhqnk,bhqnkd->bhqd', p.astype(vg.dtype), vg)

    out = lax.map(one_block, (qs, pos))
    return out.transpose(1, 0, 3, 2, 4).reshape(bsz, T, H * hd)


def rglru_mixer(u, gate, conv_buf, h0, conv_w, conv_b, wa, ba, wx, bx, lam):
    bsz, T, W = u.shape
    xc, new_buf = causal_conv(u, conv_buf, conv_w, conv_b)
    xh = xc.reshape(bsz, T, LRU_BLOCKS, LRU_BDIM)
    r = jax.nn.sigmoid(jnp.einsum('btnd,nde->btne', xh, wa).reshape(bsz, T, W) + ba).astype(jnp.float32)
    i = jax.nn.sigmoid(jnp.einsum('btnd,nde->btne', xh, wx).reshape(bsz, T, W) + bx).astype(jnp.float32)
    log_a = -LRU_C * r * jax.nn.softplus(-lam.astype(jnp.float32))
    a = jnp.exp(log_a)
    b = jnp.sqrt(-jnp.expm1(2.0 * log_a)) * (i * xc.astype(jnp.float32))
    h = linear_scan(a, b, h0.astype(jnp.float32))
    y = h * jax.nn.gelu(gate.astype(jnp.float32))
    return y.astype(u.dtype), new_buf, h[:, -1]


def ssd_chunked(xdt, dA, Bm, Cm, s0):
    bsz, T, H, P = xdt.shape
    N = Bm.shape[-1]
    Q = SSD_CHUNK
    nc = T // Q
    xc = xdt.reshape(bsz, nc, Q, H, P)
    Bc = Bm.reshape(bsz, nc, Q, H, N)
    Cc = Cm.reshape(bsz, nc, Q, H, N)
    cs = jnp.cumsum(dA.reshape(bsz, nc, Q, H), axis=2)
    seg = cs[:, :, :, None, :] - cs[:, :, None, :, :]
    causal = jnp.tril(jnp.ones((Q, Q), dtype=bool))[None, None, :, :, None]
    Lm = jnp.exp(jnp.where(causal, seg, -jnp.inf))
    y_diag = jnp.einsum('bcihn,bcjhn,bcijh,bcjhp->bcihp', Cc, Bc, Lm, xc)
    decay_to_end = jnp.exp(cs[:, :, -1:, :] - cs)
    chunk_state = jnp.einsum('bcjhn,bcjh,bcjhp->bchpn', Bc, decay_to_end, xc)
    chunk_decay = jnp.exp(cs[:, :, -1, :])
    def step(s, inp):
        cst, cd = inp
        return s * cd[:, :, None, None] + cst, s
    s_final, s_starts = lax.scan(step, s0, (jnp.moveaxis(chunk_state, 1, 0), jnp.moveaxis(chunk_decay, 1, 0)))
    s_starts = jnp.moveaxis(s_starts, 0, 1)
    y_off = jnp.einsum('bcihn,bchpn,bcih->bcihp', Cc, s_starts, jnp.exp(cs))
    return (y_diag + y_off).reshape(bsz, T, H, P), s_final


def ssd_recurrent(xdt, dA, Bm, Cm, s0):
    def step(s, inp):
        xt, dat, bt, ct = inp
        s = s * jnp.exp(dat)[:, :, None, None] + jnp.einsum('bhp,bhn->bhpn', xt, bt)
        return s, jnp.einsum('bhpn,bhn->bhp', s, ct)
    tm = lambda t: jnp.moveaxis(t, 1, 0)
    s, ys = lax.scan(step, s0, (tm(xdt), tm(dA), tm(Bm), tm(Cm)))
    return jnp.moveaxis(ys, 0, 1), s


def ssd_mixer(z, xbc, dt_raw, conv_buf, s0, conv_w, conv_b, dt_bias, a_log, d_skip, norm_w, prompt):
    bsz, T, _ = z.shape
    G = GROUP_WIDTH
    GN = SSD_GROUPS * SSD_STATE
    xbc, new_buf = causal_conv(xbc, conv_buf, conv_w, conv_b)
    xbc_f = jax.nn.silu(xbc).astype(jnp.float32)
    x = xbc_f[..., :G].reshape(bsz, T, SSD_HEADS, SSD_HEADDIM)
    rep = SSD_HEADS // SSD_GROUPS
    Bm = jnp.repeat(xbc_f[..., G:G + GN].reshape(bsz, T, SSD_GROUPS, SSD_STATE), rep, axis=2)
    Cm = jnp.repeat(xbc_f[..., G + GN:].reshape(bsz, T, SSD_GROUPS, SSD_STATE), rep, axis=2)
    dt = jax.nn.softplus(dt_raw.astype(jnp.float32) + dt_bias.astype(jnp.float32))
    dA = dt * (-jnp.exp(a_log.astype(jnp.float32)))
    xdt = x * dt[..., None]
    s0 = s0.astype(jnp.float32)
    if prompt:
        y, s = ssd_chunked(xdt, dA, Bm, Cm, s0)
    else:
        y, s = ssd_recurrent(xdt, dA, Bm, Cm, s0)
    y = y + d_skip.astype(jnp.float32)[:, None] * x
    y = y.reshape(bsz, T, G) * jax.nn.silu(z.astype(jnp.float32))
    return rmsnorm(y, norm_w).astype(z.dtype), new_buf, s


def rwkv_mixer(cur, shift_buf, s0, mu, w0, w_up, a0, a_up, g_up, k_k, k_a, r_k, ln_w, ln_b):
    bsz, T, _ = cur.shape
    G = GROUP_WIDTH
    prev = jnp.concatenate([shift_buf[:, None].astype(cur.dtype), cur[:, :-1]], axis=1)
    m = (cur + (prev - cur) * mu).astype(jnp.float32)
    r, k, v = m[..., :G], m[..., G:2 * G], m[..., 2 * G:3 * G]
    o = 3 * G
    wd = m[..., o:o + RWKV_W_RANK]
    o += RWKV_W_RANK
    ad = m[..., o:o + RWKV_A_RANK]
    o += RWKV_A_RANK
    gd = m[..., o:o + RWKV_G_RANK]
    w = -jax.nn.softplus(-(w0 + jnp.tanh(wd) @ w_up)) - 0.5
    decay = jnp.exp(-jnp.exp(w))
    a = jax.nn.sigmoid(a0 + ad @ a_up)
    g = jax.nn.sigmoid(gd) @ g_up
    hs = lambda t: t.reshape(bsz, T, RWKV_HEADS, RWKV_HEADDIM)
    kk = hs(k * k_k)
    kk = kk / jnp.maximum(jnp.sqrt(jnp.sum(kk * kk, axis=-1, keepdims=True)), 1e-12)
    k = hs(k * (1.0 + (a - 1.0) * k_a))
    r, v, decay, a = hs(r), hs(v), hs(decay), hs(a)
    def step(S, inp):
        rt, kt, vt, dt, kkt, at = inp
        sa = jnp.einsum('bhvk,bhk->bhv', S, -kkt)
        S = S * dt[:, :, None, :] + sa[..., None] * (kkt * at)[:, :, None, :] + vt[..., None] * kt[:, :, None, :]
        return S, jnp.einsum('bhvk,bhk->bhv', S, rt)
    tm = lambda t: jnp.moveaxis(t, 1, 0)
    S, ys = lax.scan(step, s0.astype(jnp.float32), (tm(r), tm(k), tm(v), tm(decay), tm(kk), tm(a)))
    y = jnp.moveaxis(ys, 0, 1)
    mean = jnp.mean(y, axis=-1, keepdims=True)
    var = jnp.mean((y - mean) ** 2, axis=-1, keepdims=True)
    y = ((y - mean) * lax.rsqrt(var + RWKV_LN_EPS)).reshape(bsz, T, G) * ln_w + ln_b
    bonus = jnp.sum(r * k * r_k, axis=-1, keepdims=True) * v
    y = (y + bonus.reshape(bsz, T, G)) * g
    return y.astype(cur.dtype), cur[:, -1], S


def memory_kv(mem, wk, wv):
    bsz, M, _ = mem.shape
    k = (mem @ wk).reshape(bsz, M, X_HEADS, X_HEADDIM)
    v = (mem @ wv).reshape(bsz, M, X_HEADS, X_HEADDIM)
    return k, v


def cross_attn(h, mk, mv, wq, wo):
    bsz, T, _ = h.shape
    q = (h @ wq).reshape(bsz, T, X_HEADS, X_HEADDIM)
    s = jnp.einsum('bthd,bmhd->bhtm', q, mk.astype(q.dtype)).astype(jnp.float32) * (X_HEADDIM ** -0.5)
    p = jax.nn.softmax(s, axis=-1)
    o = jnp.einsum('bhtm,bmhd->bthd', p.astype(q.dtype), mv.astype(q.dtype)).reshape(bsz, T, D_MODEL)
    return o @ wo


def peer_ffn(h, wq, subkeys, u_tab, v_tab):
    bsz, T, D = h.shape
    n = bsz * T
    nblk = -(-n // PEER_TOKBLK)
    hp = jnp.pad(h.reshape(n, D), ((0, nblk * PEER_TOKBLK - n), (0, 0))).reshape(nblk, PEER_TOKBLK, D)
    sk = subkeys.astype(jnp.float32)
    def one(hb):
        q = (hb @ wq).astype(jnp.float32).reshape(PEER_TOKBLK, PEER_HEADS, 2, PEER_HALF)
        s = jnp.einsum('thcd,hckd->thck', q, sk)
        s1, i1 = lax.top_k(s[:, :, 0], PEER_TOPK)
        s2, i2 = lax.top_k(s[:, :, 1], PEER_TOPK)
        cand = (s1[..., :, None] + s2[..., None, :]).reshape(PEER_TOKBLK, PEER_HEADS, PEER_TOPK * PEER_TOPK)
        cidx = (i1[..., :, None] * PEER_NKEYS + i2[..., None, :]).reshape(PEER_TOKBLK, PEER_HEADS, PEER_TOPK * PEER_TOPK)
        top_s, pick = lax.top_k(cand, PEER_TOPK)
        e = jnp.take_along_axis(cidx, pick, axis=-1)
        g = jax.nn.softmax(top_s, axis=-1)
        act = jax.nn.gelu(jnp.einsum('td,thkd->thk', hb, u_tab[e]).astype(jnp.float32))
        return jnp.einsum('thk,thkd->td', (g * act).astype(hb.dtype), v_tab[e])
    out = lax.map(one, hp)
    return out.reshape(-1, D)[:n].reshape(bsz, T, D)


def trunk_layer(x, lp, st, pos0, prompt):
    bsz, T, _ = x.shape
    G = GROUP_WIDTH
    h = rmsnorm(x, lp['norm_mix'])
    proj = h @ lp['w_in']
    pos = pos0 + jnp.arange(T, dtype=jnp.int32)
    pa = proj[..., A_OFF:A_OFF + A_COLS]
    heads = lambda t: t.reshape(bsz, T, A_HEADS, HEAD_DIM)
    q = rope(heads(pa[..., :G]), pos)
    k = rope(heads(pa[..., G:2 * G]), pos)
    v = heads(pa[..., 2 * G:])
    if prompt:
        k_all, v_all = k, v
    else:
        k_all = jnp.concatenate([st['k_past'].astype(k.dtype), k], axis=1)
        v_all = jnp.concatenate([st['v_past'].astype(v.dtype), v], axis=1)
    ya = moba_attend(q, k_all, v_all, pos0)
    pb = proj[..., B_OFF:B_OFF + B_COLS]
    yb, lru_conv, lru_h = rglru_mixer(pb[..., :LRU_WIDTH], pb[..., LRU_WIDTH:], st['lru_conv'], st['lru_h'],
                                      lp['lru_conv_w'], lp['lru_conv_b'], lp['lru_wa'], lp['lru_ba'],
                                      lp['lru_wx'], lp['lru_bx'], lp['lru_lambda'])
    pc = proj[..., C_OFF:C_OFF + C_COLS]
    yc, ssd_conv, ssd_s = ssd_mixer(pc[..., :G], pc[..., G:G + SSD_CONV_CH], pc[..., G + SSD_CONV_CH:],
                                    st['ssd_conv'], st['ssd'], lp['ssd_conv_w'], lp['ssd_conv_b'],
                                    lp['ssd_dt_bias'], lp['ssd_a_log'], lp['ssd_d'], lp['ssd_norm'], prompt)
    yd, rwkv_shift, rwkv_s = rwkv_mixer(proj[..., D_OFF:D_OFF + RWKV_COLS], st['rwkv_shift'], st['rwkv'],
                                        lp['rwkv_mu'], lp['rwkv_w0'], lp['rwkv_w_up'], lp['rwkv_a0'],
                                        lp['rwkv_a_up'], lp['rwkv_g_up'], lp['rwkv_k_k'], lp['rwkv_k_a'],
                                        lp['rwkv_r_k'], lp['rwkv_ln_w'], lp['rwkv_ln_b'])
    y_mix = jnp.concatenate([ya.astype(x.dtype), yb.astype(x.dtype), yc.astype(x.dtype), yd.astype(x.dtype)], axis=-1)
    x = x + (y_mix @ lp['w_out']).astype(x.dtype)
    x = x + cross_attn(rmsnorm(x, lp['norm_x']), st['mem_k'], st['mem_v'], lp['x_wq'], lp['x_wo']).astype(x.dtype)
    x = x + peer_ffn(rmsnorm(x, lp['norm_ffn']), lp['peer_wq'], lp['peer_subkeys'], lp['peer_u'], lp['peer_v']).astype(x.dtype)
    new = {'k': k, 'v': v, 'lru_h': lru_h, 'lru_conv': lru_conv, 'ssd': ssd_s, 'ssd_conv': ssd_conv,
           'rwkv': rwkv_s, 'rwkv_shift': rwkv_shift}
    return x, new


def setup_inputs(seed: int = 0) -> dict:
    key = jax.random.key(seed)
    ks = jax.random.split(key, 96)
    kit = iter([ks[i] for i in range(96)])
    f32 = jnp.float32
    def nrm(shape, scale):
        return jax.random.normal(next(kit), shape, f32) * scale
    def gain(shape):
        return 1.0 + 0.02 * jax.random.normal(next(kit), shape, f32)
    def unif(shape, lo, hi):
        return jax.random.uniform(next(kit), shape, f32, lo, hi)
    n_pages = PAST_LEN // PAGE_SIZE
    n_used = DEC_BATCH * n_pages
    n_phys = n_used + max(1, n_used // 4)
    page_table = jax.random.permutation(next(kit), n_phys)[:n_used].reshape(DEC_BATCH, n_pages).astype(jnp.int32)
    sig = unif((DEPTH, LRU_WIDTH), 0.9, 0.999)
    dt0 = jnp.exp(unif((DEPTH, SSD_HEADS), math.log(1e-3), math.log(1e-1)))
    return {
        'x_prompt': nrm((BATCH, SEQ, D_MODEL), 1.0),
        'x_sample': nrm((DEC_BATCH, DEC_SEQ, D_MODEL), 1.0),
        'mem_prompt': nrm((BATCH, MEM_LEN, D_MODEL), 1.0),
        'cache_moba_k': nrm((DEPTH, n_phys, PAGE_SIZE, A_HEADS, HEAD_DIM), 1.0),
        'cache_moba_v': nrm((DEPTH, n_phys, PAGE_SIZE, A_HEADS, HEAD_DIM), 1.0),
        'page_table': page_table,
        'state_lru_h': nrm((DEPTH, DEC_BATCH, LRU_WIDTH), 0.5),
        'state_lru_conv': nrm((DEPTH, DEC_BATCH, CONV_W - 1, LRU_WIDTH), 1.0),
        'state_ssd': nrm((DEPTH, DEC_BATCH, SSD_HEADS, SSD_HEADDIM, SSD_STATE), 0.1),
        'state_ssd_conv': nrm((DEPTH, DEC_BATCH, CONV_W - 1, SSD_CONV_CH), 1.0),
        'state_rwkv': nrm((DEPTH, DEC_BATCH, RWKV_HEADS, RWKV_HEADDIM, RWKV_HEADDIM), 0.1),
        'state_rwkv_shift': nrm((DEPTH, DEC_BATCH, RWKV_COLS), 1.0),
        'cache_mem_k': nrm((DEPTH, DEC_BATCH, MEM_LEN, X_HEADS, X_HEADDIM), 1.0),
        'cache_mem_v': nrm((DEPTH, DEC_BATCH, MEM_LEN, X_HEADS, X_HEADDIM), 1.0),
        'norm_mix': gain((DEPTH, D_MODEL)),
        'w_in': nrm((DEPTH, D_MODEL, IN_WIDTH), D_MODEL ** -0.5),
        'w_out': nrm((DEPTH, MIX_WIDTH, D_MODEL), MIX_WIDTH ** -0.5),
        'lru_conv_w': nrm((DEPTH, CONV_W, LRU_WIDTH), 0.5),
        'lru_conv_b': nrm((DEPTH, LRU_WIDTH), 0.02),
        'lru_wa': nrm((DEPTH, LRU_BLOCKS, LRU_BDIM, LRU_BDIM), LRU_BDIM ** -0.5),
        'lru_ba': nrm((DEPTH, LRU_WIDTH), 0.1),
        'lru_wx': nrm((DEPTH, LRU_BLOCKS, LRU_BDIM, LRU_BDIM), LRU_BDIM ** -0.5),
        'lru_bx': nrm((DEPTH, LRU_WIDTH), 0.1),
        'lru_lambda': jnp.log(sig) - jnp.log1p(-sig),
        'ssd_conv_w': nrm((DEPTH, CONV_W, SSD_CONV_CH), 0.5),
        'ssd_conv_b': nrm((DEPTH, SSD_CONV_CH), 0.02),
        'ssd_dt_bias': dt0 + jnp.log(-jnp.expm1(-dt0)),
        'ssd_a_log': jnp.log(unif((DEPTH, SSD_HEADS), 1.0, 16.0)),
        'ssd_d': gain((DEPTH, SSD_HEADS)),
        'ssd_norm': gain((DEPTH, GROUP_WIDTH)),
        'rwkv_mu': unif((DEPTH, RWKV_COLS), 0.0, 1.0),
        'rwkv_w0': unif((DEPTH, GROUP_WIDTH), -4.0, -0.5),
        'rwkv_w_up': nrm((DEPTH, RWKV_W_RANK, GROUP_WIDTH), 0.1),
        'rwkv_a0': nrm((DEPTH, GROUP_WIDTH), 0.1),
        'rwkv_a_up': nrm((DEPTH, RWKV_A_RANK, GROUP_WIDTH), 0.1),
        'rwkv_g_up': nrm((DEPTH, RWKV_G_RANK, GROUP_WIDTH), RWKV_G_RANK ** -0.5),
        'rwkv_k_k': 0.85 + nrm((DEPTH, GROUP_WIDTH), 0.05),
        'rwkv_k_a': 1.0 + nrm((DEPTH, GROUP_WIDTH), 0.05),
        'rwkv_r_k': nrm((DEPTH, RWKV_HEADS, RWKV_HEADDIM), 0.1),
        'rwkv_ln_w': gain((DEPTH, GROUP_WIDTH)),
        'rwkv_ln_b': nrm((DEPTH, GROUP_WIDTH), 0.02),
        'norm_x': gain((DEPTH, D_MODEL)),
        'x_wq': nrm((DEPTH, D_MODEL, D_MODEL), D_MODEL ** -0.5),
        'x_wk': nrm((DEPTH, D_MODEL, D_MODEL), D_MODEL ** -0.5),
        'x_wv': nrm((DEPTH, D_MODEL, D_MODEL), D_MODEL ** -0.5),
        'x_wo': nrm((DEPTH, D_MODEL, D_MODEL), D_MODEL ** -0.5),
        'norm_ffn': gain((DEPTH, D_MODEL)),
        'peer_wq': nrm((DEPTH, D_MODEL, PEER_HEADS * PEER_QDIM), D_MODEL ** -0.5),
        'peer_subkeys': nrm((DEPTH, PEER_HEADS, 2, PEER_NKEYS, PEER_HALF), PEER_HALF ** -0.5),
        'peer_u': nrm((DEPTH, PEER_EXPERTS, D_MODEL), D_MODEL ** -0.5),
        'peer_v': nrm((DEPTH, PEER_EXPERTS, D_MODEL), 0.1),
        'final_norm': gain((D_MODEL,)),
    }


def reference(x_prompt, x_sample, mem_prompt, cache_moba_k, cache_moba_v, page_table,
              state_lru_h, state_lru_conv, state_ssd, state_ssd_conv, state_rwkv, state_rwkv_shift,
              cache_mem_k, cache_mem_v,
              norm_mix, w_in, w_out,
              lru_conv_w, lru_conv_b, lru_wa, lru_ba, lru_wx, lru_bx, lru_lambda,
              ssd_conv_w, ssd_conv_b, ssd_dt_bias, ssd_a_log, ssd_d, ssd_norm,
              rwkv_mu, rwkv_w0, rwkv_w_up, rwkv_a0, rwkv_a_up, rwkv_g_up, rwkv_k_k, rwkv_k_a,
              rwkv_r_k, rwkv_ln_w, rwkv_ln_b,
              norm_x, x_wq, x_wk, x_wv, x_wo,
              norm_ffn, peer_wq, peer_subkeys, peer_u, peer_v,
              final_norm):
    n_pages = PAST_LEN // PAGE_SIZE
    bp = x_prompt.shape[0]
    bd = x_sample.shape[0]
    dt = x_prompt.dtype
    f32 = jnp.float32
    yp, ys = x_prompt, x_sample
    p_new = {n: [] for n in ('k', 'v', 'lru_h', 'lru_conv', 'ssd', 'ssd_conv', 'rwkv', 'rwkv_shift', 'mem_k', 'mem_v')}
    s_new = {n: [] for n in ('k', 'v', 'lru_h', 'lru_conv', 'ssd', 'ssd_conv', 'rwkv', 'rwkv_shift')}
    for l in range(DEPTH):
        lp = {
            'norm_mix': norm_mix[l], 'w_in': w_in[l], 'w_out': w_out[l],
            'lru_conv_w': lru_conv_w[l], 'lru_conv_b': lru_conv_b[l], 'lru_wa': lru_wa[l], 'lru_ba': lru_ba[l],
            'lru_wx': lru_wx[l], 'lru_bx': lru_bx[l], 'lru_lambda': lru_lambda[l],
            'ssd_conv_w': ssd_conv_w[l], 'ssd_conv_b': ssd_conv_b[l], 'ssd_dt_bias': ssd_dt_bias[l],
            'ssd_a_log': ssd_a_log[l], 'ssd_d': ssd_d[l], 'ssd_norm': ssd_norm[l],
            'rwkv_mu': rwkv_mu[l], 'rwkv_w0': rwkv_w0[l], 'rwkv_w_up': rwkv_w_up[l], 'rwkv_a0': rwkv_a0[l],
            'rwkv_a_up': rwkv_a_up[l], 'rwkv_g_up': rwkv_g_up[l], 'rwkv_k_k': rwkv_k_k[l], 'rwkv_k_a': rwkv_k_a[l],
            'rwkv_r_k': rwkv_r_k[l], 'rwkv_ln_w': rwkv_ln_w[l], 'rwkv_ln_b': rwkv_ln_b[l],
            'norm_x': norm_x[l], 'x_wq': x_wq[l], 'x_wo': x_wo[l],
            'norm_ffn': norm_ffn[l], 'peer_wq': peer_wq[l], 'peer_subkeys': peer_subkeys[l],
            'peer_u': peer_u[l], 'peer_v': peer_v[l],
        }
        mk, mv = memory_kv(mem_prompt, x_wk[l], x_wv[l])
        st_p = {
            'k_past': None, 'v_past': None,
            'lru_conv': jnp.zeros((bp, CONV_W - 1, LRU_WIDTH), dt),
            'lru_h': jnp.zeros((bp, LRU_WIDTH), f32),
            'ssd_conv': jnp.zeros((bp, CONV_W - 1, SSD_CONV_CH), dt),
            'ssd': jnp.zeros((bp, SSD_HEADS, SSD_HEADDIM, SSD_STATE), f32),
            'rwkv_shift': jnp.zeros((bp, RWKV_COLS), dt),
            'rwkv': jnp.zeros((bp, RWKV_HEADS, RWKV_HEADDIM, RWKV_HEADDIM), f32),
            'mem_k': mk, 'mem_v': mv,
        }
        yp, npl = trunk_layer(yp, lp, st_p, 0, True)
        for n in s_new:
            p_new[n].append(npl[n])
        p_new['mem_k'].append(mk)
        p_new['mem_v'].append(mv)
        k_past = cache_moba_k[l][page_table].reshape(bd, n_pages * PAGE_SIZE, A_HEADS, HEAD_DIM)
        v_past = cache_moba_v[l][page_table].reshape(bd, n_pages * PAGE_SIZE, A_HEADS, HEAD_DIM)
        st_s = {
            'k_past': k_past, 'v_past': v_past,
            'lru_conv': state_lru_conv[l], 'lru_h': state_lru_h[l],
            'ssd_conv': state_ssd_conv[l], 'ssd': state_ssd[l],
            'rwkv_shift': state_rwkv_shift[l], 'rwkv': state_rwkv[l],
            'mem_k': cache_mem_k[l], 'mem_v': cache_mem_v[l],
        }
        ys, nsl = trunk_layer(ys, lp, st_s, PAST_LEN, False)
        for n in s_new:
            s_new[n].append(nsl[n])
    y_prompt = rmsnorm(yp, final_norm)
    y_sample = rmsnorm(ys, final_norm)
    return (y_prompt, y_sample,
            jnp.stack(p_new['k']), jnp.stack(p_new['v']), jnp.stack(p_new['lru_h']), jnp.stack(p_new['lru_conv']),
            jnp.stack(p_new['ssd']), jnp.stack(p_new['ssd_conv']), jnp.stack(p_new['rwkv']), jnp.stack(p_new['rwkv_shift']),
            jnp.stack(p_new['mem_k']), jnp.stack(p_new['mem_v']),
            jnp.stack(s_new['k']), jnp.stack(s_new['v']), jnp.stack(s_new['lru_h']), jnp.stack(s_new['lru_conv']),
            jnp.stack(s_new['ssd']), jnp.stack(s_new['ssd_conv']), jnp.stack(s_new['rwkv']), jnp.stack(s_new['rwkv_shift']))
```

```python
import functools
import math

import jax
import jax.numpy as jnp
from jax import lax
from jax.experimental import pallas as pl
from jax.experimental.pallas import tpu as pltpu

D_MODEL = 1024
DEPTH = 2
PAST_LEN = 16384
PAGE_SIZE = 128
GROUP_WIDTH = 256
HEAD_DIM = 64
A_HEADS = GROUP_WIDTH // HEAD_DIM
MOBA_BLOCK = 256
MOBA_TOPK = 3
ROPE_THETA = 10000.0
LRU_WIDTH = GROUP_WIDTH
LRU_BLOCKS = 4
LRU_BDIM = LRU_WIDTH // LRU_BLOCKS
LRU_C = 8.0
CONV_W = 4
SSD_HEADS = 4
SSD_HEADDIM = GROUP_WIDTH // SSD_HEADS
SSD_GROUPS = 2
SSD_STATE = 64
SSD_CHUNK = 128
SSD_CONV_CH = GROUP_WIDTH + 2 * SSD_GROUPS * SSD_STATE
RWKV_HEADS = 4
RWKV_HEADDIM = GROUP_WIDTH // RWKV_HEADS
RWKV_W_RANK = 32
RWKV_A_RANK = 32
RWKV_G_RANK = 64
RWKV_COLS = 3 * GROUP_WIDTH + RWKV_W_RANK + RWKV_A_RANK + RWKV_G_RANK
RWKV_LN_EPS = 64e-5
A_OFF = 0
A_COLS = 3 * GROUP_WIDTH
B_OFF = A_OFF + A_COLS
B_COLS = 2 * LRU_WIDTH
C_OFF = B_OFF + B_COLS
C_COLS = GROUP_WIDTH + SSD_CONV_CH + SSD_HEADS
D_OFF = C_OFF + C_COLS
IN_WIDTH = D_OFF + RWKV_COLS
MEM_LEN = 256
X_HEADS = 4
X_HEADDIM = D_MODEL // X_HEADS
PEER_HEADS = 8
PEER_NKEYS = 128
PEER_QDIM = 256
PEER_HALF = PEER_QDIM // 2
PEER_TOPK = 16
NORM_EPS = 1e-6
NEG_INF = -1e30


def _mm_kernel(*refs, has_norm, has_res):
    x_ref, w_ref = refs[0], refs[1]
    pos = 2
    g_ref = r_ref = None
    if has_norm:
        g_ref = refs[pos]
        pos += 1
    if has_res:
        r_ref = refs[pos]
        pos += 1
    o_ref = refs[pos]
    x = x_ref[...]
    if has_norm:
        x = x * lax.rsqrt(jnp.mean(x * x, axis=-1, keepdims=True) + NORM_EPS) * g_ref[...]
    y = jnp.dot(x.astype(jnp.bfloat16), w_ref[...], preferred_element_type=jnp.float32)
    if has_res:
        y = y + r_ref[...]
    o_ref[...] = y


def _mm(x, w, gain=None, residual=None, tm=512, tn=512):
    M, K = x.shape
    N = w.shape[1]
    tm = min(tm, M)
    assert M % tm == 0
    n_pad = -(-N // tn) * tn
    wb = w.astype(jnp.bfloat16)
    if n_pad != N:
        wb = jnp.pad(wb, ((0, 0), (0, n_pad - N)))
    args = [x, wb]
    in_specs = [pl.BlockSpec((tm, K), lambda i, j: (i, 0)),
                pl.BlockSpec((K, tn), lambda i, j: (0, j))]
    if gain is not None:
        args.append(gain.reshape(1, K).astype(jnp.float32))
        in_specs.append(pl.BlockSpec((1, K), lambda i, j: (0, 0)))
    if residual is not None:
        assert n_pad == N
        args.append(residual)
        in_specs.append(pl.BlockSpec((tm, tn), lambda i, j: (i, j)))
    out = pl.pallas_call(
        functools.partial(_mm_kernel, has_norm=gain is not None, has_res=residual is not None),
        grid=(M // tm, n_pad // tn),
        in_specs=in_specs,
        out_specs=pl.BlockSpec((tm, tn), lambda i, j: (i, j)),
        out_shape=jax.ShapeDtypeStruct((M, n_pad), jnp.float32),
        compiler_params=pltpu.CompilerParams(dimension_semantics=("parallel", "parallel")),
        name="mm",
    )(*args)
    return out[:, :N] if n_pad != N else out


def _rmsnorm(x, g):
    return x * lax.rsqrt(jnp.mean(x * x, axis=-1, keepdims=True) + NORM_EPS) * g


def _rope(x, pos):
    half = x.shape[-1] // 2
    freq = 1.0 / (ROPE_THETA ** (jnp.arange(half, dtype=jnp.float32) / half))
    ang = pos.astype(jnp.float32)[:, None] * freq[None, :]
    cos = jnp.cos(ang)[None, :, None, :]
    sin = jnp.sin(ang)[None, :, None, :]
    x1, x2 = x[..., :half], x[..., half:]
    return jnp.concatenate([x1 * cos - x2 * sin, x2 * cos + x1 * sin], axis=-1)


def _topk_mask(gate, k):
    n = gate.shape[-1]
    a = gate[..., :, None]
    b = gate[..., None, :]
    idx = jnp.arange(n)
    beats = (b > a) | ((b == a) & (idx[None, :] < idx[:, None]))
    rank = jnp.sum(beats, axis=-1)
    return rank < k


def _moba_prompt(q, k, v):
    bsz, T, H, hd = q.shape
    nb = T // MOBA_BLOCK
    kmean = jnp.mean(k.reshape(bsz, nb, MOBA_BLOCK, H, hd), axis=2)
    gate = jnp.einsum('bthd,bnhd->bhtn', q, kmean)
    t = jnp.arange(T, dtype=jnp.int32)
    cur = t // MOBA_BLOCK
    past = jnp.arange(nb, dtype=jnp.int32)[None, :] < cur[:, None]
    gate = jnp.where(past, gate, NEG_INF)
    sel = _topk_mask(gate, MOBA_TOPK) & past
    sel = sel | (jnp.arange(nb, dtype=jnp.int32)[None, :] == cur[:, None])
    s = jnp.einsum('bthd,bjhd->bhtj', q, k) * (hd ** -0.5)
    allow = jnp.repeat(sel, MOBA_BLOCK, axis=-1) & (t[None, :] <= t[:, None])
    s = jnp.where(allow, s, NEG_INF)
    p = jax.nn.softmax(s, axis=-1)
    return jnp.einsum('bhtj,bjhd->bthd', p, v).reshape(bsz, T, H * hd)


def _moba_decode(q, k_new, v_new, k_cache, v_cache, page_table):
    bsz = q.shape[0]
    H, hd = A_HEADS, HEAD_DIM
    nb = PAST_LEN // MOBA_BLOCK
    k_past = k_cache[page_table].reshape(bsz, nb, MOBA_BLOCK, H, hd)
    v_past = v_cache[page_table].reshape(bsz, nb, MOBA_BLOCK, H, hd)
    kmean = jnp.mean(k_past, axis=2)
    gate = jnp.einsum('bhd,bnhd->bhn', q[:, 0], kmean)
    _, top = lax.top_k(gate, MOBA_TOPK)
    bi = jnp.arange(bsz)[:, None, None]
    hi = jnp.arange(H)[None, :, None]
    kg = k_past.transpose(0, 3, 1, 2, 4)[bi, hi, top].reshape(bsz, H, MOBA_TOPK * MOBA_BLOCK, hd)
    vg = v_past.transpose(0, 3, 1, 2, 4)[bi, hi, top].reshape(bsz, H, MOBA_TOPK * MOBA_BLOCK, hd)
    kg = jnp.concatenate([kg, k_new.transpose(0, 2, 1, 3)], axis=2)
    vg = jnp.concatenate([vg, v_new.transpose(0, 2, 1, 3)], axis=2)
    s = jnp.einsum('bhd,bhkd->bhk', q[:, 0], kg) * (hd ** -0.5)
    p = jax.nn.softmax(s, axis=-1)
    return jnp.einsum('bhk,bhkd->bhd', p, vg).reshape(bsz, 1, H * hd)


def _causal_conv(x, buf, w, b):
    T = x.shape[1]
    xp = jnp.concatenate([buf.astype(x.dtype), x], axis=1)
    y = b + xp[:, 0:T] * w[0]
    for j in range(1, CONV_W):
        y = y + xp[:, j:j + T] * w[j]
    return y, xp[:, -(CONV_W - 1):]


def _linear_scan(a, b, h0):
    b = b.at[:, 0].add(a[:, 0] * h0)
    def comb(l, r):
        al, bl = l
        ar, br = r
        return al * ar, ar * bl + br
    _, h = lax.associative_scan(comb, (a, b), axis=1)
    return h


def _rglru_mixer(u, gate, conv_buf, h0, conv_w, conv_b, wa, ba, wx, bx, lam):
    bsz, T, W = u.shape
    xc, new_buf = _causal_conv(u, conv_buf, conv_w, conv_b)
    xh = xc.reshape(bsz, T, LRU_BLOCKS, LRU_BDIM)
    r = jax.nn.sigmoid(jnp.einsum('btnd,nde->btne', xh, wa).reshape(bsz, T, W) + ba)
    i = jax.nn.sigmoid(jnp.einsum('btnd,nde->btne', xh, wx).reshape(bsz, T, W) + bx)
    log_a = -LRU_C * r * jax.nn.softplus(-lam)
    a = jnp.exp(log_a)
    b = jnp.sqrt(-jnp.expm1(2.0 * log_a)) * (i * xc)
    h = _linear_scan(a, b, h0)
    y = h * jax.nn.gelu(gate)
    return y, new_buf, h[:, -1]


def _ssd_chunked(xdt, dA, Bm, Cm, s0):
    bsz, T, H, P = xdt.shape
    N = Bm.shape[-1]
    Q = SSD_CHUNK
    nc = T // Q
    xc = xdt.reshape(bsz, nc, Q, H, P)
    Bc = Bm.reshape(bsz, nc, Q, H, N)
    Cc = Cm.reshape(bsz, nc, Q, H, N)
    cs = jnp.cumsum(dA.reshape(bsz, nc, Q, H), axis=2)
    seg = cs[:, :, :, None, :] - cs[:, :, None, :, :]
    causal = jnp.tril(jnp.ones((Q, Q), dtype=bool))[None, None, :, :, None]
    Lm = jnp.exp(jnp.where(causal, seg, -jnp.inf))
    y_diag = jnp.einsum('bcihn,bcjhn,bcijh,bcjhp->bcihp', Cc, Bc, Lm, xc)
    decay_to_end = jnp.exp(cs[:, :, -1:, :] - cs)
    chunk_state = jnp.einsum('bcjhn,bcjh,bcjhp->bchpn', Bc, decay_to_end, xc)
    chunk_decay = jnp.exp(cs[:, :, -1, :])
    def step(s, inp):
        cst, cd = inp
        return s * cd[:, :, None, None] + cst, s
    s_final, s_starts = lax.scan(step, s0, (jnp.moveaxis(chunk_state, 1, 0), jnp.moveaxis(chunk_decay, 1, 0)))
    s_starts = jnp.moveaxis(s_starts, 0, 1)
    y_off = jnp.einsum('bcihn,bchpn,bcih->bcihp', Cc, s_starts, jnp.exp(cs))
    return (y_diag + y_off).reshape(bsz, T, H, P), s_final


def _ssd_recurrent(xdt, dA, Bm, Cm, s0):
    def step(s, inp):
        xt, dat, bt, ct = inp
        s = s * jnp.exp(dat)[:, :, None, None] + jnp.einsum('bhp,bhn->bhpn', xt, bt)
        return s, jnp.einsum('bhpn,bhn->bhp', s, ct)
    tm = lambda t: jnp.moveaxis(t, 1, 0)
    s, ys = lax.scan(step, s0, (tm(xdt), tm(dA), tm(Bm), tm(Cm)))
    return jnp.moveaxis(ys, 0, 1), s


def _ssd_mixer(z, xbc, dt_raw, conv_buf, s0, conv_w, conv_b, dt_bias, a_log, d_skip, norm_w, prompt):
    bsz, T, _ = z.shape
    G = GROUP_WIDTH
    GN = SSD_GROUPS * SSD_STATE
    xbc, new_buf = _causal_conv(xbc, conv_buf, conv_w, conv_b)
    xbc_f = jax.nn.silu(xbc)
    x = xbc_f[..., :G].reshape(bsz, T, SSD_HEADS, SSD_HEADDIM)
    rep = SSD_HEADS // SSD_GROUPS
    Bm = jnp.repeat(xbc_f[..., G:G + GN].reshape(bsz, T, SSD_GROUPS, SSD_STATE), rep, axis=2)
    Cm = jnp.repeat(xbc_f[..., G + GN:].reshape(bsz, T, SSD_GROUPS, SSD_STATE), rep, axis=2)
    dt = jax.nn.softplus(dt_raw + dt_bias)
    dA = dt * (-jnp.exp(a_log))
    xdt = x * dt[..., None]
    if prompt:
        y, s = _ssd_chunked(xdt, dA, Bm, Cm, s0)
    else:
        y, s = _ssd_recurrent(xdt, dA, Bm, Cm, s0)
    y = y + d_skip[:, None] * x
    y = y.reshape(bsz, T, G) * jax.nn.silu(z)
    return _rmsnorm(y, norm_w), new_buf, s


def _rwkv_mixer(cur, shift_buf, s0, mu, w0, w_up, a0, a_up, g_up, k_k, k_a, r_k, ln_w, ln_b):
    bsz, T, _ = cur.shape
    G = GROUP_WIDTH
    prev = jnp.concatenate([shift_buf[:, None], cur[:, :-1]], axis=1)
    m = cur + (prev - cur) * mu
    r, k, v = m[..., :G], m[..., G:2 * G], m[..., 2 * G:3 * G]
    o = 3 * G
    wd = m[..., o:o + RWKV_W_RANK]
    o += RWKV_W_RANK
    ad = m[..., o:o + RWKV_A_RANK]
    o += RWKV_A_RANK
    gd = m[..., o:o + RWKV_G_RANK]
    w = -jax.nn.softplus(-(w0 + jnp.tanh(wd) @ w_up)) - 0.5
    decay = jnp.exp(-jnp.exp(w))
    a = jax.nn.sigmoid(a0 + ad @ a_up)
    g = jax.nn.sigmoid(gd) @ g_up
    hs = lambda t: t.reshape(bsz, T, RWKV_HEADS, RWKV_HEADDIM)
    kk = hs(k * k_k)
    kk = kk / jnp.maximum(jnp.sqrt(jnp.sum(kk * kk, axis=-1, keepdims=True)), 1e-12)
    k = hs(k * (1.0 + (a - 1.0) * k_a))
    r, v, decay, a = hs(r), hs(v), hs(decay), hs(a)
    def step(S, inp):
        rt, kt, vt, dt, kkt, at = inp
        sa = jnp.einsum('bhvk,bhk->bhv', S, -kkt)
        S = S * dt[:, :, None, :] + sa[..., None] * (kkt * at)[:, :, None, :] + vt[..., None] * kt[:, :, None, :]
        return S, jnp.einsum('bhvk,bhk->bhv', S, rt)
    tm = lambda t: jnp.moveaxis(t, 1, 0)
    S, ys = lax.scan(step, s0, (tm(r), tm(k), tm(v), tm(decay), tm(kk), tm(a)))
    y = jnp.moveaxis(ys, 0, 1)
    mean = jnp.mean(y, axis=-1, keepdims=True)
    var = jnp.mean((y - mean) ** 2, axis=-1, keepdims=True)
    y = ((y - mean) * lax.rsqrt(var + RWKV_LN_EPS)).reshape(bsz, T, G) * ln_w + ln_b
    bonus = jnp.sum(r * k * r_k, axis=-1, keepdims=True) * v
    y = (y + bonus.reshape(bsz, T, G)) * g
    return y, cur[:, -1], S


def _cross_attn(q, mk, mv):
    bsz, T, _ = q.shape
    qh = q.reshape(bsz, T, X_HEADS, X_HEADDIM)
    s = jnp.einsum('bthd,bmhd->bhtm', qh, mk) * (X_HEADDIM ** -0.5)
    p = jax.nn.softmax(s, axis=-1)
    return jnp.einsum('bhtm,bmhd->bthd', p, mv).reshape(bsz, T, D_MODEL)


def _peer_dense(hn, wq, subkeys, u_tab, v_tab, blk):
    n, D = hn.shape
    q = _mm(hn, wq)
    ub = u_tab.astype(jnp.bfloat16)
    vb = v_tab.astype(jnp.bfloat16)

    def one(args):
        hb, qb = args
        qq = qb.reshape(blk, PEER_HEADS, 2, PEER_HALF)
        s = jnp.einsum('thcd,hckd->thck', qq, subkeys)
        s1, s2 = s[:, :, 0], s[:, :, 1]
        a1, _ = lax.top_k(s1, PEER_TOPK)
        a2, _ = lax.top_k(s2, PEER_TOPK)
        cand = (a1[..., :, None] + a2[..., None, :]).reshape(blk, PEER_HEADS, PEER_TOPK * PEER_TOPK)
        top_s, _ = lax.top_k(cand, PEER_TOPK)
        tau = top_s[..., PEER_TOPK - 1][..., None, None]
        mx = top_s[..., 0]
        z = jnp.sum(jnp.exp(top_s - mx[..., None]), axis=-1)
        sums = s1[..., :, None] + s2[..., None, :]
        g = jnp.where(sums >= tau, jnp.exp(sums - mx[..., None, None]) / z[..., None, None], 0.0)
        g = jnp.sum(g, axis=1).reshape(blk, PEER_NKEYS * PEER_NKEYS)
        act = jax.nn.gelu(jnp.dot(hb.astype(jnp.bfloat16), ub.T, preferred_element_type=jnp.float32))
        return jnp.dot((g * act).astype(jnp.bfloat16), vb, preferred_element_type=jnp.float32)

    out = lax.map(one, (hn.reshape(n // blk, blk, D), q.reshape(n // blk, blk, -1)))
    return out.reshape(n, D)


def _trunk_layer(x, lp, st, prompt):
    bsz, T, _ = x.shape
    G = GROUP_WIDTH
    n = bsz * T
    x2 = x.reshape(n, D_MODEL)
    proj = _mm(x2, lp['w_in'], gain=lp['norm_mix']).reshape(bsz, T, IN_WIDTH)
    pos0 = 0 if prompt else PAST_LEN
    pos = pos0 + jnp.arange(T, dtype=jnp.int32)
    pa = proj[..., A_OFF:A_OFF + A_COLS]
    heads = lambda t: t.reshape(bsz, T, A_HEADS, HEAD_DIM)
    q = _rope(heads(pa[..., :G]), pos)
    k = _rope(heads(pa[..., G:2 * G]), pos)
    v = heads(pa[..., 2 * G:])
    if prompt:
        ya = _moba_prompt(q, k, v)
    else:
        ya = _moba_decode(q, k, v, st['k_cache'], st['v_cache'], st['page_table'])
    pb = proj[..., B_OFF:B_OFF + B_COLS]
    yb, lru_conv, lru_h = _rglru_mixer(pb[..., :LRU_WIDTH], pb[..., LRU_WIDTH:], st['lru_conv'], st['lru_h'],
                                       lp['lru_conv_w'], lp['lru_conv_b'], lp['lru_wa'], lp['lru_ba'],
                                       lp['lru_wx'], lp['lru_bx'], lp['lru_lambda'])
    pc = proj[..., C_OFF:C_OFF + C_COLS]
    yc, ssd_conv, ssd_s = _ssd_mixer(pc[..., :G], pc[..., G:G + SSD_CONV_CH], pc[..., G + SSD_CONV_CH:],
                                     st['ssd_conv'], st['ssd'], lp['ssd_conv_w'], lp['ssd_conv_b'],
                                     lp['ssd_dt_bias'], lp['ssd_a_log'], lp['ssd_d'], lp['ssd_norm'], prompt)
    yd, rwkv_shift, rwkv_s = _rwkv_mixer(proj[..., D_OFF:D_OFF + RWKV_COLS], st['rwkv_shift'], st['rwkv'],
                                         lp['rwkv_mu'], lp['rwkv_w0'], lp['rwkv_w_up'], lp['rwkv_a0'],
                                         lp['rwkv_a_up'], lp['rwkv_g_up'], lp['rwkv_k_k'], lp['rwkv_k_a'],
                                         lp['rwkv_r_k'], lp['rwkv_ln_w'], lp['rwkv_ln_b'])
    y_mix = jnp.concatenate([ya, yb, yc, yd], axis=-1).reshape(n, 4 * G)
    x2 = _mm(y_mix, lp['w_out'], residual=x2)
    qx = _mm(x2, lp['x_wq'], gain=lp['norm_x']).reshape(bsz, T, D_MODEL)
    o = _cross_attn(qx, st['mem_k'], st['mem_v']).reshape(n, D_MODEL)
    x2 = _mm(o, lp['x_wo'], residual=x2)
    hn = _rmsnorm(x2, lp['norm_ffn'])
    blk = 256 if n % 256 == 0 else n
    x2 = x2 + _peer_dense(hn, lp['peer_wq'], lp['peer_subkeys'], lp['peer_u'], lp['peer_v'], blk)
    new = {'k': k, 'v': v, 'lru_h': lru_h, 'lru_conv': lru_conv, 'ssd': ssd_s, 'ssd_conv': ssd_conv,
           'rwkv': rwkv_s, 'rwkv_shift': rwkv_shift}
    return x2.reshape(bsz, T, D_MODEL), new


def kernel(x_prompt, x_sample, mem_prompt, cache_moba_k, cache_moba_v, page_table, state_lru_h, state_lru_conv, state_ssd, state_ssd_conv, state_rwkv, state_rwkv_shift, cache_mem_k, cache_mem_v, norm_mix, w_in, w_out, lru_conv_w, lru_conv_b, lru_wa, lru_ba, lru_wx, lru_bx, lru_lambda, ssd_conv_w, ssd_conv_b, ssd_dt_bias, ssd_a_log, ssd_d, ssd_norm, rwkv_mu, rwkv_w0, rwkv_w_up, rwkv_a0, rwkv_a_up, rwkv_g_up, rwkv_k_k, rwkv_k_a, rwkv_r_k, rwkv_ln_w, rwkv_ln_b, norm_x, x_wq, x_wk, x_wv, x_wo, norm_ffn, peer_wq, peer_subkeys, peer_u, peer_v, final_norm):
    bp = x_prompt.shape[0]
    f32 = jnp.float32
    yp, ys = x_prompt, x_sample
    names = ('k', 'v', 'lru_h', 'lru_conv', 'ssd', 'ssd_conv', 'rwkv', 'rwkv_shift')
    p_new = {n: [] for n in names + ('mem_k', 'mem_v')}
    s_new = {n: [] for n in names}
    for l in range(DEPTH):
        lp = {
            'norm_mix': norm_mix[l], 'w_in': w_in[l], 'w_out': w_out[l],
            'lru_conv_w': lru_conv_w[l], 'lru_conv_b': lru_conv_b[l], 'lru_wa': lru_wa[l], 'lru_ba': lru_ba[l],
            'lru_wx': lru_wx[l], 'lru_bx': lru_bx[l], 'lru_lambda': lru_lambda[l],
            'ssd_conv_w': ssd_conv_w[l], 'ssd_conv_b': ssd_conv_b[l], 'ssd_dt_bias': ssd_dt_bias[l],
            'ssd_a_log': ssd_a_log[l], 'ssd_d': ssd_d[l], 'ssd_norm': ssd_norm[l],
            'rwkv_mu': rwkv_mu[l], 'rwkv_w0': rwkv_w0[l], 'rwkv_w_up': rwkv_w_up[l], 'rwkv_a0': rwkv_a0[l],
            'rwkv_a_up': rwkv_a_up[l], 'rwkv_g_up': rwkv_g_up[l], 'rwkv_k_k': rwkv_k_k[l], 'rwkv_k_a': rwkv_k_a[l],
            'rwkv_r_k': rwkv_r_k[l], 'rwkv_ln_w': rwkv_ln_w[l], 'rwkv_ln_b': rwkv_ln_b[l],
            'norm_x': norm_x[l], 'x_wq': x_wq[l], 'x_wo': x_wo[l],
            'norm_ffn': norm_ffn[l], 'peer_wq': peer_wq[l], 'peer_subkeys': peer_subkeys[l],
            'peer_u': peer_u[l], 'peer_v': peer_v[l],
        }
        mem2 = mem_prompt.reshape(bp * MEM_LEN, D_MODEL)
        mk = _mm(mem2, x_wk[l]).reshape(bp, MEM_LEN, X_HEADS, X_HEADDIM)
        mv = _mm(mem2, x_wv[l]).reshape(bp, MEM_LEN, X_HEADS, X_HEADDIM)
        st_p = {
            'lru_conv': jnp.zeros((bp, CONV_W - 1, LRU_WIDTH), f32),
            'lru_h': jnp.zeros((bp, LRU_WIDTH), f32),
            'ssd_conv': jnp.zeros((bp, CONV_W - 1, SSD_CONV_CH), f32),
            'ssd': jnp.zeros((bp, SSD_HEADS, SSD_HEADDIM, SSD_STATE), f32),
            'rwkv_shift': jnp.zeros((bp, RWKV_COLS), f32),
            'rwkv': jnp.zeros((bp, RWKV_HEADS, RWKV_HEADDIM, RWKV_HEADDIM), f32),
            'mem_k': mk, 'mem_v': mv,
        }
        yp, npl = _trunk_layer(yp, lp, st_p, True)
        for n in names:
            p_new[n].append(npl[n])
        p_new['mem_k'].append(mk)
        p_new['mem_v'].append(mv)
        st_s = {
            'k_cache': cache_moba_k[l], 'v_cache': cache_moba_v[l], 'page_table': page_table,
            'lru_conv': state_lru_conv[l], 'lru_h': state_lru_h[l],
            'ssd_conv': state_ssd_conv[l], 'ssd': state_ssd[l],
            'rwkv_shift': state_rwkv_shift[l], 'rwkv': state_rwkv[l],
            'mem_k': cache_mem_k[l], 'mem_v': cache_mem_v[l],
        }
        ys, nsl = _trunk_layer(ys, lp, st_s, False)
        for n in names:
            s_new[n].append(nsl[n])
    y_prompt = _rmsnorm(yp, final_norm)
    y_sample = _rmsnorm(ys, final_norm)
    return (y_prompt, y_sample,
            jnp.stack(p_new['k']), jnp.stack(p_new['v']), jnp.stack(p_new['lru_h']), jnp.stack(p_new['lru_conv']),
            jnp.stack(p_new['ssd']), jnp.stack(p_new['ssd_conv']), jnp.stack(p_new['rwkv']), jnp.stack(p_new['rwkv_shift']),
            jnp.stack(p_new['mem_k']), jnp.stack(p_new['mem_v']),
            jnp.stack(s_new['k']), jnp.stack(s_new['v']), jnp.stack(s_new['lru_h']), jnp.stack(s_new['lru_conv']),
            jnp.stack(s_new['ssd']), jnp.stack(s_new['ssd_conv']), jnp.stack(s_new['rwkv']), jnp.stack(s_new['rwkv_shift']))
```

```python
import functools

import jax
import jax.numpy as jnp
from jax import lax
from jax.experimental import pallas as pl
from jax.experimental.pallas import tpu as pltpu

D_MODEL = 1024
DEPTH = 2
PAST_LEN = 16384
PAGE_SIZE = 128
GROUP_WIDTH = 256
HEAD_DIM = 64
A_HEADS = GROUP_WIDTH // HEAD_DIM
MOBA_BLOCK = 256
MOBA_TOPK = 3
ROPE_THETA = 10000.0
LRU_WIDTH = GROUP_WIDTH
LRU_BLOCKS = 4
LRU_BDIM = LRU_WIDTH // LRU_BLOCKS
LRU_C = 8.0
CONV_W = 4
SSD_HEADS = 4
SSD_HEADDIM = GROUP_WIDTH // SSD_HEADS
SSD_GROUPS = 2
SSD_STATE = 64
SSD_CHUNK = 128
SSD_CONV_CH = GROUP_WIDTH + 2 * SSD_GROUPS * SSD_STATE
RWKV_HEADS = 4
RWKV_HEADDIM = GROUP_WIDTH // RWKV_HEADS
RWKV_W_RANK = 32
RWKV_A_RANK = 32
RWKV_G_RANK = 64
RWKV_COLS = 3 * GROUP_WIDTH + RWKV_W_RANK + RWKV_A_RANK + RWKV_G_RANK
RWKV_LN_EPS = 64e-5
A_OFF = 0
A_COLS = 3 * GROUP_WIDTH
B_OFF = A_OFF + A_COLS
B_COLS = 2 * LRU_WIDTH
C_OFF = B_OFF + B_COLS
C_COLS = GROUP_WIDTH + SSD_CONV_CH + SSD_HEADS
D_OFF = C_OFF + C_COLS
IN_WIDTH = D_OFF + RWKV_COLS
MEM_LEN = 256
X_HEADS = 4
X_HEADDIM = D_MODEL // X_HEADS
PEER_HEADS = 8
PEER_NKEYS = 128
PEER_QDIM = 256
PEER_HALF = PEER_QDIM // 2
PEER_TOPK = 16
NORM_EPS = 1e-6
NEG_INF = -1e30
BIG_NEG = -1e30

LANES = 128
ABC_PCOLS = 2176
RWKV_PCOLS = 3 * GROUP_WIDTH + 3 * LANES
PEER_TOK_PAD = 512
_VMEM_LIMIT = 56 * 1024 * 1024
_HI = lax.Precision.HIGHEST


def _mm_kernel(*refs, has_norm, has_res):
    x_ref, w_ref = refs[0], refs[1]
    pos = 2
    g_ref = r_ref = None
    if has_norm:
        g_ref = refs[pos]
        pos += 1
    if has_res:
        r_ref = refs[pos]
        pos += 1
    o_ref = refs[pos]
    x = x_ref[...]
    if has_norm:
        x = x * lax.rsqrt(jnp.mean(x * x, axis=-1, keepdims=True) + NORM_EPS) * g_ref[...]
    y = jnp.dot(x.astype(jnp.bfloat16), w_ref[...], preferred_element_type=jnp.float32)
    if has_res:
        y = y + r_ref[...]
    o_ref[...] = y


def _mm(x, w, gain=None, residual=None, tm=512, tn=512):
    M, K = x.shape
    N = w.shape[1]
    tm = min(tm, M)
    assert M % tm == 0
    n_pad = -(-N // tn) * tn
    wb = w.astype(jnp.bfloat16)
    if n_pad != N:
        wb = jnp.pad(wb, ((0, 0), (0, n_pad - N)))
    args = [x, wb]
    in_specs = [pl.BlockSpec((tm, K), lambda i, j: (i, 0)),
                pl.BlockSpec((K, tn), lambda i, j: (0, j))]
    if gain is not None:
        args.append(gain.reshape(1, K).astype(jnp.float32))
        in_specs.append(pl.BlockSpec((1, K), lambda i, j: (0, 0)))
    if residual is not None:
        assert n_pad == N
        args.append(residual)
        in_specs.append(pl.BlockSpec((tm, tn), lambda i, j: (i, j)))
    out = pl.pallas_call(
        functools.partial(_mm_kernel, has_norm=gain is not None, has_res=residual is not None),
        grid=(M // tm, n_pad // tn),
        in_specs=in_specs,
        out_specs=pl.BlockSpec((tm, tn), lambda i, j: (i, j)),
        out_shape=jax.ShapeDtypeStruct((M, n_pad), jnp.float32),
        compiler_params=pltpu.CompilerParams(dimension_semantics=("parallel", "parallel")),
        name="mm",
    )(*args)
    return out[:, :N] if n_pad != N else out


def _rmsnorm(x, g):
    return x * lax.rsqrt(jnp.mean(x * x, axis=-1, keepdims=True) + NORM_EPS) * g


def _rope(x, pos):
    half = x.shape[-1] // 2
    freq = 1.0 / (ROPE_THETA ** (jnp.arange(half, dtype=jnp.float32) / half))
    ang = pos.astype(jnp.float32)[:, None] * freq[None, :]
    cos = jnp.cos(ang)[None, :, None, :]
    sin = jnp.sin(ang)[None, :, None, :]
    x1, x2 = x[..., :half], x[..., half:]
    return jnp.concatenate([x1 * cos - x2 * sin, x2 * cos + x1 * sin], axis=-1)


def _topk_mask(gate, k):
    n = gate.shape[-1]
    a = gate[..., :, None]
    b = gate[..., None, :]
    idx = jnp.arange(n)
    beats = (b > a) | ((b == a) & (idx[None, :] < idx[:, None]))
    rank = jnp.sum(beats, axis=-1)
    return rank < k


def _moba_prompt(q, k, v):
    bsz, T, H, hd = q.shape
    nb = T // MOBA_BLOCK
    kmean = jnp.mean(k.reshape(bsz, nb, MOBA_BLOCK, H, hd), axis=2)
    gate = jnp.einsum('bthd,bnhd->bhtn', q, kmean)
    t = jnp.arange(T, dtype=jnp.int32)
    cur = t // MOBA_BLOCK
    past = jnp.arange(nb, dtype=jnp.int32)[None, :] < cur[:, None]
    gate = jnp.where(past, gate, NEG_INF)
    sel = _topk_mask(gate, MOBA_TOPK) & past
    sel = sel | (jnp.arange(nb, dtype=jnp.int32)[None, :] == cur[:, None])
    s = jnp.einsum('bthd,bjhd->bhtj', q, k) * (hd ** -0.5)
    allow = jnp.repeat(sel, MOBA_BLOCK, axis=-1) & (t[None, :] <= t[:, None])
    s = jnp.where(allow, s, NEG_INF)
    p = jax.nn.softmax(s, axis=-1)
    return jnp.einsum('bhtj,bjhd->bthd', p, v).reshape(bsz, T, H * hd)


def _moba_decode(q, k_new, v_new, k_cache, v_cache, page_table):
    bsz = q.shape[0]
    H, hd = A_HEADS, HEAD_DIM
    nb = PAST_LEN // MOBA_BLOCK
    k_past = k_cache[page_table].reshape(bsz, nb, MOBA_BLOCK, H, hd)
    v_past = v_cache[page_table].reshape(bsz, nb, MOBA_BLOCK, H, hd)
    kmean = jnp.mean(k_past, axis=2)
    gate = jnp.einsum('bhd,bnhd->bhn', q[:, 0], kmean)
    _, top = lax.top_k(gate, MOBA_TOPK)
    bi = jnp.arange(bsz)[:, None, None]
    hi = jnp.arange(H)[None, :, None]
    kg = k_past.transpose(0, 3, 1, 2, 4)[bi, hi, top].reshape(bsz, H, MOBA_TOPK * MOBA_BLOCK, hd)
    vg = v_past.transpose(0, 3, 1, 2, 4)[bi, hi, top].reshape(bsz, H, MOBA_TOPK * MOBA_BLOCK, hd)
    kg = jnp.concatenate([kg, k_new.transpose(0, 2, 1, 3)], axis=2)
    vg = jnp.concatenate([vg, v_new.transpose(0, 2, 1, 3)], axis=2)
    s = jnp.einsum('bhd,bhkd->bhk', q[:, 0], kg) * (hd ** -0.5)
    p = jax.nn.softmax(s, axis=-1)
    return jnp.einsum('bhk,bhkd->bhd', p, vg).reshape(bsz, 1, H * hd)


def _causal_conv(x, buf, w, b):
    T = x.shape[1]
    xp = jnp.concatenate([buf.astype(x.dtype), x], axis=1)
    y = b + xp[:, 0:T] * w[0]
    for j in range(1, CONV_W):
        y = y + xp[:, j:j + T] * w[j]
    return y, xp[:, -(CONV_W - 1):]


def _linear_scan(a, b, h0):
    b = b.at[:, 0].add(a[:, 0] * h0)
    def comb(l, r):
        al, bl = l
        ar, br = r
        return al * ar, ar * bl + br
    _, h = lax.associative_scan(comb, (a, b), axis=1)
    return h


def _rglru_mixer(u, gate, conv_buf, h0, conv_w, conv_b, wa, ba, wx, bx, lam):
    bsz, T, W = u.shape
    xc, new_buf = _causal_conv(u, conv_buf, conv_w, conv_b)
    xh = xc.reshape(bsz, T, LRU_BLOCKS, LRU_BDIM)
    r = jax.nn.sigmoid(jnp.einsum('btnd,nde->btne', xh, wa).reshape(bsz, T, W) + ba)
    i = jax.nn.sigmoid(jnp.einsum('btnd,nde->btne', xh, wx).reshape(bsz, T, W) + bx)
    log_a = -LRU_C * r * jax.nn.softplus(-lam)
    a = jnp.exp(log_a)
    b = jnp.sqrt(-jnp.expm1(2.0 * log_a)) * (i * xc)
    h = _linear_scan(a, b, h0)
    y = h * jax.nn.gelu(gate)
    return y, new_buf, h[:, -1]


def _ssd_chunked(xdt, dA, Bm, Cm, s0):
    bsz, T, H, P = xdt.shape
    N = Bm.shape[-1]
    Q = SSD_CHUNK
    nc = T // Q
    xc = xdt.reshape(bsz, nc, Q, H, P)
    Bc = Bm.reshape(bsz, nc, Q, H, N)
    Cc = Cm.reshape(bsz, nc, Q, H, N)
    cs = jnp.cumsum(dA.reshape(bsz, nc, Q, H), axis=2)
    seg = cs[:, :, :, None, :] - cs[:, :, None, :, :]
    causal = jnp.tril(jnp.ones((Q, Q), dtype=bool))[None, None, :, :, None]
    Lm = jnp.exp(jnp.where(causal, seg, -jnp.inf))
    y_diag = jnp.einsum('bcihn,bcjhn,bcijh,bcjhp->bcihp', Cc, Bc, Lm, xc)
    decay_to_end = jnp.exp(cs[:, :, -1:, :] - cs)
    chunk_state = jnp.einsum('bcjhn,bcjh,bcjhp->bchpn', Bc, decay_to_end, xc)
    chunk_decay = jnp.exp(cs[:, :, -1, :])
    def step(s, inp):
        cst, cd = inp
        return s * cd[:, :, None, None] + cst, s
    s_final, s_starts = lax.scan(step, s0, (jnp.moveaxis(chunk_state, 1, 0), jnp.moveaxis(chunk_decay, 1, 0)))
    s_starts = jnp.moveaxis(s_starts, 0, 1)
    y_off = jnp.einsum('bcihn,bchpn,bcih->bcihp', Cc, s_starts, jnp.exp(cs))
    return (y_diag + y_off).reshape(bsz, T, H, P), s_final


def _ssd_recurrent(xdt, dA, Bm, Cm, s0):
    def step(s, inp):
        xt, dat, bt, ct = inp
        s = s * jnp.exp(dat)[:, :, None, None] + jnp.einsum('bhp,bhn->bhpn', xt, bt)
        return s, jnp.einsum('bhpn,bhn->bhp', s, ct)
    tm = lambda t: jnp.moveaxis(t, 1, 0)
    s, ys = lax.scan(step, s0, (tm(xdt), tm(dA), tm(Bm), tm(Cm)))
    return jnp.moveaxis(ys, 0, 1), s


def _ssd_mixer(z, xbc, dt_raw, conv_buf, s0, conv_w, conv_b, dt_bias, a_log, d_skip, norm_w, prompt):
    bsz, T, _ = z.shape
    G = GROUP_WIDTH
    GN = SSD_GROUPS * SSD_STATE
    xbc, new_buf = _causal_conv(xbc, conv_buf, conv_w, conv_b)
    xbc_f = jax.nn.silu(xbc)
    x = xbc_f[..., :G].reshape(bsz, T, SSD_HEADS, SSD_HEADDIM)
    rep = SSD_HEADS // SSD_GROUPS
    Bm = jnp.repeat(xbc_f[..., G:G + GN].reshape(bsz, T, SSD_GROUPS, SSD_STATE), rep, axis=2)
    Cm = jnp.repeat(xbc_f[..., G + GN:].reshape(bsz, T, SSD_GROUPS, SSD_STATE), rep, axis=2)
    dt = jax.nn.softplus(dt_raw + dt_bias)
    dA = dt * (-jnp.exp(a_log))
    xdt = x * dt[..., None]
    if prompt:
        y, s = _ssd_chunked(xdt, dA, Bm, Cm, s0)
    else:
        y, s = _ssd_recurrent(xdt, dA, Bm, Cm, s0)
    y = y + d_skip[:, None] * x
    y = y.reshape(bsz, T, G) * jax.nn.silu(z)
    return _rmsnorm(y, norm_w), new_buf, s


def _rwkv_pad_cols(a):
    G = GROUP_WIDTH
    z = lambda w: jnp.zeros(a.shape[:-1] + (w,), a.dtype)
    o = 3 * G
    wd = a[..., o:o + RWKV_W_RANK]
    ad = a[..., o + RWKV_W_RANK:o + RWKV_W_RANK + RWKV_A_RANK]
    gd = a[..., o + RWKV_W_RANK + RWKV_A_RANK:]
    return jnp.concatenate([a[..., :o], wd, z(LANES - RWKV_W_RANK), ad, z(LANES - RWKV_A_RANK),
                            gd, z(LANES - RWKV_G_RANK)], axis=-1)


def _rwkv_unpad_cols(a):
    o = 3 * GROUP_WIDTH
    return jnp.concatenate([a[..., :o], a[..., o:o + RWKV_W_RANK], a[..., o + LANES:o + LANES + RWKV_A_RANK],
                            a[..., o + 2 * LANES:o + 2 * LANES + RWKV_G_RANK]], axis=-1)


def _rwkv_kernel(pd_ref, sh0_ref, s0_ref, mu_ref, vec_ref, wup_ref, aup_ref, gup_ref, seg_ref,
                 y_ref, sout_ref, shout_ref,
                 prev_scr, s_scr, r_scr, k_scr, d_scr, kk_scr, bb_scr, g_scr, bonus_scr, vt_scr, yt_scr, *, n_steps):
    c = pl.program_id(1)
    tc = pd_ref.shape[1]
    G = GROUP_WIDTH
    hd = RWKV_HEADDIM

    @pl.when(c == 0)
    def _():
        prev_scr[...] = sh0_ref[...]
        s_scr[...] = s0_ref[...]

    nb = pd_ref.shape[0]
    w0 = vec_ref[0:1, :]
    a0 = vec_ref[1:2, :]
    k_k = vec_ref[2:3, :]
    k_a = vec_ref[3:4, :]
    ln_w = vec_ref[4:5, :]
    ln_b = vec_ref[5:6, :]
    r_k = vec_ref[6:7, :]
    seg = seg_ref[...]
    hdot = lambda x, y: jnp.dot(x, y, precision=_HI, preferred_element_type=jnp.float32)
    for ib in range(nb):
        cur = pd_ref[ib]
        row = lax.broadcasted_iota(jnp.int32, cur.shape, 0)
        prev = jnp.where(row == 0, prev_scr[ib], pltpu.roll(cur, 1, axis=0))
        prev_scr[ib] = cur[n_steps - 1:n_steps, :] if n_steps < tc else cur[tc - 1:tc, :]
        m = cur + (prev - cur) * mu_ref[...]
        r = m[:, 0:G]
        k = m[:, G:2 * G]
        v = m[:, 2 * G:3 * G]
        wd = m[:, 3 * G:3 * G + LANES]
        ad = m[:, 3 * G + LANES:3 * G + 2 * LANES]
        gd = m[:, 3 * G + 2 * LANES:3 * G + 3 * LANES]
        z = -(w0 + hdot(jnp.tanh(wd), wup_ref[...]))
        w = -(jnp.maximum(z, 0.0) + jnp.log1p(jnp.exp(-jnp.abs(z)))) - 0.5
        a = jax.nn.sigmoid(a0 + hdot(ad, aup_ref[...]))
        kkr = k * k_k
        kk = kkr / jnp.maximum(jnp.sqrt(hdot(kkr * kkr, seg)), 1e-12)
        k2 = k * (1.0 + (a - 1.0) * k_a)
        g_scr[ib] = hdot(jax.nn.sigmoid(gd), gup_ref[...])
        bonus_scr[ib] = hdot(r * k2 * r_k, seg) * v
        r_scr[ib] = r
        k_scr[ib] = k2
        d_scr[ib] = jnp.exp(-jnp.exp(w))
        kk_scr[ib] = -kk
        bb_scr[ib] = kk * a
        vt_scr[ib] = v.T
    yt_scr[...] = jnp.zeros_like(yt_scr)

    lane = lax.broadcasted_iota(jnp.int32, (hd, LANES), 1)
    lo_half = lane < hd

    def half_sums(p):
        lo = jnp.sum(jnp.where(lo_half, p, 0.0), axis=1, keepdims=True)
        hi = jnp.sum(jnp.where(lo_half, 0.0, p), axis=1, keepdims=True)
        return lo, hi

    npair = RWKV_HEADS // 2

    def step8(i, states):
        t0 = pl.multiple_of(i * 8, 8)
        blk = pl.multiple_of((t0 // LANES) * LANES, LANES)
        rows = pl.ds(t0, 8)
        states = list(states)
        for ib in range(nb):
            for p in range(npair):
                cols = slice(p * LANES, (p + 1) * LANES)
                kk8, d8, bb8, k8, r8 = (ref[ib, rows, cols] for ref in (kk_scr, d_scr, bb_scr, k_scr, r_scr))
                tile = lambda ref, hh: ref[ib, pl.ds(p * LANES + hh * hd, hd), pl.ds(blk, LANES)]
                vt0, vt1, y0, y1 = tile(vt_scr, 0), tile(vt_scr, 1), tile(yt_scr, 0), tile(yt_scr, 1)
                s = states[ib * npair + p]
                for j in range(min(8, n_steps)):
                    hit = lane == (t0 + j - blk)
                    lo, hi = half_sums(s * kk8[j:j + 1, :])
                    sa = jnp.where(lo_half, lo, hi)
                    vc0 = jnp.sum(jnp.where(hit, vt0, 0.0), axis=1, keepdims=True)
                    vc1 = jnp.sum(jnp.where(hit, vt1, 0.0), axis=1, keepdims=True)
                    vcol = jnp.where(lo_half, vc0, vc1)
                    s = s * d8[j:j + 1, :] + sa * bb8[j:j + 1, :] + vcol * k8[j:j + 1, :]
                    ylo, yhi = half_sums(s * r8[j:j + 1, :])
                    y0 = jnp.where(hit, ylo, y0)
                    y1 = jnp.where(hit, yhi, y1)
                states[ib * npair + p] = s
                yt_scr[ib, pl.ds(p * LANES, hd), pl.ds(blk, LANES)] = y0
                yt_scr[ib, pl.ds(p * LANES + hd, hd), pl.ds(blk, LANES)] = y1
        return tuple(states)

    init = tuple(s_scr[ib, p] for ib in range(nb) for p in range(npair))
    states = lax.fori_loop(0, -(-n_steps // 8), step8, init)
    for ib in range(nb):
        for p in range(npair):
            s_scr[ib, p] = states[ib * npair + p]

    inv = 1.0 / hd
    for ib in range(nb):
        y = yt_scr[ib].T
        mean = hdot(y, seg) * inv
        yc = y - mean
        var = hdot(yc * yc, seg) * inv
        yn = yc * lax.rsqrt(var + RWKV_LN_EPS) * ln_w + ln_b
        y_ref[ib] = (yn + bonus_scr[ib]) * g_scr[ib]

    @pl.when(c == pl.num_programs(1) - 1)
    def _():
        sout_ref[...] = s_scr[...]
        shout_ref[...] = prev_scr[...]


def _rwkv_mixer(pd, shift0, s0, mu, w0, w_up, a0, a_up, g_up, k_k, k_a, r_k, ln_w, ln_b, *, n_steps, tc, nb=1):
    B, Tp, _ = pd.shape
    G = GROUP_WIDTH
    H, hd = RWKV_HEADS, RWKV_HEADDIM
    nchunk = Tp // tc
    assert Tp % tc == 0 and (n_steps == tc or nchunk == 1) and (n_steps % 8 == 0 or n_steps < 8)
    assert B % nb == 0
    f32 = jnp.float32
    sh0 = _rwkv_pad_cols(shift0)[:, None, :]
    s0p = s0.reshape(B, H // 2, 2, hd, hd).transpose(0, 1, 3, 2, 4).reshape(B, H // 2, hd, 2 * hd)
    mu_p = _rwkv_pad_cols(mu[None, :])
    vec = jnp.stack([w0, a0, k_k, k_a, ln_w, ln_b, r_k.reshape(G), jnp.zeros((G,), f32)])
    padk = lambda wgt: jnp.pad(wgt, ((0, LANES - wgt.shape[0]), (0, 0)))
    hid = jnp.arange(G) // hd
    seg = (hid[:, None] == hid[None, :]).astype(f32)
    full2 = lambda shp: pl.BlockSpec(shp, lambda b, c: (0, 0))
    tscr = lambda: pltpu.VMEM((nb, tc, G), f32)
    y, s_out, sh_out = pl.pallas_call(
        functools.partial(_rwkv_kernel, n_steps=n_steps),
        grid=(B // nb, nchunk),
        in_specs=[pl.BlockSpec((nb, tc, RWKV_PCOLS), lambda b, c: (b, c, 0)),
                  pl.BlockSpec((nb, 1, RWKV_PCOLS), lambda b, c: (b, 0, 0)),
                  pl.BlockSpec((nb, H // 2, hd, 2 * hd), lambda b, c: (b, 0, 0, 0)),
                  full2((1, RWKV_PCOLS)), full2((8, G)), full2((LANES, G)), full2((LANES, G)), full2((LANES, G)),
                  full2((G, G))],
        out_specs=[pl.BlockSpec((nb, tc, G), lambda b, c: (b, c, 0)),
                   pl.BlockSpec((nb, H // 2, hd, 2 * hd), lambda b, c: (b, 0, 0, 0)),
                   pl.BlockSpec((nb, 1, RWKV_PCOLS), lambda b, c: (b, 0, 0))],
        out_shape=[jax.ShapeDtypeStruct((B, Tp, G), f32),
                   jax.ShapeDtypeStruct((B, H // 2, hd, 2 * hd), f32),
                   jax.ShapeDtypeStruct((B, 1, RWKV_PCOLS), f32)],
        scratch_shapes=[pltpu.VMEM((nb, 1, RWKV_PCOLS), f32), pltpu.VMEM((nb, H // 2, hd, 2 * hd), f32),
                        tscr(), tscr(), tscr(), tscr(), tscr(), tscr(), tscr(),
                        pltpu.VMEM((nb, G, tc), f32), pltpu.VMEM((nb, G, tc), f32)],
        compiler_params=pltpu.CompilerParams(dimension_semantics=("parallel", "arbitrary"),
                                             vmem_limit_bytes=_VMEM_LIMIT),
        name="rwkv_mixer",
    )(pd, sh0, s0p, mu_p, vec, padk(w_up), padk(a_up), padk(g_up), seg)
    s_new = s_out.reshape(B, H // 2, hd, 2, hd).transpose(0, 1, 3, 2, 4).reshape(B, H, hd, hd)
    return y, _rwkv_unpad_cols(sh_out[:, 0, :]), s_new


def _cross_attn(q, mk, mv):
    bsz, T, _ = q.shape
    qh = q.reshape(bsz, T, X_HEADS, X_HEADDIM)
    s = jnp.einsum('bthd,bmhd->bhtm', qh, mk) * (X_HEADDIM ** -0.5)
    p = jax.nn.softmax(s, axis=-1)
    return jnp.einsum('bhtm,bmhd->bthd', p, mv).reshape(bsz, T, D_MODEL)


def _top_values(s, count, with_rank=False):
    vals = []
    cur = s
    rank = jnp.full(s.shape, float(count), jnp.float32) if with_rank else None
    for r in range(count):
        m = jnp.max(cur, axis=0, keepdims=True)
        vals.append(m)
        hit = cur == m
        if with_rank:
            rank = jnp.where(hit, float(r), rank)
        cur = jnp.where(hit, BIG_NEG, cur)
    return (vals, rank) if with_rank else vals


def _peer_route_kernel(x_ref, g_ref, wqt_ref, sk_ref, hn_ref, cnt_ref, e1_ref, r2_ref, e2_ref, a1_scr, a2_scr):
    tb = x_ref.shape[0]
    x = x_ref[...]
    hn = x * lax.rsqrt(jnp.mean(x * x, axis=-1, keepdims=True) + NORM_EPS) * g_ref[...]
    hb = hn.astype(jnp.bfloat16)
    hn_ref[...] = hb
    qt = lax.dot_general(wqt_ref[...], hb, (((1,), (1,)), ((), ())), preferred_element_type=jnp.float32)
    n_top = PEER_TOPK + 1
    row8 = lax.broadcasted_iota(jnp.int32, (8, tb), 0)
    a1_scr[...] = jnp.full(a1_scr.shape, BIG_NEG, jnp.float32)
    a2_scr[...] = jnp.full(a2_scr.shape, BIG_NEG, jnp.float32)
    for h in range(PEER_HEADS):
        q1 = qt[(2 * h) * PEER_HALF:(2 * h + 1) * PEER_HALF, :].astype(jnp.bfloat16)
        q2 = qt[(2 * h + 1) * PEER_HALF:(2 * h + 2) * PEER_HALF, :].astype(jnp.bfloat16)
        s1 = jnp.dot(sk_ref[2 * h], q1, preferred_element_type=jnp.float32)
        s2 = jnp.dot(sk_ref[2 * h + 1], q2, preferred_element_type=jnp.float32)
        v1 = _top_values(s1, n_top)
        v2, rank2 = _top_values(s2, n_top, with_rank=True)
        for r in range(n_top):
            a1_scr[r:r + 1, :] = v1[r]
            a2_scr[r:r + 1, :] = v2[r]
        a1 = a1_scr[...]
        a2 = a2_scr[...]
        pieces = [a1 + a2[0:1, :]]
        for q in range(1, 8):
            lim = n_top // (q + 1)
            pieces.append(jnp.where(row8 < lim, a1[0:8, :] + a2[q:q + 1, :], BIG_NEG))
        pieces.append(a1[0:1, :] + a2[8:24, :])
        cand = jnp.concatenate(pieces, axis=0)
        c = _top_values(cand, n_top)
        tau = 0.5 * (c[PEER_TOPK - 1] + c[PEER_TOPK])
        mx = c[0]
        z = jnp.sum(jnp.where(cand >= tau, jnp.exp(cand - mx), 0.0), axis=0, keepdims=True)
        cnt = jnp.zeros(s1.shape, jnp.float32)
        for q in range(n_top):
            cnt = cnt + jnp.where(s1 + v2[q] >= tau, 1.0, 0.0)
        cnt_ref[h] = cnt
        e1_ref[h] = jnp.exp(s1 - v1[0]) / z
        r2_ref[h] = rank2.astype(jnp.bfloat16)
        e2_ref[h] = jnp.exp(s2 - v2[0]).astype(jnp.bfloat16)


def _peer_route(x, gain, wqt_b, sk_b, tb=256):
    n = x.shape[0]
    assert n % tb == 0
    hk = PEER_HEADS
    blk3 = pl.BlockSpec((hk, PEER_NKEYS, tb), lambda i: (0, 0, i))
    shp_f = jax.ShapeDtypeStruct((hk, PEER_NKEYS, n), jnp.float32)
    shp_b = jax.ShapeDtypeStruct((hk, PEER_NKEYS, n), jnp.bfloat16)
    return pl.pallas_call(
        _peer_route_kernel,
        grid=(n // tb,),
        in_specs=[pl.BlockSpec((tb, D_MODEL), lambda i: (i, 0)),
                  pl.BlockSpec((1, D_MODEL), lambda i: (0, 0)),
                  pl.BlockSpec((2 * hk * PEER_HALF, D_MODEL), lambda i: (0, 0)),
                  pl.BlockSpec((2 * hk, PEER_NKEYS, PEER_HALF), lambda i: (0, 0, 0))],
        out_specs=[pl.BlockSpec((tb, D_MODEL), lambda i: (i, 0)), blk3, blk3, blk3, blk3],
        out_shape=[jax.ShapeDtypeStruct((n, D_MODEL), jnp.bfloat16), shp_f, shp_f, shp_b, shp_b],
        scratch_shapes=[pltpu.VMEM((24, tb), jnp.float32), pltpu.VMEM((24, tb), jnp.float32)],
        compiler_params=pltpu.CompilerParams(dimension_semantics=("parallel",), vmem_limit_bytes=_VMEM_LIMIT),
        name="peer_route",
    )(x, gain, wqt_b, sk_b)


def _gelu_tanh(x):
    return 0.5 * x * (1.0 + jnp.tanh(0.7978845608028654 * (x + 0.044715 * (x * x * x))))


def _peer_dense_kernel(x_ref, hn_ref, u_ref, vt_ref, cnt_ref, e1_ref, r2_ref, e2_ref, o_ref, acc_ref, w_ref, *, ic):
    e = pl.program_id(1)
    tb = x_ref.shape[0]

    @pl.when(e == 0)
    def _():
        acc_ref[...] = jnp.zeros_like(acc_ref)

    act = lax.dot_general(u_ref[...], hn_ref[...], (((1,), (1,)), ((), ())), preferred_element_type=jnp.float32)
    for il in range(ic):
        g = None
        for h in range(PEER_HEADS):
            cb = jnp.broadcast_to(cnt_ref[h, il:il + 1, :], (16, tb)).astype(jnp.bfloat16)[None]
            eb = jnp.broadcast_to(e1_ref[h, il:il + 1, :], (16, tb)).astype(jnp.bfloat16)[None]
            r2 = r2_ref[h].reshape(PEER_NKEYS // 16, 16, tb)
            e2 = e2_ref[h].reshape(PEER_NKEYS // 16, 16, tb)
            t = jnp.where(r2 < cb, e2 * eb, jnp.zeros_like(e2))
            g = t if g is None else g + t
        a = act[il * PEER_NKEYS:(il + 1) * PEER_NKEYS, :]
        gf = g.reshape(PEER_NKEYS, tb).astype(jnp.float32)
        w_ref[il * PEER_NKEYS:(il + 1) * PEER_NKEYS, :] = (gf * _gelu_tanh(a)).astype(jnp.bfloat16)
    acc_ref[...] += jnp.dot(vt_ref[...], w_ref[...], preferred_element_type=jnp.float32)

    @pl.when(e == pl.num_programs(1) - 1)
    def _():
        o_ref[...] = x_ref[...] + acc_ref[...].T


def _peer_dense(x, hn_b, u_b, vt_b, cnt, e1, r2, e2, tb=PEER_TOK_PAD, ic=16):
    n = x.shape[0]
    ne = u_b.shape[0]
    ec = ic * PEER_NKEYS
    assert n % tb == 0 and ne % ec == 0
    hk = PEER_HEADS
    row_blk = pl.BlockSpec((hk, ic, tb), lambda i, e: (0, e, i))
    full_blk = pl.BlockSpec((hk, PEER_NKEYS, tb), lambda i, e: (0, 0, i))
    return pl.pallas_call(
        functools.partial(_peer_dense_kernel, ic=ic),
        grid=(n // tb, ne // ec),
        in_specs=[pl.BlockSpec((tb, D_MODEL), lambda i, e: (i, 0)),
                  pl.BlockSpec((tb, D_MODEL), lambda i, e: (i, 0)),
                  pl.BlockSpec((ec, D_MODEL), lambda i, e: (e, 0)),
                  pl.BlockSpec((D_MODEL, ec), lambda i, e: (0, e)),
                  row_blk, row_blk, full_blk, full_blk],
        out_specs=pl.BlockSpec((tb, D_MODEL), lambda i, e: (i, 0)),
        out_shape=jax.ShapeDtypeStruct((n, D_MODEL), jnp.float32),
        scratch_shapes=[pltpu.VMEM((D_MODEL, tb), jnp.float32), pltpu.VMEM((ec, tb), jnp.bfloat16)],
        compiler_params=pltpu.CompilerParams(dimension_semantics=("parallel", "arbitrary"),
                                             vmem_limit_bytes=_VMEM_LIMIT),
        name="peer_dense",
    )(x, hn_b, u_b, vt_b, cnt, e1, r2, e2)


def _peer_block(x, gain, wq, subkeys, u_tab, v_tab):
    wqt_b = wq.T.astype(jnp.bfloat16)
    sk_b = subkeys.reshape(2 * PEER_HEADS, PEER_NKEYS, PEER_HALF).astype(jnp.bfloat16)
    u_b = u_tab.astype(jnp.bfloat16)
    vt_b = v_tab.T.astype(jnp.bfloat16)
    hn_b, cnt, e1, r2, e2 = _peer_route(x, gain.reshape(1, -1), wqt_b, sk_b)
    return _peer_dense(x, hn_b, u_b, vt_b, cnt, e1, r2, e2)


def _in_weight(w_in):
    abc = w_in[:, :D_OFF]
    pad = jnp.zeros((w_in.shape[0], ABC_PCOLS - D_OFF), w_in.dtype)
    return jnp.concatenate([abc, pad, _rwkv_pad_cols(w_in[:, D_OFF:])], axis=1)


def _layer_pre_peer(x, lp, st, prompt):
    bsz, T, _ = x.shape
    G = GROUP_WIDTH
    n = bsz * T
    x2 = x.reshape(n, D_MODEL)
    proj = _mm(x2, lp['w_in_p'], gain=lp['norm_mix'], tn=256).reshape(bsz, T, ABC_PCOLS + RWKV_PCOLS)
    pos0 = 0 if prompt else PAST_LEN
    pos = pos0 + jnp.arange(T, dtype=jnp.int32)
    pa = proj[..., A_OFF:A_OFF + A_COLS]
    heads = lambda t: t.reshape(bsz, T, A_HEADS, HEAD_DIM)
    q = _rope(heads(pa[..., :G]), pos)
    k = _rope(heads(pa[..., G:2 * G]), pos)
    v = heads(pa[..., 2 * G:])
    if prompt:
        ya = _moba_prompt(q, k, v)
    else:
        ya = _moba_decode(q, k, v, st['k_cache'], st['v_cache'], st['page_table'])
    pb = proj[..., B_OFF:B_OFF + B_COLS]
    yb, lru_conv, lru_h = _rglru_mixer(pb[..., :LRU_WIDTH], pb[..., LRU_WIDTH:], st['lru_conv'], st['lru_h'],
                                       lp['lru_conv_w'], lp['lru_conv_b'], lp['lru_wa'], lp['lru_ba'],
                                       lp['lru_wx'], lp['lru_bx'], lp['lru_lambda'])
    pc = proj[..., C_OFF:C_OFF + C_COLS]
    yc, ssd_conv, ssd_s = _ssd_mixer(pc[..., :G], pc[..., G:G + SSD_CONV_CH], pc[..., G + SSD_CONV_CH:],
                                     st['ssd_conv'], st['ssd'], lp['ssd_conv_w'], lp['ssd_conv_b'],
                                     lp['ssd_dt_bias'], lp['ssd_a_log'], lp['ssd_d'], lp['ssd_norm'], prompt)
    pd = proj[..., ABC_PCOLS:]
    if prompt:
        n_steps, tc = 512, 512
    else:
        n_steps, tc = T, LANES
        pd = jnp.pad(pd, ((0, 0), (0, tc - T), (0, 0)))
    yd, rwkv_shift, rwkv_s = _rwkv_mixer(pd, st['rwkv_shift'], st['rwkv'],
                                         lp['rwkv_mu'], lp['rwkv_w0'], lp['rwkv_w_up'], lp['rwkv_a0'],
                                         lp['rwkv_a_up'], lp['rwkv_g_up'], lp['rwkv_k_k'], lp['rwkv_k_a'],
                                         lp['rwkv_r_k'], lp['rwkv_ln_w'], lp['rwkv_ln_b'], n_steps=n_steps, tc=tc)
    yd = yd[:, :T]
    y_mix = jnp.concatenate([ya, yb, yc, yd], axis=-1).reshape(n, 4 * G)
    x2 = _mm(y_mix, lp['w_out'], residual=x2)
    qx = _mm(x2, lp['x_wq'], gain=lp['norm_x']).reshape(bsz, T, D_MODEL)
    o = _cross_attn(qx, st['mem_k'], st['mem_v']).reshape(n, D_MODEL)
    x2 = _mm(o, lp['x_wo'], residual=x2)
    new = {'k': k, 'v': v, 'lru_h': lru_h, 'lru_conv': lru_conv, 'ssd': ssd_s, 'ssd_conv': ssd_conv,
           'rwkv': rwkv_s, 'rwkv_shift': rwkv_shift}
    return x2, new


def kernel(x_prompt, x_sample, mem_prompt, cache_moba_k, cache_moba_v, page_table, state_lru_h, state_lru_conv, state_ssd, state_ssd_conv, state_rwkv, state_rwkv_shift, cache_mem_k, cache_mem_v, norm_mix, w_in, w_out, lru_conv_w, lru_conv_b, lru_wa, lru_ba, lru_wx, lru_bx, lru_lambda, ssd_conv_w, ssd_conv_b, ssd_dt_bias, ssd_a_log, ssd_d, ssd_norm, rwkv_mu, rwkv_w0, rwkv_w_up, rwkv_a0, rwkv_a_up, rwkv_g_up, rwkv_k_k, rwkv_k_a, rwkv_r_k, rwkv_ln_w, rwkv_ln_b, norm_x, x_wq, x_wk, x_wv, x_wo, norm_ffn, peer_wq, peer_subkeys, peer_u, peer_v, final_norm):
    bp, tp, _ = x_prompt.shape
    bd, td, _ = x_sample.shape
    n_p, n_s = bp * tp, bd * td
    n_all = -(-(n_p + n_s) // PEER_TOK_PAD) * PEER_TOK_PAD
    f32 = jnp.float32
    yp, ys = x_prompt, x_sample
    names = ('k', 'v', 'lru_h', 'lru_conv', 'ssd', 'ssd_conv', 'rwkv', 'rwkv_shift')
    p_new = {n: [] for n in names + ('mem_k', 'mem_v')}
    s_new = {n: [] for n in names}
    for l in range(DEPTH):
        lp = {
            'norm_mix': norm_mix[l], 'w_in_p': _in_weight(w_in[l]), 'w_out': w_out[l],
            'lru_conv_w': lru_conv_w[l], 'lru_conv_b': lru_conv_b[l], 'lru_wa': lru_wa[l], 'lru_ba': lru_ba[l],
            'lru_wx': lru_wx[l], 'lru_bx': lru_bx[l], 'lru_lambda': lru_lambda[l],
            'ssd_conv_w': ssd_conv_w[l], 'ssd_conv_b': ssd_conv_b[l], 'ssd_dt_bias': ssd_dt_bias[l],
            'ssd_a_log': ssd_a_log[l], 'ssd_d': ssd_d[l], 'ssd_norm': ssd_norm[l],
            'rwkv_mu': rwkv_mu[l], 'rwkv_w0': rwkv_w0[l], 'rwkv_w_up': rwkv_w_up[l], 'rwkv_a0': rwkv_a0[l],
            'rwkv_a_up': rwkv_a_up[l], 'rwkv_g_up': rwkv_g_up[l], 'rwkv_k_k': rwkv_k_k[l], 'rwkv_k_a': rwkv_k_a[l],
            'rwkv_r_k': rwkv_r_k[l], 'rwkv_ln_w': rwkv_ln_w[l], 'rwkv_ln_b': rwkv_ln_b[l],
            'norm_x': norm_x[l], 'x_wq': x_wq[l], 'x_wo': x_wo[l],
        }
        mem2 = mem_prompt.reshape(bp * MEM_LEN, D_MODEL)
        mk = _mm(mem2, x_wk[l]).reshape(bp, MEM_LEN, X_HEADS, X_HEADDIM)
        mv = _mm(mem2, x_wv[l]).reshape(bp, MEM_LEN, X_HEADS, X_HEADDIM)
        st_p = {
            'lru_conv': jnp.zeros((bp, CONV_W - 1, LRU_WIDTH), f32),
            'lru_h': jnp.zeros((bp, LRU_WIDTH), f32),
            'ssd_conv': jnp.zeros((bp, CONV_W - 1, SSD_CONV_CH), f32),
            'ssd': jnp.zeros((bp, SSD_HEADS, SSD_HEADDIM, SSD_STATE), f32),
            'rwkv_shift': jnp.zeros((bp, RWKV_COLS), f32),
            'rwkv': jnp.zeros((bp, RWKV_HEADS, RWKV_HEADDIM, RWKV_HEADDIM), f32),
            'mem_k': mk, 'mem_v': mv,
        }
        xp2, npl = _layer_pre_peer(yp, lp, st_p, True)
        for n in names:
            p_new[n].append(npl[n])
        p_new['mem_k'].append(mk)
        p_new['mem_v'].append(mv)
        st_s = {
            'k_cache': cache_moba_k[l], 'v_cache': cache_moba_v[l], 'page_table': page_table,
            'lru_conv': state_lru_conv[l], 'lru_h': state_lru_h[l],
            'ssd_conv': state_ssd_conv[l], 'ssd': state_ssd[l],
            'rwkv_shift': state_rwkv_shift[l], 'rwkv': state_rwkv[l],
            'mem_k': cache_mem_k[l], 'mem_v': cache_mem_v[l],
        }
        xs2, nsl = _layer_pre_peer(ys, lp, st_s, False)
        for n in names:
            s_new[n].append(nsl[n])
        x_all = jnp.concatenate([xp2, xs2, jnp.zeros((n_all - n_p - n_s, D_MODEL), f32)], axis=0)
        x_all = _peer_block(x_all, norm_ffn[l], peer_wq[l], peer_subkeys[l], peer_u[l], peer_v[l])
        yp = x_all[:n_p].reshape(bp, tp, D_MODEL)
        ys = x_all[n_p:n_p + n_s].reshape(bd, td, D_MODEL)
    y_prompt = _rmsnorm(yp, final_norm)
    y_sample = _rmsnorm(ys, final_norm)
    return (y_prompt, y_sample,
            jnp.stack(p_new['k']), jnp.stack(p_new['v']), jnp.stack(p_new['lru_h']), jnp.stack(p_new['lru_conv']),
            jnp.stack(p_new['ssd']), jnp.stack(p_new['ssd_conv']), jnp.stack(p_new['rwkv']), jnp.stack(p_new['rwkv_shift']),
            jnp.stack(p_new['mem_k']), jnp.stack(p_new['mem_v']),
            jnp.stack(s_new['k']), jnp.stack(s_new['v']), jnp.stack(s_new['lru_h']), jnp.stack(s_new['lru_conv']),
            jnp.stack(s_new['ssd']), jnp.stack(s_new['ssd_conv']), jnp.stack(s_new['rwkv']), jnp.stack(s_new['rwkv_shift']))
```

```python
import functools

import jax
import jax.numpy as jnp
from jax import lax
from jax.experimental import pallas as pl
from jax.experimental.pallas import tpu as pltpu

D_MODEL = 1024
DEPTH = 2
PAST_LEN = 16384
PAGE_SIZE = 128
GROUP_WIDTH = 256
HEAD_DIM = 64
A_HEADS = GROUP_WIDTH // HEAD_DIM
MOBA_BLOCK = 256
MOBA_TOPK = 3
ROPE_THETA = 10000.0
LRU_WIDTH = GROUP_WIDTH
LRU_C = 8.0
CONV_W = 4
SSD_HEADS = 4
SSD_HEADDIM = GROUP_WIDTH // SSD_HEADS
SSD_GROUPS = 2
SSD_STATE = 64
SSD_CHUNK = 128
SSD_CONV_CH = GROUP_WIDTH + 2 * SSD_GROUPS * SSD_STATE
RWKV_HEADS = 4
RWKV_HEADDIM = GROUP_WIDTH // RWKV_HEADS
RWKV_W_RANK = 32
RWKV_A_RANK = 32
RWKV_G_RANK = 64
RWKV_COLS = 3 * GROUP_WIDTH + RWKV_W_RANK + RWKV_A_RANK + RWKV_G_RANK
RWKV_LN_EPS = 64e-5
D_OFF = 3 * GROUP_WIDTH + 2 * LRU_WIDTH + GROUP_WIDTH + SSD_CONV_CH + SSD_HEADS
MEM_LEN = 256
X_HEADS = 4
X_HEADDIM = D_MODEL // X_HEADS
PEER_HEADS = 8
PEER_NKEYS = 128
PEER_HALF = 128
PEER_TOPK = 16
NORM_EPS = 1e-6
NEG_INF = -1e30

LANES = 128
RWKV_PCOLS = 3 * GROUP_WIDTH + 3 * LANES
ABC_PCOLS = 2 * RWKV_PCOLS
COL_Q, COL_K, COL_V, COL_U, COL_GATE, COL_Z = 0, 1, 2, 3, 4, 5
COL_XBC = 3
COL_DT = 8
COL_RWKV = 2
PAGES_PER_BLOCK = MOBA_BLOCK // PAGE_SIZE
KMEAN_PAGES = 16
PEER_TOK_PAD = 512
SEQ_CHUNK = 512
DEC_ROWS = 128
_VMEM_LIMIT = 56 * 1024 * 1024
_HI = lax.Precision.HIGHEST


def _softplus(x):
    return jnp.maximum(x, 0.0) + jnp.log1p(jnp.exp(-jnp.abs(x)))


def _silu(x):
    return x * jax.nn.sigmoid(x)


def _gelu_tanh(x):
    return 0.5 * x * (1.0 + jnp.tanh(0.7978845608028654 * (x + 0.044715 * (x * x * x))))


def _mm_kernel(*refs, n_x, has_norm, has_res):
    x_refs = refs[:n_x]
    w_ref = refs[n_x]
    pos = n_x + 1
    g_ref = r_ref = None
    if has_norm:
        g_ref = refs[pos]
        pos += 1
    if has_res:
        r_ref = refs[pos]
        pos += 1
    o_ref = refs[pos]
    y = None
    off = 0
    for x_ref in x_refs:
        x = x_ref[...]
        kx = x.shape[1]
        if has_norm:
            x = x * lax.rsqrt(jnp.mean(x * x, axis=-1, keepdims=True) + NORM_EPS) * g_ref[...]
        t = jnp.dot(x.astype(jnp.bfloat16), w_ref[off:off + kx, :], preferred_element_type=jnp.float32)
        y = t if y is None else y + t
        off += kx
    if has_res:
        y = y + r_ref[...]
    o_ref[...] = y


def _mm(xs, w, gain=None, residual=None, tm=512):
    if not isinstance(xs, (list, tuple)):
        xs = [xs]
    assert gain is None or len(xs) == 1
    M = xs[0].shape[0]
    K, N = w.shape
    tm = min(tm, M)
    assert M % tm == 0 and sum(x.shape[1] for x in xs) == K
    args = list(xs) + [w.astype(jnp.bfloat16)]
    in_specs = [pl.BlockSpec((tm, x.shape[1]), lambda i: (i, 0)) for x in xs]
    in_specs.append(pl.BlockSpec((K, N), lambda i: (0, 0)))
    if gain is not None:
        args.append(gain.reshape(1, K).astype(jnp.float32))
        in_specs.append(pl.BlockSpec((1, K), lambda i: (0, 0)))
    if residual is not None:
        args.append(residual)
        in_specs.append(pl.BlockSpec((tm, N), lambda i: (i, 0)))
    return pl.pallas_call(
        functools.partial(_mm_kernel, n_x=len(xs), has_norm=gain is not None, has_res=residual is not None),
        grid=(M // tm,),
        in_specs=in_specs,
        out_specs=pl.BlockSpec((tm, N), lambda i: (i, 0)),
        out_shape=jax.ShapeDtypeStruct((M, N), jnp.float32),
        compiler_params=pltpu.CompilerParams(dimension_semantics=("parallel",), vmem_limit_bytes=_VMEM_LIMIT),
        name="mm",
    )(*args)


def _rmsnorm(x, g):
    return x * lax.rsqrt(jnp.mean(x * x, axis=-1, keepdims=True) + NORM_EPS) * g


def _rope_tables(pos):
    half = HEAD_DIM // 2
    freq = 1.0 / (ROPE_THETA ** (jnp.arange(half, dtype=jnp.float32) / half))
    ang = pos.astype(jnp.float32)[:, None] * freq[None, :]
    cos = jnp.cos(ang)
    sin = jnp.sin(ang)
    cos_t = jnp.tile(jnp.concatenate([cos, cos], axis=-1), (1, A_HEADS))
    sin_t = jnp.tile(jnp.concatenate([-sin, sin], axis=-1), (1, A_HEADS))
    return cos_t, sin_t


def _rope_apply(x, cos_t, sin_t):
    half = HEAD_DIM // 2
    lane = lax.broadcasted_iota(jnp.int32, x.shape, 1)
    first = (lane % HEAD_DIM) < half
    w = x.shape[1]
    swapped = jnp.where(first, pltpu.roll(x, w - half, axis=1), pltpu.roll(x, half, axis=1))
    return x * cos_t + swapped * sin_t


def _moba_prompt_kernel(q_ref, k_ref, v_ref, cos_ref, sin_ref, ya_ref, kr_ref, vo_ref, ot_scr):
    T = q_ref.shape[1]
    G = GROUP_WIDTH
    hd = HEAD_DIM
    blk = MOBA_BLOCK
    nb = T // blk
    cos_t = cos_ref[...]
    sin_t = sin_ref[...]
    q = _rope_apply(q_ref[0], cos_t, sin_t)
    k = _rope_apply(k_ref[0], cos_t, sin_t)
    v = v_ref[0]
    kr_ref[0] = k
    vo_ref[0] = v
    kmean = jnp.concatenate([jnp.sum(k[n * blk:(n + 1) * blk, :], axis=0, keepdims=True) for n in range(nb)],
                            axis=0) * (1.0 / blk)
    qt = q.T.astype(jnp.bfloat16)
    vt = v.T.astype(jnp.bfloat16)
    kb = k.astype(jnp.bfloat16)
    lane_g = lax.broadcasted_iota(jnp.int32, (nb, G), 1)
    rown = lax.broadcasted_iota(jnp.int32, (nb, T), 0)
    cur = lax.broadcasted_iota(jnp.int32, (nb, T), 1) // blk
    kpos = lax.broadcasted_iota(jnp.int32, (blk, blk), 0)
    qpos = lax.broadcasted_iota(jnp.int32, (blk, blk), 1)
    causal = kpos <= qpos
    scale = hd ** -0.5
    for h in range(A_HEADS):
        km_h = jnp.where(lane_g // hd == h, kmean, 0.0)
        gate = lax.dot_general(km_h, q, (((1,), (1,)), ((), ())), precision=_HI,
                               preferred_element_type=jnp.float32)
        gate = jnp.where(rown < cur, gate, NEG_INF)
        sel_rows = []
        for n in range(nb):
            gn = gate[n:n + 1, :]
            beats = (gate > gn) | ((gate == gn) & (rown < n))
            rank = jnp.sum(beats.astype(jnp.float32), axis=0, keepdims=True)
            sel_rows.append(rank < MOBA_TOPK)
        k_h = kb[:, h * hd:(h + 1) * hd]
        qt_h = qt[h * hd:(h + 1) * hd, :]
        vt_h = vt[h * hd:(h + 1) * hd, :]
        for qi in range(nb):
            qs = slice(qi * blk, (qi + 1) * blk)
            q_blk = qt_h[:, qs]
            s = jnp.dot(k_h[qs, :], q_blk, preferred_element_type=jnp.float32) * scale
            s = jnp.where(causal, s, NEG_INF)
            m = jnp.max(s, axis=0, keepdims=True)
            p = jnp.exp(s - m)
            l = jnp.sum(p, axis=0, keepdims=True)
            acc = jnp.dot(vt_h[:, qs], p.astype(jnp.bfloat16), preferred_element_type=jnp.float32)
            for n in range(qi):
                ks = slice(n * blk, (n + 1) * blk)
                s = jnp.dot(k_h[ks, :], q_blk, preferred_element_type=jnp.float32) * scale
                s = jnp.where(sel_rows[n][:, qs], s, NEG_INF)
                m_new = jnp.maximum(m, jnp.max(s, axis=0, keepdims=True))
                alpha = jnp.exp(m - m_new)
                p = jnp.exp(s - m_new)
                l = l * alpha + jnp.sum(p, axis=0, keepdims=True)
                acc = acc * alpha + jnp.dot(vt_h[:, ks], p.astype(jnp.bfloat16), preferred_element_type=jnp.float32)
                m = m_new
            ot_scr[h * hd:(h + 1) * hd, qs] = acc / l
    ya_ref[0] = ot_scr[...].T


def _moba_prompt(proj3, pos):
    B, T, _ = proj3.shape
    G = GROUP_WIDTH
    cos_t, sin_t = _rope_tables(pos)
    col = lambda j: pl.BlockSpec((1, T, G), lambda b: (b, 0, j))
    tab = pl.BlockSpec((T, G), lambda b: (0, 0))
    out = pl.BlockSpec((1, T, G), lambda b: (b, 0, 0))
    shp = jax.ShapeDtypeStruct((B, T, G), jnp.float32)
    return pl.pallas_call(
        _moba_prompt_kernel,
        grid=(B,),
        in_specs=[col(COL_Q), col(COL_K), col(COL_V), tab, tab],
        out_specs=[out, out, out],
        out_shape=[shp, shp, shp],
        scratch_shapes=[pltpu.VMEM((G, T), jnp.float32)],
        compiler_params=pltpu.CompilerParams(dimension_semantics=("parallel",), vmem_limit_bytes=_VMEM_LIMIT),
        name="moba_prompt",
    )(proj3, proj3, proj3, cos_t, sin_t)


def _kmean_kernel(pt_ref, *refs):
    pages, o_ref = refs[:-1], refs[-1]
    rows = []
    for blk in range(len(pages) // PAGES_PER_BLOCK):
        s = None
        for i in range(PAGES_PER_BLOCK):
            part = jnp.sum(pages[blk * PAGES_PER_BLOCK + i][0], axis=0, keepdims=True)
            s = part if s is None else s + part
        rows.append(s)
    o_ref[0] = jnp.concatenate(rows, axis=0) * (1.0 / MOBA_BLOCK)


def _moba_block_means(cache_k, page_table):
    B, n_pages = page_table.shape
    G = GROUP_WIDTH
    assert n_pages % KMEAN_PAGES == 0
    steps = n_pages // KMEAN_PAGES
    nblk = KMEAN_PAGES // PAGES_PER_BLOCK

    def page_spec(i):
        return pl.BlockSpec((1, PAGE_SIZE, G), lambda b, j, pt: (pt[b * n_pages + j * KMEAN_PAGES + i], 0, 0))

    return pl.pallas_call(
        _kmean_kernel,
        grid_spec=pltpu.PrefetchScalarGridSpec(
            num_scalar_prefetch=1,
            grid=(B, steps),
            in_specs=[page_spec(i) for i in range(KMEAN_PAGES)],
            out_specs=pl.BlockSpec((1, nblk, G), lambda b, j, pt: (b, j, 0)),
        ),
        out_shape=jax.ShapeDtypeStruct((B, n_pages // PAGES_PER_BLOCK, G), jnp.float32),
        compiler_params=pltpu.CompilerParams(dimension_semantics=("parallel", "arbitrary"),
                                             vmem_limit_bytes=_VMEM_LIMIT),
        name="moba_block_means",
    )(page_table.reshape(-1), *([cache_k] * KMEAN_PAGES))


def _moba_select_kernel(q_ref, k_ref, cos_ref, sin_ref, km_ref, seg_ref, qr_ref, kr_ref, top_ref):
    q = _rope_apply(q_ref[0], cos_ref[...], sin_ref[...])
    k = _rope_apply(k_ref[0], cos_ref[...], sin_ref[...])
    qr_ref[0] = q
    kr_ref[0] = k
    km = km_ref[0]
    nb = km.shape[0]
    gate = jnp.dot(km * q[0:1, :], seg_ref[...], precision=_HI, preferred_element_type=jnp.float32)
    rowi = lax.broadcasted_iota(jnp.int32, gate.shape, 0)
    picks = []
    for _ in range(MOBA_TOPK):
        m = jnp.max(gate, axis=0, keepdims=True)
        idx = jnp.min(jnp.where(gate == m, rowi, nb), axis=0, keepdims=True)
        picks.append(idx)
        gate = jnp.where(rowi == idx, NEG_INF, gate)
    picks.append(jnp.zeros((8 - MOBA_TOPK, gate.shape[1]), jnp.int32))
    top_ref[0] = jnp.concatenate(picks, axis=0)


def _moba_select(q8, k8, pos, kmean):
    B = q8.shape[0]
    G = GROUP_WIDTH
    nb = kmean.shape[1]
    cos_t, sin_t = _rope_tables(pos)
    seg = (jnp.arange(G)[:, None] // HEAD_DIM == jnp.arange(LANES)[None, :]).astype(jnp.float32)
    row = pl.BlockSpec((1, 8, G), lambda b: (b, 0, 0))
    tab = pl.BlockSpec((1, G), lambda b: (0, 0))
    return pl.pallas_call(
        _moba_select_kernel,
        grid=(B,),
        in_specs=[row, row, tab, tab, pl.BlockSpec((1, nb, G), lambda b: (b, 0, 0)),
                  pl.BlockSpec((G, LANES), lambda b: (0, 0))],
        out_specs=[row, row, pl.BlockSpec((1, 8, LANES), lambda b: (b, 0, 0))],
        out_shape=[jax.ShapeDtypeStruct((B, 8, G), jnp.float32), jax.ShapeDtypeStruct((B, 8, G), jnp.float32),
                   jax.ShapeDtypeStruct((B, 8, LANES), jnp.int32)],
        compiler_params=pltpu.CompilerParams(dimension_semantics=("parallel",)),
        name="moba_select",
    )(q8, k8, cos_t, sin_t, kmean, seg)


def _moba_decode_attn_kernel(pp_ref, q_ref, kn_ref, vn_ref, *refs):
    npg = MOBA_TOPK * PAGES_PER_BLOCK
    kp, vp, o_ref = refs[:npg], refs[npg:2 * npg], refs[2 * npg]
    h = pl.program_id(1)
    lane = lax.broadcasted_iota(jnp.int32, (8, GROUP_WIDTH), 1)
    mine = lane // HEAD_DIM == h
    bf = lambda a: a.astype(jnp.bfloat16)
    qh = bf(jnp.where(mine, q_ref[0], 0.0))
    scale = HEAD_DIM ** -0.5
    s_pages = [lax.dot_general(qh, bf(kp[i][0]), (((1,), (1,)), ((), ())), preferred_element_type=jnp.float32) * scale
               for i in range(npg)]
    s_own = jnp.sum(qh.astype(jnp.float32) * bf(kn_ref[0]).astype(jnp.float32), axis=-1, keepdims=True) * scale
    m = s_own
    for s in s_pages:
        m = jnp.maximum(m, jnp.max(s, axis=-1, keepdims=True))
    p_own = jnp.exp(s_own - m)
    l = p_own
    acc = p_own * bf(vn_ref[0]).astype(jnp.float32)
    for i in range(npg):
        p = jnp.exp(s_pages[i] - m)
        l = l + jnp.sum(p, axis=-1, keepdims=True)
        acc = acc + jnp.dot(bf(p), bf(vp[i][0]), preferred_element_type=jnp.float32)

    @pl.when(h == 0)
    def _():
        o_ref[...] = jnp.zeros_like(o_ref)

    o_ref[0] += jnp.where(mine, acc / l, 0.0)


def _moba_decode_attn(q8, k8, v8, cache_k, cache_v, phys):
    B = q8.shape[0]
    G = GROUP_WIDTH
    npg = MOBA_TOPK * PAGES_PER_BLOCK
    row = pl.BlockSpec((1, 8, G), lambda b, h, pp: (b, 0, 0))

    def page_spec(i):
        return pl.BlockSpec((1, PAGE_SIZE, G), lambda b, h, pp: (pp[(b * A_HEADS + h) * npg + i], 0, 0))

    return pl.pallas_call(
        _moba_decode_attn_kernel,
        grid_spec=pltpu.PrefetchScalarGridSpec(
            num_scalar_prefetch=1,
            grid=(B, A_HEADS),
            in_specs=[row, row, row] + [page_spec(i) for i in range(npg)] * 2,
            out_specs=row,
        ),
        out_shape=jax.ShapeDtypeStruct((B, 8, G), jnp.float32),
        compiler_params=pltpu.CompilerParams(dimension_semantics=("parallel", "arbitrary")),
        name="moba_decode_attn",
    )(phys.reshape(-1), q8, k8, v8, *([cache_k] * npg), *([cache_v] * npg))


def _moba_decode(q, k, v, cache_k, cache_v, page_table, pos0):
    B = q.shape[0]
    G = GROUP_WIDTH
    n_phys = cache_k.shape[0]
    ck = cache_k.reshape(n_phys, PAGE_SIZE, G)
    cv = cache_v.reshape(n_phys, PAGE_SIZE, G)
    pad8 = lambda a: jnp.pad(a[:, None, :], ((0, 0), (0, 7), (0, 0)))
    kmean = _moba_block_means(ck, page_table)
    q8, k8, top = _moba_select(pad8(q), pad8(k), jnp.full((1,), pos0, jnp.int32), kmean)
    blocks = jnp.swapaxes(top[:, :MOBA_TOPK, :A_HEADS], 1, 2)
    pages = (blocks[..., None] * PAGES_PER_BLOCK + jnp.arange(PAGES_PER_BLOCK)).reshape(B, -1)
    phys = jnp.take_along_axis(page_table, pages, axis=1)
    out = _moba_decode_attn(q8, k8, pad8(v), ck, cv, phys)
    return out[:, 0, :], k8[:, 0, :]


def _causal_conv_tile(xs, prev3, w_ref, b_ref):
    row = lax.broadcasted_iota(jnp.int32, xs.shape, 0)
    y = b_ref[...] + xs * w_ref[CONV_W - 1:CONV_W, :]
    for k in range(1, CONV_W):
        sh = pltpu.roll(xs, k, axis=0)
        for r in range(k):
            src = r + (CONV_W - 1) - k
            sh = jnp.where(row == r, prev3[src:src + 1, :], sh)
        y = y + sh * w_ref[CONV_W - 1 - k:CONV_W - k, :]
    return y


def _conv_tail(xs, prev3, n_valid):
    rows = []
    for i in range(CONV_W - 1):
        idx = n_valid - (CONV_W - 1) + i
        rows.append(xs[idx:idx + 1, :] if idx >= 0 else prev3[idx + CONV_W - 1:idx + CONV_W, :])
    return rows


def _lru_kernel(u_ref, gate_ref, cb0_ref, h0_ref, cw_ref, cbias_ref, wa_ref, wx_ref, vec_ref,
                y_ref, cbo_ref, ho_ref, conv_scr, h_scr, a_scr, b_scr, hs_scr, *, n_valid):
    c = pl.program_id(1)
    tc = u_ref.shape[1]

    @pl.when(c == 0)
    def _():
        conv_scr[...] = cb0_ref[0]
        h_scr[...] = h0_ref[0]

    us = u_ref[0]
    prev3 = conv_scr[...]
    xc = _causal_conv_tile(us, prev3, cw_ref, cbias_ref)
    tail = _conv_tail(us, prev3, n_valid)
    for i in range(CONV_W - 1):
        conv_scr[i:i + 1, :] = tail[i]
    xb = xc.astype(jnp.bfloat16)
    r = jax.nn.sigmoid(jnp.dot(xb, wa_ref[...], preferred_element_type=jnp.float32) + vec_ref[0:1, :])
    i_g = jax.nn.sigmoid(jnp.dot(xb, wx_ref[...], preferred_element_type=jnp.float32) + vec_ref[1:2, :])
    log_a = -LRU_C * r * _softplus(-vec_ref[2:3, :])
    a_scr[...] = jnp.exp(log_a)
    b_scr[...] = jnp.sqrt(-jnp.tanh(log_a) * (jnp.exp(2.0 * log_a) + 1.0)) * (i_g * xc)

    row8 = lax.broadcasted_iota(jnp.int32, (8, LRU_WIDTH), 0)

    def step8(i, h):
        t0 = pl.multiple_of(i * 8, 8)
        a8 = a_scr[pl.ds(t0, 8), :]
        b8 = b_scr[pl.ds(t0, 8), :]
        out = jnp.zeros((8, LRU_WIDTH), jnp.float32)
        for j in range(min(8, n_valid)):
            h = a8[j:j + 1, :] * h + b8[j:j + 1, :]
            out = jnp.where(row8 == j, h, out)
        hs_scr[pl.ds(t0, 8), :] = out
        return h

    if n_valid < tc:
        hs_scr[...] = jnp.zeros_like(hs_scr)
    h = lax.fori_loop(0, -(-n_valid // 8), step8, h_scr[...])
    h_scr[...] = h
    y_ref[0] = hs_scr[...] * _gelu_tanh(gate_ref[0])

    @pl.when(c == pl.num_programs(1) - 1)
    def _():
        cbo_ref[0] = conv_scr[...]
        ho_ref[0] = h_scr[...]


def _block_diag(w):
    n, d, e = w.shape
    eye = jnp.eye(n, dtype=w.dtype)
    return (eye[:, None, :, None] * w[:, :, None, :]).reshape(n * d, n * e)


def _lru_mixer(proj3, conv0, h0, conv_w, conv_b, wa, ba, wx, bx, lam, *, n_valid, tc):
    B, Tp, _ = proj3.shape
    G = GROUP_WIDTH
    f32 = jnp.float32
    nchunk = Tp // tc
    assert Tp % tc == 0 and (n_valid == tc or nchunk == 1) and (n_valid % 8 == 0 or n_valid < 8)
    cb0 = jnp.pad(conv0, ((0, 0), (0, 8 - (CONV_W - 1)), (0, 0)))
    vec = jnp.zeros((8, G), f32).at[0].set(ba).at[1].set(bx).at[2].set(lam)
    full2 = lambda shp: pl.BlockSpec(shp, lambda b, c: (0, 0))
    y, cbo, ho = pl.pallas_call(
        functools.partial(_lru_kernel, n_valid=n_valid),
        grid=(B, nchunk),
        in_specs=[pl.BlockSpec((1, tc, G), lambda b, c: (b, c, COL_U)),
                  pl.BlockSpec((1, tc, G), lambda b, c: (b, c, COL_GATE)),
                  pl.BlockSpec((1, 8, G), lambda b, c: (b, 0, 0)),
                  pl.BlockSpec((1, 1, G), lambda b, c: (b, 0, 0)),
                  full2((CONV_W, G)), full2((1, G)), full2((G, G)), full2((G, G)), full2((8, G))],
        out_specs=[pl.BlockSpec((1, tc, G), lambda b, c: (b, c, 0)),
                   pl.BlockSpec((1, 8, G), lambda b, c: (b, 0, 0)),
                   pl.BlockSpec((1, 1, G), lambda b, c: (b, 0, 0))],
        out_shape=[jax.ShapeDtypeStruct((B, Tp, G), f32), jax.ShapeDtypeStruct((B, 8, G), f32),
                   jax.ShapeDtypeStruct((B, 1, G), f32)],
        scratch_shapes=[pltpu.VMEM((8, G), f32), pltpu.VMEM((1, G), f32),
                        pltpu.VMEM((tc, G), f32), pltpu.VMEM((tc, G), f32), pltpu.VMEM((tc, G), f32)],
        compiler_params=pltpu.CompilerParams(dimension_semantics=("parallel", "arbitrary"),
                                             vmem_limit_bytes=_VMEM_LIMIT),
        name="lru_mixer",
    )(proj3, proj3, cb0, h0[:, None, :], conv_w, conv_b[None, :],
      _block_diag(wa).astype(jnp.bfloat16), _block_diag(wx).astype(jnp.bfloat16), vec)
    return y, cbo[:, :CONV_W - 1, :], ho[:, 0, :]


def _ssd_kernel(z_ref, xbc_ref, dt_ref, cb0_ref, s0_ref, cw_ref, cbias_ref, prow_ref, pcol_ref, dvec_ref, nw_ref,
                y_ref, cbo_ref, so_ref, conv_scr, st_scr, y_scr, *, n_valid):
    c = pl.program_id(1)
    tc = xbc_ref.shape[1]
    Q = SSD_CHUNK
    G = GROUP_WIDTH
    P = SSD_HEADDIM
    N = SSD_STATE
    GN = SSD_GROUPS * SSD_STATE

    @pl.when(c == 0)
    def _():
        conv_scr[...] = cb0_ref[0]
        st_scr[...] = s0_ref[0]

    xs = xbc_ref[0]
    prev3 = conv_scr[...]
    xbc_f = _silu(_causal_conv_tile(xs, prev3, cw_ref, cbias_ref))
    tail = _conv_tail(xs, prev3, n_valid)
    for i in range(CONV_W - 1):
        conv_scr[i:i + 1, :] = tail[i]

    ii = lax.broadcasted_iota(jnp.int32, (Q, Q), 0)
    jj = lax.broadcasted_iota(jnp.int32, (Q, Q), 1)
    lower = ii >= jj
    tri_l = lower.astype(jnp.float32)
    tri_u = (ii <= jj).astype(jnp.float32)
    bias_row = prow_ref[0:1, :]
    alog_row = prow_ref[1:2, :]
    bias_col = pcol_ref[:, 0:1]
    alog_col = pcol_ref[:, 1:2]
    rowq = lax.broadcasted_iota(jnp.int32, (Q, LANES), 0)
    laneq = lax.broadcasted_iota(jnp.int32, (8, Q), 1)
    bf = lambda a: a.astype(jnp.bfloat16)
    for j in range(tc // Q):
        rows = slice(j * Q, (j + 1) * Q)
        xq = xbc_f[rows, :]
        xt = xq.T
        dcol = dt_ref[0, rows, 0:LANES]
        drow = dcol.T[0:8, :]
        dt_col = _softplus(dcol + bias_row)
        dt_row = _softplus(drow + bias_col)
        if n_valid < tc:
            dt_col = jnp.where(rowq + j * Q < n_valid, dt_col, 0.0)
            dt_row = jnp.where(laneq + j * Q < n_valid, dt_row, 0.0)
        da_col = dt_col * (-jnp.exp(alog_row))
        da_row = dt_row * (-jnp.exp(alog_col))
        cs_col = jnp.dot(tri_l, da_col, precision=_HI, preferred_element_type=jnp.float32)
        cs_row = jnp.dot(da_row, tri_u, precision=_HI, preferred_element_type=jnp.float32)
        for g in range(SSD_GROUPS):
            bg = xq[:, G + g * N:G + (g + 1) * N]
            cg = xq[:, G + GN + g * N:G + GN + (g + 1) * N]
            bgt = xt[G + g * N:G + (g + 1) * N, :]
            cb = lax.dot_general(bf(cg), bf(bg), (((1,), (1,)), ((), ())), preferred_element_type=jnp.float32)
            for h in range(g * (SSD_HEADS // SSD_GROUPS), (g + 1) * (SSD_HEADS // SSD_GROUPS)):
                ci = cs_col[:, h:h + 1]
                cj = cs_row[h:h + 1, :]
                c_last = cs_row[h:h + 1, Q - 1:Q]
                lm = jnp.exp(jnp.where(lower, ci - cj, NEG_INF))
                x_h = xq[:, h * P:(h + 1) * P]
                xdt = x_h * dt_col[:, h:h + 1]
                st = st_scr[h]
                y_h = jnp.dot(bf(cb * lm), bf(xdt), preferred_element_type=jnp.float32)
                y_h = y_h + jnp.dot(bf(cg * jnp.exp(ci)), bf(st), preferred_element_type=jnp.float32)
                y_h = y_h + dvec_ref[:, h * P:(h + 1) * P] * x_h
                y_scr[rows, h * P:(h + 1) * P] = y_h
                st_scr[h] = st * jnp.exp(c_last) + jnp.dot(bf(bgt), bf(xdt * jnp.exp(c_last - ci)),
                                                            preferred_element_type=jnp.float32)
    y = y_scr[...] * _silu(z_ref[0])
    y_ref[0] = y * lax.rsqrt(jnp.mean(y * y, axis=-1, keepdims=True) + NORM_EPS) * nw_ref[...]

    @pl.when(c == pl.num_programs(1) - 1)
    def _():
        cbo_ref[0] = conv_scr[...]
        so_ref[0] = st_scr[...]


def _ssd_mixer(proj3, conv0, s0, conv_w, conv_b, dt_bias, a_log, d_skip, norm_w, *, n_valid, tc):
    B, Tp, _ = proj3.shape
    G = GROUP_WIDTH
    H, P, N = SSD_HEADS, SSD_HEADDIM, SSD_STATE
    C = SSD_CONV_CH
    f32 = jnp.float32
    nchunk = Tp // tc
    assert Tp % tc == 0 and tc % SSD_CHUNK == 0 and (n_valid == tc or nchunk == 1)
    cb0 = jnp.pad(conv0, ((0, 0), (0, 8 - (CONV_W - 1)), (0, 0)))
    s0t = jnp.swapaxes(s0, -1, -2)
    prow = jnp.zeros((8, LANES), f32).at[0, :H].set(dt_bias).at[1, :H].set(a_log)
    pcol = jnp.zeros((8, LANES), f32).at[:H, 0].set(dt_bias).at[:H, 1].set(a_log)
    dvec = jnp.repeat(d_skip, P)[None, :]
    full2 = lambda shp: pl.BlockSpec(shp, lambda b, c: (0, 0))
    y, cbo, so = pl.pallas_call(
        functools.partial(_ssd_kernel, n_valid=n_valid),
        grid=(B, nchunk),
        in_specs=[pl.BlockSpec((1, tc, G), lambda b, c: (b, c, COL_Z)),
                  pl.BlockSpec((1, tc, C), lambda b, c: (b, c, COL_XBC)),
                  pl.BlockSpec((1, tc, G), lambda b, c: (b, c, COL_DT)),
                  pl.BlockSpec((1, 8, C), lambda b, c: (b, 0, 0)),
                  pl.BlockSpec((1, H, N, P), lambda b, c: (b, 0, 0, 0)),
                  full2((CONV_W, C)), full2((1, C)), full2((8, LANES)), full2((8, LANES)), full2((1, G)),
                  full2((1, G))],
        out_specs=[pl.BlockSpec((1, tc, G), lambda b, c: (b, c, 0)),
                   pl.BlockSpec((1, 8, C), lambda b, c: (b, 0, 0)),
                   pl.BlockSpec((1, H, N, P), lambda b, c: (b, 0, 0, 0))],
        out_shape=[jax.ShapeDtypeStruct((B, Tp, G), f32), jax.ShapeDtypeStruct((B, 8, C), f32),
                   jax.ShapeDtypeStruct((B, H, N, P), f32)],
        scratch_shapes=[pltpu.VMEM((8, C), f32), pltpu.VMEM((H, N, P), f32), pltpu.VMEM((tc, G), f32)],
        compiler_params=pltpu.CompilerParams(dimension_semantics=("parallel", "arbitrary"),
                                             vmem_limit_bytes=_VMEM_LIMIT),
        name="ssd_mixer",
    )(proj3, proj3, proj3, cb0, s0t, conv_w, conv_b[None, :], prow, pcol, dvec, norm_w[None, :])
    return y, cbo[:, :CONV_W - 1, :], jnp.swapaxes(so, -1, -2)


def _rwkv_pad_cols(a):
    G = GROUP_WIDTH
    z = lambda w: jnp.zeros(a.shape[:-1] + (w,), a.dtype)
    o = 3 * G
    wd = a[..., o:o + RWKV_W_RANK]
    ad = a[..., o + RWKV_W_RANK:o + RWKV_W_RANK + RWKV_A_RANK]
    gd = a[..., o + RWKV_W_RANK + RWKV_A_RANK:]
    return jnp.concatenate([a[..., :o], wd, z(LANES - RWKV_W_RANK), ad, z(LANES - RWKV_A_RANK),
                            gd, z(LANES - RWKV_G_RANK)], axis=-1)


def _rwkv_unpad_cols(a):
    o = 3 * GROUP_WIDTH
    return jnp.concatenate([a[..., :o], a[..., o:o + RWKV_W_RANK], a[..., o + LANES:o + LANES + RWKV_A_RANK],
                            a[..., o + 2 * LANES:o + 2 * LANES + RWKV_G_RANK]], axis=-1)


def _rwkv_kernel(pd_ref, sh0_ref, s0_ref, mu_ref, vec_ref, wup_ref, aup_ref, gup_ref, seg_ref,
                 y_ref, sout_ref, shout_ref,
                 prev_scr, s_scr, r_scr, k_scr, d_scr, kk_scr, bb_scr, g_scr, bonus_scr, vt_scr, yt_scr, *, n_steps):
    c = pl.program_id(1)
    tc = pd_ref.shape[1]
    G = GROUP_WIDTH
    hd = RWKV_HEADDIM

    @pl.when(c == 0)
    def _():
        prev_scr[...] = sh0_ref[...]
        s_scr[...] = s0_ref[...]

    nb = pd_ref.shape[0]
    w0 = vec_ref[0:1, :]
    a0 = vec_ref[1:2, :]
    k_k = vec_ref[2:3, :]
    k_a = vec_ref[3:4, :]
    ln_w = vec_ref[4:5, :]
    ln_b = vec_ref[5:6, :]
    r_k = vec_ref[6:7, :]
    seg = seg_ref[...]
    hdot = lambda x, y: jnp.dot(x, y, precision=_HI, preferred_element_type=jnp.float32)
    for ib in range(nb):
        cur = pd_ref[ib]
        row = lax.broadcasted_iota(jnp.int32, cur.shape, 0)
        prev = jnp.where(row == 0, prev_scr[ib], pltpu.roll(cur, 1, axis=0))
        prev_scr[ib] = cur[n_steps - 1:n_steps, :] if n_steps < tc else cur[tc - 1:tc, :]
        m = cur + (prev - cur) * mu_ref[...]
        r = m[:, 0:G]
        k = m[:, G:2 * G]
        v = m[:, 2 * G:3 * G]
        wd = m[:, 3 * G:3 * G + LANES]
        ad = m[:, 3 * G + LANES:3 * G + 2 * LANES]
        gd = m[:, 3 * G + 2 * LANES:3 * G + 3 * LANES]
        w = -_softplus(-(w0 + hdot(jnp.tanh(wd), wup_ref[...]))) - 0.5
        a = jax.nn.sigmoid(a0 + hdot(ad, aup_ref[...]))
        kkr = k * k_k
        kk = kkr / jnp.maximum(jnp.sqrt(hdot(kkr * kkr, seg)), 1e-12)
        k2 = k * (1.0 + (a - 1.0) * k_a)
        g_scr[ib] = hdot(jax.nn.sigmoid(gd), gup_ref[...])
        bonus_scr[ib] = hdot(r * k2 * r_k, seg) * v
        r_scr[ib] = r
        k_scr[ib] = k2
        d_scr[ib] = jnp.exp(-jnp.exp(w))
        kk_scr[ib] = -kk
        bb_scr[ib] = kk * a
        vt_scr[ib] = v.T
    yt_scr[...] = jnp.zeros_like(yt_scr)

    lane = lax.broadcasted_iota(jnp.int32, (hd, LANES), 1)
    lo_half = lane < hd

    def half_sums(p):
        lo = jnp.sum(jnp.where(lo_half, p, 0.0), axis=1, keepdims=True)
        hi = jnp.sum(jnp.where(lo_half, 0.0, p), axis=1, keepdims=True)
        return lo, hi

    npair = RWKV_HEADS // 2

    def step8(i, states):
        t0 = pl.multiple_of(i * 8, 8)
        blk = pl.multiple_of((t0 // LANES) * LANES, LANES)
        rows = pl.ds(t0, 8)
        states = list(states)
        for ib in range(nb):
            for p in range(npair):
                cols = slice(p * LANES, (p + 1) * LANES)
                kk8, d8, bb8, k8, r8 = (ref[ib, rows, cols] for ref in (kk_scr, d_scr, bb_scr, k_scr, r_scr))
                tile = lambda ref, hh: ref[ib, pl.ds(p * LANES + hh * hd, hd), pl.ds(blk, LANES)]
                vt0, vt1, y0, y1 = tile(vt_scr, 0), tile(vt_scr, 1), tile(yt_scr, 0), tile(yt_scr, 1)
                s = states[ib * npair + p]
                for j in range(min(8, n_steps)):
                    hit = lane == (t0 + j - blk)
                    lo, hi = half_sums(s * kk8[j:j + 1, :])
                    sa = jnp.where(lo_half, lo, hi)
                    vc0 = jnp.sum(jnp.where(hit, vt0, 0.0), axis=1, keepdims=True)
                    vc1 = jnp.sum(jnp.where(hit, vt1, 0.0), axis=1, keepdims=True)
                    vcol = jnp.where(lo_half, vc0, vc1)
                    s = s * d8[j:j + 1, :] + sa * bb8[j:j + 1, :] + vcol * k8[j:j + 1, :]
                    ylo, yhi = half_sums(s * r8[j:j + 1, :])
                    y0 = jnp.where(hit, ylo, y0)
                    y1 = jnp.where(hit, yhi, y1)
                states[ib * npair + p] = s
                yt_scr[ib, pl.ds(p * LANES, hd), pl.ds(blk, LANES)] = y0
                yt_scr[ib, pl.ds(p * LANES + hd, hd), pl.ds(blk, LANES)] = y1
        return tuple(states)

    init = tuple(s_scr[ib, p] for ib in range(nb) for p in range(npair))
    states = lax.fori_loop(0, -(-n_steps // 8), step8, init)
    for ib in range(nb):
        for p in range(npair):
            s_scr[ib, p] = states[ib * npair + p]

    inv = 1.0 / hd
    for ib in range(nb):
        y = yt_scr[ib].T
        mean = hdot(y, seg) * inv
        yc = y - mean
        var = hdot(yc * yc, seg) * inv
        yn = yc * lax.rsqrt(var + RWKV_LN_EPS) * ln_w + ln_b
        y_ref[ib] = (yn + bonus_scr[ib]) * g_scr[ib]

    @pl.when(c == pl.num_programs(1) - 1)
    def _():
        sout_ref[...] = s_scr[...]
        shout_ref[...] = prev_scr[...]


def _rwkv_mixer(proj3, shift0, s0, mu, w0, w_up, a0, a_up, g_up, k_k, k_a, r_k, ln_w, ln_b, *, n_steps, tc, nb=1):
    B, Tp, _ = proj3.shape
    G = GROUP_WIDTH
    H, hd = RWKV_HEADS, RWKV_HEADDIM
    nchunk = Tp // tc
    assert Tp % tc == 0 and (n_steps == tc or nchunk == 1) and (n_steps % 8 == 0 or n_steps < 8)
    assert B % nb == 0
    f32 = jnp.float32
    sh0 = _rwkv_pad_cols(shift0)[:, None, :]
    s0p = s0.reshape(B, H // 2, 2, hd, hd).transpose(0, 1, 3, 2, 4).reshape(B, H // 2, hd, 2 * hd)
    mu_p = _rwkv_pad_cols(mu[None, :])
    vec = jnp.stack([w0, a0, k_k, k_a, ln_w, ln_b, r_k.reshape(G), jnp.zeros((G,), f32)])
    padk = lambda wgt: jnp.pad(wgt, ((0, LANES - wgt.shape[0]), (0, 0)))
    hid = jnp.arange(G) // hd
    seg = (hid[:, None] == hid[None, :]).astype(f32)
    full2 = lambda shp: pl.BlockSpec(shp, lambda b, c: (0, 0))
    tscr = lambda: pltpu.VMEM((nb, tc, G), f32)
    y, s_out, sh_out = pl.pallas_call(
        functools.partial(_rwkv_kernel, n_steps=n_steps),
        grid=(B // nb, nchunk),
        in_specs=[pl.BlockSpec((nb, tc, RWKV_PCOLS), lambda b, c: (b, c, COL_RWKV)),
                  pl.BlockSpec((nb, 1, RWKV_PCOLS), lambda b, c: (b, 0, 0)),
                  pl.BlockSpec((nb, H // 2, hd, 2 * hd), lambda b, c: (b, 0, 0, 0)),
                  full2((1, RWKV_PCOLS)), full2((8, G)), full2((LANES, G)), full2((LANES, G)), full2((LANES, G)),
                  full2((G, G))],
        out_specs=[pl.BlockSpec((nb, tc, G), lambda b, c: (b, c, 0)),
                   pl.BlockSpec((nb, H // 2, hd, 2 * hd), lambda b, c: (b, 0, 0, 0)),
                   pl.BlockSpec((nb, 1, RWKV_PCOLS), lambda b, c: (b, 0, 0))],
        out_shape=[jax.ShapeDtypeStruct((B, Tp, G), f32),
                   jax.ShapeDtypeStruct((B, H // 2, hd, 2 * hd), f32),
                   jax.ShapeDtypeStruct((B, 1, RWKV_PCOLS), f32)],
        scratch_shapes=[pltpu.VMEM((nb, 1, RWKV_PCOLS), f32), pltpu.VMEM((nb, H // 2, hd, 2 * hd), f32),
                        tscr(), tscr(), tscr(), tscr(), tscr(), tscr(), tscr(),
                        pltpu.VMEM((nb, G, tc), f32), pltpu.VMEM((nb, G, tc), f32)],
        compiler_params=pltpu.CompilerParams(dimension_semantics=("parallel", "arbitrary"),
                                             vmem_limit_bytes=_VMEM_LIMIT),
        name="rwkv_mixer",
    )(proj3, sh0, s0p, mu_p, vec, padk(w_up), padk(a_up), padk(g_up), seg)
    s_new = s_out.reshape(B, H // 2, hd, 2, hd).transpose(0, 1, 3, 2, 4).reshape(B, H, hd, hd)
    return y, _rwkv_unpad_cols(sh_out[:, 0, :]), s_new


def _xattn_kernel(x_ref, g_ref, wq_ref, wo_ref, mk_ref, mv_ref, o_ref, cat_scr):
    x = x_ref[...]
    hn = x * lax.rsqrt(jnp.mean(x * x, axis=-1, keepdims=True) + NORM_EPS) * g_ref[...]
    q = jnp.dot(hn.astype(jnp.bfloat16), wq_ref[...], preferred_element_type=jnp.float32)
    qb = q.astype(jnp.bfloat16)
    mk = mk_ref[0].astype(jnp.bfloat16)
    mv = mv_ref[0].astype(jnp.bfloat16)
    hd = X_HEADDIM
    scale = hd ** -0.5
    for h in range(X_HEADS):
        cols = slice(h * hd, (h + 1) * hd)
        s = lax.dot_general(qb[:, cols], mk[:, cols], (((1,), (1,)), ((), ())),
                            preferred_element_type=jnp.float32) * scale
        m = jnp.max(s, axis=-1, keepdims=True)
        p = jnp.exp(s - m)
        p = p / jnp.sum(p, axis=-1, keepdims=True)
        cat_scr[:, cols] = jnp.dot(p.astype(jnp.bfloat16), mv[:, cols], preferred_element_type=jnp.float32)
    o_ref[...] = x + jnp.dot(cat_scr[...].astype(jnp.bfloat16), wo_ref[...], preferred_element_type=jnp.float32)


def _xattn_block(x, gain, wq, wo, mk, mv, *, tb, rows_per_mem):
    n = x.shape[0]
    M = mk.shape[1]
    assert n % tb == 0 and rows_per_mem % tb == 0
    per = rows_per_mem // tb
    return pl.pallas_call(
        _xattn_kernel,
        grid=(n // tb,),
        in_specs=[pl.BlockSpec((tb, D_MODEL), lambda i: (i, 0)),
                  pl.BlockSpec((1, D_MODEL), lambda i: (0, 0)),
                  pl.BlockSpec((D_MODEL, D_MODEL), lambda i: (0, 0)),
                  pl.BlockSpec((D_MODEL, D_MODEL), lambda i: (0, 0)),
                  pl.BlockSpec((1, M, D_MODEL), lambda i: (i // per, 0, 0)),
                  pl.BlockSpec((1, M, D_MODEL), lambda i: (i // per, 0, 0))],
        out_specs=pl.BlockSpec((tb, D_MODEL), lambda i: (i, 0)),
        out_shape=jax.ShapeDtypeStruct((n, D_MODEL), jnp.float32),
        scratch_shapes=[pltpu.VMEM((tb, D_MODEL), jnp.float32)],
        compiler_params=pltpu.CompilerParams(dimension_semantics=("parallel",), vmem_limit_bytes=_VMEM_LIMIT),
        name="xattn_block",
    )(x, gain.reshape(1, -1), wq.astype(jnp.bfloat16), wo.astype(jnp.bfloat16), mk, mv)


def _top_values(s, count, with_rank=False):
    vals = []
    cur = s
    rank = jnp.full(s.shape, float(count), jnp.float32) if with_rank else None
    for r in range(count):
        m = jnp.max(cur, axis=0, keepdims=True)
        vals.append(m)
        hit = cur == m
        if with_rank:
            rank = jnp.where(hit, float(r), rank)
        cur = jnp.where(hit, NEG_INF, cur)
    return (vals, rank) if with_rank else vals


def _peer_route_kernel(x_ref, g_ref, wqt_ref, sk_ref, hn_ref, cnt_ref, e1_ref, r2_ref, e2_ref, a1_scr, a2_scr):
    tb = x_ref.shape[0]
    x = x_ref[...]
    hn = x * lax.rsqrt(jnp.mean(x * x, axis=-1, keepdims=True) + NORM_EPS) * g_ref[...]
    hb = hn.astype(jnp.bfloat16)
    hn_ref[...] = hb
    qt = lax.dot_general(wqt_ref[...], hb, (((1,), (1,)), ((), ())), preferred_element_type=jnp.float32)
    n_top = PEER_TOPK + 1
    row8 = lax.broadcasted_iota(jnp.int32, (8, tb), 0)
    a1_scr[...] = jnp.full(a1_scr.shape, NEG_INF, jnp.float32)
    a2_scr[...] = jnp.full(a2_scr.shape, NEG_INF, jnp.float32)
    for h in range(PEER_HEADS):
        q1 = qt[(2 * h) * PEER_HALF:(2 * h + 1) * PEER_HALF, :].astype(jnp.bfloat16)
        q2 = qt[(2 * h + 1) * PEER_HALF:(2 * h + 2) * PEER_HALF, :].astype(jnp.bfloat16)
        s1 = jnp.dot(sk_ref[2 * h], q1, preferred_element_type=jnp.float32)
        s2 = jnp.dot(sk_ref[2 * h + 1], q2, preferred_element_type=jnp.float32)
        v1 = _top_values(s1, n_top)
        v2, rank2 = _top_values(s2, n_top, with_rank=True)
        for r in range(n_top):
            a1_scr[r:r + 1, :] = v1[r]
            a2_scr[r:r + 1, :] = v2[r]
        a1 = a1_scr[...]
        a2 = a2_scr[...]
        pieces = [a1 + a2[0:1, :]]
        for q in range(1, 8):
            lim = n_top // (q + 1)
            pieces.append(jnp.where(row8 < lim, a1[0:8, :] + a2[q:q + 1, :], NEG_INF))
        pieces.append(a1[0:1, :] + a2[8:24, :])
        cand = jnp.concatenate(pieces, axis=0)
        c = _top_values(cand, n_top)
        tau = 0.5 * (c[PEER_TOPK - 1] + c[PEER_TOPK])
        mx = c[0]
        z = jnp.sum(jnp.where(cand >= tau, jnp.exp(cand - mx), 0.0), axis=0, keepdims=True)
        cnt = jnp.zeros(s1.shape, jnp.float32)
        for q in range(n_top):
            cnt = cnt + jnp.where(s1 + v2[q] >= tau, 1.0, 0.0)
        cnt_ref[h] = cnt
        e1_ref[h] = jnp.exp(s1 - v1[0]) / z
        r2_ref[h] = rank2.astype(jnp.bfloat16)
        e2_ref[h] = jnp.exp(s2 - v2[0]).astype(jnp.bfloat16)


def _peer_route(x, gain, wqt_b, sk_b, tb=256):
    n = x.shape[0]
    assert n % tb == 0
    hk = PEER_HEADS
    blk3 = pl.BlockSpec((hk, PEER_NKEYS, tb), lambda i: (0, 0, i))
    shp_f = jax.ShapeDtypeStruct((hk, PEER_NKEYS, n), jnp.float32)
    shp_b = jax.ShapeDtypeStruct((hk, PEER_NKEYS, n), jnp.bfloat16)
    return pl.pallas_call(
        _peer_route_kernel,
        grid=(n // tb,),
        in_specs=[pl.BlockSpec((tb, D_MODEL), lambda i: (i, 0)),
                  pl.BlockSpec((1, D_MODEL), lambda i: (0, 0)),
                  pl.BlockSpec((2 * hk * PEER_HALF, D_MODEL), lambda i: (0, 0)),
                  pl.BlockSpec((2 * hk, PEER_NKEYS, PEER_HALF), lambda i: (0, 0, 0))],
        out_specs=[pl.BlockSpec((tb, D_MODEL), lambda i: (i, 0)), blk3, blk3, blk3, blk3],
        out_shape=[jax.ShapeDtypeStruct((n, D_MODEL), jnp.bfloat16), shp_f, shp_f, shp_b, shp_b],
        scratch_shapes=[pltpu.VMEM((24, tb), jnp.float32), pltpu.VMEM((24, tb), jnp.float32)],
        compiler_params=pltpu.CompilerParams(dimension_semantics=("parallel",), vmem_limit_bytes=_VMEM_LIMIT),
        name="peer_route",
    )(x, gain, wqt_b, sk_b)


def _peer_dense_kernel(x_ref, hn_ref, u_ref, vt_ref, cnt_ref, e1_ref, r2_ref, e2_ref, o_ref, acc_ref, w_ref, *, ic):
    e = pl.program_id(1)
    tb = x_ref.shape[0]

    @pl.when(e == 0)
    def _():
        acc_ref[...] = jnp.zeros_like(acc_ref)

    act = lax.dot_general(u_ref[...], hn_ref[...], (((1,), (1,)), ((), ())), preferred_element_type=jnp.float32)
    for il in range(ic):
        g = None
        for h in range(PEER_HEADS):
            cb = jnp.broadcast_to(cnt_ref[h, il:il + 1, :], (16, tb)).astype(jnp.bfloat16)[None]
            eb = jnp.broadcast_to(e1_ref[h, il:il + 1, :], (16, tb)).astype(jnp.bfloat16)[None]
            r2 = r2_ref[h].reshape(PEER_NKEYS // 16, 16, tb)
            e2 = e2_ref[h].reshape(PEER_NKEYS // 16, 16, tb)
            t = jnp.where(r2 < cb, e2 * eb, jnp.zeros_like(e2))
            g = t if g is None else g + t
        a = act[il * PEER_NKEYS:(il + 1) * PEER_NKEYS, :]
        gf = g.reshape(PEER_NKEYS, tb).astype(jnp.float32)
        w_ref[il * PEER_NKEYS:(il + 1) * PEER_NKEYS, :] = (gf * _gelu_tanh(a)).astype(jnp.bfloat16)
    acc_ref[...] += jnp.dot(vt_ref[...], w_ref[...], preferred_element_type=jnp.float32)

    @pl.when(e == pl.num_programs(1) - 1)
    def _():
        o_ref[...] = x_ref[...] + acc_ref[...].T


def _peer_dense(x, hn_b, u_b, vt_b, cnt, e1, r2, e2, tb=PEER_TOK_PAD, ic=16):
    n = x.shape[0]
    ne = u_b.shape[0]
    ec = ic * PEER_NKEYS
    assert n % tb == 0 and ne % ec == 0
    hk = PEER_HEADS
    row_blk = pl.BlockSpec((hk, ic, tb), lambda i, e: (0, e, i))
    full_blk = pl.BlockSpec((hk, PEER_NKEYS, tb), lambda i, e: (0, 0, i))
    return pl.pallas_call(
        functools.partial(_peer_dense_kernel, ic=ic),
        grid=(n // tb, ne // ec),
        in_specs=[pl.BlockSpec((tb, D_MODEL), lambda i, e: (i, 0)),
                  pl.BlockSpec((tb, D_MODEL), lambda i, e: (i, 0)),
                  pl.BlockSpec((ec, D_MODEL), lambda i, e: (e, 0)),
                  pl.BlockSpec((D_MODEL, ec), lambda i, e: (0, e)),
                  row_blk, row_blk, full_blk, full_blk],
        out_specs=pl.BlockSpec((tb, D_MODEL), lambda i, e: (i, 0)),
        out_shape=jax.ShapeDtypeStruct((n, D_MODEL), jnp.float32),
        scratch_shapes=[pltpu.VMEM((D_MODEL, tb), jnp.float32), pltpu.VMEM((ec, tb), jnp.bfloat16)],
        compiler_params=pltpu.CompilerParams(dimension_semantics=("parallel", "arbitrary"),
                                             vmem_limit_bytes=_VMEM_LIMIT),
        name="peer_dense",
    )(x, hn_b, u_b, vt_b, cnt, e1, r2, e2)


def _peer_block(x, gain, wq, subkeys, u_tab, v_tab):
    wqt_b = wq.T.astype(jnp.bfloat16)
    sk_b = subkeys.reshape(2 * PEER_HEADS, PEER_NKEYS, PEER_HALF).astype(jnp.bfloat16)
    u_b = u_tab.astype(jnp.bfloat16)
    vt_b = v_tab.T.astype(jnp.bfloat16)
    hn_b, cnt, e1, r2, e2 = _peer_route(x, gain.reshape(1, -1), wqt_b, sk_b)
    return _peer_dense(x, hn_b, u_b, vt_b, cnt, e1, r2, e2)


def _in_weight(w_in):
    abc = w_in[:, :D_OFF]
    pad = jnp.zeros((w_in.shape[0], ABC_PCOLS - D_OFF), w_in.dtype)
    return jnp.concatenate([abc, pad, _rwkv_pad_cols(w_in[:, D_OFF:])], axis=1)


def _layer_pre_peer(x2, bsz, T, lp, st, prompt):
    G = GROUP_WIDTH
    n = bsz * T
    proj = _mm(x2, lp['w_in_p'], gain=lp['norm_mix'])
    if prompt:
        n_valid, tc = SEQ_CHUNK, SEQ_CHUNK
        proj3 = proj.reshape(bsz, T, -1)
        ya, k_r, v = _moba_prompt(proj3, jnp.arange(T, dtype=jnp.int32))
    else:
        assert T == 1
        n_valid, tc = T, DEC_ROWS
        ya, k_r = _moba_decode(proj[:, 0:G], proj[:, G:2 * G], proj[:, 2 * G:3 * G],
                               st['k_cache'], st['v_cache'], st['page_table'], PAST_LEN)
        v = proj[:, 2 * G:3 * G]
        proj3 = jnp.pad(proj[:, None, :], ((0, 0), (0, tc - T), (0, 0)))
    yb, lru_conv, lru_h = _lru_mixer(proj3, st['lru_conv'], st['lru_h'], lp['lru_conv_w'], lp['lru_conv_b'],
                                     lp['lru_wa'], lp['lru_ba'], lp['lru_wx'], lp['lru_bx'], lp['lru_lambda'],
                                     n_valid=n_valid, tc=tc)
    yc, ssd_conv, ssd_s = _ssd_mixer(proj3, st['ssd_conv'], st['ssd'], lp['ssd_conv_w'], lp['ssd_conv_b'],
                                     lp['ssd_dt_bias'], lp['ssd_a_log'], lp['ssd_d'], lp['ssd_norm'],
                                     n_valid=n_valid, tc=tc)
    yd, rwkv_shift, rwkv_s = _rwkv_mixer(proj3, st['rwkv_shift'], st['rwkv'],
                                         lp['rwkv_mu'], lp['rwkv_w0'], lp['rwkv_w_up'], lp['rwkv_a0'],
                                         lp['rwkv_a_up'], lp['rwkv_g_up'], lp['rwkv_k_k'], lp['rwkv_k_a'],
                                         lp['rwkv_r_k'], lp['rwkv_ln_w'], lp['rwkv_ln_b'], n_steps=n_valid, tc=tc)
    rows = lambda y: y[:, :T].reshape(n, G)
    x2 = _mm([ya.reshape(n, G), rows(yb), rows(yc), rows(yd)], lp['w_out'], residual=x2)
    if prompt:
        x2 = _xattn_block(x2, lp['norm_x'], lp['x_wq'], lp['x_wo'], st['mem_k'], st['mem_v'],
                          tb=SEQ_CHUNK, rows_per_mem=T)
    else:
        x8 = jnp.pad(x2[:, None, :], ((0, 0), (0, 7), (0, 0))).reshape(n * 8, D_MODEL)
        x8 = _xattn_block(x8, lp['norm_x'], lp['x_wq'], lp['x_wo'], st['mem_k'], st['mem_v'], tb=8, rows_per_mem=8)
        x2 = x8.reshape(n, 8, D_MODEL)[:, 0, :]
    heads = lambda t: t.reshape(bsz, T, A_HEADS, HEAD_DIM)
    new = {'k': heads(k_r), 'v': heads(v), 'lru_h': lru_h, 'lru_conv': lru_conv, 'ssd': ssd_s, 'ssd_conv': ssd_conv,
           'rwkv': rwkv_s, 'rwkv_shift': rwkv_shift}
    return x2, new


def kernel(x_prompt, x_sample, mem_prompt, cache_moba_k, cache_moba_v, page_table, state_lru_h, state_lru_conv, state_ssd, state_ssd_conv, state_rwkv, state_rwkv_shift, cache_mem_k, cache_mem_v, norm_mix, w_in, w_out, lru_conv_w, lru_conv_b, lru_wa, lru_ba, lru_wx, lru_bx, lru_lambda, ssd_conv_w, ssd_conv_b, ssd_dt_bias, ssd_a_log, ssd_d, ssd_norm, rwkv_mu, rwkv_w0, rwkv_w_up, rwkv_a0, rwkv_a_up, rwkv_g_up, rwkv_k_k, rwkv_k_a, rwkv_r_k, rwkv_ln_w, rwkv_ln_b, norm_x, x_wq, x_wk, x_wv, x_wo, norm_ffn, peer_wq, peer_subkeys, peer_u, peer_v, final_norm):
    bp, tp, _ = x_prompt.shape
    bd, td, _ = x_sample.shape
    n_p, n_s = bp * tp, bd * td
    n_all = -(-(n_p + n_s) // PEER_TOK_PAD) * PEER_TOK_PAD
    f32 = jnp.float32
    xp2 = x_prompt.reshape(n_p, D_MODEL)
    xs2 = x_sample.reshape(n_s, D_MODEL)
    names = ('k', 'v', 'lru_h', 'lru_conv', 'ssd', 'ssd_conv', 'rwkv', 'rwkv_shift')
    p_new = {n: [] for n in names + ('mem_k', 'mem_v')}
    s_new = {n: [] for n in names}
    mem2 = mem_prompt.reshape(bp * MEM_LEN, D_MODEL)
    for l in range(DEPTH):
        lp = {
            'norm_mix': norm_mix[l], 'w_in_p': _in_weight(w_in[l]), 'w_out': w_out[l],
            'lru_conv_w': lru_conv_w[l], 'lru_conv_b': lru_conv_b[l], 'lru_wa': lru_wa[l], 'lru_ba': lru_ba[l],
            'lru_wx': lru_wx[l], 'lru_bx': lru_bx[l], 'lru_lambda': lru_lambda[l],
            'ssd_conv_w': ssd_conv_w[l], 'ssd_conv_b': ssd_conv_b[l], 'ssd_dt_bias': ssd_dt_bias[l],
            'ssd_a_log': ssd_a_log[l], 'ssd_d': ssd_d[l], 'ssd_norm': ssd_norm[l],
            'rwkv_mu': rwkv_mu[l], 'rwkv_w0': rwkv_w0[l], 'rwkv_w_up': rwkv_w_up[l], 'rwkv_a0': rwkv_a0[l],
            'rwkv_a_up': rwkv_a_up[l], 'rwkv_g_up': rwkv_g_up[l], 'rwkv_k_k': rwkv_k_k[l], 'rwkv_k_a': rwkv_k_a[l],
            'rwkv_r_k': rwkv_r_k[l], 'rwkv_ln_w': rwkv_ln_w[l], 'rwkv_ln_b': rwkv_ln_b[l],
            'norm_x': norm_x[l], 'x_wq': x_wq[l], 'x_wo': x_wo[l],
        }
        mk = _mm(mem2, x_wk[l]).reshape(bp, MEM_LEN, D_MODEL)
        mv = _mm(mem2, x_wv[l]).reshape(bp, MEM_LEN, D_MODEL)
        st_p = {
            'lru_conv': jnp.zeros((bp, CONV_W - 1, LRU_WIDTH), f32),
            'lru_h': jnp.zeros((bp, LRU_WIDTH), f32),
            'ssd_conv': jnp.zeros((bp, CONV_W - 1, SSD_CONV_CH), f32),
            'ssd': jnp.zeros((bp, SSD_HEADS, SSD_HEADDIM, SSD_STATE), f32),
            'rwkv_shift': jnp.zeros((bp, RWKV_COLS), f32),
            'rwkv': jnp.zeros((bp, RWKV_HEADS, RWKV_HEADDIM, RWKV_HEADDIM), f32),
            'mem_k': mk, 'mem_v': mv,
        }
        xp2, npl = _layer_pre_peer(xp2, bp, tp, lp, st_p, True)
        for n in names:
            p_new[n].append(npl[n])
        p_new['mem_k'].append(mk.reshape(bp, MEM_LEN, X_HEADS, X_HEADDIM))
        p_new['mem_v'].append(mv.reshape(bp, MEM_LEN, X_HEADS, X_HEADDIM))
        st_s = {
            'k_cache': cache_moba_k[l], 'v_cache': cache_moba_v[l], 'page_table': page_table,
            'lru_conv': state_lru_conv[l], 'lru_h': state_lru_h[l],
            'ssd_conv': state_ssd_conv[l], 'ssd': state_ssd[l],
            'rwkv_shift': state_rwkv_shift[l], 'rwkv': state_rwkv[l],
            'mem_k': cache_mem_k[l].reshape(bd, MEM_LEN, D_MODEL), 'mem_v': cache_mem_v[l].reshape(bd, MEM_LEN, D_MODEL),
        }
        xs2, nsl = _layer_pre_peer(xs2, bd, td, lp, st_s, False)
        for n in names:
            s_new[n].append(nsl[n])
        x_all = jnp.concatenate([xp2, xs2, jnp.zeros((n_all - n_p - n_s, D_MODEL), f32)], axis=0)
        x_all = _peer_block(x_all, norm_ffn[l], peer_wq[l], peer_subkeys[l], peer_u[l], peer_v[l])
        xp2 = x_all[:n_p]
        xs2 = x_all[n_p:n_p + n_s]
    y_prompt = _rmsnorm(xp2, final_norm).reshape(bp, tp, D_MODEL)
    y_sample = _rmsnorm(xs2, final_norm).reshape(bd, td, D_MODEL)
    return (y_prompt, y_sample,
            jnp.stack(p_new['k']), jnp.stack(p_new['v']), jnp.stack(p_new['lru_h']), jnp.stack(p_new['lru_conv']),
            jnp.stack(p_new['ssd']), jnp.stack(p_new['ssd_conv']), jnp.stack(p_new['rwkv']), jnp.stack(p_new['rwkv_shift']),
            jnp.stack(p_new['mem_k']), jnp.stack(p_new['mem_v']),
            jnp.stack(s_new['k']), jnp.stack(s_new['v']), jnp.stack(s_new['lru_h']), jnp.stack(s_new['lru_conv']),
            jnp.stack(s_new['ssd']), jnp.stack(s_new['ssd_conv']), jnp.stack(s_new['rwkv']), jnp.stack(s_new['rwkv_shift']))
```

```python
import functools

import jax
import jax.numpy as jnp
from jax import lax
from jax.experimental import pallas as pl
from jax.experimental.pallas import tpu as pltpu

D_MODEL = 1024
DEPTH = 2
PAST_LEN = 16384
PAGE_SIZE = 128
GROUP_WIDTH = 256
HEAD_DIM = 64
A_HEADS = GROUP_WIDTH // HEAD_DIM
MOBA_BLOCK = 256
MOBA_TOPK = 3
ROPE_THETA = 10000.0
LRU_WIDTH = GROUP_WIDTH
LRU_C = 8.0
CONV_W = 4
SSD_HEADS = 4
SSD_HEADDIM = GROUP_WIDTH // SSD_HEADS
SSD_GROUPS = 2
SSD_STATE = 64
SSD_CHUNK = 128
SSD_CONV_CH = GROUP_WIDTH + 2 * SSD_GROUPS * SSD_STATE
RWKV_HEADS = 4
RWKV_HEADDIM = GROUP_WIDTH // RWKV_HEADS
RWKV_W_RANK = 32
RWKV_A_RANK = 32
RWKV_G_RANK = 64
RWKV_COLS = 3 * GROUP_WIDTH + RWKV_W_RANK + RWKV_A_RANK + RWKV_G_RANK
RWKV_LN_EPS = 64e-5
D_OFF = 3 * GROUP_WIDTH + 2 * LRU_WIDTH + GROUP_WIDTH + SSD_CONV_CH + SSD_HEADS
MEM_LEN = 256
X_HEADS = 4
X_HEADDIM = D_MODEL // X_HEADS
PEER_HEADS = 8
PEER_NKEYS = 128
PEER_HALF = 128
PEER_TOPK = 16
NORM_EPS = 1e-6
NEG_INF = -1e30

LANES = 128
RWKV_PCOLS = 3 * GROUP_WIDTH + 3 * LANES
ABC_PCOLS = 2 * RWKV_PCOLS
COL_Q, COL_K, COL_V, COL_U, COL_GATE, COL_Z = 0, 1, 2, 3, 4, 5
COL_XBC = 3
COL_DT = 8
COL_RWKV = 2
PAGES_PER_BLOCK = MOBA_BLOCK // PAGE_SIZE
KMEAN_PAGES = 16
PEER_TOK_PAD = 512
SEQ_CHUNK = 512
DEC_ROWS = 128
_VMEM_LIMIT = 56 * 1024 * 1024
_HI = lax.Precision.HIGHEST


def _softplus(x):
    return jnp.maximum(x, 0.0) + jnp.log1p(jnp.exp(-jnp.abs(x)))


def _silu(x):
    return x * jax.nn.sigmoid(x)


def _gelu_tanh(x):
    return 0.5 * x * (1.0 + jnp.tanh(0.7978845608028654 * (x + 0.044715 * (x * x * x))))


def _mm_kernel(*refs, n_x, has_norm, has_res):
    x_refs = refs[:n_x]
    w_ref = refs[n_x]
    pos = n_x + 1
    g_ref = r_ref = None
    if has_norm:
        g_ref = refs[pos]
        pos += 1
    if has_res:
        r_ref = refs[pos]
        pos += 1
    o_ref = refs[pos]
    y = None
    off = 0
    for x_ref in x_refs:
        x = x_ref[...]
        kx = x.shape[1]
        if has_norm:
            x = x * lax.rsqrt(jnp.mean(x * x, axis=-1, keepdims=True) + NORM_EPS) * g_ref[...]
        t = jnp.dot(x.astype(jnp.bfloat16), w_ref[off:off + kx, :], preferred_element_type=jnp.float32)
        y = t if y is None else y + t
        off += kx
    if has_res:
        y = y + r_ref[...]
    o_ref[...] = y


def _mm(xs, w, gain=None, residual=None, tm=512):
    if not isinstance(xs, (list, tuple)):
        xs = [xs]
    assert gain is None or len(xs) == 1
    M = xs[0].shape[0]
    K, N = w.shape
    tm = min(tm, M)
    assert M % tm == 0 and sum(x.shape[1] for x in xs) == K
    args = list(xs) + [w.astype(jnp.bfloat16)]
    in_specs = [pl.BlockSpec((tm, x.shape[1]), lambda i: (i, 0)) for x in xs]
    in_specs.append(pl.BlockSpec((K, N), lambda i: (0, 0)))
    if gain is not None:
        args.append(gain.reshape(1, K).astype(jnp.float32))
        in_specs.append(pl.BlockSpec((1, K), lambda i: (0, 0)))
    if residual is not None:
        args.append(residual)
        in_specs.append(pl.BlockSpec((tm, N), lambda i: (i, 0)))
    return pl.pallas_call(
        functools.partial(_mm_kernel, n_x=len(xs), has_norm=gain is not None, has_res=residual is not None),
        grid=(M // tm,),
        in_specs=in_specs,
        out_specs=pl.BlockSpec((tm, N), lambda i: (i, 0)),
        out_shape=jax.ShapeDtypeStruct((M, N), jnp.float32),
        compiler_params=pltpu.CompilerParams(dimension_semantics=("parallel",), vmem_limit_bytes=_VMEM_LIMIT),
        name="mm",
    )(*args)


def _rmsnorm(x, g):
    return x * lax.rsqrt(jnp.mean(x * x, axis=-1, keepdims=True) + NORM_EPS) * g


def _rope_tables(pos):
    half = HEAD_DIM // 2
    freq = 1.0 / (ROPE_THETA ** (jnp.arange(half, dtype=jnp.float32) / half))
    ang = pos.astype(jnp.float32)[:, None] * freq[None, :]
    cos = jnp.cos(ang)
    sin = jnp.sin(ang)
    cos_t = jnp.tile(jnp.concatenate([cos, cos], axis=-1), (1, A_HEADS))
    sin_t = jnp.tile(jnp.concatenate([-sin, sin], axis=-1), (1, A_HEADS))
    return cos_t, sin_t


def _rope_apply(x, cos_t, sin_t):
    half = HEAD_DIM // 2
    lane = lax.broadcasted_iota(jnp.int32, x.shape, 1)
    first = (lane % HEAD_DIM) < half
    w = x.shape[1]
    swapped = jnp.where(first, pltpu.roll(x, w - half, axis=1), pltpu.roll(x, half, axis=1))
    return x * cos_t + swapped * sin_t


def _moba_prompt_kernel(q_ref, k_ref, v_ref, cos_ref, sin_ref, ya_ref, kr_ref, vo_ref, ot_scr):
    T = q_ref.shape[1]
    G = GROUP_WIDTH
    hd = HEAD_DIM
    blk = MOBA_BLOCK
    nb = T // blk
    cos_t = cos_ref[...]
    sin_t = sin_ref[...]
    q = _rope_apply(q_ref[0], cos_t, sin_t)
    k = _rope_apply(k_ref[0], cos_t, sin_t)
    v = v_ref[0]
    kr_ref[0] = k
    vo_ref[0] = v
    kmean = jnp.concatenate([jnp.sum(k[n * blk:(n + 1) * blk, :], axis=0, keepdims=True) for n in range(nb)],
                            axis=0) * (1.0 / blk)
    qt = q.T.astype(jnp.bfloat16)
    vt = v.T.astype(jnp.bfloat16)
    kb = k.astype(jnp.bfloat16)
    lane_g = lax.broadcasted_iota(jnp.int32, (nb, G), 1)
    rown = lax.broadcasted_iota(jnp.int32, (nb, T), 0)
    cur = lax.broadcasted_iota(jnp.int32, (nb, T), 1) // blk
    kpos = lax.broadcasted_iota(jnp.int32, (blk, blk), 0)
    qpos = lax.broadcasted_iota(jnp.int32, (blk, blk), 1)
    causal = kpos <= qpos
    scale = hd ** -0.5
    for h in range(A_HEADS):
        km_h = jnp.where(lane_g // hd == h, kmean, 0.0)
        gate = lax.dot_general(km_h, q, (((1,), (1,)), ((), ())), precision=_HI,
                               preferred_element_type=jnp.float32)
        gate = jnp.where(rown < cur, gate, NEG_INF)
        sel_rows = []
        for n in range(nb):
            gn = gate[n:n + 1, :]
            beats = (gate > gn) | ((gate == gn) & (rown < n))
            rank = jnp.sum(beats.astype(jnp.float32), axis=0, keepdims=True)
            sel_rows.append(rank < MOBA_TOPK)
        k_h = kb[:, h * hd:(h + 1) * hd]
        qt_h = qt[h * hd:(h + 1) * hd, :]
        vt_h = vt[h * hd:(h + 1) * hd, :]
        for qi in range(nb):
            qs = slice(qi * blk, (qi + 1) * blk)
            q_blk = qt_h[:, qs]
            s = jnp.dot(k_h[qs, :], q_blk, preferred_element_type=jnp.float32) * scale
            s = jnp.where(causal, s, NEG_INF)
            m = jnp.max(s, axis=0, keepdims=True)
            p = jnp.exp(s - m)
            l = jnp.sum(p, axis=0, keepdims=True)
            acc = jnp.dot(vt_h[:, qs], p.astype(jnp.bfloat16), preferred_element_type=jnp.float32)
            for n in range(qi):
                ks = slice(n * blk, (n + 1) * blk)
                s = jnp.dot(k_h[ks, :], q_blk, preferred_element_type=jnp.float32) * scale
                s = jnp.where(sel_rows[n][:, qs], s, NEG_INF)
                m_new = jnp.maximum(m, jnp.max(s, axis=0, keepdims=True))
                alpha = jnp.exp(m - m_new)
                p = jnp.exp(s - m_new)
                l = l * alpha + jnp.sum(p, axis=0, keepdims=True)
                acc = acc * alpha + jnp.dot(vt_h[:, ks], p.astype(jnp.bfloat16), preferred_element_type=jnp.float32)
                m = m_new
            ot_scr[h * hd:(h + 1) * hd, qs] = acc / l
    ya_ref[0] = ot_scr[...].T


def _moba_prompt(proj3, pos):
    B, T, _ = proj3.shape
    G = GROUP_WIDTH
    cos_t, sin_t = _rope_tables(pos)
    col = lambda j: pl.BlockSpec((1, T, G), lambda b: (b, 0, j))
    tab = pl.BlockSpec((T, G), lambda b: (0, 0))
    out = pl.BlockSpec((1, T, G), lambda b: (b, 0, 0))
    shp = jax.ShapeDtypeStruct((B, T, G), jnp.float32)
    return pl.pallas_call(
        _moba_prompt_kernel,
        grid=(B,),
        in_specs=[col(COL_Q), col(COL_K), col(COL_V), tab, tab],
        out_specs=[out, out, out],
        out_shape=[shp, shp, shp],
        scratch_shapes=[pltpu.VMEM((G, T), jnp.float32)],
        compiler_params=pltpu.CompilerParams(dimension_semantics=("parallel",), vmem_limit_bytes=_VMEM_LIMIT),
        name="moba_prompt",
    )(proj3, proj3, proj3, cos_t, sin_t)


def _kmean_kernel(pt_ref, *refs):
    pages, o_ref = refs[:-1], refs[-1]
    rows = []
    for blk in range(len(pages) // PAGES_PER_BLOCK):
        s = None
        for i in range(PAGES_PER_BLOCK):
            part = jnp.sum(pages[blk * PAGES_PER_BLOCK + i][0], axis=0, keepdims=True)
            s = part if s is None else s + part
        rows.append(s)
    o_ref[0] = jnp.concatenate(rows, axis=0) * (1.0 / MOBA_BLOCK)


def _moba_block_means(cache_k, page_table):
    B, n_pages = page_table.shape
    G = GROUP_WIDTH
    assert n_pages % KMEAN_PAGES == 0
    steps = n_pages // KMEAN_PAGES
    nblk = KMEAN_PAGES // PAGES_PER_BLOCK

    def page_spec(i):
        return pl.BlockSpec((1, PAGE_SIZE, G), lambda b, j, pt: (pt[b * n_pages + j * KMEAN_PAGES + i], 0, 0))

    return pl.pallas_call(
        _kmean_kernel,
        grid_spec=pltpu.PrefetchScalarGridSpec(
            num_scalar_prefetch=1,
            grid=(B, steps),
            in_specs=[page_spec(i) for i in range(KMEAN_PAGES)],
            out_specs=pl.BlockSpec((1, nblk, G), lambda b, j, pt: (b, j, 0)),
        ),
        out_shape=jax.ShapeDtypeStruct((B, n_pages // PAGES_PER_BLOCK, G), jnp.float32),
        compiler_params=pltpu.CompilerParams(dimension_semantics=("parallel", "arbitrary"),
                                             vmem_limit_bytes=_VMEM_LIMIT),
        name="moba_block_means",
    )(page_table.reshape(-1), *([cache_k] * KMEAN_PAGES))


def _moba_select_kernel(q_ref, k_ref, cos_ref, sin_ref, km_ref, seg_ref, qr_ref, kr_ref, top_ref):
    q = _rope_apply(q_ref[0], cos_ref[...], sin_ref[...])
    k = _rope_apply(k_ref[0], cos_ref[...], sin_ref[...])
    qr_ref[0] = q
    kr_ref[0] = k
    km = km_ref[0]
    nb = km.shape[0]
    gate = jnp.dot(km * q[0:1, :], seg_ref[...], precision=_HI, preferred_element_type=jnp.float32)
    rowi = lax.broadcasted_iota(jnp.int32, gate.shape, 0)
    picks = []
    for _ in range(MOBA_TOPK):
        m = jnp.max(gate, axis=0, keepdims=True)
        idx = jnp.min(jnp.where(gate == m, rowi, nb), axis=0, keepdims=True)
        picks.append(idx)
        gate = jnp.where(rowi == idx, NEG_INF, gate)
    picks.append(jnp.zeros((8 - MOBA_TOPK, gate.shape[1]), jnp.int32))
    top_ref[0] = jnp.concatenate(picks, axis=0)


def _moba_select(q8, k8, pos, kmean):
    B = q8.shape[0]
    G = GROUP_WIDTH
    nb = kmean.shape[1]
    cos_t, sin_t = _rope_tables(pos)
    seg = (jnp.arange(G)[:, None] // HEAD_DIM == jnp.arange(LANES)[None, :]).astype(jnp.float32)
    row = pl.BlockSpec((1, 8, G), lambda b: (b, 0, 0))
    tab = pl.BlockSpec((1, G), lambda b: (0, 0))
    return pl.pallas_call(
        _moba_select_kernel,
        grid=(B,),
        in_specs=[row, row, tab, tab, pl.BlockSpec((1, nb, G), lambda b: (b, 0, 0)),
                  pl.BlockSpec((G, LANES), lambda b: (0, 0))],
        out_specs=[row, row, pl.BlockSpec((1, 8, LANES), lambda b: (b, 0, 0))],
        out_shape=[jax.ShapeDtypeStruct((B, 8, G), jnp.float32), jax.ShapeDtypeStruct((B, 8, G), jnp.float32),
                   jax.ShapeDtypeStruct((B, 8, LANES), jnp.int32)],
        compiler_params=pltpu.CompilerParams(dimension_semantics=("parallel",)),
        name="moba_select",
    )(q8, k8, cos_t, sin_t, kmean, seg)


def _moba_decode_attn_kernel(pp_ref, q_ref, kn_ref, vn_ref, *refs):
    npg = MOBA_TOPK * PAGES_PER_BLOCK
    kp, vp, o_ref = refs[:npg], refs[npg:2 * npg], refs[2 * npg]
    h = pl.program_id(1)
    lane = lax.broadcasted_iota(jnp.int32, (8, GROUP_WIDTH), 1)
    mine = lane // HEAD_DIM == h
    bf = lambda a: a.astype(jnp.bfloat16)
    qh = bf(jnp.where(mine, q_ref[0], 0.0))
    scale = HEAD_DIM ** -0.5
    s_pages = [lax.dot_general(qh, bf(kp[i][0]), (((1,), (1,)), ((), ())), preferred_element_type=jnp.float32) * scale
               for i in range(npg)]
    s_own = jnp.sum(qh.astype(jnp.float32) * bf(kn_ref[0]).astype(jnp.float32), axis=-1, keepdims=True) * scale
    m = s_own
    for s in s_pages:
        m = jnp.maximum(m, jnp.max(s, axis=-1, keepdims=True))
    p_own = jnp.exp(s_own - m)
    l = p_own
    acc = p_own * bf(vn_ref[0]).astype(jnp.float32)
    for i in range(npg):
        p = jnp.exp(s_pages[i] - m)
        l = l + jnp.sum(p, axis=-1, keepdims=True)
        acc = acc + jnp.dot(bf(p), bf(vp[i][0]), preferred_element_type=jnp.float32)

    @pl.when(h == 0)
    def _():
        o_ref[...] = jnp.zeros_like(o_ref)

    o_ref[0] += jnp.where(mine, acc / l, 0.0)


def _moba_decode_attn(q8, k8, v8, cache_k, cache_v, phys):
    B = q8.shape[0]
    G = GROUP_WIDTH
    npg = MOBA_TOPK * PAGES_PER_BLOCK
    row = pl.BlockSpec((1, 8, G), lambda b, h, pp: (b, 0, 0))

    def page_spec(i):
        return pl.BlockSpec((1, PAGE_SIZE, G), lambda b, h, pp: (pp[(b * A_HEADS + h) * npg + i], 0, 0))

    return pl.pallas_call(
        _moba_decode_attn_kernel,
        grid_spec=pltpu.PrefetchScalarGridSpec(
            num_scalar_prefetch=1,
            grid=(B, A_HEADS),
            in_specs=[row, row, row] + [page_spec(i) for i in range(npg)] * 2,
            out_specs=row,
        ),
        out_shape=jax.ShapeDtypeStruct((B, 8, G), jnp.float32),
        compiler_params=pltpu.CompilerParams(dimension_semantics=("parallel", "arbitrary")),
        name="moba_decode_attn",
    )(phys.reshape(-1), q8, k8, v8, *([cache_k] * npg), *([cache_v] * npg))


def _moba_decode(q, k, v, cache_k, cache_v, page_table, pos0):
    B = q.shape[0]
    G = GROUP_WIDTH
    ck = cache_k.reshape(-1, PAGE_SIZE, G)
    cv = cache_v.reshape(-1, PAGE_SIZE, G)
    pad8 = lambda a: jnp.pad(a[:, None, :], ((0, 0), (0, 7), (0, 0)))
    kmean = _moba_block_means(ck, page_table)
    q8, k8, top = _moba_select(pad8(q), pad8(k), jnp.full((1,), pos0, jnp.int32), kmean)
    blocks = jnp.swapaxes(top[:, :MOBA_TOPK, :A_HEADS], 1, 2)
    pages = (blocks[..., None] * PAGES_PER_BLOCK + jnp.arange(PAGES_PER_BLOCK)).reshape(B, -1)
    phys = jnp.take_along_axis(page_table, pages, axis=1)
    out = _moba_decode_attn(q8, k8, pad8(v), ck, cv, phys)
    return out[:, 0, :], k8[:, 0, :]


def _causal_conv_tile(xs, prev3, w_ref, b_ref):
    row = lax.broadcasted_iota(jnp.int32, xs.shape, 0)
    y = b_ref[...] + xs * w_ref[CONV_W - 1:CONV_W, :]
    for k in range(1, CONV_W):
        sh = pltpu.roll(xs, k, axis=0)
        for r in range(k):
            src = r + (CONV_W - 1) - k
            sh = jnp.where(row == r, prev3[src:src + 1, :], sh)
        y = y + sh * w_ref[CONV_W - 1 - k:CONV_W - k, :]
    return y


def _conv_tail(xs, prev3, n_valid):
    rows = []
    for i in range(CONV_W - 1):
        idx = n_valid - (CONV_W - 1) + i
        rows.append(xs[idx:idx + 1, :] if idx >= 0 else prev3[idx + CONV_W - 1:idx + CONV_W, :])
    return rows


def _lru_kernel(u_ref, gate_ref, cb0_ref, h0_ref, cw_ref, cbias_ref, wa_ref, wx_ref, vec_ref,
                y_ref, cbo_ref, ho_ref, conv_scr, h_scr, a_scr, b_scr, hs_scr, *, n_valid):
    c = pl.program_id(1)
    tc = u_ref.shape[1]

    @pl.when(c == 0)
    def _():
        conv_scr[...] = cb0_ref[0]
        h_scr[...] = h0_ref[0]

    us = u_ref[0]
    prev3 = conv_scr[...]
    xc = _causal_conv_tile(us, prev3, cw_ref, cbias_ref)
    tail = _conv_tail(us, prev3, n_valid)
    for i in range(CONV_W - 1):
        conv_scr[i:i + 1, :] = tail[i]
    xb = xc.astype(jnp.bfloat16)
    r = jax.nn.sigmoid(jnp.dot(xb, wa_ref[...], preferred_element_type=jnp.float32) + vec_ref[0:1, :])
    i_g = jax.nn.sigmoid(jnp.dot(xb, wx_ref[...], preferred_element_type=jnp.float32) + vec_ref[1:2, :])
    log_a = -LRU_C * r * _softplus(-vec_ref[2:3, :])
    a_scr[...] = jnp.exp(log_a)
    b_scr[...] = jnp.sqrt(-jnp.tanh(log_a) * (jnp.exp(2.0 * log_a) + 1.0)) * (i_g * xc)

    row8 = lax.broadcasted_iota(jnp.int32, (8, LRU_WIDTH), 0)

    def step8(i, h):
        t0 = pl.multiple_of(i * 8, 8)
        a8 = a_scr[pl.ds(t0, 8), :]
        b8 = b_scr[pl.ds(t0, 8), :]
        out = jnp.zeros((8, LRU_WIDTH), jnp.float32)
        for j in range(min(8, n_valid)):
            h = a8[j:j + 1, :] * h + b8[j:j + 1, :]
            out = jnp.where(row8 == j, h, out)
        hs_scr[pl.ds(t0, 8), :] = out
        return h

    if n_valid < tc:
        hs_scr[...] = jnp.zeros_like(hs_scr)
    h = lax.fori_loop(0, -(-n_valid // 8), step8, h_scr[...])
    h_scr[...] = h
    y_ref[0] = hs_scr[...] * _gelu_tanh(gate_ref[0])

    @pl.when(c == pl.num_programs(1) - 1)
    def _():
        cbo_ref[0] = conv_scr[...]
        ho_ref[0] = h_scr[...]


def _block_diag(w):
    n, d, e = w.shape
    eye = jnp.eye(n, dtype=w.dtype)
    return (eye[:, None, :, None] * w[:, :, None, :]).reshape(n * d, n * e)


def _lru_mixer(proj3, conv0, h0, conv_w, conv_b, wa, ba, wx, bx, lam, *, n_valid, tc):
    B, Tp, _ = proj3.shape
    G = GROUP_WIDTH
    f32 = jnp.float32
    nchunk = Tp // tc
    assert Tp % tc == 0 and (n_valid == tc or nchunk == 1) and (n_valid % 8 == 0 or n_valid < 8)
    cb0 = jnp.pad(conv0, ((0, 0), (0, 8 - (CONV_W - 1)), (0, 0)))
    vec = jnp.zeros((8, G), f32).at[0].set(ba).at[1].set(bx).at[2].set(lam)
    full2 = lambda shp: pl.BlockSpec(shp, lambda b, c: (0, 0))
    y, cbo, ho = pl.pallas_call(
        functools.partial(_lru_kernel, n_valid=n_valid),
        grid=(B, nchunk),
        in_specs=[pl.BlockSpec((1, tc, G), lambda b, c: (b, c, COL_U)),
                  pl.BlockSpec((1, tc, G), lambda b, c: (b, c, COL_GATE)),
                  pl.BlockSpec((1, 8, G), lambda b, c: (b, 0, 0)),
                  pl.BlockSpec((1, 1, G), lambda b, c: (b, 0, 0)),
                  full2((CONV_W, G)), full2((1, G)), full2((G, G)), full2((G, G)), full2((8, G))],
        out_specs=[pl.BlockSpec((1, tc, G), lambda b, c: (b, c, 0)),
                   pl.BlockSpec((1, 8, G), lambda b, c: (b, 0, 0)),
                   pl.BlockSpec((1, 1, G), lambda b, c: (b, 0, 0))],
        out_shape=[jax.ShapeDtypeStruct((B, Tp, G), f32), jax.ShapeDtypeStruct((B, 8, G), f32),
                   jax.ShapeDtypeStruct((B, 1, G), f32)],
        scratch_shapes=[pltpu.VMEM((8, G), f32), pltpu.VMEM((1, G), f32),
                        pltpu.VMEM((tc, G), f32), pltpu.VMEM((tc, G), f32), pltpu.VMEM((tc, G), f32)],
        compiler_params=pltpu.CompilerParams(dimension_semantics=("parallel", "arbitrary"),
                                             vmem_limit_bytes=_VMEM_LIMIT),
        name="lru_mixer",
    )(proj3, proj3, cb0, h0[:, None, :], conv_w, conv_b[None, :],
      _block_diag(wa).astype(jnp.bfloat16), _block_diag(wx).astype(jnp.bfloat16), vec)
    return y, cbo[:, :CONV_W - 1, :], ho[:, 0, :]


def _ssd_kernel(z_ref, xbc_ref, dt_ref, cb0_ref, s0_ref, cw_ref, cbias_ref, prow_ref, pcol_ref, dvec_ref, nw_ref,
                y_ref, cbo_ref, so_ref, conv_scr, st_scr, y_scr, *, n_valid):
    c = pl.program_id(1)
    tc = xbc_ref.shape[1]
    Q = SSD_CHUNK
    G = GROUP_WIDTH
    P = SSD_HEADDIM
    N = SSD_STATE
    GN = SSD_GROUPS * SSD_STATE

    @pl.when(c == 0)
    def _():
        conv_scr[...] = cb0_ref[0]
        st_scr[...] = s0_ref[0]

    xs = xbc_ref[0]
    prev3 = conv_scr[...]
    xbc_f = _silu(_causal_conv_tile(xs, prev3, cw_ref, cbias_ref))
    tail = _conv_tail(xs, prev3, n_valid)
    for i in range(CONV_W - 1):
        conv_scr[i:i + 1, :] = tail[i]

    ii = lax.broadcasted_iota(jnp.int32, (Q, Q), 0)
    jj = lax.broadcasted_iota(jnp.int32, (Q, Q), 1)
    lower = ii >= jj
    tri_l = lower.astype(jnp.float32)
    tri_u = (ii <= jj).astype(jnp.float32)
    bias_row = prow_ref[0:1, :]
    alog_row = prow_ref[1:2, :]
    bias_col = pcol_ref[:, 0:1]
    alog_col = pcol_ref[:, 1:2]
    rowq = lax.broadcasted_iota(jnp.int32, (Q, LANES), 0)
    laneq = lax.broadcasted_iota(jnp.int32, (8, Q), 1)
    bf = lambda a: a.astype(jnp.bfloat16)
    for j in range(tc // Q):
        rows = slice(j * Q, (j + 1) * Q)
        xq = xbc_f[rows, :]
        xt = xq.T
        dcol = dt_ref[0, rows, 0:LANES]
        drow = dcol.T[0:8, :]
        dt_col = _softplus(dcol + bias_row)
        dt_row = _softplus(drow + bias_col)
        if n_valid < tc:
            dt_col = jnp.where(rowq + j * Q < n_valid, dt_col, 0.0)
            dt_row = jnp.where(laneq + j * Q < n_valid, dt_row, 0.0)
        da_col = dt_col * (-jnp.exp(alog_row))
        da_row = dt_row * (-jnp.exp(alog_col))
        cs_col = jnp.dot(tri_l, da_col, precision=_HI, preferred_element_type=jnp.float32)
        cs_row = jnp.dot(da_row, tri_u, precision=_HI, preferred_element_type=jnp.float32)
        for g in range(SSD_GROUPS):
            bg = xq[:, G + g * N:G + (g + 1) * N]
            cg = xq[:, G + GN + g * N:G + GN + (g + 1) * N]
            bgt = xt[G + g * N:G + (g + 1) * N, :]
            cb = lax.dot_general(bf(cg), bf(bg), (((1,), (1,)), ((), ())), preferred_element_type=jnp.float32)
            for h in range(g * (SSD_HEADS // SSD_GROUPS), (g + 1) * (SSD_HEADS // SSD_GROUPS)):
                ci = cs_col[:, h:h + 1]
                cj = cs_row[h:h + 1, :]
                c_last = cs_row[h:h + 1, Q - 1:Q]
                lm = jnp.exp(jnp.where(lower, ci - cj, NEG_INF))
                x_h = xq[:, h * P:(h + 1) * P]
                xdt = x_h * dt_col[:, h:h + 1]
                st = st_scr[h]
                y_h = jnp.dot(bf(cb * lm), bf(xdt), preferred_element_type=jnp.float32)
                y_h = y_h + jnp.dot(bf(cg * jnp.exp(ci)), bf(st), preferred_element_type=jnp.float32)
                y_h = y_h + dvec_ref[:, h * P:(h + 1) * P] * x_h
                y_scr[rows, h * P:(h + 1) * P] = y_h
                st_scr[h] = st * jnp.exp(c_last) + jnp.dot(bf(bgt), bf(xdt * jnp.exp(c_last - ci)),
                                                            preferred_element_type=jnp.float32)
    y = y_scr[...] * _silu(z_ref[0])
    y_ref[0] = y * lax.rsqrt(jnp.mean(y * y, axis=-1, keepdims=True) + NORM_EPS) * nw_ref[...]

    @pl.when(c == pl.num_programs(1) - 1)
    def _():
        cbo_ref[0] = conv_scr[...]
        so_ref[0] = st_scr[...]


def _ssd_mixer(proj3, conv0, s0, conv_w, conv_b, dt_bias, a_log, d_skip, norm_w, *, n_valid, tc):
    B, Tp, _ = proj3.shape
    G = GROUP_WIDTH
    H, P, N = SSD_HEADS, SSD_HEADDIM, SSD_STATE
    C = SSD_CONV_CH
    f32 = jnp.float32
    nchunk = Tp // tc
    assert Tp % tc == 0 and tc % SSD_CHUNK == 0 and (n_valid == tc or nchunk == 1)
    cb0 = jnp.pad(conv0, ((0, 0), (0, 8 - (CONV_W - 1)), (0, 0)))
    s0t = jnp.swapaxes(s0, -1, -2)
    prow = jnp.zeros((8, LANES), f32).at[0, :H].set(dt_bias).at[1, :H].set(a_log)
    pcol = jnp.zeros((8, LANES), f32).at[:H, 0].set(dt_bias).at[:H, 1].set(a_log)
    dvec = jnp.repeat(d_skip, P)[None, :]
    full2 = lambda shp: pl.BlockSpec(shp, lambda b, c: (0, 0))
    y, cbo, so = pl.pallas_call(
        functools.partial(_ssd_kernel, n_valid=n_valid),
        grid=(B, nchunk),
        in_specs=[pl.BlockSpec((1, tc, G), lambda b, c: (b, c, COL_Z)),
                  pl.BlockSpec((1, tc, C), lambda b, c: (b, c, COL_XBC)),
                  pl.BlockSpec((1, tc, G), lambda b, c: (b, c, COL_DT)),
                  pl.BlockSpec((1, 8, C), lambda b, c: (b, 0, 0)),
                  pl.BlockSpec((1, H, N, P), lambda b, c: (b, 0, 0, 0)),
                  full2((CONV_W, C)), full2((1, C)), full2((8, LANES)), full2((8, LANES)), full2((1, G)),
                  full2((1, G))],
        out_specs=[pl.BlockSpec((1, tc, G), lambda b, c: (b, c, 0)),
                   pl.BlockSpec((1, 8, C), lambda b, c: (b, 0, 0)),
                   pl.BlockSpec((1, H, N, P), lambda b, c: (b, 0, 0, 0))],
        out_shape=[jax.ShapeDtypeStruct((B, Tp, G), f32), jax.ShapeDtypeStruct((B, 8, C), f32),
                   jax.ShapeDtypeStruct((B, H, N, P), f32)],
        scratch_shapes=[pltpu.VMEM((8, C), f32), pltpu.VMEM((H, N, P), f32), pltpu.VMEM((tc, G), f32)],
        compiler_params=pltpu.CompilerParams(dimension_semantics=("parallel", "arbitrary"),
                                             vmem_limit_bytes=_VMEM_LIMIT),
        name="ssd_mixer",
    )(proj3, proj3, proj3, cb0, s0t, conv_w, conv_b[None, :], prow, pcol, dvec, norm_w[None, :])
    return y, cbo[:, :CONV_W - 1, :], jnp.swapaxes(so, -1, -2)


def _rwkv_pad_cols(a):
    G = GROUP_WIDTH
    z = lambda w: jnp.zeros(a.shape[:-1] + (w,), a.dtype)
    o = 3 * G
    wd = a[..., o:o + RWKV_W_RANK]
    ad = a[..., o + RWKV_W_RANK:o + RWKV_W_RANK + RWKV_A_RANK]
    gd = a[..., o + RWKV_W_RANK + RWKV_A_RANK:]
    return jnp.concatenate([a[..., :o], wd, z(LANES - RWKV_W_RANK), ad, z(LANES - RWKV_A_RANK),
                            gd, z(LANES - RWKV_G_RANK)], axis=-1)


def _rwkv_unpad_cols(a):
    o = 3 * GROUP_WIDTH
    return jnp.concatenate([a[..., :o], a[..., o:o + RWKV_W_RANK], a[..., o + LANES:o + LANES + RWKV_A_RANK],
                            a[..., o + 2 * LANES:o + 2 * LANES + RWKV_G_RANK]], axis=-1)


def _rwkv_kernel(pd_ref, sh0_ref, s0_ref, mu_ref, vec_ref, wup_ref, aup_ref, gup_ref, seg_ref,
                 y_ref, sout_ref, shout_ref,
                 prev_scr, s_scr, r_scr, k_scr, d_scr, kk_scr, bb_scr, g_scr, bonus_scr, vt_scr, yt_scr, *, n_steps):
    c = pl.program_id(1)
    tc = pd_ref.shape[1]
    G = GROUP_WIDTH
    hd = RWKV_HEADDIM

    @pl.when(c == 0)
    def _():
        prev_scr[...] = sh0_ref[...]
        s_scr[...] = s0_ref[...]

    nb = pd_ref.shape[0]
    w0 = vec_ref[0:1, :]
    a0 = vec_ref[1:2, :]
    k_k = vec_ref[2:3, :]
    k_a = vec_ref[3:4, :]
    ln_w = vec_ref[4:5, :]
    ln_b = vec_ref[5:6, :]
    r_k = vec_ref[6:7, :]
    seg = seg_ref[...]
    hdot = lambda x, y: jnp.dot(x, y, precision=_HI, preferred_element_type=jnp.float32)
    for ib in range(nb):
        cur = pd_ref[ib]
        row = lax.broadcasted_iota(jnp.int32, cur.shape, 0)
        prev = jnp.where(row == 0, prev_scr[ib], pltpu.roll(cur, 1, axis=0))
        prev_scr[ib] = cur[n_steps - 1:n_steps, :] if n_steps < tc else cur[tc - 1:tc, :]
        m = cur + (prev - cur) * mu_ref[...]
        r = m[:, 0:G]
        k = m[:, G:2 * G]
        v = m[:, 2 * G:3 * G]
        wd = m[:, 3 * G:3 * G + LANES]
        ad = m[:, 3 * G + LANES:3 * G + 2 * LANES]
        gd = m[:, 3 * G + 2 * LANES:3 * G + 3 * LANES]
        w = -_softplus(-(w0 + hdot(jnp.tanh(wd), wup_ref[...]))) - 0.5
        a = jax.nn.sigmoid(a0 + hdot(ad, aup_ref[...]))
        kkr = k * k_k
        kk = kkr / jnp.maximum(jnp.sqrt(hdot(kkr * kkr, seg)), 1e-12)
        k2 = k * (1.0 + (a - 1.0) * k_a)
        g_scr[ib] = hdot(jax.nn.sigmoid(gd), gup_ref[...])
        bonus_scr[ib] = hdot(r * k2 * r_k, seg) * v
        r_scr[ib] = r
        k_scr[ib] = k2
        d_scr[ib] = jnp.exp(-jnp.exp(w))
        kk_scr[ib] = -kk
        bb_scr[ib] = kk * a
        vt_scr[ib] = v.T
    yt_scr[...] = jnp.zeros_like(yt_scr)

    lane = lax.broadcasted_iota(jnp.int32, (hd, LANES), 1)
    lo_half = lane < hd

    def half_sums(p):
        lo = jnp.sum(jnp.where(lo_half, p, 0.0), axis=1, keepdims=True)
        hi = jnp.sum(jnp.where(lo_half, 0.0, p), axis=1, keepdims=True)
        return lo, hi

    npair = RWKV_HEADS // 2

    def step8(i, states):
        t0 = pl.multiple_of(i * 8, 8)
        blk = pl.multiple_of((t0 // LANES) * LANES, LANES)
        rows = pl.ds(t0, 8)
        chains = [(ib, p) for ib in range(nb) for p in range(npair)]
        states = list(states)
        rowv = {}
        tiles = {}
        for ci, (ib, p) in enumerate(chains):
            cols = slice(p * LANES, (p + 1) * LANES)
            rowv[ci] = tuple(ref[ib, rows, cols] for ref in (kk_scr, d_scr, bb_scr, k_scr, r_scr))
            tile = lambda ref, hh: ref[ib, pl.ds(p * LANES + hh * hd, hd), pl.ds(blk, LANES)]
            tiles[ci] = [tile(vt_scr, 0), tile(vt_scr, 1), tile(yt_scr, 0), tile(yt_scr, 1)]

        def vcol_of(ci, j):
            hit = lane == (t0 + j - blk)
            vc0 = jnp.sum(jnp.where(hit, tiles[ci][0], 0.0), axis=1, keepdims=True)
            vc1 = jnp.sum(jnp.where(hit, tiles[ci][1], 0.0), axis=1, keepdims=True)
            return jnp.where(lo_half, vc0, vc1)

        def emit_y(ci, j):
            hit = lane == (t0 + j - blk)
            ylo, yhi = half_sums(states[ci] * rowv[ci][4][j:j + 1, :])
            tiles[ci][2] = jnp.where(hit, ylo, tiles[ci][2])
            tiles[ci][3] = jnp.where(hit, yhi, tiles[ci][3])

        nj = min(8, n_steps)
        vcols = [vcol_of(ci, 0) for ci in range(len(chains))]
        for j in range(nj):
            sas = [half_sums(states[ci] * rowv[ci][0][j:j + 1, :]) for ci in range(len(chains))]
            if j > 0:
                for ci in range(len(chains)):
                    emit_y(ci, j - 1)
            nxt = [vcol_of(ci, j + 1) for ci in range(len(chains))] if j + 1 < nj else None
            for ci in range(len(chains)):
                _, d8, bb8, k8, _ = rowv[ci]
                sa = jnp.where(lo_half, sas[ci][0], sas[ci][1])
                states[ci] = states[ci] * d8[j:j + 1, :] + sa * bb8[j:j + 1, :] + vcols[ci] * k8[j:j + 1, :]
            vcols = nxt
        for ci in range(len(chains)):
            emit_y(ci, nj - 1)
        for ci, (ib, p) in enumerate(chains):
            yt_scr[ib, pl.ds(p * LANES, hd), pl.ds(blk, LANES)] = tiles[ci][2]
            yt_scr[ib, pl.ds(p * LANES + hd, hd), pl.ds(blk, LANES)] = tiles[ci][3]
        return tuple(states)

    init = tuple(s_scr[ib, p] for ib in range(nb) for p in range(npair))
    states = lax.fori_loop(0, -(-n_steps // 8), step8, init)
    for ib in range(nb):
        for p in range(npair):
            s_scr[ib, p] = states[ib * npair + p]

    inv = 1.0 / hd
    for ib in range(nb):
        y = yt_scr[ib].T
        mean = hdot(y, seg) * inv
        yc = y - mean
        var = hdot(yc * yc, seg) * inv
        yn = yc * lax.rsqrt(var + RWKV_LN_EPS) * ln_w + ln_b
        y_ref[ib] = (yn + bonus_scr[ib]) * g_scr[ib]

    @pl.when(c == pl.num_programs(1) - 1)
    def _():
        sout_ref[...] = s_scr[...]
        shout_ref[...] = prev_scr[...]


def _rwkv_mixer(proj3, shift0, s0, mu, w0, w_up, a0, a_up, g_up, k_k, k_a, r_k, ln_w, ln_b, *, n_steps, tc, nb=1):
    B, Tp, _ = proj3.shape
    G = GROUP_WIDTH
    H, hd = RWKV_HEADS, RWKV_HEADDIM
    nchunk = Tp // tc
    assert Tp % tc == 0 and (n_steps == tc or nchunk == 1) and (n_steps % 8 == 0 or n_steps < 8)
    assert B % nb == 0
    f32 = jnp.float32
    sh0 = _rwkv_pad_cols(shift0)[:, None, :]
    s0p = s0.reshape(B, H // 2, 2, hd, hd).transpose(0, 1, 3, 2, 4).reshape(B, H // 2, hd, 2 * hd)
    mu_p = _rwkv_pad_cols(mu[None, :])
    vec = jnp.stack([w0, a0, k_k, k_a, ln_w, ln_b, r_k.reshape(G), jnp.zeros((G,), f32)])
    padk = lambda wgt: jnp.pad(wgt, ((0, LANES - wgt.shape[0]), (0, 0)))
    hid = jnp.arange(G) // hd
    seg = (hid[:, None] == hid[None, :]).astype(f32)
    full2 = lambda shp: pl.BlockSpec(shp, lambda b, c: (0, 0))
    tscr = lambda: pltpu.VMEM((nb, tc, G), f32)
    y, s_out, sh_out = pl.pallas_call(
        functools.partial(_rwkv_kernel, n_steps=n_steps),
        grid=(B // nb, nchunk),
        in_specs=[pl.BlockSpec((nb, tc, RWKV_PCOLS), lambda b, c: (b, c, COL_RWKV)),
                  pl.BlockSpec((nb, 1, RWKV_PCOLS), lambda b, c: (b, 0, 0)),
                  pl.BlockSpec((nb, H // 2, hd, 2 * hd), lambda b, c: (b, 0, 0, 0)),
                  full2((1, RWKV_PCOLS)), full2((8, G)), full2((LANES, G)), full2((LANES, G)), full2((LANES, G)),
                  full2((G, G))],
        out_specs=[pl.BlockSpec((nb, tc, G), lambda b, c: (b, c, 0)),
                   pl.BlockSpec((nb, H // 2, hd, 2 * hd), lambda b, c: (b, 0, 0, 0)),
                   pl.BlockSpec((nb, 1, RWKV_PCOLS), lambda b, c: (b, 0, 0))],
        out_shape=[jax.ShapeDtypeStruct((B, Tp, G), f32),
                   jax.ShapeDtypeStruct((B, H // 2, hd, 2 * hd), f32),
                   jax.ShapeDtypeStruct((B, 1, RWKV_PCOLS), f32)],
        scratch_shapes=[pltpu.VMEM((nb, 1, RWKV_PCOLS), f32), pltpu.VMEM((nb, H // 2, hd, 2 * hd), f32),
                        tscr(), tscr(), tscr(), tscr(), tscr(), tscr(), tscr(),
                        pltpu.VMEM((nb, G, tc), f32), pltpu.VMEM((nb, G, tc), f32)],
        compiler_params=pltpu.CompilerParams(dimension_semantics=("parallel", "arbitrary"),
                                             vmem_limit_bytes=_VMEM_LIMIT),
        name="rwkv_mixer",
    )(proj3, sh0, s0p, mu_p, vec, padk(w_up), padk(a_up), padk(g_up), seg)
    s_new = s_out.reshape(B, H // 2, hd, 2, hd).transpose(0, 1, 3, 2, 4).reshape(B, H, hd, hd)
    return y, _rwkv_unpad_cols(sh_out[:, 0, :]), s_new


def _xattn_kernel(x_ref, g_ref, wq_ref, wo_ref, mk_ref, mv_ref, o_ref, cat_scr):
    x = x_ref[...]
    hn = x * lax.rsqrt(jnp.mean(x * x, axis=-1, keepdims=True) + NORM_EPS) * g_ref[...]
    q = jnp.dot(hn.astype(jnp.bfloat16), wq_ref[...], preferred_element_type=jnp.float32)
    qb = q.astype(jnp.bfloat16)
    mk = mk_ref[0].astype(jnp.bfloat16)
    mv = mv_ref[0].astype(jnp.bfloat16)
    hd = X_HEADDIM
    scale = hd ** -0.5
    for h in range(X_HEADS):
        cols = slice(h * hd, (h + 1) * hd)
        s = lax.dot_general(qb[:, cols], mk[:, cols], (((1,), (1,)), ((), ())),
                            preferred_element_type=jnp.float32) * scale
        m = jnp.max(s, axis=-1, keepdims=True)
        p = jnp.exp(s - m)
        p = p / jnp.sum(p, axis=-1, keepdims=True)
        cat_scr[:, cols] = jnp.dot(p.astype(jnp.bfloat16), mv[:, cols], preferred_element_type=jnp.float32)
    o_ref[...] = x + jnp.dot(cat_scr[...].astype(jnp.bfloat16), wo_ref[...], preferred_element_type=jnp.float32)


def _xattn_block(x, gain, wq, wo, mk, mv, *, tb, rows_per_mem):
    n = x.shape[0]
    M = mk.shape[1]
    assert n % tb == 0 and rows_per_mem % tb == 0
    per = rows_per_mem // tb
    return pl.pallas_call(
        _xattn_kernel,
        grid=(n // tb,),
        in_specs=[pl.BlockSpec((tb, D_MODEL), lambda i: (i, 0)),
                  pl.BlockSpec((1, D_MODEL), lambda i: (0, 0)),
                  pl.BlockSpec((D_MODEL, D_MODEL), lambda i: (0, 0)),
                  pl.BlockSpec((D_MODEL, D_MODEL), lambda i: (0, 0)),
                  pl.BlockSpec((1, M, D_MODEL), lambda i: (i // per, 0, 0)),
                  pl.BlockSpec((1, M, D_MODEL), lambda i: (i // per, 0, 0))],
        out_specs=pl.BlockSpec((tb, D_MODEL), lambda i: (i, 0)),
        out_shape=jax.ShapeDtypeStruct((n, D_MODEL), jnp.float32),
        scratch_shapes=[pltpu.VMEM((tb, D_MODEL), jnp.float32)],
        compiler_params=pltpu.CompilerParams(dimension_semantics=("parallel",), vmem_limit_bytes=_VMEM_LIMIT),
        name="xattn_block",
    )(x, gain.reshape(1, -1), wq.astype(jnp.bfloat16), wo.astype(jnp.bfloat16), mk, mv)


def _top_values(s, count, with_rank=False):
    vals = []
    cur = s
    rank = jnp.full(s.shape, float(count), jnp.float32) if with_rank else None
    for r in range(count):
        m = jnp.max(cur, axis=0, keepdims=True)
        vals.append(m)
        hit = cur == m
        if with_rank:
            rank = jnp.where(hit, float(r), rank)
        cur = jnp.where(hit, NEG_INF, cur)
    return (vals, rank) if with_rank else vals


def _peer_route_kernel(x_ref, g_ref, wqt_ref, sk_ref, hn_ref, cnt_ref, e1_ref, r2_ref, e2_ref, a1_scr, a2_scr):
    tb = x_ref.shape[0]
    x = x_ref[...]
    hn = x * lax.rsqrt(jnp.mean(x * x, axis=-1, keepdims=True) + NORM_EPS) * g_ref[...]
    hb = hn.astype(jnp.bfloat16)
    hn_ref[...] = hb
    qt = lax.dot_general(wqt_ref[...], hb, (((1,), (1,)), ((), ())), preferred_element_type=jnp.float32)
    n_top = PEER_TOPK + 1
    row8 = lax.broadcasted_iota(jnp.int32, (8, tb), 0)
    a1_scr[...] = jnp.full(a1_scr.shape, NEG_INF, jnp.float32)
    a2_scr[...] = jnp.full(a2_scr.shape, NEG_INF, jnp.float32)
    for h in range(PEER_HEADS):
        q1 = qt[(2 * h) * PEER_HALF:(2 * h + 1) * PEER_HALF, :].astype(jnp.bfloat16)
        q2 = qt[(2 * h + 1) * PEER_HALF:(2 * h + 2) * PEER_HALF, :].astype(jnp.bfloat16)
        s1 = jnp.dot(sk_ref[2 * h], q1, preferred_element_type=jnp.float32)
        s2 = jnp.dot(sk_ref[2 * h + 1], q2, preferred_element_type=jnp.float32)
        v1 = _top_values(s1, n_top)
        v2, rank2 = _top_values(s2, n_top, with_rank=True)
        for r in range(n_top):
            a1_scr[r:r + 1, :] = v1[r]
            a2_scr[r:r + 1, :] = v2[r]
        a1 = a1_scr[...]
        a2 = a2_scr[...]
        pieces = [a1 + a2[0:1, :]]
        for q in range(1, 8):
            lim = n_top // (q + 1)
            pieces.append(jnp.where(row8 < lim, a1[0:8, :] + a2[q:q + 1, :], NEG_INF))
        pieces.append(a1[0:1, :] + a2[8:24, :])
        cand = jnp.concatenate(pieces, axis=0)
        c = _top_values(cand, n_top)
        tau = 0.5 * (c[PEER_TOPK - 1] + c[PEER_TOPK])
        mx = c[0]
        z = jnp.sum(jnp.where(cand >= tau, jnp.exp(cand - mx), 0.0), axis=0, keepdims=True)
        cnt = jnp.zeros(s1.shape, jnp.float32)
        for q in range(n_top):
            cnt = cnt + jnp.where(s1 + v2[q] >= tau, 1.0, 0.0)
        cnt_ref[h] = cnt
        e1_ref[h] = jnp.exp(s1 - v1[0]) / z
        r2_ref[h] = rank2.astype(jnp.bfloat16)
        e2_ref[h] = jnp.exp(s2 - v2[0]).astype(jnp.bfloat16)


def _peer_route(x, gain, wqt_b, sk_b, tb=256):
    n = x.shape[0]
    assert n % tb == 0
    hk = PEER_HEADS
    blk3 = pl.BlockSpec((hk, PEER_NKEYS, tb), lambda i: (0, 0, i))
    shp_f = jax.ShapeDtypeStruct((hk, PEER_NKEYS, n), jnp.float32)
    shp_b = jax.ShapeDtypeStruct((hk, PEER_NKEYS, n), jnp.bfloat16)
    return pl.pallas_call(
        _peer_route_kernel,
        grid=(n // tb,),
        in_specs=[pl.BlockSpec((tb, D_MODEL), lambda i: (i, 0)),
                  pl.BlockSpec((1, D_MODEL), lambda i: (0, 0)),
                  pl.BlockSpec((2 * hk * PEER_HALF, D_MODEL), lambda i: (0, 0)),
                  pl.BlockSpec((2 * hk, PEER_NKEYS, PEER_HALF), lambda i: (0, 0, 0))],
        out_specs=[pl.BlockSpec((tb, D_MODEL), lambda i: (i, 0)), blk3, blk3, blk3, blk3],
        out_shape=[jax.ShapeDtypeStruct((n, D_MODEL), jnp.bfloat16), shp_f, shp_f, shp_b, shp_b],
        scratch_shapes=[pltpu.VMEM((24, tb), jnp.float32), pltpu.VMEM((24, tb), jnp.float32)],
        compiler_params=pltpu.CompilerParams(dimension_semantics=("parallel",), vmem_limit_bytes=_VMEM_LIMIT),
        name="peer_route",
    )(x, gain, wqt_b, sk_b)


def _peer_dense_kernel(x_ref, hn_ref, u_ref, v_ref, cnt_ref, e1_ref, r2_ref, e2_ref, o_ref, acc_ref, w_ref, *, ic):
    e = pl.program_id(1)
    tb = x_ref.shape[0]

    @pl.when(e == 0)
    def _():
        acc_ref[...] = jnp.zeros_like(acc_ref)

    act = lax.dot_general(u_ref[...], hn_ref[...], (((1,), (1,)), ((), ())), preferred_element_type=jnp.float32)
    for il in range(ic):
        g = None
        for h in range(PEER_HEADS):
            cb = jnp.broadcast_to(cnt_ref[h, il:il + 1, :], (16, tb)).astype(jnp.bfloat16)[None]
            eb = jnp.broadcast_to(e1_ref[h, il:il + 1, :], (16, tb)).astype(jnp.bfloat16)[None]
            r2 = r2_ref[h].reshape(PEER_NKEYS // 16, 16, tb)
            e2 = e2_ref[h].reshape(PEER_NKEYS // 16, 16, tb)
            t = jnp.where(r2 < cb, e2 * eb, jnp.zeros_like(e2))
            g = t if g is None else g + t
        a = act[il * PEER_NKEYS:(il + 1) * PEER_NKEYS, :]
        gf = g.reshape(PEER_NKEYS, tb).astype(jnp.float32)
        w_ref[il * PEER_NKEYS:(il + 1) * PEER_NKEYS, :] = (gf * _gelu_tanh(a)).astype(jnp.bfloat16)
    acc_ref[...] += lax.dot_general(w_ref[...], v_ref[...], (((0,), (0,)), ((), ())),
                                    preferred_element_type=jnp.float32)

    @pl.when(e == pl.num_programs(1) - 1)
    def _():
        o_ref[...] = x_ref[...] + acc_ref[...]


def _peer_dense(x, hn_b, u_b, v_b, cnt, e1, r2, e2, tb=PEER_TOK_PAD, ic=16):
    n = x.shape[0]
    ne = u_b.shape[0]
    ec = ic * PEER_NKEYS
    assert n % tb == 0 and ne % ec == 0
    hk = PEER_HEADS
    row_blk = pl.BlockSpec((hk, ic, tb), lambda i, e: (0, e, i))
    full_blk = pl.BlockSpec((hk, PEER_NKEYS, tb), lambda i, e: (0, 0, i))
    return pl.pallas_call(
        functools.partial(_peer_dense_kernel, ic=ic),
        grid=(n // tb, ne // ec),
        in_specs=[pl.BlockSpec((tb, D_MODEL), lambda i, e: (i, 0)),
                  pl.BlockSpec((tb, D_MODEL), lambda i, e: (i, 0)),
                  pl.BlockSpec((ec, D_MODEL), lambda i, e: (e, 0)),
                  pl.BlockSpec((ec, D_MODEL), lambda i, e: (e, 0)),
                  row_blk, row_blk, full_blk, full_blk],
        out_specs=pl.BlockSpec((tb, D_MODEL), lambda i, e: (i, 0)),
        out_shape=jax.ShapeDtypeStruct((n, D_MODEL), jnp.float32),
        scratch_shapes=[pltpu.VMEM((tb, D_MODEL), jnp.float32), pltpu.VMEM((ec, tb), jnp.bfloat16)],
        compiler_params=pltpu.CompilerParams(dimension_semantics=("parallel", "arbitrary"),
                                             vmem_limit_bytes=_VMEM_LIMIT),
        name="peer_dense",
    )(x, hn_b, u_b, v_b, cnt, e1, r2, e2)


def _peer_block(x, gain, wq, subkeys, u_tab, v_tab):
    wqt_b = wq.T.astype(jnp.bfloat16)
    sk_b = subkeys.reshape(2 * PEER_HEADS, PEER_NKEYS, PEER_HALF).astype(jnp.bfloat16)
    u_b = u_tab.astype(jnp.bfloat16)
    v_b = v_tab.astype(jnp.bfloat16)
    hn_b, cnt, e1, r2, e2 = _peer_route(x, gain.reshape(1, -1), wqt_b, sk_b)
    return _peer_dense(x, hn_b, u_b, v_b, cnt, e1, r2, e2)


def _in_weight(w_in):
    abc = w_in[:, :D_OFF]
    pad = jnp.zeros((w_in.shape[0], ABC_PCOLS - D_OFF), w_in.dtype)
    return jnp.concatenate([abc, pad, _rwkv_pad_cols(w_in[:, D_OFF:])], axis=1)


def _layer_pre_peer(x2, bsz, T, lp, st, prompt):
    G = GROUP_WIDTH
    n = bsz * T
    proj = _mm(x2, lp['w_in_p'], gain=lp['norm_mix'])
    if prompt:
        n_valid, tc = SEQ_CHUNK, SEQ_CHUNK
        proj3 = proj.reshape(bsz, T, -1)
        ya, k_r, v = _moba_prompt(proj3, jnp.arange(T, dtype=jnp.int32))
    else:
        assert T == 1
        n_valid, tc = T, DEC_ROWS
        ya, k_r = _moba_decode(proj[:, 0:G], proj[:, G:2 * G], proj[:, 2 * G:3 * G],
                               st['k_cache'], st['v_cache'], st['page_table'], PAST_LEN)
        v = proj[:, 2 * G:3 * G]
        proj3 = jnp.pad(proj[:, None, :], ((0, 0), (0, tc - T), (0, 0)))
    yb, lru_conv, lru_h = _lru_mixer(proj3, st['lru_conv'], st['lru_h'], lp['lru_conv_w'], lp['lru_conv_b'],
                                     lp['lru_wa'], lp['lru_ba'], lp['lru_wx'], lp['lru_bx'], lp['lru_lambda'],
                                     n_valid=n_valid, tc=tc)
    yc, ssd_conv, ssd_s = _ssd_mixer(proj3, st['ssd_conv'], st['ssd'], lp['ssd_conv_w'], lp['ssd_conv_b'],
                                     lp['ssd_dt_bias'], lp['ssd_a_log'], lp['ssd_d'], lp['ssd_norm'],
                                     n_valid=n_valid, tc=tc)
    yd, rwkv_shift, rwkv_s = _rwkv_mixer(proj3, st['rwkv_shift'], st['rwkv'],
                                         lp['rwkv_mu'], lp['rwkv_w0'], lp['rwkv_w_up'], lp['rwkv_a0'],
                                         lp['rwkv_a_up'], lp['rwkv_g_up'], lp['rwkv_k_k'], lp['rwkv_k_a'],
                                         lp['rwkv_r_k'], lp['rwkv_ln_w'], lp['rwkv_ln_b'], n_steps=n_valid, tc=tc)
    rows = lambda y: y[:, :T].reshape(n, G)
    x2 = _mm([ya.reshape(n, G), rows(yb), rows(yc), rows(yd)], lp['w_out'], residual=x2)
    if prompt:
        x2 = _xattn_block(x2, lp['norm_x'], lp['x_wq'], lp['x_wo'], st['mem_k'], st['mem_v'],
                          tb=SEQ_CHUNK, rows_per_mem=T)
    else:
        x8 = jnp.pad(x2[:, None, :], ((0, 0), (0, 7), (0, 0))).reshape(n * 8, D_MODEL)
        x8 = _xattn_block(x8, lp['norm_x'], lp['x_wq'], lp['x_wo'], st['mem_k'], st['mem_v'], tb=8, rows_per_mem=8)
        x2 = x8.reshape(n, 8, D_MODEL)[:, 0, :]
    heads = lambda t: t.reshape(bsz, T, A_HEADS, HEAD_DIM)
    new = {'k': heads(k_r), 'v': heads(v), 'lru_h': lru_h, 'lru_conv': lru_conv, 'ssd': ssd_s, 'ssd_conv': ssd_conv,
           'rwkv': rwkv_s, 'rwkv_shift': rwkv_shift}
    return x2, new


def kernel(x_prompt, x_sample, mem_prompt, cache_moba_k, cache_moba_v, page_table, state_lru_h, state_lru_conv, state_ssd, state_ssd_conv, state_rwkv, state_rwkv_shift, cache_mem_k, cache_mem_v, norm_mix, w_in, w_out, lru_conv_w, lru_conv_b, lru_wa, lru_ba, lru_wx, lru_bx, lru_lambda, ssd_conv_w, ssd_conv_b, ssd_dt_bias, ssd_a_log, ssd_d, ssd_norm, rwkv_mu, rwkv_w0, rwkv_w_up, rwkv_a0, rwkv_a_up, rwkv_g_up, rwkv_k_k, rwkv_k_a, rwkv_r_k, rwkv_ln_w, rwkv_ln_b, norm_x, x_wq, x_wk, x_wv, x_wo, norm_ffn, peer_wq, peer_subkeys, peer_u, peer_v, final_norm):
    bp, tp, _ = x_prompt.shape
    bd, td, _ = x_sample.shape
    n_p, n_s = bp * tp, bd * td
    n_all = -(-(n_p + n_s) // PEER_TOK_PAD) * PEER_TOK_PAD
    f32 = jnp.float32
    xp2 = x_prompt.reshape(n_p, D_MODEL)
    xs2 = x_sample.reshape(n_s, D_MODEL)
    names = ('k', 'v', 'lru_h', 'lru_conv', 'ssd', 'ssd_conv', 'rwkv', 'rwkv_shift')
    p_new = {n: [] for n in names + ('mem_k', 'mem_v')}
    s_new = {n: [] for n in names}
    mem2 = mem_prompt.reshape(bp * MEM_LEN, D_MODEL)
    for l in range(DEPTH):
        lp = {
            'norm_mix': norm_mix[l], 'w_in_p': _in_weight(w_in[l]), 'w_out': w_out[l],
            'lru_conv_w': lru_conv_w[l], 'lru_conv_b': lru_conv_b[l], 'lru_wa': lru_wa[l], 'lru_ba': lru_ba[l],
            'lru_wx': lru_wx[l], 'lru_bx': lru_bx[l], 'lru_lambda': lru_lambda[l],
            'ssd_conv_w': ssd_conv_w[l], 'ssd_conv_b': ssd_conv_b[l], 'ssd_dt_bias': ssd_dt_bias[l],
            'ssd_a_log': ssd_a_log[l], 'ssd_d': ssd_d[l], 'ssd_norm': ssd_norm[l],
            'rwkv_mu': rwkv_mu[l], 'rwkv_w0': rwkv_w0[l], 'rwkv_w_up': rwkv_w_up[l], 'rwkv_a0': rwkv_a0[l],
            'rwkv_a_up': rwkv_a_up[l], 'rwkv_g_up': rwkv_g_up[l], 'rwkv_k_k': rwkv_k_k[l], 'rwkv_k_a': rwkv_k_a[l],
            'rwkv_r_k': rwkv_r_k[l], 'rwkv_ln_w': rwkv_ln_w[l], 'rwkv_ln_b': rwkv_ln_b[l],
            'norm_x': norm_x[l], 'x_wq': x_wq[l], 'x_wo': x_wo[l],
        }
        mk = _mm(mem2, x_wk[l]).reshape(bp, MEM_LEN, D_MODEL)
        mv = _mm(mem2, x_wv[l]).reshape(bp, MEM_LEN, D_MODEL)
        st_p = {
            'lru_conv': jnp.zeros((bp, CONV_W - 1, LRU_WIDTH), f32),
            'lru_h': jnp.zeros((bp, LRU_WIDTH), f32),
            'ssd_conv': jnp.zeros((bp, CONV_W - 1, SSD_CONV_CH), f32),
            'ssd': jnp.zeros((bp, SSD_HEADS, SSD_HEADDIM, SSD_STATE), f32),
            'rwkv_shift': jnp.zeros((bp, RWKV_COLS), f32),
            'rwkv': jnp.zeros((bp, RWKV_HEADS, RWKV_HEADDIM, RWKV_HEADDIM), f32),
            'mem_k': mk, 'mem_v': mv,
        }
        xp2, npl = _layer_pre_peer(xp2, bp, tp, lp, st_p, True)
        for n in names:
            p_new[n].append(npl[n])
        p_new['mem_k'].append(mk.reshape(bp, MEM_LEN, X_HEADS, X_HEADDIM))
        p_new['mem_v'].append(mv.reshape(bp, MEM_LEN, X_HEADS, X_HEADDIM))
        st_s = {
            'k_cache': cache_moba_k.reshape((-1,) + cache_moba_k.shape[2:]),
            'v_cache': cache_moba_v.reshape((-1,) + cache_moba_v.shape[2:]),
            'page_table': page_table + l * cache_moba_k.shape[1],
            'lru_conv': state_lru_conv[l], 'lru_h': state_lru_h[l],
            'ssd_conv': state_ssd_conv[l], 'ssd': state_ssd[l],
            'rwkv_shift': state_rwkv_shift[l], 'rwkv': state_rwkv[l],
            'mem_k': cache_mem_k[l].reshape(bd, MEM_LEN, D_MODEL), 'mem_v': cache_mem_v[l].reshape(bd, MEM_LEN, D_MODEL),
        }
        xs2, nsl = _layer_pre_peer(xs2, bd, td, lp, st_s, False)
        for n in names:
            s_new[n].append(nsl[n])
        x_all = jnp.concatenate([xp2, xs2, jnp.zeros((n_all - n_p - n_s, D_MODEL), f32)], axis=0)
        x_all = _peer_block(x_all, norm_ffn[l], peer_wq[l], peer_subkeys[l], peer_u[l], peer_v[l])
        xp2 = x_all[:n_p]
        xs2 = x_all[n_p:n_p + n_s]
    y_prompt = _rmsnorm(xp2, final_norm).reshape(bp, tp, D_MODEL)
    y_sample = _rmsnorm(xs2, final_norm).reshape(bd, td, D_MODEL)
    return (y_prompt, y_sample,
            jnp.stack(p_new['k']), jnp.stack(p_new['v']), jnp.stack(p_new['lru_h']), jnp.stack(p_new['lru_conv']),
            jnp.stack(p_new['ssd']), jnp.stack(p_new['ssd_conv']), jnp.stack(p_new['rwkv']), jnp.stack(p_new['rwkv_shift']),
            jnp.stack(p_new['mem_k']), jnp.stack(p_new['mem_v']),
            jnp.stack(s_new['k']), jnp.stack(s_new['v']), jnp.stack(s_new['lru_h']), jnp.stack(s_new['lru_conv']),
            jnp.stack(s_new['ssd']), jnp.stack(s_new['ssd_conv']), jnp.stack(s_new['rwkv']), jnp.stack(s_new['rwkv_shift']))
```

```python
import functools

import jax
import jax.numpy as jnp
from jax import lax
from jax.experimental import pallas as pl
from jax.experimental.pallas import tpu as pltpu

D_MODEL = 1024
DEPTH = 2
PAST_LEN = 16384
PAGE_SIZE = 128
GROUP_WIDTH = 256
HEAD_DIM = 64
A_HEADS = GROUP_WIDTH // HEAD_DIM
MOBA_BLOCK = 256
MOBA_TOPK = 3
ROPE_THETA = 10000.0
LRU_WIDTH = GROUP_WIDTH
LRU_C = 8.0
CONV_W = 4
SSD_HEADS = 4
SSD_HEADDIM = GROUP_WIDTH // SSD_HEADS
SSD_GROUPS = 2
SSD_STATE = 64
SSD_CHUNK = 128
SSD_CONV_CH = GROUP_WIDTH + 2 * SSD_GROUPS * SSD_STATE
RWKV_HEADS = 4
RWKV_HEADDIM = GROUP_WIDTH // RWKV_HEADS
RWKV_W_RANK = 32
RWKV_A_RANK = 32
RWKV_G_RANK = 64
RWKV_COLS = 3 * GROUP_WIDTH + RWKV_W_RANK + RWKV_A_RANK + RWKV_G_RANK
RWKV_LN_EPS = 64e-5
D_OFF = 3 * GROUP_WIDTH + 2 * LRU_WIDTH + GROUP_WIDTH + SSD_CONV_CH + SSD_HEADS
MEM_LEN = 256
X_HEADS = 4
X_HEADDIM = D_MODEL // X_HEADS
PEER_HEADS = 8
PEER_NKEYS = 128
PEER_HALF = 128
PEER_TOPK = 16
NORM_EPS = 1e-6
NEG_INF = -1e30

LANES = 128
RWKV_PCOLS = 3 * GROUP_WIDTH + 3 * LANES
ABC_PCOLS = 2 * RWKV_PCOLS
COL_Q, COL_K, COL_V, COL_U, COL_GATE, COL_Z = 0, 1, 2, 3, 4, 5
COL_XBC = 3
COL_DT = 8
COL_RWKV = 2
PAGES_PER_BLOCK = MOBA_BLOCK // PAGE_SIZE
KMEAN_PAGES = 16
PEER_TOK_PAD = 512
SEQ_CHUNK = 512
DEC_ROWS = 128
_VMEM_LIMIT = 56 * 1024 * 1024
_HI = lax.Precision.HIGHEST


def _softplus(x):
    return jnp.maximum(x, 0.0) + jnp.log1p(jnp.exp(-jnp.abs(x)))


def _silu(x):
    return x * jax.nn.sigmoid(x)


def _gelu_tanh(x):
    return 0.5 * x * (1.0 + jnp.tanh(0.7978845608028654 * (x + 0.044715 * (x * x * x))))


def _mm_kernel(*refs, n_x, has_norm, has_res):
    x_refs = refs[:n_x]
    w_ref = refs[n_x]
    pos = n_x + 1
    g_ref = r_ref = None
    if has_norm:
        g_ref = refs[pos]
        pos += 1
    if has_res:
        r_ref = refs[pos]
        pos += 1
    o_ref = refs[pos]
    y = None
    off = 0
    for x_ref in x_refs:
        x = x_ref[...]
        kx = x.shape[1]
        if has_norm:
            x = x * lax.rsqrt(jnp.mean(x * x, axis=-1, keepdims=True) + NORM_EPS) * g_ref[...]
        t = jnp.dot(x.astype(jnp.bfloat16), w_ref[off:off + kx, :], preferred_element_type=jnp.float32)
        y = t if y is None else y + t
        off += kx
    if has_res:
        y = y + r_ref[...]
    o_ref[...] = y


def _mm(xs, w, gain=None, residual=None, tm=512):
    if not isinstance(xs, (list, tuple)):
        xs = [xs]
    assert gain is None or len(xs) == 1
    M = xs[0].shape[0]
    K, N = w.shape
    tm = min(tm, M)
    assert M % tm == 0 and sum(x.shape[1] for x in xs) == K
    args = list(xs) + [w.astype(jnp.bfloat16)]
    in_specs = [pl.BlockSpec((tm, x.shape[1]), lambda i: (i, 0)) for x in xs]
    in_specs.append(pl.BlockSpec((K, N), lambda i: (0, 0)))
    if gain is not None:
        args.append(gain.reshape(1, K).astype(jnp.float32))
        in_specs.append(pl.BlockSpec((1, K), lambda i: (0, 0)))
    if residual is not None:
        args.append(residual)
        in_specs.append(pl.BlockSpec((tm, N), lambda i: (i, 0)))
    return pl.pallas_call(
        functools.partial(_mm_kernel, n_x=len(xs), has_norm=gain is not None, has_res=residual is not None),
        grid=(M // tm,),
        in_specs=in_specs,
        out_specs=pl.BlockSpec((tm, N), lambda i: (i, 0)),
        out_shape=jax.ShapeDtypeStruct((M, N), jnp.float32),
        compiler_params=pltpu.CompilerParams(dimension_semantics=("parallel",), vmem_limit_bytes=_VMEM_LIMIT),
        name="mm",
    )(*args)


def _rmsnorm(x, g):
    return x * lax.rsqrt(jnp.mean(x * x, axis=-1, keepdims=True) + NORM_EPS) * g


def _rope_tables(pos):
    half = HEAD_DIM // 2
    freq = 1.0 / (ROPE_THETA ** (jnp.arange(half, dtype=jnp.float32) / half))
    ang = pos.astype(jnp.float32)[:, None] * freq[None, :]
    cos = jnp.cos(ang)
    sin = jnp.sin(ang)
    cos_t = jnp.tile(jnp.concatenate([cos, cos], axis=-1), (1, A_HEADS))
    sin_t = jnp.tile(jnp.concatenate([-sin, sin], axis=-1), (1, A_HEADS))
    return cos_t, sin_t


def _rope_apply(x, cos_t, sin_t):
    half = HEAD_DIM // 2
    lane = lax.broadcasted_iota(jnp.int32, x.shape, 1)
    first = (lane % HEAD_DIM) < half
    w = x.shape[1]
    swapped = jnp.where(first, pltpu.roll(x, w - half, axis=1), pltpu.roll(x, half, axis=1))
    return x * cos_t + swapped * sin_t


def _moba_prompt_kernel(q_ref, k_ref, v_ref, cos_ref, sin_ref, ya_ref, kr_ref, vo_ref, ot_scr):
    T = q_ref.shape[1]
    G = GROUP_WIDTH
    hd = HEAD_DIM
    blk = MOBA_BLOCK
    nb = T // blk
    cos_t = cos_ref[...]
    sin_t = sin_ref[...]
    q = _rope_apply(q_ref[0], cos_t, sin_t)
    k = _rope_apply(k_ref[0], cos_t, sin_t)
    v = v_ref[0]
    kr_ref[0] = k
    vo_ref[0] = v
    kmean = jnp.concatenate([jnp.sum(k[n * blk:(n + 1) * blk, :], axis=0, keepdims=True) for n in range(nb)],
                            axis=0) * (1.0 / blk)
    qt = q.T.astype(jnp.bfloat16)
    vt = v.T.astype(jnp.bfloat16)
    kb = k.astype(jnp.bfloat16)
    lane_g = lax.broadcasted_iota(jnp.int32, (nb, G), 1)
    rown = lax.broadcasted_iota(jnp.int32, (nb, T), 0)
    cur = lax.broadcasted_iota(jnp.int32, (nb, T), 1) // blk
    kpos = lax.broadcasted_iota(jnp.int32, (blk, blk), 0)
    qpos = lax.broadcasted_iota(jnp.int32, (blk, blk), 1)
    causal = kpos <= qpos
    scale = hd ** -0.5
    for h in range(A_HEADS):
        km_h = jnp.where(lane_g // hd == h, kmean, 0.0)
        gate = lax.dot_general(km_h, q, (((1,), (1,)), ((), ())), precision=_HI,
                               preferred_element_type=jnp.float32)
        gate = jnp.where(rown < cur, gate, NEG_INF)
        sel_rows = []
        for n in range(nb):
            gn = gate[n:n + 1, :]
            beats = (gate > gn) | ((gate == gn) & (rown < n))
            rank = jnp.sum(beats.astype(jnp.float32), axis=0, keepdims=True)
            sel_rows.append(rank < MOBA_TOPK)
        k_h = kb[:, h * hd:(h + 1) * hd]
        qt_h = qt[h * hd:(h + 1) * hd, :]
        vt_h = vt[h * hd:(h + 1) * hd, :]
        for qi in range(nb):
            qs = slice(qi * blk, (qi + 1) * blk)
            q_blk = qt_h[:, qs]
            s = jnp.dot(k_h[qs, :], q_blk, preferred_element_type=jnp.float32) * scale
            s = jnp.where(causal, s, NEG_INF)
            m = jnp.max(s, axis=0, keepdims=True)
            p = jnp.exp(s - m)
            l = jnp.sum(p, axis=0, keepdims=True)
            acc = jnp.dot(vt_h[:, qs], p.astype(jnp.bfloat16), preferred_element_type=jnp.float32)
            for n in range(qi):
                ks = slice(n * blk, (n + 1) * blk)
                s = jnp.dot(k_h[ks, :], q_blk, preferred_element_type=jnp.float32) * scale
                s = jnp.where(sel_rows[n][:, qs], s, NEG_INF)
                m_new = jnp.maximum(m, jnp.max(s, axis=0, keepdims=True))
                alpha = jnp.exp(m - m_new)
                p = jnp.exp(s - m_new)
                l = l * alpha + jnp.sum(p, axis=0, keepdims=True)
                acc = acc * alpha + jnp.dot(vt_h[:, ks], p.astype(jnp.bfloat16), preferred_element_type=jnp.float32)
                m = m_new
            ot_scr[h * hd:(h + 1) * hd, qs] = acc / l
    ya_ref[0] = ot_scr[...].T


def _moba_prompt(proj3, pos):
    B, T, _ = proj3.shape
    G = GROUP_WIDTH
    cos_t, sin_t = _rope_tables(pos)
    col = lambda j: pl.BlockSpec((1, T, G), lambda b: (b, 0, j))
    tab = pl.BlockSpec((T, G), lambda b: (0, 0))
    out = pl.BlockSpec((1, T, G), lambda b: (b, 0, 0))
    shp = jax.ShapeDtypeStruct((B, T, G), jnp.float32)
    return pl.pallas_call(
        _moba_prompt_kernel,
        grid=(B,),
        in_specs=[col(COL_Q), col(COL_K), col(COL_V), tab, tab],
        out_specs=[out, out, out],
        out_shape=[shp, shp, shp],
        scratch_shapes=[pltpu.VMEM((G, T), jnp.float32)],
        compiler_params=pltpu.CompilerParams(dimension_semantics=("parallel",), vmem_limit_bytes=_VMEM_LIMIT),
        name="moba_prompt",
    )(proj3, proj3, proj3, cos_t, sin_t)


def _kmean_kernel(pt_ref, *refs):
    pages, o_ref = refs[:-1], refs[-1]
    j = pl.program_id(1)
    nblk = len(pages) // PAGES_PER_BLOCK

    @pl.when(j == 0)
    def _():
        o_ref[...] = jnp.zeros_like(o_ref)

    lane = lax.broadcasted_iota(jnp.int32, o_ref.shape[1:], 2)
    acc = o_ref[0]
    for blk in range(nblk):
        s = None
        for i in range(PAGES_PER_BLOCK):
            part = pages[blk * PAGES_PER_BLOCK + i][0]
            s = part if s is None else s + part
        mean = jnp.sum(s, axis=-1, keepdims=True) * (1.0 / MOBA_BLOCK)
        acc = jnp.where(lane == j * nblk + blk, mean, acc)
    o_ref[0] = acc


def _moba_block_means(cache_kt, page_table):
    B, n_pages = page_table.shape
    H, hd = A_HEADS, HEAD_DIM
    assert n_pages % KMEAN_PAGES == 0 and n_pages // PAGES_PER_BLOCK <= LANES
    steps = n_pages // KMEAN_PAGES

    def page_spec(i):
        return pl.BlockSpec((1, H, hd, PAGE_SIZE), lambda b, j, pt: (pt[b * n_pages + j * KMEAN_PAGES + i], 0, 0, 0))

    return pl.pallas_call(
        _kmean_kernel,
        grid_spec=pltpu.PrefetchScalarGridSpec(
            num_scalar_prefetch=1,
            grid=(B, steps),
            in_specs=[page_spec(i) for i in range(KMEAN_PAGES)],
            out_specs=pl.BlockSpec((1, H, hd, LANES), lambda b, j, pt: (b, 0, 0, 0)),
        ),
        out_shape=jax.ShapeDtypeStruct((B, H, hd, LANES), jnp.float32),
        compiler_params=pltpu.CompilerParams(dimension_semantics=("parallel", "arbitrary"),
                                             vmem_limit_bytes=_VMEM_LIMIT),
        name="moba_block_means",
    )(page_table.reshape(-1), *([cache_kt] * KMEAN_PAGES))


def _moba_select_kernel(q_ref, k_ref, cos_ref, sin_ref, km_ref, qr_ref, kr_ref, top_ref, *, nb):
    q = _rope_apply(q_ref[0], cos_ref[...], sin_ref[...])
    k = _rope_apply(k_ref[0], cos_ref[...], sin_ref[...])
    qr_ref[0] = q
    kr_ref[0] = k
    qcol = q.T[:, 0:1]
    lane = lax.broadcasted_iota(jnp.int32, (1, LANES), 1)
    row8 = lax.broadcasted_iota(jnp.int32, (8, LANES), 0)
    lane8 = lax.broadcasted_iota(jnp.int32, (8, LANES), 1)
    top = jnp.zeros((8, LANES), jnp.int32)
    for h in range(A_HEADS):
        gate = jnp.sum(km_ref[0, h] * qcol[h * HEAD_DIM:(h + 1) * HEAD_DIM, :], axis=0, keepdims=True)
        gate = jnp.where(lane < nb, gate, NEG_INF)
        for r in range(MOBA_TOPK):
            m = jnp.max(gate, axis=1, keepdims=True)
            idx = jnp.min(jnp.where(gate == m, lane, LANES), axis=1, keepdims=True)
            top = jnp.where((row8 == r) & (lane8 == h), idx, top)
            gate = jnp.where(lane == idx, NEG_INF, gate)
    top_ref[0] = top


def _moba_select(q8, k8, pos, kmean_t, nb):
    B = q8.shape[0]
    G = GROUP_WIDTH
    cos_t, sin_t = _rope_tables(pos)
    row = pl.BlockSpec((1, 8, G), lambda b: (b, 0, 0))
    tab = pl.BlockSpec((1, G), lambda b: (0, 0))
    return pl.pallas_call(
        functools.partial(_moba_select_kernel, nb=nb),
        grid=(B,),
        in_specs=[row, row, tab, tab, pl.BlockSpec((1, A_HEADS, HEAD_DIM, LANES), lambda b: (b, 0, 0, 0))],
        out_specs=[row, row, pl.BlockSpec((1, 8, LANES), lambda b: (b, 0, 0))],
        out_shape=[jax.ShapeDtypeStruct((B, 8, G), jnp.float32), jax.ShapeDtypeStruct((B, 8, G), jnp.float32),
                   jax.ShapeDtypeStruct((B, 8, LANES), jnp.int32)],
        compiler_params=pltpu.CompilerParams(dimension_semantics=("parallel",)),
        name="moba_select",
    )(q8, k8, cos_t, sin_t, kmean_t)


def _moba_decode_attn_kernel(pp_ref, q_ref, kn_ref, vn_ref, *refs):
    npg = MOBA_TOPK * PAGES_PER_BLOCK
    kp, vp, o_ref = refs[:npg], refs[npg:2 * npg], refs[2 * npg]
    bf = lambda a: a.astype(jnp.bfloat16)
    qh = bf(q_ref[0, 0])
    scale = HEAD_DIM ** -0.5
    s_pages = [jnp.dot(qh, bf(kp[i][0, 0]), preferred_element_type=jnp.float32) * scale for i in range(npg)]
    s_own = jnp.sum(qh.astype(jnp.float32) * bf(kn_ref[0, 0]).astype(jnp.float32), axis=-1, keepdims=True) * scale
    m = s_own
    for s in s_pages:
        m = jnp.maximum(m, jnp.max(s, axis=-1, keepdims=True))
    p_own = jnp.exp(s_own - m)
    l = p_own
    acc = p_own * bf(vn_ref[0, 0]).astype(jnp.float32)
    for i in range(npg):
        p = jnp.exp(s_pages[i] - m)
        l = l + jnp.sum(p, axis=-1, keepdims=True)
        acc = acc + lax.dot_general(bf(p), bf(vp[i][0, 0]), (((1,), (1,)), ((), ())),
                                    preferred_element_type=jnp.float32)
    o_ref[0, 0] = acc / l


def _moba_decode_attn(qh, kh, vh, cache_kt, cache_vt, phys):
    B = qh.shape[0]
    H, hd = A_HEADS, HEAD_DIM
    npg = MOBA_TOPK * PAGES_PER_BLOCK
    row = pl.BlockSpec((1, 1, 8, hd), lambda b, h, pp: (b, h, 0, 0))

    def page_spec(i):
        return pl.BlockSpec((1, 1, hd, PAGE_SIZE), lambda b, h, pp: (pp[(b * H + h) * npg + i], h, 0, 0))

    return pl.pallas_call(
        _moba_decode_attn_kernel,
        grid_spec=pltpu.PrefetchScalarGridSpec(
            num_scalar_prefetch=1,
            grid=(B, H),
            in_specs=[row, row, row] + [page_spec(i) for i in range(npg)] * 2,
            out_specs=row,
        ),
        out_shape=jax.ShapeDtypeStruct((B, H, 8, hd), jnp.float32),
        compiler_params=pltpu.CompilerParams(dimension_semantics=("parallel", "arbitrary")),
        name="moba_decode_attn",
    )(phys.reshape(-1), qh, kh, vh, *([cache_kt] * npg), *([cache_vt] * npg))


def _moba_decode(q, k, v, cache_kt, cache_vt, page_table, pos0):
    B = q.shape[0]
    H, hd = A_HEADS, HEAD_DIM
    nb = page_table.shape[1] // PAGES_PER_BLOCK
    pad8 = lambda a: jnp.pad(a[:, None, :], ((0, 0), (0, 7), (0, 0)))
    kmean_t = _moba_block_means(cache_kt, page_table)
    q8, k8, top = _moba_select(pad8(q), pad8(k), jnp.full((1,), pos0, jnp.int32), kmean_t, nb)
    blocks = jnp.swapaxes(top[:, :MOBA_TOPK, :H], 1, 2)
    pages = (blocks[..., None] * PAGES_PER_BLOCK + jnp.arange(PAGES_PER_BLOCK)).reshape(B, -1)
    phys = jnp.take_along_axis(page_table, pages, axis=1)
    by_head = lambda a8: a8.reshape(B, 8, H, hd).transpose(0, 2, 1, 3)
    out = _moba_decode_attn(by_head(q8), by_head(k8), by_head(pad8(v)), cache_kt, cache_vt, phys)
    return out[:, :, 0, :].reshape(B, H * hd), k8[:, 0, :]


def _causal_conv_tile(xs, prev3, w_ref, b_ref):
    row = lax.broadcasted_iota(jnp.int32, xs.shape, 0)
    y = b_ref[...] + xs * w_ref[CONV_W - 1:CONV_W, :]
    for k in range(1, CONV_W):
        sh = pltpu.roll(xs, k, axis=0)
        for r in range(k):
            src = r + (CONV_W - 1) - k
            sh = jnp.where(row == r, prev3[src:src + 1, :], sh)
        y = y + sh * w_ref[CONV_W - 1 - k:CONV_W - k, :]
    return y


def _conv_tail(xs, prev3, n_valid):
    rows = []
    for i in range(CONV_W - 1):
        idx = n_valid - (CONV_W - 1) + i
        rows.append(xs[idx:idx + 1, :] if idx >= 0 else prev3[idx + CONV_W - 1:idx + CONV_W, :])
    return rows


def _lru_kernel(u_ref, gate_ref, cb0_ref, h0_ref, cw_ref, cbias_ref, wa_ref, wx_ref, vec_ref,
                y_ref, cbo_ref, ho_ref, conv_scr, h_scr, a_scr, b_scr, hs_scr, *, n_valid):
    c = pl.program_id(1)
    tc = u_ref.shape[1]

    @pl.when(c == 0)
    def _():
        conv_scr[...] = cb0_ref[0]
        h_scr[...] = h0_ref[0]

    us = u_ref[0]
    prev3 = conv_scr[...]
    xc = _causal_conv_tile(us, prev3, cw_ref, cbias_ref)
    tail = _conv_tail(us, prev3, n_valid)
    for i in range(CONV_W - 1):
        conv_scr[i:i + 1, :] = tail[i]
    xb = xc.astype(jnp.bfloat16)
    r = jax.nn.sigmoid(jnp.dot(xb, wa_ref[...], preferred_element_type=jnp.float32) + vec_ref[0:1, :])
    i_g = jax.nn.sigmoid(jnp.dot(xb, wx_ref[...], preferred_element_type=jnp.float32) + vec_ref[1:2, :])
    log_a = -LRU_C * r * _softplus(-vec_ref[2:3, :])
    a_scr[...] = jnp.exp(log_a)
    b_scr[...] = jnp.sqrt(-jnp.tanh(log_a) * (jnp.exp(2.0 * log_a) + 1.0)) * (i_g * xc)

    row8 = lax.broadcasted_iota(jnp.int32, (8, LRU_WIDTH), 0)

    def step8(i, h):
        t0 = pl.multiple_of(i * 8, 8)
        a8 = a_scr[pl.ds(t0, 8), :]
        b8 = b_scr[pl.ds(t0, 8), :]
        out = jnp.zeros((8, LRU_WIDTH), jnp.float32)
        for j in range(min(8, n_valid)):
            h = a8[j:j + 1, :] * h + b8[j:j + 1, :]
            out = jnp.where(row8 == j, h, out)
        hs_scr[pl.ds(t0, 8), :] = out
        return h

    if n_valid < tc:
        hs_scr[...] = jnp.zeros_like(hs_scr)
    h = lax.fori_loop(0, -(-n_valid // 8), step8, h_scr[...])
    h_scr[...] = h
    y_ref[0] = hs_scr[...] * _gelu_tanh(gate_ref[0])

    @pl.when(c == pl.num_programs(1) - 1)
    def _():
        cbo_ref[0] = conv_scr[...]
        ho_ref[0] = h_scr[...]


def _block_diag(w):
    n, d, e = w.shape
    eye = jnp.eye(n, dtype=w.dtype)
    return (eye[:, None, :, None] * w[:, :, None, :]).reshape(n * d, n * e)


def _lru_mixer(proj3, conv0, h0, conv_w, conv_b, wa, ba, wx, bx, lam, *, n_valid, tc):
    B, Tp, _ = proj3.shape
    G = GROUP_WIDTH
    f32 = jnp.float32
    nchunk = Tp // tc
    assert Tp % tc == 0 and (n_valid == tc or nchunk == 1) and (n_valid % 8 == 0 or n_valid < 8)
    cb0 = jnp.pad(conv0, ((0, 0), (0, 8 - (CONV_W - 1)), (0, 0)))
    vec = jnp.zeros((8, G), f32).at[0].set(ba).at[1].set(bx).at[2].set(lam)
    full2 = lambda shp: pl.BlockSpec(shp, lambda b, c: (0, 0))
    y, cbo, ho = pl.pallas_call(
        functools.partial(_lru_kernel, n_valid=n_valid),
        grid=(B, nchunk),
        in_specs=[pl.BlockSpec((1, tc, G), lambda b, c: (b, c, COL_U)),
                  pl.BlockSpec((1, tc, G), lambda b, c: (b, c, COL_GATE)),
                  pl.BlockSpec((1, 8, G), lambda b, c: (b, 0, 0)),
                  pl.BlockSpec((1, 1, G), lambda b, c: (b, 0, 0)),
                  full2((CONV_W, G)), full2((1, G)), full2((G, G)), full2((G, G)), full2((8, G))],
        out_specs=[pl.BlockSpec((1, tc, G), lambda b, c: (b, c, 0)),
                   pl.BlockSpec((1, 8, G), lambda b, c: (b, 0, 0)),
                   pl.BlockSpec((1, 1, G), lambda b, c: (b, 0, 0))],
        out_shape=[jax.ShapeDtypeStruct((B, Tp, G), f32), jax.ShapeDtypeStruct((B, 8, G), f32),
                   jax.ShapeDtypeStruct((B, 1, G), f32)],
        scratch_shapes=[pltpu.VMEM((8, G), f32), pltpu.VMEM((1, G), f32),
                        pltpu.VMEM((tc, G), f32), pltpu.VMEM((tc, G), f32), pltpu.VMEM((tc, G), f32)],
        compiler_params=pltpu.CompilerParams(dimension_semantics=("parallel", "arbitrary"),
                                             vmem_limit_bytes=_VMEM_LIMIT),
        name="lru_mixer",
    )(proj3, proj3, cb0, h0[:, None, :], conv_w, conv_b[None, :],
      _block_diag(wa).astype(jnp.bfloat16), _block_diag(wx).astype(jnp.bfloat16), vec)
    return y, cbo[:, :CONV_W - 1, :], ho[:, 0, :]


def _ssd_kernel(z_ref, xbc_ref, dt_ref, cb0_ref, s0_ref, cw_ref, cbias_ref, prow_ref, pcol_ref, dvec_ref, nw_ref,
                y_ref, cbo_ref, so_ref, conv_scr, st_scr, y_scr, *, n_valid):
    c = pl.program_id(1)
    tc = xbc_ref.shape[1]
    Q = SSD_CHUNK
    G = GROUP_WIDTH
    P = SSD_HEADDIM
    N = SSD_STATE
    GN = SSD_GROUPS * SSD_STATE

    @pl.when(c == 0)
    def _():
        conv_scr[...] = cb0_ref[0]
        st_scr[...] = s0_ref[0]

    xs = xbc_ref[0]
    prev3 = conv_scr[...]
    xbc_f = _silu(_causal_conv_tile(xs, prev3, cw_ref, cbias_ref))
    tail = _conv_tail(xs, prev3, n_valid)
    for i in range(CONV_W - 1):
        conv_scr[i:i + 1, :] = tail[i]

    ii = lax.broadcasted_iota(jnp.int32, (Q, Q), 0)
    jj = lax.broadcasted_iota(jnp.int32, (Q, Q), 1)
    lower = ii >= jj
    tri_l = lower.astype(jnp.float32)
    tri_u = (ii <= jj).astype(jnp.float32)
    bias_row = prow_ref[0:1, :]
    alog_row = prow_ref[1:2, :]
    bias_col = pcol_ref[:, 0:1]
    alog_col = pcol_ref[:, 1:2]
    rowq = lax.broadcasted_iota(jnp.int32, (Q, LANES), 0)
    laneq = lax.broadcasted_iota(jnp.int32, (8, Q), 1)
    bf = lambda a: a.astype(jnp.bfloat16)
    for j in range(tc // Q):
        rows = slice(j * Q, (j + 1) * Q)
        xq = xbc_f[rows, :]
        xt = xq.T
        dcol = dt_ref[0, rows, 0:LANES]
        drow = dcol.T[0:8, :]
        dt_col = _softplus(dcol + bias_row)
        dt_row = _softplus(drow + bias_col)
        if n_valid < tc:
            dt_col = jnp.where(rowq + j * Q < n_valid, dt_col, 0.0)
            dt_row = jnp.where(laneq + j * Q < n_valid, dt_row, 0.0)
        da_col = dt_col * (-jnp.exp(alog_row))
        da_row = dt_row * (-jnp.exp(alog_col))
        cs_col = jnp.dot(tri_l, da_col, precision=_HI, preferred_element_type=jnp.float32)
        cs_row = jnp.dot(da_row, tri_u, precision=_HI, preferred_element_type=jnp.float32)
        for g in range(SSD_GROUPS):
            bg = xq[:, G + g * N:G + (g + 1) * N]
            cg = xq[:, G + GN + g * N:G + GN + (g + 1) * N]
            bgt = xt[G + g * N:G + (g + 1) * N, :]
            cb = lax.dot_general(bf(cg), bf(bg), (((1,), (1,)), ((), ())), preferred_element_type=jnp.float32)
            for h in range(g * (SSD_HEADS // SSD_GROUPS), (g + 1) * (SSD_HEADS // SSD_GROUPS)):
                ci = cs_col[:, h:h + 1]
                cj = cs_row[h:h + 1, :]
                c_last = cs_row[h:h + 1, Q - 1:Q]
                lm = jnp.exp(jnp.where(lower, ci - cj, NEG_INF))
                x_h = xq[:, h * P:(h + 1) * P]
                xdt = x_h * dt_col[:, h:h + 1]
                st = st_scr[h]
                y_h = jnp.dot(bf(cb * lm), bf(xdt), preferred_element_type=jnp.float32)
                y_h = y_h + jnp.dot(bf(cg * jnp.exp(ci)), bf(st), preferred_element_type=jnp.float32)
                y_h = y_h + dvec_ref[:, h * P:(h + 1) * P] * x_h
                y_scr[rows, h * P:(h + 1) * P] = y_h
                st_scr[h] = st * jnp.exp(c_last) + jnp.dot(bf(bgt), bf(xdt * jnp.exp(c_last - ci)),
                                                            preferred_element_type=jnp.float32)
    y = y_scr[...] * _silu(z_ref[0])
    y_ref[0] = y * lax.rsqrt(jnp.mean(y * y, axis=-1, keepdims=True) + NORM_EPS) * nw_ref[...]

    @pl.when(c == pl.num_programs(1) - 1)
    def _():
        cbo_ref[0] = conv_scr[...]
        so_ref[0] = st_scr[...]


def _ssd_mixer(proj3, conv0, s0, conv_w, conv_b, dt_bias, a_log, d_skip, norm_w, *, n_valid, tc):
    B, Tp, _ = proj3.shape
    G = GROUP_WIDTH
    H, P, N = SSD_HEADS, SSD_HEADDIM, SSD_STATE
    C = SSD_CONV_CH
    f32 = jnp.float32
    nchunk = Tp // tc
    assert Tp % tc == 0 and tc % SSD_CHUNK == 0 and (n_valid == tc or nchunk == 1)
    cb0 = jnp.pad(conv0, ((0, 0), (0, 8 - (CONV_W - 1)), (0, 0)))
    s0t = jnp.swapaxes(s0, -1, -2)
    prow = jnp.zeros((8, LANES), f32).at[0, :H].set(dt_bias).at[1, :H].set(a_log)
    pcol = jnp.zeros((8, LANES), f32).at[:H, 0].set(dt_bias).at[:H, 1].set(a_log)
    dvec = jnp.repeat(d_skip, P)[None, :]
    full2 = lambda shp: pl.BlockSpec(shp, lambda b, c: (0, 0))
    y, cbo, so = pl.pallas_call(
        functools.partial(_ssd_kernel, n_valid=n_valid),
        grid=(B, nchunk),
        in_specs=[pl.BlockSpec((1, tc, G), lambda b, c: (b, c, COL_Z)),
                  pl.BlockSpec((1, tc, C), lambda b, c: (b, c, COL_XBC)),
                  pl.BlockSpec((1, tc, G), lambda b, c: (b, c, COL_DT)),
                  pl.BlockSpec((1, 8, C), lambda b, c: (b, 0, 0)),
                  pl.BlockSpec((1, H, N, P), lambda b, c: (b, 0, 0, 0)),
                  full2((CONV_W, C)), full2((1, C)), full2((8, LANES)), full2((8, LANES)), full2((1, G)),
                  full2((1, G))],
        out_specs=[pl.BlockSpec((1, tc, G), lambda b, c: (b, c, 0)),
                   pl.BlockSpec((1, 8, C), lambda b, c: (b, 0, 0)),
                   pl.BlockSpec((1, H, N, P), lambda b, c: (b, 0, 0, 0))],
        out_shape=[jax.ShapeDtypeStruct((B, Tp, G), f32), jax.ShapeDtypeStruct((B, 8, C), f32),
                   jax.ShapeDtypeStruct((B, H, N, P), f32)],
        scratch_shapes=[pltpu.VMEM((8, C), f32), pltpu.VMEM((H, N, P), f32), pltpu.VMEM((tc, G), f32)],
        compiler_params=pltpu.CompilerParams(dimension_semantics=("parallel", "arbitrary"),
                                             vmem_limit_bytes=_VMEM_LIMIT),
        name="ssd_mixer",
    )(proj3, proj3, proj3, cb0, s0t, conv_w, conv_b[None, :], prow, pcol, dvec, norm_w[None, :])
    return y, cbo[:, :CONV_W - 1, :], jnp.swapaxes(so, -1, -2)


def _rwkv_pad_cols(a):
    G = GROUP_WIDTH
    z = lambda w: jnp.zeros(a.shape[:-1] + (w,), a.dtype)
    o = 3 * G
    wd = a[..., o:o + RWKV_W_RANK]
    ad = a[..., o + RWKV_W_RANK:o + RWKV_W_RANK + RWKV_A_RANK]
    gd = a[..., o + RWKV_W_RANK + RWKV_A_RANK:]
    return jnp.concatenate([a[..., :o], wd, z(LANES - RWKV_W_RANK), ad, z(LANES - RWKV_A_RANK),
                            gd, z(LANES - RWKV_G_RANK)], axis=-1)


def _rwkv_unpad_cols(a):
    o = 3 * GROUP_WIDTH
    return jnp.concatenate([a[..., :o], a[..., o:o + RWKV_W_RANK], a[..., o + LANES:o + LANES + RWKV_A_RANK],
                            a[..., o + 2 * LANES:o + 2 * LANES + RWKV_G_RANK]], axis=-1)


def _rwkv_kernel(pd_ref, sh0_ref, s0_ref, mu_ref, vec_ref, wup_ref, aup_ref, gup_ref, seg_ref,
                 y_ref, sout_ref, shout_ref,
                 prev_scr, s_scr, r_scr, k_scr, d_scr, kk_scr, bb_scr, g_scr, bonus_scr, vt_scr, yt_scr, *, n_steps):
    c = pl.program_id(1)
    tc = pd_ref.shape[1]
    G = GROUP_WIDTH
    hd = RWKV_HEADDIM

    @pl.when(c == 0)
    def _():
        prev_scr[...] = sh0_ref[...]
        s_scr[...] = s0_ref[...]

    nb = pd_ref.shape[0]
    w0 = vec_ref[0:1, :]
    a0 = vec_ref[1:2, :]
    k_k = vec_ref[2:3, :]
    k_a = vec_ref[3:4, :]
    ln_w = vec_ref[4:5, :]
    ln_b = vec_ref[5:6, :]
    r_k = vec_ref[6:7, :]
    seg = seg_ref[...]
    hdot = lambda x, y: jnp.dot(x, y, precision=_HI, preferred_element_type=jnp.float32)
    for ib in range(nb):
        cur = pd_ref[ib]
        row = lax.broadcasted_iota(jnp.int32, cur.shape, 0)
        prev = jnp.where(row == 0, prev_scr[ib], pltpu.roll(cur, 1, axis=0))
        prev_scr[ib] = cur[n_steps - 1:n_steps, :] if n_steps < tc else cur[tc - 1:tc, :]
        m = cur + (prev - cur) * mu_ref[...]
        r = m[:, 0:G]
        k = m[:, G:2 * G]
        v = m[:, 2 * G:3 * G]
        wd = m[:, 3 * G:3 * G + LANES]
        ad = m[:, 3 * G + LANES:3 * G + 2 * LANES]
        gd = m[:, 3 * G + 2 * LANES:3 * G + 3 * LANES]
        w = -_softplus(-(w0 + hdot(jnp.tanh(wd), wup_ref[...]))) - 0.5
        a = jax.nn.sigmoid(a0 + hdot(ad, aup_ref[...]))
        kkr = k * k_k
        kk = kkr / jnp.maximum(jnp.sqrt(hdot(kkr * kkr, seg)), 1e-12)
        k2 = k * (1.0 + (a - 1.0) * k_a)
        g_scr[ib] = hdot(jax.nn.sigmoid(gd), gup_ref[...])
        bonus_scr[ib] = hdot(r * k2 * r_k, seg) * v
        r_scr[ib] = r
        k_scr[ib] = k2
        d_scr[ib] = jnp.exp(-jnp.exp(w))
        kk_scr[ib] = -kk
        bb_scr[ib] = kk * a
        vt_scr[ib] = v.T
    yt_scr[...] = jnp.zeros_like(yt_scr)

    lane = lax.broadcasted_iota(jnp.int32, (hd, LANES), 1)
    lo_half = lane < hd

    def half_sums(p):
        lo = jnp.sum(jnp.where(lo_half, p, 0.0), axis=1, keepdims=True)
        hi = jnp.sum(jnp.where(lo_half, 0.0, p), axis=1, keepdims=True)
        return lo, hi

    npair = RWKV_HEADS // 2

    def step8(i, states):
        t0 = pl.multiple_of(i * 8, 8)
        blk = pl.multiple_of((t0 // LANES) * LANES, LANES)
        rows = pl.ds(t0, 8)
        chains = [(ib, p) for ib in range(nb) for p in range(npair)]
        states = list(states)
        rowv = {}
        tiles = {}
        for ci, (ib, p) in enumerate(chains):
            cols = slice(p * LANES, (p + 1) * LANES)
            rowv[ci] = tuple(ref[ib, rows, cols] for ref in (kk_scr, d_scr, bb_scr, k_scr, r_scr))
            tile = lambda ref, hh: ref[ib, pl.ds(p * LANES + hh * hd, hd), pl.ds(blk, LANES)]
            tiles[ci] = [tile(vt_scr, 0), tile(vt_scr, 1), tile(yt_scr, 0), tile(yt_scr, 1)]

        def vcol_of(ci, j):
            hit = lane == (t0 + j - blk)
            vc0 = jnp.sum(jnp.where(hit, tiles[ci][0], 0.0), axis=1, keepdims=True)
            vc1 = jnp.sum(jnp.where(hit, tiles[ci][1], 0.0), axis=1, keepdims=True)
            return jnp.where(lo_half, vc0, vc1)

        def emit_y(ci, j):
            hit = lane == (t0 + j - blk)
            ylo, yhi = half_sums(states[ci] * rowv[ci][4][j:j + 1, :])
            tiles[ci][2] = jnp.where(hit, ylo, tiles[ci][2])
            tiles[ci][3] = jnp.where(hit, yhi, tiles[ci][3])

        nj = min(8, n_steps)
        vcols = [vcol_of(ci, 0) for ci in range(len(chains))]
        for j in range(nj):
            sas = [half_sums(states[ci] * rowv[ci][0][j:j + 1, :]) for ci in range(len(chains))]
            if j > 0:
                for ci in range(len(chains)):
                    emit_y(ci, j - 1)
            nxt = [vcol_of(ci, j + 1) for ci in range(len(chains))] if j + 1 < nj else None
            for ci in range(len(chains)):
                _, d8, bb8, k8, _ = rowv[ci]
                sa = jnp.where(lo_half, sas[ci][0], sas[ci][1])
                states[ci] = states[ci] * d8[j:j + 1, :] + sa * bb8[j:j + 1, :] + vcols[ci] * k8[j:j + 1, :]
            vcols = nxt
        for ci in range(len(chains)):
            emit_y(ci, nj - 1)
        for ci, (ib, p) in enumerate(chains):
            yt_scr[ib, pl.ds(p * LANES, hd), pl.ds(blk, LANES)] = tiles[ci][2]
            yt_scr[ib, pl.ds(p * LANES + hd, hd), pl.ds(blk, LANES)] = tiles[ci][3]
        return tuple(states)

    init = tuple(s_scr[ib, p] for ib in range(nb) for p in range(npair))
    states = lax.fori_loop(0, -(-n_steps // 8), step8, init)
    for ib in range(nb):
        for p in range(npair):
            s_scr[ib, p] = states[ib * npair + p]

    inv = 1.0 / hd
    for ib in range(nb):
        y = yt_scr[ib].T
        mean = hdot(y, seg) * inv
        yc = y - mean
        var = hdot(yc * yc, seg) * inv
        yn = yc * lax.rsqrt(var + RWKV_LN_EPS) * ln_w + ln_b
        y_ref[ib] = (yn + bonus_scr[ib]) * g_scr[ib]

    @pl.when(c == pl.num_programs(1) - 1)
    def _():
        sout_ref[...] = s_scr[...]
        shout_ref[...] = prev_scr[...]


def _rwkv_mixer(proj3, shift0, s0, mu, w0, w_up, a0, a_up, g_up, k_k, k_a, r_k, ln_w, ln_b, *, n_steps, tc, nb=1):
    B, Tp, _ = proj3.shape
    G = GROUP_WIDTH
    H, hd = RWKV_HEADS, RWKV_HEADDIM
    nchunk = Tp // tc
    assert Tp % tc == 0 and (n_steps == tc or nchunk == 1) and (n_steps % 8 == 0 or n_steps < 8)
    assert B % nb == 0
    f32 = jnp.float32
    sh0 = _rwkv_pad_cols(shift0)[:, None, :]
    s0p = s0.reshape(B, H // 2, 2, hd, hd).transpose(0, 1, 3, 2, 4).reshape(B, H // 2, hd, 2 * hd)
    mu_p = _rwkv_pad_cols(mu[None, :])
    vec = jnp.stack([w0, a0, k_k, k_a, ln_w, ln_b, r_k.reshape(G), jnp.zeros((G,), f32)])
    padk = lambda wgt: jnp.pad(wgt, ((0, LANES - wgt.shape[0]), (0, 0)))
    hid = jnp.arange(G) // hd
    seg = (hid[:, None] == hid[None, :]).astype(f32)
    full2 = lambda shp: pl.BlockSpec(shp, lambda b, c: (0, 0))
    tscr = lambda: pltpu.VMEM((nb, tc, G), f32)
    y, s_out, sh_out = pl.pallas_call(
        functools.partial(_rwkv_kernel, n_steps=n_steps),
        grid=(B // nb, nchunk),
        in_specs=[pl.BlockSpec((nb, tc, RWKV_PCOLS), lambda b, c: (b, c, COL_RWKV)),
                  pl.BlockSpec((nb, 1, RWKV_PCOLS), lambda b, c: (b, 0, 0)),
                  pl.BlockSpec((nb, H // 2, hd, 2 * hd), lambda b, c: (b, 0, 0, 0)),
                  full2((1, RWKV_PCOLS)), full2((8, G)), full2((LANES, G)), full2((LANES, G)), full2((LANES, G)),
                  full2((G, G))],
        out_specs=[pl.BlockSpec((nb, tc, G), lambda b, c: (b, c, 0)),
                   pl.BlockSpec((nb, H // 2, hd, 2 * hd), lambda b, c: (b, 0, 0, 0)),
                   pl.BlockSpec((nb, 1, RWKV_PCOLS), lambda b, c: (b, 0, 0))],
        out_shape=[jax.ShapeDtypeStruct((B, Tp, G), f32),
                   jax.ShapeDtypeStruct((B, H // 2, hd, 2 * hd), f32),
                   jax.ShapeDtypeStruct((B, 1, RWKV_PCOLS), f32)],
        scratch_shapes=[pltpu.VMEM((nb, 1, RWKV_PCOLS), f32), pltpu.VMEM((nb, H // 2, hd, 2 * hd), f32),
                        tscr(), tscr(), tscr(), tscr(), tscr(), tscr(), tscr(),
                        pltpu.VMEM((nb, G, tc), f32), pltpu.VMEM((nb, G, tc), f32)],
        compiler_params=pltpu.CompilerParams(dimension_semantics=("parallel", "arbitrary"),
                                             vmem_limit_bytes=_VMEM_LIMIT),
        name="rwkv_mixer",
    )(proj3, sh0, s0p, mu_p, vec, padk(w_up), padk(a_up), padk(g_up), seg)
    s_new = s_out.reshape(B, H // 2, hd, 2, hd).transpose(0, 1, 3, 2, 4).reshape(B, H, hd, hd)
    return y, _rwkv_unpad_cols(sh_out[:, 0, :]), s_new


def _xattn_kernel(x_ref, g_ref, wq_ref, wo_ref, mk_ref, mv_ref, o_ref, cat_scr):
    x = x_ref[...]
    hn = x * lax.rsqrt(jnp.mean(x * x, axis=-1, keepdims=True) + NORM_EPS) * g_ref[...]
    q = jnp.dot(hn.astype(jnp.bfloat16), wq_ref[...], preferred_element_type=jnp.float32)
    qb = q.astype(jnp.bfloat16)
    mk = mk_ref[0].astype(jnp.bfloat16)
    mv = mv_ref[0].astype(jnp.bfloat16)
    hd = X_HEADDIM
    scale = hd ** -0.5
    for h in range(X_HEADS):
        cols = slice(h * hd, (h + 1) * hd)
        s = lax.dot_general(qb[:, cols], mk[:, cols], (((1,), (1,)), ((), ())),
                            preferred_element_type=jnp.float32) * scale
        m = jnp.max(s, axis=-1, keepdims=True)
        p = jnp.exp(s - m)
        p = p / jnp.sum(p, axis=-1, keepdims=True)
        cat_scr[:, cols] = jnp.dot(p.astype(jnp.bfloat16), mv[:, cols], preferred_element_type=jnp.float32)
    o_ref[...] = x + jnp.dot(cat_scr[...].astype(jnp.bfloat16), wo_ref[...], preferred_element_type=jnp.float32)


def _xattn_block(x, gain, wq, wo, mk, mv, *, tb, rows_per_mem):
    n = x.shape[0]
    M = mk.shape[1]
    assert n % tb == 0 and rows_per_mem % tb == 0
    per = rows_per_mem // tb
    return pl.pallas_call(
        _xattn_kernel,
        grid=(n // tb,),
        in_specs=[pl.BlockSpec((tb, D_MODEL), lambda i: (i, 0)),
                  pl.BlockSpec((1, D_MODEL), lambda i: (0, 0)),
                  pl.BlockSpec((D_MODEL, D_MODEL), lambda i: (0, 0)),
                  pl.BlockSpec((D_MODEL, D_MODEL), lambda i: (0, 0)),
                  pl.BlockSpec((1, M, D_MODEL), lambda i: (i // per, 0, 0)),
                  pl.BlockSpec((1, M, D_MODEL), lambda i: (i // per, 0, 0))],
        out_specs=pl.BlockSpec((tb, D_MODEL), lambda i: (i, 0)),
        out_shape=jax.ShapeDtypeStruct((n, D_MODEL), jnp.float32),
        scratch_shapes=[pltpu.VMEM((tb, D_MODEL), jnp.float32)],
        compiler_params=pltpu.CompilerParams(dimension_semantics=("parallel",), vmem_limit_bytes=_VMEM_LIMIT),
        name="xattn_block",
    )(x, gain.reshape(1, -1), wq.astype(jnp.bfloat16), wo.astype(jnp.bfloat16), mk, mv)


def _top_values(s, count, with_rank=False):
    vals = []
    cur = s
    rank = jnp.full(s.shape, float(count), jnp.float32) if with_rank else None
    for r in range(count):
        m = jnp.max(cur, axis=0, keepdims=True)
        vals.append(m)
        hit = cur == m
        if with_rank:
            rank = jnp.where(hit, float(r), rank)
        cur = jnp.where(hit, NEG_INF, cur)
    return (vals, rank) if with_rank else vals


def _peer_route_kernel(x_ref, g_ref, wqt_ref, sk_ref, hn_ref, cnt_ref, e1_ref, r2_ref, e2_ref, a1_scr, a2_scr):
    tb = x_ref.shape[0]
    x = x_ref[...]
    hn = x * lax.rsqrt(jnp.mean(x * x, axis=-1, keepdims=True) + NORM_EPS) * g_ref[...]
    hb = hn.astype(jnp.bfloat16)
    hn_ref[...] = hb
    qt = lax.dot_general(wqt_ref[...], hb, (((1,), (1,)), ((), ())), preferred_element_type=jnp.float32)
    n_top = PEER_TOPK + 1
    row8 = lax.broadcasted_iota(jnp.int32, (8, tb), 0)
    a1_scr[...] = jnp.full(a1_scr.shape, NEG_INF, jnp.float32)
    a2_scr[...] = jnp.full(a2_scr.shape, NEG_INF, jnp.float32)
    for h in range(PEER_HEADS):
        q1 = qt[(2 * h) * PEER_HALF:(2 * h + 1) * PEER_HALF, :].astype(jnp.bfloat16)
        q2 = qt[(2 * h + 1) * PEER_HALF:(2 * h + 2) * PEER_HALF, :].astype(jnp.bfloat16)
        s1 = jnp.dot(sk_ref[2 * h], q1, preferred_element_type=jnp.float32)
        s2 = jnp.dot(sk_ref[2 * h + 1], q2, preferred_element_type=jnp.float32)
        v1 = _top_values(s1, n_top)
        v2, rank2 = _top_values(s2, n_top, with_rank=True)
        for r in range(n_top):
            a1_scr[r:r + 1, :] = v1[r]
            a2_scr[r:r + 1, :] = v2[r]
        a1 = a1_scr[...]
        a2 = a2_scr[...]
        pieces = [a1 + a2[0:1, :]]
        for q in range(1, 8):
            lim = n_top // (q + 1)
            pieces.append(jnp.where(row8 < lim, a1[0:8, :] + a2[q:q + 1, :], NEG_INF))
        pieces.append(a1[0:1, :] + a2[8:24, :])
        cand = jnp.concatenate(pieces, axis=0)
        c = _top_values(cand, n_top)
        tau = 0.5 * (c[PEER_TOPK - 1] + c[PEER_TOPK])
        mx = c[0]
        z = jnp.sum(jnp.where(cand >= tau, jnp.exp(cand - mx), 0.0), axis=0, keepdims=True)
        cnt = jnp.zeros(s1.shape, jnp.float32)
        for q in range(n_top):
            cnt = cnt + jnp.where(s1 + v2[q] >= tau, 1.0, 0.0)
        cnt_ref[h] = cnt
        e1_ref[h] = jnp.exp(s1 - v1[0]) / z
        r2_ref[h] = rank2.astype(jnp.bfloat16)
        e2_ref[h] = jnp.exp(s2 - v2[0]).astype(jnp.bfloat16)


def _peer_route(x, gain, wqt_b, sk_b, tb=256):
    n = x.shape[0]
    assert n % tb == 0
    hk = PEER_HEADS
    blk3 = pl.BlockSpec((hk, PEER_NKEYS, tb), lambda i: (0, 0, i))
    shp_f = jax.ShapeDtypeStruct((hk, PEER_NKEYS, n), jnp.float32)
    shp_b = jax.ShapeDtypeStruct((hk, PEER_NKEYS, n), jnp.bfloat16)
    return pl.pallas_call(
        _peer_route_kernel,
        grid=(n // tb,),
        in_specs=[pl.BlockSpec((tb, D_MODEL), lambda i: (i, 0)),
                  pl.BlockSpec((1, D_MODEL), lambda i: (0, 0)),
                  pl.BlockSpec((2 * hk * PEER_HALF, D_MODEL), lambda i: (0, 0)),
                  pl.BlockSpec((2 * hk, PEER_NKEYS, PEER_HALF), lambda i: (0, 0, 0))],
        out_specs=[pl.BlockSpec((tb, D_MODEL), lambda i: (i, 0)), blk3, blk3, blk3, blk3],
        out_shape=[jax.ShapeDtypeStruct((n, D_MODEL), jnp.bfloat16), shp_f, shp_f, shp_b, shp_b],
        scratch_shapes=[pltpu.VMEM((24, tb), jnp.float32), pltpu.VMEM((24, tb), jnp.float32)],
        compiler_params=pltpu.CompilerParams(dimension_semantics=("parallel",), vmem_limit_bytes=_VMEM_LIMIT),
        name="peer_route",
    )(x, gain, wqt_b, sk_b)


def _peer_dense_kernel(x_ref, hn_ref, u_ref, v_ref, cnt_ref, e1_ref, r2_ref, e2_ref, o_ref, acc_ref, w_ref, *, ic):
    e = pl.program_id(1)
    tb = x_ref.shape[0]

    @pl.when(e == 0)
    def _():
        acc_ref[...] = jnp.zeros_like(acc_ref)

    act = lax.dot_general(u_ref[...], hn_ref[...], (((1,), (1,)), ((), ())), preferred_element_type=jnp.float32)
    for il in range(ic):
        g = None
        for h in range(PEER_HEADS):
            cb = jnp.broadcast_to(cnt_ref[h, il:il + 1, :], (16, tb)).astype(jnp.bfloat16)[None]
            eb = jnp.broadcast_to(e1_ref[h, il:il + 1, :], (16, tb)).astype(jnp.bfloat16)[None]
            r2 = r2_ref[h].reshape(PEER_NKEYS // 16, 16, tb)
            e2 = e2_ref[h].reshape(PEER_NKEYS // 16, 16, tb)
            t = jnp.where(r2 < cb, e2 * eb, jnp.zeros_like(e2))
            g = t if g is None else g + t
        rows = slice(il * PEER_NKEYS, (il + 1) * PEER_NKEYS)
        a = act[rows, :].astype(jnp.bfloat16).reshape(PEER_NKEYS // 16, 16, tb)
        w_ref[rows, :] = (g * _gelu_tanh(a)).reshape(PEER_NKEYS, tb)
    acc_ref[...] += lax.dot_general(w_ref[...], v_ref[...], (((0,), (0,)), ((), ())),
                                    preferred_element_type=jnp.float32)

    @pl.when(e == pl.num_programs(1) - 1)
    def _():
        o_ref[...] = x_ref[...] + acc_ref[...]


def _peer_dense(x, hn_b, u_b, v_b, cnt, e1, r2, e2, tb=PEER_TOK_PAD, ic=16):
    n = x.shape[0]
    ne = u_b.shape[0]
    ec = ic * PEER_NKEYS
    assert n % tb == 0 and ne % ec == 0
    hk = PEER_HEADS
    row_blk = pl.BlockSpec((hk, ic, tb), lambda i, e: (0, e, i))
    full_blk = pl.BlockSpec((hk, PEER_NKEYS, tb), lambda i, e: (0, 0, i))
    return pl.pallas_call(
        functools.partial(_peer_dense_kernel, ic=ic),
        grid=(n // tb, ne // ec),
        in_specs=[pl.BlockSpec((tb, D_MODEL), lambda i, e: (i, 0)),
                  pl.BlockSpec((tb, D_MODEL), lambda i, e: (i, 0)),
                  pl.BlockSpec((ec, D_MODEL), lambda i, e: (e, 0)),
                  pl.BlockSpec((ec, D_MODEL), lambda i, e: (e, 0)),
                  row_blk, row_blk, full_blk, full_blk],
        out_specs=pl.BlockSpec((tb, D_MODEL), lambda i, e: (i, 0)),
        out_shape=jax.ShapeDtypeStruct((n, D_MODEL), jnp.float32),
        scratch_shapes=[pltpu.VMEM((tb, D_MODEL), jnp.float32), pltpu.VMEM((ec, tb), jnp.bfloat16)],
        compiler_params=pltpu.CompilerParams(dimension_semantics=("parallel", "arbitrary"),
                                             vmem_limit_bytes=_VMEM_LIMIT),
        name="peer_dense",
    )(x, hn_b, u_b, v_b, cnt, e1, r2, e2)


def _peer_block(x, gain, wq, subkeys, u_tab, v_tab):
    wqt_b = wq.T.astype(jnp.bfloat16)
    sk_b = subkeys.reshape(2 * PEER_HEADS, PEER_NKEYS, PEER_HALF).astype(jnp.bfloat16)
    u_b = u_tab.astype(jnp.bfloat16)
    v_b = v_tab.astype(jnp.bfloat16)
    hn_b, cnt, e1, r2, e2 = _peer_route(x, gain.reshape(1, -1), wqt_b, sk_b)
    return _peer_dense(x, hn_b, u_b, v_b, cnt, e1, r2, e2)


def _in_weight(w_in):
    abc = w_in[:, :D_OFF]
    pad = jnp.zeros((w_in.shape[0], ABC_PCOLS - D_OFF), w_in.dtype)
    return jnp.concatenate([abc, pad, _rwkv_pad_cols(w_in[:, D_OFF:])], axis=1)


def _layer_pre_peer(x2, bsz, T, lp, st, prompt):
    G = GROUP_WIDTH
    n = bsz * T
    proj = _mm(x2, lp['w_in_p'], gain=lp['norm_mix'])
    if prompt:
        n_valid, tc = SEQ_CHUNK, SEQ_CHUNK
        proj3 = proj.reshape(bsz, T, -1)
        ya, k_r, v = _moba_prompt(proj3, jnp.arange(T, dtype=jnp.int32))
    else:
        assert T == 1
        n_valid, tc = T, DEC_ROWS
        ya, k_r = _moba_decode(proj[:, 0:G], proj[:, G:2 * G], proj[:, 2 * G:3 * G],
                               st['k_cache'], st['v_cache'], st['page_table'], PAST_LEN)
        v = proj[:, 2 * G:3 * G]
        proj3 = jnp.pad(proj[:, None, :], ((0, 0), (0, tc - T), (0, 0)))
    yb, lru_conv, lru_h = _lru_mixer(proj3, st['lru_conv'], st['lru_h'], lp['lru_conv_w'], lp['lru_conv_b'],
                                     lp['lru_wa'], lp['lru_ba'], lp['lru_wx'], lp['lru_bx'], lp['lru_lambda'],
                                     n_valid=n_valid, tc=tc)
    yc, ssd_conv, ssd_s = _ssd_mixer(proj3, st['ssd_conv'], st['ssd'], lp['ssd_conv_w'], lp['ssd_conv_b'],
                                     lp['ssd_dt_bias'], lp['ssd_a_log'], lp['ssd_d'], lp['ssd_norm'],
                                     n_valid=n_valid, tc=tc)
    yd, rwkv_shift, rwkv_s = _rwkv_mixer(proj3, st['rwkv_shift'], st['rwkv'],
                                         lp['rwkv_mu'], lp['rwkv_w0'], lp['rwkv_w_up'], lp['rwkv_a0'],
                                         lp['rwkv_a_up'], lp['rwkv_g_up'], lp['rwkv_k_k'], lp['rwkv_k_a'],
                                         lp['rwkv_r_k'], lp['rwkv_ln_w'], lp['rwkv_ln_b'], n_steps=n_valid, tc=tc)
    rows = lambda y: y[:, :T].reshape(n, G)
    x2 = _mm([ya.reshape(n, G), rows(yb), rows(yc), rows(yd)], lp['w_out'], residual=x2)
    if prompt:
        x2 = _xattn_block(x2, lp['norm_x'], lp['x_wq'], lp['x_wo'], st['mem_k'], st['mem_v'],
                          tb=SEQ_CHUNK, rows_per_mem=T)
    else:
        x8 = jnp.pad(x2[:, None, :], ((0, 0), (0, 7), (0, 0))).reshape(n * 8, D_MODEL)
        x8 = _xattn_block(x8, lp['norm_x'], lp['x_wq'], lp['x_wo'], st['mem_k'], st['mem_v'], tb=8, rows_per_mem=8)
        x2 = x8.reshape(n, 8, D_MODEL)[:, 0, :]
    heads = lambda t: t.reshape(bsz, T, A_HEADS, HEAD_DIM)
    new = {'k': heads(k_r), 'v': heads(v), 'lru_h': lru_h, 'lru_conv': lru_conv, 'ssd': ssd_s, 'ssd_conv': ssd_conv,
           'rwkv': rwkv_s, 'rwkv_shift': rwkv_shift}
    return x2, new


def kernel(x_prompt, x_sample, mem_prompt, cache_moba_k, cache_moba_v, page_table, state_lru_h, state_lru_conv, state_ssd, state_ssd_conv, state_rwkv, state_rwkv_shift, cache_mem_k, cache_mem_v, norm_mix, w_in, w_out, lru_conv_w, lru_conv_b, lru_wa, lru_ba, lru_wx, lru_bx, lru_lambda, ssd_conv_w, ssd_conv_b, ssd_dt_bias, ssd_a_log, ssd_d, ssd_norm, rwkv_mu, rwkv_w0, rwkv_w_up, rwkv_a0, rwkv_a_up, rwkv_g_up, rwkv_k_k, rwkv_k_a, rwkv_r_k, rwkv_ln_w, rwkv_ln_b, norm_x, x_wq, x_wk, x_wv, x_wo, norm_ffn, peer_wq, peer_subkeys, peer_u, peer_v, final_norm):
    bp, tp, _ = x_prompt.shape
    bd, td, _ = x_sample.shape
    n_p, n_s = bp * tp, bd * td
    n_all = -(-(n_p + n_s) // PEER_TOK_PAD) * PEER_TOK_PAD
    f32 = jnp.float32
    xp2 = x_prompt.reshape(n_p, D_MODEL)
    xs2 = x_sample.reshape(n_s, D_MODEL)
    names = ('k', 'v', 'lru_h', 'lru_conv', 'ssd', 'ssd_conv', 'rwkv', 'rwkv_shift')
    p_new = {n: [] for n in names + ('mem_k', 'mem_v')}
    s_new = {n: [] for n in names}
    mem2 = mem_prompt.reshape(bp * MEM_LEN, D_MODEL)
    pool = lambda c: jnp.transpose(c, (0, 1, 3, 4, 2)).reshape((-1, A_HEADS, HEAD_DIM, PAGE_SIZE))
    cache_kt, cache_vt = pool(cache_moba_k), pool(cache_moba_v)
    for l in range(DEPTH):
        lp = {
            'norm_mix': norm_mix[l], 'w_in_p': _in_weight(w_in[l]), 'w_out': w_out[l],
            'lru_conv_w': lru_conv_w[l], 'lru_conv_b': lru_conv_b[l], 'lru_wa': lru_wa[l], 'lru_ba': lru_ba[l],
            'lru_wx': lru_wx[l], 'lru_bx': lru_bx[l], 'lru_lambda': lru_lambda[l],
            'ssd_conv_w': ssd_conv_w[l], 'ssd_conv_b': ssd_conv_b[l], 'ssd_dt_bias': ssd_dt_bias[l],
            'ssd_a_log': ssd_a_log[l], 'ssd_d': ssd_d[l], 'ssd_norm': ssd_norm[l],
            'rwkv_mu': rwkv_mu[l], 'rwkv_w0': rwkv_w0[l], 'rwkv_w_up': rwkv_w_up[l], 'rwkv_a0': rwkv_a0[l],
            'rwkv_a_up': rwkv_a_up[l], 'rwkv_g_up': rwkv_g_up[l], 'rwkv_k_k': rwkv_k_k[l], 'rwkv_k_a': rwkv_k_a[l],
            'rwkv_r_k': rwkv_r_k[l], 'rwkv_ln_w': rwkv_ln_w[l], 'rwkv_ln_b': rwkv_ln_b[l],
            'norm_x': norm_x[l], 'x_wq': x_wq[l], 'x_wo': x_wo[l],
        }
        mk = _mm(mem2, x_wk[l]).reshape(bp, MEM_LEN, D_MODEL)
        mv = _mm(mem2, x_wv[l]).reshape(bp, MEM_LEN, D_MODEL)
        st_p = {
            'lru_conv': jnp.zeros((bp, CONV_W - 1, LRU_WIDTH), f32),
            'lru_h': jnp.zeros((bp, LRU_WIDTH), f32),
            'ssd_conv': jnp.zeros((bp, CONV_W - 1, SSD_CONV_CH), f32),
            'ssd': jnp.zeros((bp, SSD_HEADS, SSD_HEADDIM, SSD_STATE), f32),
            'rwkv_shift': jnp.zeros((bp, RWKV_COLS), f32),
            'rwkv': jnp.zeros((bp, RWKV_HEADS, RWKV_HEADDIM, RWKV_HEADDIM), f32),
            'mem_k': mk, 'mem_v': mv,
        }
        xp2, npl = _layer_pre_peer(xp2, bp, tp, lp, st_p, True)
        for n in names:
            p_new[n].append(npl[n])
        p_new['mem_k'].append(mk.reshape(bp, MEM_LEN, X_HEADS, X_HEADDIM))
        p_new['mem_v'].append(mv.reshape(bp, MEM_LEN, X_HEADS, X_HEADDIM))
        st_s = {
            'k_cache': cache_kt, 'v_cache': cache_vt, 'page_table': page_table + l * cache_moba_k.shape[1],
            'lru_conv': state_lru_conv[l], 'lru_h': state_lru_h[l],
            'ssd_conv': state_ssd_conv[l], 'ssd': state_ssd[l],
            'rwkv_shift': state_rwkv_shift[l], 'rwkv': state_rwkv[l],
            'mem_k': cache_mem_k[l].reshape(bd, MEM_LEN, D_MODEL), 'mem_v': cache_mem_v[l].reshape(bd, MEM_LEN, D_MODEL),
        }
        xs2, nsl = _layer_pre_peer(xs2, bd, td, lp, st_s, False)
        for n in names:
            s_new[n].append(nsl[n])
        x_all = jnp.concatenate([xp2, xs2, jnp.zeros((n_all - n_p - n_s, D_MODEL), f32)], axis=0)
        x_all = _peer_block(x_all, norm_ffn[l], peer_wq[l], peer_subkeys[l], peer_u[l], peer_v[l])
        xp2 = x_all[:n_p]
        xs2 = x_all[n_p:n_p + n_s]
    y_prompt = _rmsnorm(xp2, final_norm).reshape(bp, tp, D_MODEL)
    y_sample = _rmsnorm(xs2, final_norm).reshape(bd, td, D_MODEL)
    return (y_prompt, y_sample,
            jnp.stack(p_new['k']), jnp.stack(p_new['v']), jnp.stack(p_new['lru_h']), jnp.stack(p_new['lru_conv']),
            jnp.stack(p_new['ssd']), jnp.stack(p_new['ssd_conv']), jnp.stack(p_new['rwkv']), jnp.stack(p_new['rwkv_shift']),
            jnp.stack(p_new['mem_k']), jnp.stack(p_new['mem_v']),
            jnp.stack(s_new['k']), jnp.stack(s_new['v']), jnp.stack(s_new['lru_h']), jnp.stack(s_new['lru_conv']),
            jnp.stack(s_new['ssd']), jnp.stack(s_new['ssd_conv']), jnp.stack(s_new['rwkv']), jnp.stack(s_new['rwkv_shift']))
```

```python
import functools

import jax
import jax.numpy as jnp
from jax import lax
from jax.experimental import pallas as pl
from jax.experimental.pallas import tpu as pltpu

D_MODEL = 1024
DEPTH = 2
PAST_LEN = 16384
PAGE_SIZE = 128
GROUP_WIDTH = 256
HEAD_DIM = 64
A_HEADS = GROUP_WIDTH // HEAD_DIM
MOBA_BLOCK = 256
MOBA_TOPK = 3
ROPE_THETA = 10000.0
LRU_WIDTH = GROUP_WIDTH
LRU_C = 8.0
CONV_W = 4
SSD_HEADS = 4
SSD_HEADDIM = GROUP_WIDTH // SSD_HEADS
SSD_GROUPS = 2
SSD_STATE = 64
SSD_CHUNK = 128
SSD_CONV_CH = GROUP_WIDTH + 2 * SSD_GROUPS * SSD_STATE
RWKV_HEADS = 4
RWKV_HEADDIM = GROUP_WIDTH // RWKV_HEADS
RWKV_W_RANK = 32
RWKV_A_RANK = 32
RWKV_G_RANK = 64
RWKV_COLS = 3 * GROUP_WIDTH + RWKV_W_RANK + RWKV_A_RANK + RWKV_G_RANK
RWKV_LN_EPS = 64e-5
D_OFF = 3 * GROUP_WIDTH + 2 * LRU_WIDTH + GROUP_WIDTH + SSD_CONV_CH + SSD_HEADS
MEM_LEN = 256
X_HEADS = 4
X_HEADDIM = D_MODEL // X_HEADS
PEER_HEADS = 8
PEER_NKEYS = 128
PEER_HALF = 128
PEER_TOPK = 16
NORM_EPS = 1e-6
NEG_INF = -1e30

LANES = 128
RWKV_PCOLS = 3 * GROUP_WIDTH + 3 * LANES
ABC_PCOLS = 2 * RWKV_PCOLS
COL_Q, COL_K, COL_V, COL_U, COL_GATE, COL_Z = 0, 1, 2, 3, 4, 5
COL_XBC = 3
COL_DT = 8
COL_RWKV = 2
PAGES_PER_BLOCK = MOBA_BLOCK // PAGE_SIZE
KMEAN_PAGES = 32
PEER_TOK_PAD = 512
SEQ_CHUNK = 512
DEC_ROWS = 128
_VMEM_LIMIT = 56 * 1024 * 1024
_HI = lax.Precision.HIGHEST


def _softplus(x):
    return jnp.maximum(x, 0.0) + jnp.log1p(jnp.exp(-jnp.abs(x)))


def _silu(x):
    return x * jax.nn.sigmoid(x)


def _gelu_tanh(x):
    return 0.5 * x * (1.0 + jnp.tanh(0.7978845608028654 * (x + 0.044715 * (x * x * x))))


def _mm_kernel(*refs, n_x, has_norm, has_res):
    x_refs = refs[:n_x]
    w_ref = refs[n_x]
    pos = n_x + 1
    g_ref = r_ref = None
    if has_norm:
        g_ref = refs[pos]
        pos += 1
    if has_res:
        r_ref = refs[pos]
        pos += 1
    o_ref = refs[pos]
    y = None
    off = 0
    for x_ref in x_refs:
        x = x_ref[...]
        kx = x.shape[1]
        if has_norm:
            x = x * lax.rsqrt(jnp.mean(x * x, axis=-1, keepdims=True) + NORM_EPS) * g_ref[...]
        t = jnp.dot(x.astype(jnp.bfloat16), w_ref[off:off + kx, :], preferred_element_type=jnp.float32)
        y = t if y is None else y + t
        off += kx
    if has_res:
        y = y + r_ref[...]
    o_ref[...] = y


def _mm(xs, w, gain=None, residual=None, tm=512):
    if not isinstance(xs, (list, tuple)):
        xs = [xs]
    assert gain is None or len(xs) == 1
    M = xs[0].shape[0]
    K, N = w.shape
    tm = min(tm, M)
    assert M % tm == 0 and sum(x.shape[1] for x in xs) == K
    args = list(xs) + [w.astype(jnp.bfloat16)]
    in_specs = [pl.BlockSpec((tm, x.shape[1]), lambda i: (i, 0)) for x in xs]
    in_specs.append(pl.BlockSpec((K, N), lambda i: (0, 0)))
    if gain is not None:
        args.append(gain.reshape(1, K).astype(jnp.float32))
        in_specs.append(pl.BlockSpec((1, K), lambda i: (0, 0)))
    if residual is not None:
        args.append(residual)
        in_specs.append(pl.BlockSpec((tm, N), lambda i: (i, 0)))
    return pl.pallas_call(
        functools.partial(_mm_kernel, n_x=len(xs), has_norm=gain is not None, has_res=residual is not None),
        grid=(M // tm,),
        in_specs=in_specs,
        out_specs=pl.BlockSpec((tm, N), lambda i: (i, 0)),
        out_shape=jax.ShapeDtypeStruct((M, N), jnp.float32),
        compiler_params=pltpu.CompilerParams(dimension_semantics=("parallel",), vmem_limit_bytes=_VMEM_LIMIT),
        name="mm",
    )(*args)


def _rmsnorm(x, g):
    return x * lax.rsqrt(jnp.mean(x * x, axis=-1, keepdims=True) + NORM_EPS) * g


def _rope_tables(pos):
    half = HEAD_DIM // 2
    freq = 1.0 / (ROPE_THETA ** (jnp.arange(half, dtype=jnp.float32) / half))
    ang = pos.astype(jnp.float32)[:, None] * freq[None, :]
    cos = jnp.cos(ang)
    sin = jnp.sin(ang)
    cos_t = jnp.tile(jnp.concatenate([cos, cos], axis=-1), (1, A_HEADS))
    sin_t = jnp.tile(jnp.concatenate([-sin, sin], axis=-1), (1, A_HEADS))
    return cos_t, sin_t


def _rope_apply(x, cos_t, sin_t):
    half = HEAD_DIM // 2
    lane = lax.broadcasted_iota(jnp.int32, x.shape, 1)
    first = (lane % HEAD_DIM) < half
    w = x.shape[1]
    swapped = jnp.where(first, pltpu.roll(x, w - half, axis=1), pltpu.roll(x, half, axis=1))
    return x * cos_t + swapped * sin_t


def _moba_prompt_kernel(q_ref, k_ref, v_ref, cos_ref, sin_ref, ya_ref, kr_ref, vo_ref, ot_scr):
    T = q_ref.shape[1]
    G = GROUP_WIDTH
    hd = HEAD_DIM
    blk = MOBA_BLOCK
    nb = T // blk
    cos_t = cos_ref[...]
    sin_t = sin_ref[...]
    q = _rope_apply(q_ref[0], cos_t, sin_t)
    k = _rope_apply(k_ref[0], cos_t, sin_t)
    v = v_ref[0]
    v_t = v.T
    kr_ref[0] = k.T
    vo_ref[0] = v_t
    kmean = jnp.concatenate([jnp.sum(k[n * blk:(n + 1) * blk, :], axis=0, keepdims=True) for n in range(nb)],
                            axis=0) * (1.0 / blk)
    qt = q.T.astype(jnp.bfloat16)
    vt = v_t.astype(jnp.bfloat16)
    kb = k.astype(jnp.bfloat16)
    lane_g = lax.broadcasted_iota(jnp.int32, (nb, G), 1)
    rown = lax.broadcasted_iota(jnp.int32, (nb, T), 0)
    cur = lax.broadcasted_iota(jnp.int32, (nb, T), 1) // blk
    kpos = lax.broadcasted_iota(jnp.int32, (blk, blk), 0)
    qpos = lax.broadcasted_iota(jnp.int32, (blk, blk), 1)
    causal = kpos <= qpos
    scale = hd ** -0.5
    for h in range(A_HEADS):
        km_h = jnp.where(lane_g // hd == h, kmean, 0.0)
        gate = lax.dot_general(km_h, q, (((1,), (1,)), ((), ())), precision=_HI,
                               preferred_element_type=jnp.float32)
        gate = jnp.where(rown < cur, gate, NEG_INF)
        sel_rows = []
        for n in range(nb):
            gn = gate[n:n + 1, :]
            beats = (gate > gn) | ((gate == gn) & (rown < n))
            rank = jnp.sum(beats.astype(jnp.float32), axis=0, keepdims=True)
            sel_rows.append(rank < MOBA_TOPK)
        k_h = kb[:, h * hd:(h + 1) * hd]
        qt_h = qt[h * hd:(h + 1) * hd, :]
        vt_h = vt[h * hd:(h + 1) * hd, :]
        for qi in range(nb):
            qs = slice(qi * blk, (qi + 1) * blk)
            q_blk = qt_h[:, qs]
            s = jnp.dot(k_h[qs, :], q_blk, preferred_element_type=jnp.float32) * scale
            s = jnp.where(causal, s, NEG_INF)
            m = jnp.max(s, axis=0, keepdims=True)
            p = jnp.exp(s - m)
            l = jnp.sum(p, axis=0, keepdims=True)
            acc = jnp.dot(vt_h[:, qs], p.astype(jnp.bfloat16), preferred_element_type=jnp.float32)
            for n in range(qi):
                ks = slice(n * blk, (n + 1) * blk)
                s = jnp.dot(k_h[ks, :], q_blk, preferred_element_type=jnp.float32) * scale
                s = jnp.where(sel_rows[n][:, qs], s, NEG_INF)
                m_new = jnp.maximum(m, jnp.max(s, axis=0, keepdims=True))
                alpha = jnp.exp(m - m_new)
                p = jnp.exp(s - m_new)
                l = l * alpha + jnp.sum(p, axis=0, keepdims=True)
                acc = acc * alpha + jnp.dot(vt_h[:, ks], p.astype(jnp.bfloat16), preferred_element_type=jnp.float32)
                m = m_new
            ot_scr[h * hd:(h + 1) * hd, qs] = acc / l
    ya_ref[0] = ot_scr[...].T


def _moba_prompt(proj3, pos):
    B, T, _ = proj3.shape
    G = GROUP_WIDTH
    cos_t, sin_t = _rope_tables(pos)
    col = lambda j: pl.BlockSpec((1, T, G), lambda b: (b, 0, j))
    tab = pl.BlockSpec((T, G), lambda b: (0, 0))
    out = pl.BlockSpec((1, T, G), lambda b: (b, 0, 0))
    out_t = pl.BlockSpec((1, G, T), lambda b: (b, 0, 0))
    shp = jax.ShapeDtypeStruct((B, T, G), jnp.float32)
    shp_t = jax.ShapeDtypeStruct((B, G, T), jnp.float32)
    return pl.pallas_call(
        _moba_prompt_kernel,
        grid=(B,),
        in_specs=[col(COL_Q), col(COL_K), col(COL_V), tab, tab],
        out_specs=[out, out_t, out_t],
        out_shape=[shp, shp_t, shp_t],
        scratch_shapes=[pltpu.VMEM((G, T), jnp.float32)],
        compiler_params=pltpu.CompilerParams(dimension_semantics=("parallel",), vmem_limit_bytes=_VMEM_LIMIT),
        name="moba_prompt",
    )(proj3, proj3, proj3, cos_t, sin_t)


def _kmean_kernel(pt_ref, *refs):
    pages, o_ref = refs[:-1], refs[-1]
    j = pl.program_id(1)
    nblk = len(pages) // PAGES_PER_BLOCK

    @pl.when(j == 0)
    def _():
        o_ref[...] = jnp.zeros_like(o_ref)

    lane = lax.broadcasted_iota(jnp.int32, o_ref.shape[1:], 2)
    acc = o_ref[0]
    for blk in range(nblk):
        s = None
        for i in range(PAGES_PER_BLOCK):
            part = pages[blk * PAGES_PER_BLOCK + i][0]
            s = part if s is None else s + part
        mean = jnp.sum(s, axis=-1, keepdims=True) * (1.0 / MOBA_BLOCK)
        acc = jnp.where(lane == j * nblk + blk, mean, acc)
    o_ref[0] = acc


def _moba_block_means(cache_kt, page_table):
    B, n_pages = page_table.shape
    H, hd = A_HEADS, HEAD_DIM
    assert n_pages % KMEAN_PAGES == 0 and n_pages // PAGES_PER_BLOCK <= LANES
    steps = n_pages // KMEAN_PAGES

    def page_spec(i):
        return pl.BlockSpec((1, H, hd, PAGE_SIZE), lambda b, j, pt: (pt[b * n_pages + j * KMEAN_PAGES + i], 0, 0, 0))

    return pl.pallas_call(
        _kmean_kernel,
        grid_spec=pltpu.PrefetchScalarGridSpec(
            num_scalar_prefetch=1,
            grid=(B, steps),
            in_specs=[page_spec(i) for i in range(KMEAN_PAGES)],
            out_specs=pl.BlockSpec((1, H, hd, LANES), lambda b, j, pt: (b, 0, 0, 0)),
        ),
        out_shape=jax.ShapeDtypeStruct((B, H, hd, LANES), jnp.float32),
        compiler_params=pltpu.CompilerParams(dimension_semantics=("parallel", "arbitrary"),
                                             vmem_limit_bytes=_VMEM_LIMIT),
        name="moba_block_means",
    )(page_table.reshape(-1), *([cache_kt] * KMEAN_PAGES))


def _moba_select_kernel(q_ref, k_ref, cos_ref, sin_ref, km_ref, qr_ref, kr_ref, top_ref, *, nb):
    q = _rope_apply(q_ref[0], cos_ref[...], sin_ref[...])
    k = _rope_apply(k_ref[0], cos_ref[...], sin_ref[...])
    qr_ref[0] = q
    kr_ref[0] = k
    qcol = q.T[:, 0:1]
    lane = lax.broadcasted_iota(jnp.int32, (1, LANES), 1)
    row8 = lax.broadcasted_iota(jnp.int32, (8, LANES), 0)
    lane8 = lax.broadcasted_iota(jnp.int32, (8, LANES), 1)
    top = jnp.zeros((8, LANES), jnp.int32)
    for h in range(A_HEADS):
        gate = jnp.sum(km_ref[0, h] * qcol[h * HEAD_DIM:(h + 1) * HEAD_DIM, :], axis=0, keepdims=True)
        gate = jnp.where(lane < nb, gate, NEG_INF)
        for r in range(MOBA_TOPK):
            m = jnp.max(gate, axis=1, keepdims=True)
            idx = jnp.min(jnp.where(gate == m, lane, LANES), axis=1, keepdims=True)
            top = jnp.where((row8 == r) & (lane8 == h), idx, top)
            gate = jnp.where(lane == idx, NEG_INF, gate)
    top_ref[0] = top


def _moba_select(q8, k8, pos, kmean_t, nb):
    B = q8.shape[0]
    G = GROUP_WIDTH
    cos_t, sin_t = _rope_tables(pos)
    row = pl.BlockSpec((1, 8, G), lambda b: (b, 0, 0))
    tab = pl.BlockSpec((1, G), lambda b: (0, 0))
    return pl.pallas_call(
        functools.partial(_moba_select_kernel, nb=nb),
        grid=(B,),
        in_specs=[row, row, tab, tab, pl.BlockSpec((1, A_HEADS, HEAD_DIM, LANES), lambda b: (b, 0, 0, 0))],
        out_specs=[row, row, pl.BlockSpec((1, 8, LANES), lambda b: (b, 0, 0))],
        out_shape=[jax.ShapeDtypeStruct((B, 8, G), jnp.float32), jax.ShapeDtypeStruct((B, 8, G), jnp.float32),
                   jax.ShapeDtypeStruct((B, 8, LANES), jnp.int32)],
        compiler_params=pltpu.CompilerParams(dimension_semantics=("parallel",)),
        name="moba_select",
    )(q8, k8, cos_t, sin_t, kmean_t)


def _moba_decode_attn_kernel(pp_ref, q_ref, kn_ref, vn_ref, *refs):
    npg = MOBA_TOPK * PAGES_PER_BLOCK
    kp, vp, o_ref = refs[:npg], refs[npg:2 * npg], refs[2 * npg]
    bf = lambda a: a.astype(jnp.bfloat16)
    qh = bf(q_ref[0, 0])
    scale = HEAD_DIM ** -0.5
    s_pages = [jnp.dot(qh, bf(kp[i][0, 0]), preferred_element_type=jnp.float32) * scale for i in range(npg)]
    s_own = jnp.sum(qh.astype(jnp.float32) * bf(kn_ref[0, 0]).astype(jnp.float32), axis=-1, keepdims=True) * scale
    m = s_own
    for s in s_pages:
        m = jnp.maximum(m, jnp.max(s, axis=-1, keepdims=True))
    p_own = jnp.exp(s_own - m)
    l = p_own
    acc = p_own * bf(vn_ref[0, 0]).astype(jnp.float32)
    for i in range(npg):
        p = jnp.exp(s_pages[i] - m)
        l = l + jnp.sum(p, axis=-1, keepdims=True)
        acc = acc + lax.dot_general(bf(p), bf(vp[i][0, 0]), (((1,), (1,)), ((), ())),
                                    preferred_element_type=jnp.float32)
    o_ref[0, 0] = acc / l


def _moba_decode_attn(qh, kh, vh, cache_kt, cache_vt, phys):
    B = qh.shape[0]
    H, hd = A_HEADS, HEAD_DIM
    npg = MOBA_TOPK * PAGES_PER_BLOCK
    row = pl.BlockSpec((1, 1, 8, hd), lambda b, h, pp: (b, h, 0, 0))

    def page_spec(i):
        return pl.BlockSpec((1, 1, hd, PAGE_SIZE), lambda b, h, pp: (pp[(b * H + h) * npg + i], h, 0, 0))

    return pl.pallas_call(
        _moba_decode_attn_kernel,
        grid_spec=pltpu.PrefetchScalarGridSpec(
            num_scalar_prefetch=1,
            grid=(B, H),
            in_specs=[row, row, row] + [page_spec(i) for i in range(npg)] * 2,
            out_specs=row,
        ),
        out_shape=jax.ShapeDtypeStruct((B, H, 8, hd), jnp.float32),
        compiler_params=pltpu.CompilerParams(dimension_semantics=("parallel", "arbitrary")),
        name="moba_decode_attn",
    )(phys.reshape(-1), qh, kh, vh, *([cache_kt] * npg), *([cache_vt] * npg))


def _moba_decode(q, k, v, cache_kt, cache_vt, page_table, pos0):
    B = q.shape[0]
    H, hd = A_HEADS, HEAD_DIM
    nb = page_table.shape[1] // PAGES_PER_BLOCK
    pad8 = lambda a: jnp.pad(a[:, None, :], ((0, 0), (0, 7), (0, 0)))
    kmean_t = _moba_block_means(cache_kt, page_table)
    q8, k8, top = _moba_select(pad8(q), pad8(k), jnp.full((1,), pos0, jnp.int32), kmean_t, nb)
    blocks = jnp.swapaxes(top[:, :MOBA_TOPK, :H], 1, 2)
    pages = (blocks[..., None] * PAGES_PER_BLOCK + jnp.arange(PAGES_PER_BLOCK)).reshape(B, -1)
    phys = jnp.take_along_axis(page_table, pages, axis=1)
    by_head = lambda a8: a8.reshape(B, 8, H, hd).transpose(0, 2, 1, 3)
    out = _moba_decode_attn(by_head(q8), by_head(k8), by_head(pad8(v)), cache_kt, cache_vt, phys)
    return out[:, :, 0, :].reshape(B, H * hd), k8[:, 0, :]


def _causal_conv_tile(xs, prev3, w_ref, b_ref):
    row = lax.broadcasted_iota(jnp.int32, xs.shape, 0)
    y = b_ref[...] + xs * w_ref[CONV_W - 1:CONV_W, :]
    for k in range(1, CONV_W):
        sh = pltpu.roll(xs, k, axis=0)
        for r in range(k):
            src = r + (CONV_W - 1) - k
            sh = jnp.where(row == r, prev3[src:src + 1, :], sh)
        y = y + sh * w_ref[CONV_W - 1 - k:CONV_W - k, :]
    return y


def _conv_tail(xs, prev3, n_valid):
    rows = []
    for i in range(CONV_W - 1):
        idx = n_valid - (CONV_W - 1) + i
        rows.append(xs[idx:idx + 1, :] if idx >= 0 else prev3[idx + CONV_W - 1:idx + CONV_W, :])
    return rows


def _lru_kernel(u_ref, gate_ref, cb0_ref, h0_ref, cw_ref, cbias_ref, wa_ref, wx_ref, vec_ref,
                y_ref, cbo_ref, ho_ref, conv_scr, h_scr, a_scr, b_scr, hs_scr, *, n_valid):
    c = pl.program_id(1)
    tc = u_ref.shape[1]

    @pl.when(c == 0)
    def _():
        conv_scr[...] = cb0_ref[0]
        h_scr[...] = h0_ref[0]

    us = u_ref[0]
    prev3 = conv_scr[...]
    xc = _causal_conv_tile(us, prev3, cw_ref, cbias_ref)
    tail = _conv_tail(us, prev3, n_valid)
    for i in range(CONV_W - 1):
        conv_scr[i:i + 1, :] = tail[i]
    xb = xc.astype(jnp.bfloat16)
    r = jax.nn.sigmoid(jnp.dot(xb, wa_ref[...], preferred_element_type=jnp.float32) + vec_ref[0:1, :])
    i_g = jax.nn.sigmoid(jnp.dot(xb, wx_ref[...], preferred_element_type=jnp.float32) + vec_ref[1:2, :])
    log_a = -LRU_C * r * _softplus(-vec_ref[2:3, :])
    a_scr[...] = jnp.exp(log_a)
    b_scr[...] = jnp.sqrt(-jnp.tanh(log_a) * (jnp.exp(2.0 * log_a) + 1.0)) * (i_g * xc)

    row8 = lax.broadcasted_iota(jnp.int32, (8, LRU_WIDTH), 0)

    def step8(i, h):
        t0 = pl.multiple_of(i * 8, 8)
        a8 = a_scr[pl.ds(t0, 8), :]
        b8 = b_scr[pl.ds(t0, 8), :]
        out = jnp.zeros((8, LRU_WIDTH), jnp.float32)
        for j in range(min(8, n_valid)):
            h = a8[j:j + 1, :] * h + b8[j:j + 1, :]
            out = jnp.where(row8 == j, h, out)
        hs_scr[pl.ds(t0, 8), :] = out
        return h

    if n_valid < tc:
        hs_scr[...] = jnp.zeros_like(hs_scr)
    h = lax.fori_loop(0, -(-n_valid // 8), step8, h_scr[...])
    h_scr[...] = h
    y_ref[0] = hs_scr[...] * _gelu_tanh(gate_ref[0])

    @pl.when(c == pl.num_programs(1) - 1)
    def _():
        cbo_ref[0] = conv_scr[...]
        ho_ref[0] = h_scr[...]


def _block_diag(w):
    n, d, e = w.shape
    eye = jnp.eye(n, dtype=w.dtype)
    return (eye[:, None, :, None] * w[:, :, None, :]).reshape(n * d, n * e)


def _lru_mixer(proj3, conv0, h0, conv_w, conv_b, wa, ba, wx, bx, lam, *, n_valid, tc):
    B, Tp, _ = proj3.shape
    G = GROUP_WIDTH
    f32 = jnp.float32
    nchunk = Tp // tc
    assert Tp % tc == 0 and (n_valid == tc or nchunk == 1) and (n_valid % 8 == 0 or n_valid < 8)
    cb0 = jnp.pad(conv0, ((0, 0), (0, 8 - (CONV_W - 1)), (0, 0)))
    vec = jnp.zeros((8, G), f32).at[0].set(ba).at[1].set(bx).at[2].set(lam)
    full2 = lambda shp: pl.BlockSpec(shp, lambda b, c: (0, 0))
    y, cbo, ho = pl.pallas_call(
        functools.partial(_lru_kernel, n_valid=n_valid),
        grid=(B, nchunk),
        in_specs=[pl.BlockSpec((1, tc, G), lambda b, c: (b, c, COL_U)),
                  pl.BlockSpec((1, tc, G), lambda b, c: (b, c, COL_GATE)),
                  pl.BlockSpec((1, 8, G), lambda b, c: (b, 0, 0)),
                  pl.BlockSpec((1, 1, G), lambda b, c: (b, 0, 0)),
                  full2((CONV_W, G)), full2((1, G)), full2((G, G)), full2((G, G)), full2((8, G))],
        out_specs=[pl.BlockSpec((1, tc, G), lambda b, c: (b, c, 0)),
                   pl.BlockSpec((1, 8, G), lambda b, c: (b, 0, 0)),
                   pl.BlockSpec((1, 1, G), lambda b, c: (b, 0, 0))],
        out_shape=[jax.ShapeDtypeStruct((B, Tp, G), f32), jax.ShapeDtypeStruct((B, 8, G), f32),
                   jax.ShapeDtypeStruct((B, 1, G), f32)],
        scratch_shapes=[pltpu.VMEM((8, G), f32), pltpu.VMEM((1, G), f32),
                        pltpu.VMEM((tc, G), f32), pltpu.VMEM((tc, G), f32), pltpu.VMEM((tc, G), f32)],
        compiler_params=pltpu.CompilerParams(dimension_semantics=("parallel", "arbitrary"),
                                             vmem_limit_bytes=_VMEM_LIMIT),
        name="lru_mixer",
    )(proj3, proj3, cb0, h0[:, None, :], conv_w, conv_b[None, :],
      _block_diag(wa).astype(jnp.bfloat16), _block_diag(wx).astype(jnp.bfloat16), vec)
    return y, cbo[:, :CONV_W - 1, :], ho[:, 0, :]


def _ssd_kernel(z_ref, xbc_ref, dt_ref, cb0_ref, s0_ref, cw_ref, cbias_ref, prow_ref, pcol_ref, dvec_ref, nw_ref,
                y_ref, cbo_ref, so_ref, conv_scr, st_scr, y_scr, *, n_valid):
    c = pl.program_id(1)
    tc = xbc_ref.shape[1]
    Q = SSD_CHUNK
    G = GROUP_WIDTH
    P = SSD_HEADDIM
    N = SSD_STATE
    GN = SSD_GROUPS * SSD_STATE

    @pl.when(c == 0)
    def _():
        conv_scr[...] = cb0_ref[0]
        st_scr[...] = s0_ref[0]

    xs = xbc_ref[0]
    prev3 = conv_scr[...]
    xbc_f = _silu(_causal_conv_tile(xs, prev3, cw_ref, cbias_ref))
    tail = _conv_tail(xs, prev3, n_valid)
    for i in range(CONV_W - 1):
        conv_scr[i:i + 1, :] = tail[i]

    ii = lax.broadcasted_iota(jnp.int32, (Q, Q), 0)
    jj = lax.broadcasted_iota(jnp.int32, (Q, Q), 1)
    lower = ii >= jj
    tri_l = lower.astype(jnp.float32)
    tri_u = (ii <= jj).astype(jnp.float32)
    bias_row = prow_ref[0:1, :]
    alog_row = prow_ref[1:2, :]
    bias_col = pcol_ref[:, 0:1]
    alog_col = pcol_ref[:, 1:2]
    rowq = lax.broadcasted_iota(jnp.int32, (Q, LANES), 0)
    laneq = lax.broadcasted_iota(jnp.int32, (8, Q), 1)
    bf = lambda a: a.astype(jnp.bfloat16)
    for j in range(tc // Q):
        rows = slice(j * Q, (j + 1) * Q)
        xq = xbc_f[rows, :]
        xt = xq.T
        dcol = dt_ref[0, rows, 0:LANES]
        drow = dcol.T[0:8, :]
        dt_col = _softplus(dcol + bias_row)
        dt_row = _softplus(drow + bias_col)
        if n_valid < tc:
            dt_col = jnp.where(rowq + j * Q < n_valid, dt_col, 0.0)
            dt_row = jnp.where(laneq + j * Q < n_valid, dt_row, 0.0)
        da_col = dt_col * (-jnp.exp(alog_row))
        da_row = dt_row * (-jnp.exp(alog_col))
        cs_col = jnp.dot(tri_l, da_col, precision=_HI, preferred_element_type=jnp.float32)
        cs_row = jnp.dot(da_row, tri_u, precision=_HI, preferred_element_type=jnp.float32)
        for g in range(SSD_GROUPS):
            bg = xq[:, G + g * N:G + (g + 1) * N]
            cg = xq[:, G + GN + g * N:G + GN + (g + 1) * N]
            bgt = xt[G + g * N:G + (g + 1) * N, :]
            cb = lax.dot_general(bf(cg), bf(bg), (((1,), (1,)), ((), ())), preferred_element_type=jnp.float32)
            for h in range(g * (SSD_HEADS // SSD_GROUPS), (g + 1) * (SSD_HEADS // SSD_GROUPS)):
                ci = cs_col[:, h:h + 1]
                cj = cs_row[h:h + 1, :]
                c_last = cs_row[h:h + 1, Q - 1:Q]
                lm = jnp.exp(jnp.where(lower, ci - cj, NEG_INF))
                x_h = xq[:, h * P:(h + 1) * P]
                xdt = x_h * dt_col[:, h:h + 1]
                st = st_scr[h]
                y_h = jnp.dot(bf(cb * lm), bf(xdt), preferred_element_type=jnp.float32)
                y_h = y_h + jnp.dot(bf(cg * jnp.exp(ci)), bf(st), preferred_element_type=jnp.float32)
                y_h = y_h + dvec_ref[:, h * P:(h + 1) * P] * x_h
                y_scr[rows, h * P:(h + 1) * P] = y_h
                st_scr[h] = st * jnp.exp(c_last) + jnp.dot(bf(bgt), bf(xdt * jnp.exp(c_last - ci)),
                                                            preferred_element_type=jnp.float32)
    y = y_scr[...] * _silu(z_ref[0])
    y_ref[0] = y * lax.rsqrt(jnp.mean(y * y, axis=-1, keepdims=True) + NORM_EPS) * nw_ref[...]

    @pl.when(c == pl.num_programs(1) - 1)
    def _():
        cbo_ref[0] = conv_scr[...]
        so_ref[0] = st_scr[...]


def _ssd_mixer(proj3, conv0, s0, conv_w, conv_b, dt_bias, a_log, d_skip, norm_w, *, n_valid, tc):
    B, Tp, _ = proj3.shape
    G = GROUP_WIDTH
    H, P, N = SSD_HEADS, SSD_HEADDIM, SSD_STATE
    C = SSD_CONV_CH
    f32 = jnp.float32
    nchunk = Tp // tc
    assert Tp % tc == 0 and tc % SSD_CHUNK == 0 and (n_valid == tc or nchunk == 1)
    cb0 = jnp.pad(conv0, ((0, 0), (0, 8 - (CONV_W - 1)), (0, 0)))
    s0t = jnp.swapaxes(s0, -1, -2)
    prow = jnp.zeros((8, LANES), f32).at[0, :H].set(dt_bias).at[1, :H].set(a_log)
    pcol = jnp.zeros((8, LANES), f32).at[:H, 0].set(dt_bias).at[:H, 1].set(a_log)
    dvec = jnp.repeat(d_skip, P)[None, :]
    full2 = lambda shp: pl.BlockSpec(shp, lambda b, c: (0, 0))
    y, cbo, so = pl.pallas_call(
        functools.partial(_ssd_kernel, n_valid=n_valid),
        grid=(B, nchunk),
        in_specs=[pl.BlockSpec((1, tc, G), lambda b, c: (b, c, COL_Z)),
                  pl.BlockSpec((1, tc, C), lambda b, c: (b, c, COL_XBC)),
                  pl.BlockSpec((1, tc, G), lambda b, c: (b, c, COL_DT)),
                  pl.BlockSpec((1, 8, C), lambda b, c: (b, 0, 0)),
                  pl.BlockSpec((1, H, N, P), lambda b, c: (b, 0, 0, 0)),
                  full2((CONV_W, C)), full2((1, C)), full2((8, LANES)), full2((8, LANES)), full2((1, G)),
                  full2((1, G))],
        out_specs=[pl.BlockSpec((1, tc, G), lambda b, c: (b, c, 0)),
                   pl.BlockSpec((1, 8, C), lambda b, c: (b, 0, 0)),
                   pl.BlockSpec((1, H, N, P), lambda b, c: (b, 0, 0, 0))],
        out_shape=[jax.ShapeDtypeStruct((B, Tp, G), f32), jax.ShapeDtypeStruct((B, 8, C), f32),
                   jax.ShapeDtypeStruct((B, H, N, P), f32)],
        scratch_shapes=[pltpu.VMEM((8, C), f32), pltpu.VMEM((H, N, P), f32), pltpu.VMEM((tc, G), f32)],
        compiler_params=pltpu.CompilerParams(dimension_semantics=("parallel", "arbitrary"),
                                             vmem_limit_bytes=_VMEM_LIMIT),
        name="ssd_mixer",
    )(proj3, proj3, proj3, cb0, s0t, conv_w, conv_b[None, :], prow, pcol, dvec, norm_w[None, :])
    return y, cbo[:, :CONV_W - 1, :], jnp.swapaxes(so, -1, -2)


def _rwkv_pad_cols(a):
    G = GROUP_WIDTH
    z = lambda w: jnp.zeros(a.shape[:-1] + (w,), a.dtype)
    o = 3 * G
    wd = a[..., o:o + RWKV_W_RANK]
    ad = a[..., o + RWKV_W_RANK:o + RWKV_W_RANK + RWKV_A_RANK]
    gd = a[..., o + RWKV_W_RANK + RWKV_A_RANK:]
    return jnp.concatenate([a[..., :o], wd, z(LANES - RWKV_W_RANK), ad, z(LANES - RWKV_A_RANK),
                            gd, z(LANES - RWKV_G_RANK)], axis=-1)


def _rwkv_unpad_cols(a):
    o = 3 * GROUP_WIDTH
    return jnp.concatenate([a[..., :o], a[..., o:o + RWKV_W_RANK], a[..., o + LANES:o + LANES + RWKV_A_RANK],
                            a[..., o + 2 * LANES:o + 2 * LANES + RWKV_G_RANK]], axis=-1)


def _rwkv_kernel(pd_ref, sh0_ref, s0_ref, mu_ref, vec_ref, wup_ref, aup_ref, gup_ref, seg_ref,
                 y_ref, sout_ref, shout_ref,
                 prev_scr, s_scr, r_scr, k_scr, d_scr, kk_scr, bb_scr, g_scr, bonus_scr, vt_scr, yt_scr, *, n_steps):
    c = pl.program_id(1)
    tc = pd_ref.shape[1]
    G = GROUP_WIDTH
    hd = RWKV_HEADDIM

    @pl.when(c == 0)
    def _():
        prev_scr[...] = sh0_ref[...]
        s_scr[...] = s0_ref[...]

    nb = pd_ref.shape[0]
    w0 = vec_ref[0:1, :]
    a0 = vec_ref[1:2, :]
    k_k = vec_ref[2:3, :]
    k_a = vec_ref[3:4, :]
    ln_w = vec_ref[4:5, :]
    ln_b = vec_ref[5:6, :]
    r_k = vec_ref[6:7, :]
    seg = seg_ref[...]
    hdot = lambda x, y: jnp.dot(x, y, precision=_HI, preferred_element_type=jnp.float32)
    for ib in range(nb):
        cur = pd_ref[ib]
        row = lax.broadcasted_iota(jnp.int32, cur.shape, 0)
        prev = jnp.where(row == 0, prev_scr[ib], pltpu.roll(cur, 1, axis=0))
        prev_scr[ib] = cur[n_steps - 1:n_steps, :] if n_steps < tc else cur[tc - 1:tc, :]
        m = cur + (prev - cur) * mu_ref[...]
        r = m[:, 0:G]
        k = m[:, G:2 * G]
        v = m[:, 2 * G:3 * G]
        wd = m[:, 3 * G:3 * G + LANES]
        ad = m[:, 3 * G + LANES:3 * G + 2 * LANES]
        gd = m[:, 3 * G + 2 * LANES:3 * G + 3 * LANES]
        w = -_softplus(-(w0 + hdot(jnp.tanh(wd), wup_ref[...]))) - 0.5
        a = jax.nn.sigmoid(a0 + hdot(ad, aup_ref[...]))
        kkr = k * k_k
        kk = kkr / jnp.maximum(jnp.sqrt(hdot(kkr * kkr, seg)), 1e-12)
        k2 = k * (1.0 + (a - 1.0) * k_a)
        g_scr[ib] = hdot(jax.nn.sigmoid(gd), gup_ref[...])
        bonus_scr[ib] = hdot(r * k2 * r_k, seg) * v
        r_scr[ib] = r
        k_scr[ib] = k2
        d_scr[ib] = jnp.exp(-jnp.exp(w))
        kk_scr[ib] = -kk
        bb_scr[ib] = kk * a
        vt_scr[ib] = v.T
    yt_scr[...] = jnp.zeros_like(yt_scr)

    lane = lax.broadcasted_iota(jnp.int32, (hd, LANES), 1)
    lo_half = lane < hd

    def half_sums(p):
        lo = jnp.sum(jnp.where(lo_half, p, 0.0), axis=1, keepdims=True)
        hi = jnp.sum(jnp.where(lo_half, 0.0, p), axis=1, keepdims=True)
        return lo, hi

    npair = RWKV_HEADS // 2

    def step8(i, states):
        t0 = pl.multiple_of(i * 8, 8)
        blk = pl.multiple_of((t0 // LANES) * LANES, LANES)
        rows = pl.ds(t0, 8)
        chains = [(ib, p) for ib in range(nb) for p in range(npair)]
        states = list(states)
        rowv = {}
        tiles = {}
        for ci, (ib, p) in enumerate(chains):
            cols = slice(p * LANES, (p + 1) * LANES)
            rowv[ci] = tuple(ref[ib, rows, cols] for ref in (kk_scr, d_scr, bb_scr, k_scr, r_scr))
            tile = lambda ref, hh: ref[ib, pl.ds(p * LANES + hh * hd, hd), pl.ds(blk, LANES)]
            tiles[ci] = [tile(vt_scr, 0), tile(vt_scr, 1), tile(yt_scr, 0), tile(yt_scr, 1)]

        def vcol_of(ci, j):
            hit = lane == (t0 + j - blk)
            vc0 = jnp.sum(jnp.where(hit, tiles[ci][0], 0.0), axis=1, keepdims=True)
            vc1 = jnp.sum(jnp.where(hit, tiles[ci][1], 0.0), axis=1, keepdims=True)
            return jnp.where(lo_half, vc0, vc1)

        def emit_y(ci, j):
            hit = lane == (t0 + j - blk)
            ylo, yhi = half_sums(states[ci] * rowv[ci][4][j:j + 1, :])
            tiles[ci][2] = jnp.where(hit, ylo, tiles[ci][2])
            tiles[ci][3] = jnp.where(hit, yhi, tiles[ci][3])

        nj = min(8, n_steps)
        vcols = [vcol_of(ci, 0) for ci in range(len(chains))]
        for j in range(nj):
            sas = [half_sums(states[ci] * rowv[ci][0][j:j + 1, :]) for ci in range(len(chains))]
            if j > 0:
                for ci in range(len(chains)):
                    emit_y(ci, j - 1)
            nxt = [vcol_of(ci, j + 1) for ci in range(len(chains))] if j + 1 < nj else None
            for ci in range(len(chains)):
                _, d8, bb8, k8, _ = rowv[ci]
                sa = jnp.where(lo_half, sas[ci][0], sas[ci][1])
                states[ci] = states[ci] * d8[j:j + 1, :] + sa * bb8[j:j + 1, :] + vcols[ci] * k8[j:j + 1, :]
            vcols = nxt
        for ci in range(len(chains)):
            emit_y(ci, nj - 1)
        for ci, (ib, p) in enumerate(chains):
            yt_scr[ib, pl.ds(p * LANES, hd), pl.ds(blk, LANES)] = tiles[ci][2]
            yt_scr[ib, pl.ds(p * LANES + hd, hd), pl.ds(blk, LANES)] = tiles[ci][3]
        return tuple(states)

    init = tuple(s_scr[ib, p] for ib in range(nb) for p in range(npair))
    states = lax.fori_loop(0, -(-n_steps // 8), step8, init)
    for ib in range(nb):
        for p in range(npair):
            s_scr[ib, p] = states[ib * npair + p]

    inv = 1.0 / hd
    for ib in range(nb):
        y = yt_scr[ib].T
        mean = hdot(y, seg) * inv
        yc = y - mean
        var = hdot(yc * yc, seg) * inv
        yn = yc * lax.rsqrt(var + RWKV_LN_EPS) * ln_w + ln_b
        y_ref[ib] = (yn + bonus_scr[ib]) * g_scr[ib]

    @pl.when(c == pl.num_programs(1) - 1)
    def _():
        sout_ref[...] = s_scr[...]
        shout_ref[...] = prev_scr[...]


def _rwkv_mixer(proj3, shift0, s0, mu, w0, w_up, a0, a_up, g_up, k_k, k_a, r_k, ln_w, ln_b, *, n_steps, tc, nb=1):
    B, Tp, _ = proj3.shape
    G = GROUP_WIDTH
    H, hd = RWKV_HEADS, RWKV_HEADDIM
    nchunk = Tp // tc
    assert Tp % tc == 0 and (n_steps == tc or nchunk == 1) and (n_steps % 8 == 0 or n_steps < 8)
    assert B % nb == 0
    f32 = jnp.float32
    sh0 = _rwkv_pad_cols(shift0)[:, None, :]
    s0p = s0.reshape(B, H // 2, 2, hd, hd).transpose(0, 1, 3, 2, 4).reshape(B, H // 2, hd, 2 * hd)
    mu_p = _rwkv_pad_cols(mu[None, :])
    vec = jnp.stack([w0, a0, k_k, k_a, ln_w, ln_b, r_k.reshape(G), jnp.zeros((G,), f32)])
    padk = lambda wgt: jnp.pad(wgt, ((0, LANES - wgt.shape[0]), (0, 0)))
    hid = jnp.arange(G) // hd
    seg = (hid[:, None] == hid[None, :]).astype(f32)
    full2 = lambda shp: pl.BlockSpec(shp, lambda b, c: (0, 0))
    tscr = lambda: pltpu.VMEM((nb, tc, G), f32)
    y, s_out, sh_out = pl.pallas_call(
        functools.partial(_rwkv_kernel, n_steps=n_steps),
        grid=(B // nb, nchunk),
        in_specs=[pl.BlockSpec((nb, tc, RWKV_PCOLS), lambda b, c: (b, c, COL_RWKV)),
                  pl.BlockSpec((nb, 1, RWKV_PCOLS), lambda b, c: (b, 0, 0)),
                  pl.BlockSpec((nb, H // 2, hd, 2 * hd), lambda b, c: (b, 0, 0, 0)),
                  full2((1, RWKV_PCOLS)), full2((8, G)), full2((LANES, G)), full2((LANES, G)), full2((LANES, G)),
                  full2((G, G))],
        out_specs=[pl.BlockSpec((nb, tc, G), lambda b, c: (b, c, 0)),
                   pl.BlockSpec((nb, H // 2, hd, 2 * hd), lambda b, c: (b, 0, 0, 0)),
                   pl.BlockSpec((nb, 1, RWKV_PCOLS), lambda b, c: (b, 0, 0))],
        out_shape=[jax.ShapeDtypeStruct((B, Tp, G), f32),
                   jax.ShapeDtypeStruct((B, H // 2, hd, 2 * hd), f32),
                   jax.ShapeDtypeStruct((B, 1, RWKV_PCOLS), f32)],
        scratch_shapes=[pltpu.VMEM((nb, 1, RWKV_PCOLS), f32), pltpu.VMEM((nb, H // 2, hd, 2 * hd), f32),
                        tscr(), tscr(), tscr(), tscr(), tscr(), tscr(), tscr(),
                        pltpu.VMEM((nb, G, tc), f32), pltpu.VMEM((nb, G, tc), f32)],
        compiler_params=pltpu.CompilerParams(dimension_semantics=("parallel", "arbitrary"),
                                             vmem_limit_bytes=_VMEM_LIMIT),
        name="rwkv_mixer",
    )(proj3, sh0, s0p, mu_p, vec, padk(w_up), padk(a_up), padk(g_up), seg)
    s_new = s_out.reshape(B, H // 2, hd, 2, hd).transpose(0, 1, 3, 2, 4).reshape(B, H, hd, hd)
    return y, _rwkv_unpad_cols(sh_out[:, 0, :]), s_new


def _xattn_kernel(x_ref, g_ref, wq_ref, wo_ref, mk_ref, mv_ref, o_ref, cat_scr):
    x = x_ref[...]
    hn = x * lax.rsqrt(jnp.mean(x * x, axis=-1, keepdims=True) + NORM_EPS) * g_ref[...]
    q = jnp.dot(hn.astype(jnp.bfloat16), wq_ref[...], preferred_element_type=jnp.float32)
    qb = q.astype(jnp.bfloat16)
    mk = mk_ref[0].astype(jnp.bfloat16)
    mv = mv_ref[0].astype(jnp.bfloat16)
    hd = X_HEADDIM
    scale = hd ** -0.5
    for h in range(X_HEADS):
        cols = slice(h * hd, (h + 1) * hd)
        s = lax.dot_general(qb[:, cols], mk[:, cols], (((1,), (1,)), ((), ())),
                            preferred_element_type=jnp.float32) * scale
        m = jnp.max(s, axis=-1, keepdims=True)
        p = jnp.exp(s - m)
        p = p / jnp.sum(p, axis=-1, keepdims=True)
        cat_scr[:, cols] = jnp.dot(p.astype(jnp.bfloat16), mv[:, cols], preferred_element_type=jnp.float32)
    o_ref[...] = x + jnp.dot(cat_scr[...].astype(jnp.bfloat16), wo_ref[...], preferred_element_type=jnp.float32)


def _xattn_block(x, gain, wq, wo, mk, mv, *, tb, rows_per_mem):
    n = x.shape[0]
    M = mk.shape[1]
    assert n % tb == 0 and rows_per_mem % tb == 0
    per = rows_per_mem // tb
    return pl.pallas_call(
        _xattn_kernel,
        grid=(n // tb,),
        in_specs=[pl.BlockSpec((tb, D_MODEL), lambda i: (i, 0)),
                  pl.BlockSpec((1, D_MODEL), lambda i: (0, 0)),
                  pl.BlockSpec((D_MODEL, D_MODEL), lambda i: (0, 0)),
                  pl.BlockSpec((D_MODEL, D_MODEL), lambda i: (0, 0)),
                  pl.BlockSpec((1, M, D_MODEL), lambda i: (i // per, 0, 0)),
                  pl.BlockSpec((1, M, D_MODEL), lambda i: (i // per, 0, 0))],
        out_specs=pl.BlockSpec((tb, D_MODEL), lambda i: (i, 0)),
        out_shape=jax.ShapeDtypeStruct((n, D_MODEL), jnp.float32),
        scratch_shapes=[pltpu.VMEM((tb, D_MODEL), jnp.float32)],
        compiler_params=pltpu.CompilerParams(dimension_semantics=("parallel",), vmem_limit_bytes=_VMEM_LIMIT),
        name="xattn_block",
    )(x, gain.reshape(1, -1), wq.astype(jnp.bfloat16), wo.astype(jnp.bfloat16), mk, mv)


def _top_values(s, count, with_rank=False):
    vals = []
    cur = s
    rank = jnp.full(s.shape, float(count), jnp.float32) if with_rank else None
    for r in range(count):
        m = jnp.max(cur, axis=0, keepdims=True)
        vals.append(m)
        hit = cur == m
        if with_rank:
            rank = jnp.where(hit, float(r), rank)
        cur = jnp.where(hit, NEG_INF, cur)
    return (vals, rank) if with_rank else vals


def _peer_route_kernel(x_ref, g_ref, wqt_ref, sk_ref, hn_ref, cnt_ref, e1_ref, r2_ref, e2_ref, a1_scr, a2_scr):
    tb = x_ref.shape[0]
    x = x_ref[...]
    hn = x * lax.rsqrt(jnp.mean(x * x, axis=-1, keepdims=True) + NORM_EPS) * g_ref[...]
    hb = hn.astype(jnp.bfloat16)
    hn_ref[...] = hb
    qt = lax.dot_general(wqt_ref[...], hb, (((1,), (1,)), ((), ())), preferred_element_type=jnp.float32)
    n_top = PEER_TOPK
    row8 = lax.broadcasted_iota(jnp.int32, (8, tb), 0)
    for h in range(PEER_HEADS):
        q1 = qt[(2 * h) * PEER_HALF:(2 * h + 1) * PEER_HALF, :].astype(jnp.bfloat16)
        q2 = qt[(2 * h + 1) * PEER_HALF:(2 * h + 2) * PEER_HALF, :].astype(jnp.bfloat16)
        s1 = jnp.dot(sk_ref[2 * h], q1, preferred_element_type=jnp.float32)
        s2 = jnp.dot(sk_ref[2 * h + 1], q2, preferred_element_type=jnp.float32)
        v1 = _top_values(s1, n_top)
        v2, rank2 = _top_values(s2, n_top, with_rank=True)
        for r in range(n_top):
            a1_scr[r:r + 1, :] = v1[r]
            a2_scr[r:r + 1, :] = v2[r]
        a1 = a1_scr[...]
        a2 = a2_scr[...]
        pieces = [a1 + a2[0:1, :]]
        for q in range(1, 8):
            lim = n_top // (q + 1)
            pieces.append(jnp.where(row8 < lim, a1[0:8, :] + a2[q:q + 1, :], NEG_INF))
        pieces.append(a1[0:1, :] + a2[8:16, :])
        cand = jnp.concatenate(pieces, axis=0)
        c = _top_values(cand, n_top)
        tau = c[PEER_TOPK - 1]
        mx = c[0]
        z = jnp.sum(jnp.where(cand >= tau, jnp.exp(cand - mx), 0.0), axis=0, keepdims=True)
        cnt = jnp.zeros(s1.shape, jnp.float32)
        for q in range(n_top):
            cnt = cnt + jnp.where(s1 + v2[q] >= tau, 1.0, 0.0)
        cnt_ref[h] = cnt
        e1_ref[h] = jnp.exp(s1 - v1[0]) / z
        r2_ref[h] = rank2.astype(jnp.bfloat16)
        e2_ref[h] = jnp.exp(s2 - v2[0]).astype(jnp.bfloat16)


def _peer_route(x, gain, wqt_b, sk_b, tb=256):
    n = x.shape[0]
    assert n % tb == 0
    hk = PEER_HEADS
    blk3 = pl.BlockSpec((hk, PEER_NKEYS, tb), lambda i: (0, 0, i))
    shp_f = jax.ShapeDtypeStruct((hk, PEER_NKEYS, n), jnp.float32)
    shp_b = jax.ShapeDtypeStruct((hk, PEER_NKEYS, n), jnp.bfloat16)
    return pl.pallas_call(
        _peer_route_kernel,
        grid=(n // tb,),
        in_specs=[pl.BlockSpec((tb, D_MODEL), lambda i: (i, 0)),
                  pl.BlockSpec((1, D_MODEL), lambda i: (0, 0)),
                  pl.BlockSpec((2 * hk * PEER_HALF, D_MODEL), lambda i: (0, 0)),
                  pl.BlockSpec((2 * hk, PEER_NKEYS, PEER_HALF), lambda i: (0, 0, 0))],
        out_specs=[pl.BlockSpec((tb, D_MODEL), lambda i: (i, 0)), blk3, blk3, blk3, blk3],
        out_shape=[jax.ShapeDtypeStruct((n, D_MODEL), jnp.bfloat16), shp_f, shp_f, shp_b, shp_b],
        scratch_shapes=[pltpu.VMEM((PEER_TOPK, tb), jnp.float32), pltpu.VMEM((PEER_TOPK, tb), jnp.float32)],
        compiler_params=pltpu.CompilerParams(dimension_semantics=("parallel",), vmem_limit_bytes=_VMEM_LIMIT),
        name="peer_route",
    )(x, gain, wqt_b, sk_b)


def _peer_dense_kernel(x_ref, hn_ref, u_ref, v_ref, cnt_ref, e1_ref, r2_ref, e2_ref, o_ref, acc_ref, w_ref, *, ic):
    e = pl.program_id(1)
    tb = x_ref.shape[0]

    @pl.when(e == 0)
    def _():
        acc_ref[...] = jnp.zeros_like(acc_ref)

    act = lax.dot_general(u_ref[...], hn_ref[...], (((1,), (1,)), ((), ())), preferred_element_type=jnp.float32)
    for il in range(ic):
        g = None
        for h in range(PEER_HEADS):
            cb = jnp.broadcast_to(cnt_ref[h, il:il + 1, :], (16, tb)).astype(jnp.bfloat16)[None]
            eb = jnp.broadcast_to(e1_ref[h, il:il + 1, :], (16, tb)).astype(jnp.bfloat16)[None]
            r2 = r2_ref[h].reshape(PEER_NKEYS // 16, 16, tb)
            e2 = e2_ref[h].reshape(PEER_NKEYS // 16, 16, tb)
            t = jnp.where(r2 < cb, e2 * eb, jnp.zeros_like(e2))
            g = t if g is None else g + t
        rows = slice(il * PEER_NKEYS, (il + 1) * PEER_NKEYS)
        a = act[rows, :].astype(jnp.bfloat16).reshape(PEER_NKEYS // 16, 16, tb)
        w_ref[rows, :] = (g * _gelu_tanh(a)).reshape(PEER_NKEYS, tb)
    acc_ref[...] += lax.dot_general(w_ref[...], v_ref[...], (((0,), (0,)), ((), ())),
                                    preferred_element_type=jnp.float32)

    @pl.when(e == pl.num_programs(1) - 1)
    def _():
        o_ref[...] = x_ref[...] + acc_ref[...]


def _peer_dense(x, hn_b, u_b, v_b, cnt, e1, r2, e2, tb=PEER_TOK_PAD, ic=16):
    n = x.shape[0]
    ne = u_b.shape[0]
    ec = ic * PEER_NKEYS
    assert n % tb == 0 and ne % ec == 0
    hk = PEER_HEADS
    row_blk = pl.BlockSpec((hk, ic, tb), lambda i, e: (0, e, i))
    full_blk = pl.BlockSpec((hk, PEER_NKEYS, tb), lambda i, e: (0, 0, i))
    return pl.pallas_call(
        functools.partial(_peer_dense_kernel, ic=ic),
        grid=(n // tb, ne // ec),
        in_specs=[pl.BlockSpec((tb, D_MODEL), lambda i, e: (i, 0)),
                  pl.BlockSpec((tb, D_MODEL), lambda i, e: (i, 0)),
                  pl.BlockSpec((ec, D_MODEL), lambda i, e: (e, 0)),
                  pl.BlockSpec((ec, D_MODEL), lambda i, e: (e, 0)),
                  row_blk, row_blk, full_blk, full_blk],
        out_specs=pl.BlockSpec((tb, D_MODEL), lambda i, e: (i, 0)),
        out_shape=jax.ShapeDtypeStruct((n, D_MODEL), jnp.float32),
        scratch_shapes=[pltpu.VMEM((tb, D_MODEL), jnp.float32), pltpu.VMEM((ec, tb), jnp.bfloat16)],
        compiler_params=pltpu.CompilerParams(dimension_semantics=("parallel", "arbitrary"),
                                             vmem_limit_bytes=_VMEM_LIMIT),
        name="peer_dense",
    )(x, hn_b, u_b, v_b, cnt, e1, r2, e2)


def _peer_block(x, gain, wq, subkeys, u_tab, v_tab):
    wqt_b = wq.T.astype(jnp.bfloat16)
    sk_b = subkeys.reshape(2 * PEER_HEADS, PEER_NKEYS, PEER_HALF).astype(jnp.bfloat16)
    u_b = u_tab.astype(jnp.bfloat16)
    v_b = v_tab.astype(jnp.bfloat16)
    hn_b, cnt, e1, r2, e2 = _peer_route(x, gain.reshape(1, -1), wqt_b, sk_b)
    return _peer_dense(x, hn_b, u_b, v_b, cnt, e1, r2, e2)


def _in_weight(w_in):
    abc = w_in[:, :D_OFF]
    pad = jnp.zeros((w_in.shape[0], ABC_PCOLS - D_OFF), w_in.dtype)
    return jnp.concatenate([abc, pad, _rwkv_pad_cols(w_in[:, D_OFF:])], axis=1)


def _layer_pre_peer(x2, bsz, T, lp, st, prompt):
    G = GROUP_WIDTH
    n = bsz * T
    proj = _mm(x2, lp['w_in_p'], gain=lp['norm_mix'])
    if prompt:
        n_valid, tc = SEQ_CHUNK, SEQ_CHUNK
        proj3 = proj.reshape(bsz, T, -1)
        ya, k_t, v_t = _moba_prompt(proj3, jnp.arange(T, dtype=jnp.int32))
        heads = lambda t: t.reshape(bsz, A_HEADS, HEAD_DIM, T).transpose(0, 3, 1, 2)
        k_new, v_new = heads(k_t), heads(v_t)
    else:
        assert T == 1
        n_valid, tc = T, DEC_ROWS
        ya, k_r = _moba_decode(proj[:, 0:G], proj[:, G:2 * G], proj[:, 2 * G:3 * G],
                               st['k_cache'], st['v_cache'], st['page_table'], PAST_LEN)
        heads = lambda t: t.reshape(bsz, T, A_HEADS, HEAD_DIM)
        k_new, v_new = heads(k_r), heads(proj[:, 2 * G:3 * G])
        proj3 = jnp.pad(proj[:, None, :], ((0, 0), (0, tc - T), (0, 0)))
    yb, lru_conv, lru_h = _lru_mixer(proj3, st['lru_conv'], st['lru_h'], lp['lru_conv_w'], lp['lru_conv_b'],
                                     lp['lru_wa'], lp['lru_ba'], lp['lru_wx'], lp['lru_bx'], lp['lru_lambda'],
                                     n_valid=n_valid, tc=tc)
    yc, ssd_conv, ssd_s = _ssd_mixer(proj3, st['ssd_conv'], st['ssd'], lp['ssd_conv_w'], lp['ssd_conv_b'],
                                     lp['ssd_dt_bias'], lp['ssd_a_log'], lp['ssd_d'], lp['ssd_norm'],
                                     n_valid=n_valid, tc=tc)
    yd, rwkv_shift, rwkv_s = _rwkv_mixer(proj3, st['rwkv_shift'], st['rwkv'],
                                         lp['rwkv_mu'], lp['rwkv_w0'], lp['rwkv_w_up'], lp['rwkv_a0'],
                                         lp['rwkv_a_up'], lp['rwkv_g_up'], lp['rwkv_k_k'], lp['rwkv_k_a'],
                                         lp['rwkv_r_k'], lp['rwkv_ln_w'], lp['rwkv_ln_b'], n_steps=n_valid, tc=tc)
    rows = lambda y: y[:, :T].reshape(n, G)
    x2 = _mm([ya.reshape(n, G), rows(yb), rows(yc), rows(yd)], lp['w_out'], residual=x2)
    if prompt:
        x2 = _xattn_block(x2, lp['norm_x'], lp['x_wq'], lp['x_wo'], st['mem_k'], st['mem_v'],
                          tb=SEQ_CHUNK, rows_per_mem=T)
    else:
        x8 = jnp.pad(x2[:, None, :], ((0, 0), (0, 7), (0, 0))).reshape(n * 8, D_MODEL)
        x8 = _xattn_block(x8, lp['norm_x'], lp['x_wq'], lp['x_wo'], st['mem_k'], st['mem_v'], tb=8, rows_per_mem=8)
        x2 = x8.reshape(n, 8, D_MODEL)[:, 0, :]
    new = {'k': k_new, 'v': v_new, 'lru_h': lru_h, 'lru_conv': lru_conv, 'ssd': ssd_s, 'ssd_conv': ssd_conv,
           'rwkv': rwkv_s, 'rwkv_shift': rwkv_shift}
    return x2, new


def kernel(x_prompt, x_sample, mem_prompt, cache_moba_k, cache_moba_v, page_table, state_lru_h, state_lru_conv, state_ssd, state_ssd_conv, state_rwkv, state_rwkv_shift, cache_mem_k, cache_mem_v, norm_mix, w_in, w_out, lru_conv_w, lru_conv_b, lru_wa, lru_ba, lru_wx, lru_bx, lru_lambda, ssd_conv_w, ssd_conv_b, ssd_dt_bias, ssd_a_log, ssd_d, ssd_norm, rwkv_mu, rwkv_w0, rwkv_w_up, rwkv_a0, rwkv_a_up, rwkv_g_up, rwkv_k_k, rwkv_k_a, rwkv_r_k, rwkv_ln_w, rwkv_ln_b, norm_x, x_wq, x_wk, x_wv, x_wo, norm_ffn, peer_wq, peer_subkeys, peer_u, peer_v, final_norm):
    bp, tp, _ = x_prompt.shape
    bd, td, _ = x_sample.shape
    n_p, n_s = bp * tp, bd * td
    n_all = -(-(n_p + n_s) // PEER_TOK_PAD) * PEER_TOK_PAD
    f32 = jnp.float32
    xp2 = x_prompt.reshape(n_p, D_MODEL)
    xs2 = x_sample.reshape(n_s, D_MODEL)
    names = ('k', 'v', 'lru_h', 'lru_conv', 'ssd', 'ssd_conv', 'rwkv', 'rwkv_shift')
    p_new = {n: [] for n in names + ('mem_k', 'mem_v')}
    s_new = {n: [] for n in names}
    mem2 = mem_prompt.reshape(bp * MEM_LEN, D_MODEL)
    pool = lambda c: jnp.transpose(c, (0, 1, 3, 4, 2)).reshape((-1, A_HEADS, HEAD_DIM, PAGE_SIZE))
    cache_kt, cache_vt = pool(cache_moba_k), pool(cache_moba_v)
    for l in range(DEPTH):
        lp = {
            'norm_mix': norm_mix[l], 'w_in_p': _in_weight(w_in[l]), 'w_out': w_out[l],
            'lru_conv_w': lru_conv_w[l], 'lru_conv_b': lru_conv_b[l], 'lru_wa': lru_wa[l], 'lru_ba': lru_ba[l],
            'lru_wx': lru_wx[l], 'lru_bx': lru_bx[l], 'lru_lambda': lru_lambda[l],
            'ssd_conv_w': ssd_conv_w[l], 'ssd_conv_b': ssd_conv_b[l], 'ssd_dt_bias': ssd_dt_bias[l],
            'ssd_a_log': ssd_a_log[l], 'ssd_d': ssd_d[l], 'ssd_norm': ssd_norm[l],
            'rwkv_mu': rwkv_mu[l], 'rwkv_w0': rwkv_w0[l], 'rwkv_w_up': rwkv_w_up[l], 'rwkv_a0': rwkv_a0[l],
            'rwkv_a_up': rwkv_a_up[l], 'rwkv_g_up': rwkv_g_up[l], 'rwkv_k_k': rwkv_k_k[l], 'rwkv_k_a': rwkv_k_a[l],
            'rwkv_r_k': rwkv_r_k[l], 'rwkv_ln_w': rwkv_ln_w[l], 'rwkv_ln_b': rwkv_ln_b[l],
            'norm_x': norm_x[l], 'x_wq': x_wq[l], 'x_wo': x_wo[l],
        }
        mk = _mm(mem2, x_wk[l]).reshape(bp, MEM_LEN, D_MODEL)
        mv = _mm(mem2, x_wv[l]).reshape(bp, MEM_LEN, D_MODEL)
        st_p = {
            'lru_conv': jnp.zeros((bp, CONV_W - 1, LRU_WIDTH), f32),
            'lru_h': jnp.zeros((bp, LRU_WIDTH), f32),
            'ssd_conv': jnp.zeros((bp, CONV_W - 1, SSD_CONV_CH), f32),
            'ssd': jnp.zeros((bp, SSD_HEADS, SSD_HEADDIM, SSD_STATE), f32),
            'rwkv_shift': jnp.zeros((bp, RWKV_COLS), f32),
            'rwkv': jnp.zeros((bp, RWKV_HEADS, RWKV_HEADDIM, RWKV_HEADDIM), f32),
            'mem_k': mk, 'mem_v': mv,
        }
        xp2, npl = _layer_pre_peer(xp2, bp, tp, lp, st_p, True)
        for n in names:
            p_new[n].append(npl[n])
        p_new['mem_k'].append(mk.reshape(bp, MEM_LEN, X_HEADS, X_HEADDIM))
        p_new['mem_v'].append(mv.reshape(bp, MEM_LEN, X_HEADS, X_HEADDIM))
        st_s = {
            'k_cache': cache_kt, 'v_cache': cache_vt, 'page_table': page_table + l * cache_moba_k.shape[1],
            'lru_conv': state_lru_conv[l], 'lru_h': state_lru_h[l],
            'ssd_conv': state_ssd_conv[l], 'ssd': state_ssd[l],
            'rwkv_shift': state_rwkv_shift[l], 'rwkv': state_rwkv[l],
            'mem_k': cache_mem_k[l].reshape(bd, MEM_LEN, D_MODEL), 'mem_v': cache_mem_v[l].reshape(bd, MEM_LEN, D_MODEL),
        }
        xs2, nsl = _layer_pre_peer(xs2, bd, td, lp, st_s, False)
        for n in names:
            s_new[n].append(nsl[n])
        x_all = jnp.concatenate([xp2, xs2, jnp.zeros((n_all - n_p - n_s, D_MODEL), f32)], axis=0)
        x_all = _peer_block(x_all, norm_ffn[l], peer_wq[l], peer_subkeys[l], peer_u[l], peer_v[l])
        xp2 = x_all[:n_p]
        xs2 = x_all[n_p:n_p + n_s]
    y_prompt = _rmsnorm(xp2, final_norm).reshape(bp, tp, D_MODEL)
    y_sample = _rmsnorm(xs2, final_norm).reshape(bd, td, D_MODEL)
    return (y_prompt, y_sample,
            jnp.stack(p_new['k']), jnp.stack(p_new['v']), jnp.stack(p_new['lru_h']), jnp.stack(p_new['lru_conv']),
            jnp.stack(p_new['ssd']), jnp.stack(p_new['ssd_conv']), jnp.stack(p_new['rwkv']), jnp.stack(p_new['rwkv_shift']),
            jnp.stack(p_new['mem_k']), jnp.stack(p_new['mem_v']),
            jnp.stack(s_new['k']), jnp.stack(s_new['v']), jnp.stack(s_new['lru_h']), jnp.stack(s_new['lru_conv']),
            jnp.stack(s_new['ssd']), jnp.stack(s_new['ssd_conv']), jnp.stack(s_new['rwkv']), jnp.stack(s_new['rwkv_shift']))
```

```python
import functools

import jax
import jax.numpy as jnp
from jax import lax
from jax.experimental import pallas as pl
from jax.experimental.pallas import tpu as pltpu

D_MODEL = 1024
DEPTH = 2
PAST_LEN = 16384
PAGE_SIZE = 128
GROUP_WIDTH = 256
HEAD_DIM = 64
A_HEADS = GROUP_WIDTH // HEAD_DIM
MOBA_BLOCK = 256
MOBA_TOPK = 3
ROPE_THETA = 10000.0
LRU_WIDTH = GROUP_WIDTH
LRU_C = 8.0
CONV_W = 4
SSD_HEADS = 4
SSD_HEADDIM = GROUP_WIDTH // SSD_HEADS
SSD_GROUPS = 2
SSD_STATE = 64
SSD_CHUNK = 128
SSD_CONV_CH = GROUP_WIDTH + 2 * SSD_GROUPS * SSD_STATE
RWKV_HEADS = 4
RWKV_HEADDIM = GROUP_WIDTH // RWKV_HEADS
RWKV_W_RANK = 32
RWKV_A_RANK = 32
RWKV_G_RANK = 64
RWKV_COLS = 3 * GROUP_WIDTH + RWKV_W_RANK + RWKV_A_RANK + RWKV_G_RANK
RWKV_LN_EPS = 64e-5
D_OFF = 3 * GROUP_WIDTH + 2 * LRU_WIDTH + GROUP_WIDTH + SSD_CONV_CH + SSD_HEADS
MEM_LEN = 256
X_HEADS = 4
X_HEADDIM = D_MODEL // X_HEADS
PEER_HEADS = 8
PEER_NKEYS = 128
PEER_HALF = 128
PEER_TOPK = 16
NORM_EPS = 1e-6
NEG_INF = -1e30

LANES = 128
RWKV_PCOLS = 3 * GROUP_WIDTH + 3 * LANES
ABC_PCOLS = 2 * RWKV_PCOLS
COL_Q, COL_K, COL_V, COL_U, COL_GATE, COL_Z = 0, 1, 2, 3, 4, 5
COL_XBC = 3
COL_DT = 8
COL_RWKV = 2
PAGES_PER_BLOCK = MOBA_BLOCK // PAGE_SIZE
KMEAN_PAGES = 32
PEER_TOK = 512
PEER_ROUTE_TOK = 256
SEQ_CHUNK = 512
DEC_ROWS = 128
_VMEM_LIMIT = 56 * 1024 * 1024
_HI = lax.Precision.HIGHEST


def _softplus(x):
    return jnp.maximum(x, 0.0) + jnp.log1p(jnp.exp(-jnp.abs(x)))


def _silu(x):
    return x * jax.nn.sigmoid(x)


def _gelu_tanh(x):
    return 0.5 * x * (1.0 + jnp.tanh(0.7978845608028654 * (x + 0.044715 * (x * x * x))))


def _mm_kernel(*refs, n_x, has_norm, has_res):
    x_refs = refs[:n_x]
    w_ref = refs[n_x]
    pos = n_x + 1
    g_ref = r_ref = None
    if has_norm:
        g_ref = refs[pos]
        pos += 1
    if has_res:
        r_ref = refs[pos]
        pos += 1
    o_ref = refs[pos]
    y = None
    off = 0
    for x_ref in x_refs:
        x = x_ref[...]
        kx = x.shape[1]
        if has_norm:
            x = x * lax.rsqrt(jnp.mean(x * x, axis=-1, keepdims=True) + NORM_EPS) * g_ref[...]
        t = jnp.dot(x.astype(jnp.bfloat16), w_ref[off:off + kx, :], preferred_element_type=jnp.float32)
        y = t if y is None else y + t
        off += kx
    if has_res:
        y = y + r_ref[...]
    o_ref[...] = y


def _mm(xs, w, gain=None, residual=None, tm=512):
    if not isinstance(xs, (list, tuple)):
        xs = [xs]
    assert gain is None or len(xs) == 1
    M = xs[0].shape[0]
    K, N = w.shape
    tm = min(tm, M)
    assert M % tm == 0 and sum(x.shape[1] for x in xs) == K
    args = list(xs) + [w.astype(jnp.bfloat16)]
    in_specs = [pl.BlockSpec((tm, x.shape[1]), lambda i: (i, 0)) for x in xs]
    in_specs.append(pl.BlockSpec((K, N), lambda i: (0, 0)))
    if gain is not None:
        args.append(gain.reshape(1, K).astype(jnp.float32))
        in_specs.append(pl.BlockSpec((1, K), lambda i: (0, 0)))
    if residual is not None:
        args.append(residual)
        in_specs.append(pl.BlockSpec((tm, N), lambda i: (i, 0)))
    return pl.pallas_call(
        functools.partial(_mm_kernel, n_x=len(xs), has_norm=gain is not None, has_res=residual is not None),
        grid=(M // tm,),
        in_specs=in_specs,
        out_specs=pl.BlockSpec((tm, N), lambda i: (i, 0)),
        out_shape=jax.ShapeDtypeStruct((M, N), jnp.float32),
        compiler_params=pltpu.CompilerParams(dimension_semantics=("parallel",), vmem_limit_bytes=_VMEM_LIMIT),
        name="mm",
    )(*args)


def _rmsnorm(x, g):
    return x * lax.rsqrt(jnp.mean(x * x, axis=-1, keepdims=True) + NORM_EPS) * g


def _rope_tables(pos):
    half = HEAD_DIM // 2
    freq = 1.0 / (ROPE_THETA ** (jnp.arange(half, dtype=jnp.float32) / half))
    ang = pos.astype(jnp.float32)[:, None] * freq[None, :]
    cos = jnp.cos(ang)
    sin = jnp.sin(ang)
    cos_t = jnp.tile(jnp.concatenate([cos, cos], axis=-1), (1, A_HEADS))
    sin_t = jnp.tile(jnp.concatenate([-sin, sin], axis=-1), (1, A_HEADS))
    return cos_t, sin_t


def _rope_apply(x, cos_t, sin_t):
    half = HEAD_DIM // 2
    lane = lax.broadcasted_iota(jnp.int32, x.shape, 1)
    first = (lane % HEAD_DIM) < half
    w = x.shape[1]
    swapped = jnp.where(first, pltpu.roll(x, w - half, axis=1), pltpu.roll(x, half, axis=1))
    return x * cos_t + swapped * sin_t


def _moba_prompt_kernel(q_ref, k_ref, v_ref, cos_ref, sin_ref, ya_ref, kr_ref, vo_ref, ot_scr):
    T = q_ref.shape[1]
    G = GROUP_WIDTH
    hd = HEAD_DIM
    blk = MOBA_BLOCK
    nb = T // blk
    cos_t = cos_ref[...]
    sin_t = sin_ref[...]
    q = _rope_apply(q_ref[0], cos_t, sin_t)
    k = _rope_apply(k_ref[0], cos_t, sin_t)
    v = v_ref[0]
    v_t = v.T
    kr_ref[0] = k.T
    vo_ref[0] = v_t
    kmean = jnp.concatenate([jnp.sum(k[n * blk:(n + 1) * blk, :], axis=0, keepdims=True) for n in range(nb)],
                            axis=0) * (1.0 / blk)
    qt = q.T.astype(jnp.bfloat16)
    vt = v_t.astype(jnp.bfloat16)
    kb = k.astype(jnp.bfloat16)
    lane_g = lax.broadcasted_iota(jnp.int32, (nb, G), 1)
    rown = lax.broadcasted_iota(jnp.int32, (nb, T), 0)
    cur = lax.broadcasted_iota(jnp.int32, (nb, T), 1) // blk
    kpos = lax.broadcasted_iota(jnp.int32, (blk, blk), 0)
    qpos = lax.broadcasted_iota(jnp.int32, (blk, blk), 1)
    causal = kpos <= qpos
    scale = hd ** -0.5
    for h in range(A_HEADS):
        km_h = jnp.where(lane_g // hd == h, kmean, 0.0)
        gate = lax.dot_general(km_h, q, (((1,), (1,)), ((), ())), precision=_HI,
                               preferred_element_type=jnp.float32)
        gate = jnp.where(rown < cur, gate, NEG_INF)
        sel_rows = []
        for n in range(nb):
            gn = gate[n:n + 1, :]
            beats = (gate > gn) | ((gate == gn) & (rown < n))
            rank = jnp.sum(beats.astype(jnp.float32), axis=0, keepdims=True)
            sel_rows.append(rank < MOBA_TOPK)
        k_h = kb[:, h * hd:(h + 1) * hd]
        qt_h = qt[h * hd:(h + 1) * hd, :]
        vt_h = vt[h * hd:(h + 1) * hd, :]
        for qi in range(nb):
            qs = slice(qi * blk, (qi + 1) * blk)
            q_blk = qt_h[:, qs]
            s = jnp.dot(k_h[qs, :], q_blk, preferred_element_type=jnp.float32) * scale
            s = jnp.where(causal, s, NEG_INF)
            m = jnp.max(s, axis=0, keepdims=True)
            p = jnp.exp(s - m)
            l = jnp.sum(p, axis=0, keepdims=True)
            acc = jnp.dot(vt_h[:, qs], p.astype(jnp.bfloat16), preferred_element_type=jnp.float32)
            for n in range(qi):
                ks = slice(n * blk, (n + 1) * blk)
                s = jnp.dot(k_h[ks, :], q_blk, preferred_element_type=jnp.float32) * scale
                s = jnp.where(sel_rows[n][:, qs], s, NEG_INF)
                m_new = jnp.maximum(m, jnp.max(s, axis=0, keepdims=True))
                alpha = jnp.exp(m - m_new)
                p = jnp.exp(s - m_new)
                l = l * alpha + jnp.sum(p, axis=0, keepdims=True)
                acc = acc * alpha + jnp.dot(vt_h[:, ks], p.astype(jnp.bfloat16), preferred_element_type=jnp.float32)
                m = m_new
            ot_scr[h * hd:(h + 1) * hd, qs] = acc / l
    ya_ref[0] = ot_scr[...].T


def _moba_prompt(proj3, pos):
    B, T, _ = proj3.shape
    G = GROUP_WIDTH
    cos_t, sin_t = _rope_tables(pos)
    col = lambda j: pl.BlockSpec((1, T, G), lambda b: (b, 0, j))
    tab = pl.BlockSpec((T, G), lambda b: (0, 0))
    out = pl.BlockSpec((1, T, G), lambda b: (b, 0, 0))
    out_t = pl.BlockSpec((1, G, T), lambda b: (b, 0, 0))
    shp = jax.ShapeDtypeStruct((B, T, G), jnp.float32)
    shp_t = jax.ShapeDtypeStruct((B, G, T), jnp.float32)
    return pl.pallas_call(
        _moba_prompt_kernel,
        grid=(B,),
        in_specs=[col(COL_Q), col(COL_K), col(COL_V), tab, tab],
        out_specs=[out, out_t, out_t],
        out_shape=[shp, shp_t, shp_t],
        scratch_shapes=[pltpu.VMEM((G, T), jnp.float32)],
        compiler_params=pltpu.CompilerParams(dimension_semantics=("parallel",), vmem_limit_bytes=_VMEM_LIMIT),
        name="moba_prompt",
    )(proj3, proj3, proj3, cos_t, sin_t)


def _kmean_kernel(pt_ref, *refs):
    pages, o_ref = refs[:-1], refs[-1]
    j = pl.program_id(1)
    nblk = len(pages) // PAGES_PER_BLOCK

    @pl.when(j == 0)
    def _():
        o_ref[...] = jnp.zeros_like(o_ref)

    lane = lax.broadcasted_iota(jnp.int32, o_ref.shape[1:], 2)
    acc = o_ref[0]
    for blk in range(nblk):
        s = None
        for i in range(PAGES_PER_BLOCK):
            part = pages[blk * PAGES_PER_BLOCK + i][0]
            s = part if s is None else s + part
        mean = jnp.sum(s, axis=-1, keepdims=True) * (1.0 / MOBA_BLOCK)
        acc = jnp.where(lane == j * nblk + blk, mean, acc)
    o_ref[0] = acc


def _moba_block_means(cache_kt, page_table):
    B, n_pages = page_table.shape
    H, hd = A_HEADS, HEAD_DIM
    assert n_pages % KMEAN_PAGES == 0 and n_pages // PAGES_PER_BLOCK <= LANES
    steps = n_pages // KMEAN_PAGES

    def page_spec(i):
        return pl.BlockSpec((1, H, hd, PAGE_SIZE), lambda b, j, pt: (pt[b * n_pages + j * KMEAN_PAGES + i], 0, 0, 0))

    return pl.pallas_call(
        _kmean_kernel,
        grid_spec=pltpu.PrefetchScalarGridSpec(
            num_scalar_prefetch=1,
            grid=(B, steps),
            in_specs=[page_spec(i) for i in range(KMEAN_PAGES)],
            out_specs=pl.BlockSpec((1, H, hd, LANES), lambda b, j, pt: (b, 0, 0, 0)),
        ),
        out_shape=jax.ShapeDtypeStruct((B, H, hd, LANES), jnp.float32),
        compiler_params=pltpu.CompilerParams(dimension_semantics=("parallel", "arbitrary"),
                                             vmem_limit_bytes=_VMEM_LIMIT),
        name="moba_block_means",
    )(page_table.reshape(-1), *([cache_kt] * KMEAN_PAGES))


def _moba_select_kernel(q_ref, k_ref, cos_ref, sin_ref, km_ref, qr_ref, kr_ref, top_ref, *, nb):
    q = _rope_apply(q_ref[0], cos_ref[...], sin_ref[...])
    k = _rope_apply(k_ref[0], cos_ref[...], sin_ref[...])
    qr_ref[0] = q
    kr_ref[0] = k
    qcol = q.T[:, 0:1]
    lane = lax.broadcasted_iota(jnp.int32, (1, LANES), 1)
    row8 = lax.broadcasted_iota(jnp.int32, (8, LANES), 0)
    lane8 = lax.broadcasted_iota(jnp.int32, (8, LANES), 1)
    top = jnp.zeros((8, LANES), jnp.int32)
    for h in range(A_HEADS):
        gate = jnp.sum(km_ref[0, h] * qcol[h * HEAD_DIM:(h + 1) * HEAD_DIM, :], axis=0, keepdims=True)
        gate = jnp.where(lane < nb, gate, NEG_INF)
        for r in range(MOBA_TOPK):
            m = jnp.max(gate, axis=1, keepdims=True)
            idx = jnp.min(jnp.where(gate == m, lane, LANES), axis=1, keepdims=True)
            top = jnp.where((row8 == r) & (lane8 == h), idx, top)
            gate = jnp.where(lane == idx, NEG_INF, gate)
    top_ref[0] = top


def _moba_select(q8, k8, pos, kmean_t, nb):
    B = q8.shape[0]
    G = GROUP_WIDTH
    cos_t, sin_t = _rope_tables(pos)
    row = pl.BlockSpec((1, 8, G), lambda b: (b, 0, 0))
    tab = pl.BlockSpec((1, G), lambda b: (0, 0))
    return pl.pallas_call(
        functools.partial(_moba_select_kernel, nb=nb),
        grid=(B,),
        in_specs=[row, row, tab, tab, pl.BlockSpec((1, A_HEADS, HEAD_DIM, LANES), lambda b: (b, 0, 0, 0))],
        out_specs=[row, row, pl.BlockSpec((1, 8, LANES), lambda b: (b, 0, 0))],
        out_shape=[jax.ShapeDtypeStruct((B, 8, G), jnp.float32), jax.ShapeDtypeStruct((B, 8, G), jnp.float32),
                   jax.ShapeDtypeStruct((B, 8, LANES), jnp.int32)],
        compiler_params=pltpu.CompilerParams(dimension_semantics=("parallel",)),
        name="moba_select",
    )(q8, k8, cos_t, sin_t, kmean_t)


def _moba_decode_attn_kernel(pp_ref, q_ref, kn_ref, vn_ref, *refs):
    npg = MOBA_TOPK * PAGES_PER_BLOCK
    kp, vp, o_ref = refs[:npg], refs[npg:2 * npg], refs[2 * npg]
    bf = lambda a: a.astype(jnp.bfloat16)
    qh = bf(q_ref[0, 0])
    scale = HEAD_DIM ** -0.5
    s_pages = [jnp.dot(qh, bf(kp[i][0, 0]), preferred_element_type=jnp.float32) * scale for i in range(npg)]
    s_own = jnp.sum(qh.astype(jnp.float32) * bf(kn_ref[0, 0]).astype(jnp.float32), axis=-1, keepdims=True) * scale
    m = s_own
    for s in s_pages:
        m = jnp.maximum(m, jnp.max(s, axis=-1, keepdims=True))
    p_own = jnp.exp(s_own - m)
    l = p_own
    acc = p_own * bf(vn_ref[0, 0]).astype(jnp.float32)
    for i in range(npg):
        p = jnp.exp(s_pages[i] - m)
        l = l + jnp.sum(p, axis=-1, keepdims=True)
        acc = acc + lax.dot_general(bf(p), bf(vp[i][0, 0]), (((1,), (1,)), ((), ())),
                                    preferred_element_type=jnp.float32)
    o_ref[0, 0] = acc / l


def _moba_decode_attn(qh, kh, vh, cache_kt, cache_vt, phys):
    B = qh.shape[0]
    H, hd = A_HEADS, HEAD_DIM
    npg = MOBA_TOPK * PAGES_PER_BLOCK
    row = pl.BlockSpec((1, 1, 8, hd), lambda b, h, pp: (b, h, 0, 0))

    def page_spec(i):
        return pl.BlockSpec((1, 1, hd, PAGE_SIZE), lambda b, h, pp: (pp[(b * H + h) * npg + i], h, 0, 0))

    return pl.pallas_call(
        _moba_decode_attn_kernel,
        grid_spec=pltpu.PrefetchScalarGridSpec(
            num_scalar_prefetch=1,
            grid=(B, H),
            in_specs=[row, row, row] + [page_spec(i) for i in range(npg)] * 2,
            out_specs=row,
        ),
        out_shape=jax.ShapeDtypeStruct((B, H, 8, hd), jnp.float32),
        compiler_params=pltpu.CompilerParams(dimension_semantics=("parallel", "arbitrary")),
        name="moba_decode_attn",
    )(phys.reshape(-1), qh, kh, vh, *([cache_kt] * npg), *([cache_vt] * npg))


def _moba_decode(q, k, v, cache_kt, cache_vt, page_table, pos0):
    B = q.shape[0]
    H, hd = A_HEADS, HEAD_DIM
    nb = page_table.shape[1] // PAGES_PER_BLOCK
    pad8 = lambda a: jnp.pad(a[:, None, :], ((0, 0), (0, 7), (0, 0)))
    kmean_t = _moba_block_means(cache_kt, page_table)
    q8, k8, top = _moba_select(pad8(q), pad8(k), jnp.full((1,), pos0, jnp.int32), kmean_t, nb)
    blocks = jnp.swapaxes(top[:, :MOBA_TOPK, :H], 1, 2)
    pages = (blocks[..., None] * PAGES_PER_BLOCK + jnp.arange(PAGES_PER_BLOCK)).reshape(B, -1)
    phys = jnp.take_along_axis(page_table, pages, axis=1)
    by_head = lambda a8: a8.reshape(B, 8, H, hd).transpose(0, 2, 1, 3)
    out = _moba_decode_attn(by_head(q8), by_head(k8), by_head(pad8(v)), cache_kt, cache_vt, phys)
    return out[:, :, 0, :].reshape(B, H * hd), k8[:, 0, :]


def _causal_conv_tile(xs, prev3, w_ref, b_ref):
    row = lax.broadcasted_iota(jnp.int32, xs.shape, 0)
    y = b_ref[...] + xs * w_ref[CONV_W - 1:CONV_W, :]
    for k in range(1, CONV_W):
        sh = pltpu.roll(xs, k, axis=0)
        for r in range(k):
            src = r + (CONV_W - 1) - k
            sh = jnp.where(row == r, prev3[src:src + 1, :], sh)
        y = y + sh * w_ref[CONV_W - 1 - k:CONV_W - k, :]
    return y


def _conv_tail(xs, prev3, n_valid):
    rows = []
    for i in range(CONV_W - 1):
        idx = n_valid - (CONV_W - 1) + i
        rows.append(xs[idx:idx + 1, :] if idx >= 0 else prev3[idx + CONV_W - 1:idx + CONV_W, :])
    return rows


def _lru_kernel(u_ref, gate_ref, cb0_ref, h0_ref, cw_ref, cbias_ref, wa_ref, wx_ref, vec_ref,
                y_ref, cbo_ref, ho_ref, conv_scr, h_scr, a_scr, b_scr, hs_scr, *, n_valid):
    c = pl.program_id(1)
    tc = u_ref.shape[1]

    @pl.when(c == 0)
    def _():
        conv_scr[...] = cb0_ref[0]
        h_scr[...] = h0_ref[0]

    us = u_ref[0]
    prev3 = conv_scr[...]
    xc = _causal_conv_tile(us, prev3, cw_ref, cbias_ref)
    tail = _conv_tail(us, prev3, n_valid)
    for i in range(CONV_W - 1):
        conv_scr[i:i + 1, :] = tail[i]
    xb = xc.astype(jnp.bfloat16)
    r = jax.nn.sigmoid(jnp.dot(xb, wa_ref[...], preferred_element_type=jnp.float32) + vec_ref[0:1, :])
    i_g = jax.nn.sigmoid(jnp.dot(xb, wx_ref[...], preferred_element_type=jnp.float32) + vec_ref[1:2, :])
    log_a = -LRU_C * r * _softplus(-vec_ref[2:3, :])
    a_scr[...] = jnp.exp(log_a)
    b_scr[...] = jnp.sqrt(-jnp.tanh(log_a) * (jnp.exp(2.0 * log_a) + 1.0)) * (i_g * xc)

    row8 = lax.broadcasted_iota(jnp.int32, (8, LRU_WIDTH), 0)

    def step8(i, h):
        t0 = pl.multiple_of(i * 8, 8)
        a8 = a_scr[pl.ds(t0, 8), :]
        b8 = b_scr[pl.ds(t0, 8), :]
        out = jnp.zeros((8, LRU_WIDTH), jnp.float32)
        for j in range(min(8, n_valid)):
            h = a8[j:j + 1, :] * h + b8[j:j + 1, :]
            out = jnp.where(row8 == j, h, out)
        hs_scr[pl.ds(t0, 8), :] = out
        return h

    if n_valid < tc:
        hs_scr[...] = jnp.zeros_like(hs_scr)
    h = lax.fori_loop(0, -(-n_valid // 8), step8, h_scr[...])
    h_scr[...] = h
    y_ref[0] = hs_scr[...] * _gelu_tanh(gate_ref[0])

    @pl.when(c == pl.num_programs(1) - 1)
    def _():
        cbo_ref[0] = conv_scr[...]
        ho_ref[0] = h_scr[...]


def _block_diag(w):
    n, d, e = w.shape
    eye = jnp.eye(n, dtype=w.dtype)
    return (eye[:, None, :, None] * w[:, :, None, :]).reshape(n * d, n * e)


def _lru_mixer(proj3, conv0, h0, conv_w, conv_b, wa, ba, wx, bx, lam, *, n_valid, tc):
    B, Tp, _ = proj3.shape
    G = GROUP_WIDTH
    f32 = jnp.float32
    nchunk = Tp // tc
    assert Tp % tc == 0 and (n_valid == tc or nchunk == 1) and (n_valid % 8 == 0 or n_valid < 8)
    cb0 = jnp.pad(conv0, ((0, 0), (0, 8 - (CONV_W - 1)), (0, 0)))
    vec = jnp.zeros((8, G), f32).at[0].set(ba).at[1].set(bx).at[2].set(lam)
    full2 = lambda shp: pl.BlockSpec(shp, lambda b, c: (0, 0))
    y, cbo, ho = pl.pallas_call(
        functools.partial(_lru_kernel, n_valid=n_valid),
        grid=(B, nchunk),
        in_specs=[pl.BlockSpec((1, tc, G), lambda b, c: (b, c, COL_U)),
                  pl.BlockSpec((1, tc, G), lambda b, c: (b, c, COL_GATE)),
                  pl.BlockSpec((1, 8, G), lambda b, c: (b, 0, 0)),
                  pl.BlockSpec((1, 1, G), lambda b, c: (b, 0, 0)),
                  full2((CONV_W, G)), full2((1, G)), full2((G, G)), full2((G, G)), full2((8, G))],
        out_specs=[pl.BlockSpec((1, tc, G), lambda b, c: (b, c, 0)),
                   pl.BlockSpec((1, 8, G), lambda b, c: (b, 0, 0)),
                   pl.BlockSpec((1, 1, G), lambda b, c: (b, 0, 0))],
        out_shape=[jax.ShapeDtypeStruct((B, Tp, G), f32), jax.ShapeDtypeStruct((B, 8, G), f32),
                   jax.ShapeDtypeStruct((B, 1, G), f32)],
        scratch_shapes=[pltpu.VMEM((8, G), f32), pltpu.VMEM((1, G), f32),
                        pltpu.VMEM((tc, G), f32), pltpu.VMEM((tc, G), f32), pltpu.VMEM((tc, G), f32)],
        compiler_params=pltpu.CompilerParams(dimension_semantics=("parallel", "arbitrary"),
                                             vmem_limit_bytes=_VMEM_LIMIT),
        name="lru_mixer",
    )(proj3, proj3, cb0, h0[:, None, :], conv_w, conv_b[None, :],
      _block_diag(wa).astype(jnp.bfloat16), _block_diag(wx).astype(jnp.bfloat16), vec)
    return y, cbo[:, :CONV_W - 1, :], ho[:, 0, :]


def _ssd_kernel(z_ref, xbc_ref, dt_ref, cb0_ref, s0_ref, cw_ref, cbias_ref, prow_ref, pcol_ref, dvec_ref, nw_ref,
                y_ref, cbo_ref, so_ref, conv_scr, st_scr, y_scr, *, n_valid):
    c = pl.program_id(1)
    tc = xbc_ref.shape[1]
    Q = SSD_CHUNK
    G = GROUP_WIDTH
    P = SSD_HEADDIM
    N = SSD_STATE
    GN = SSD_GROUPS * SSD_STATE

    @pl.when(c == 0)
    def _():
        conv_scr[...] = cb0_ref[0]
        st_scr[...] = s0_ref[0]

    xs = xbc_ref[0]
    prev3 = conv_scr[...]
    xbc_f = _silu(_causal_conv_tile(xs, prev3, cw_ref, cbias_ref))
    tail = _conv_tail(xs, prev3, n_valid)
    for i in range(CONV_W - 1):
        conv_scr[i:i + 1, :] = tail[i]

    ii = lax.broadcasted_iota(jnp.int32, (Q, Q), 0)
    jj = lax.broadcasted_iota(jnp.int32, (Q, Q), 1)
    lower = ii >= jj
    tri_l = lower.astype(jnp.float32)
    tri_u = (ii <= jj).astype(jnp.float32)
    bias_row = prow_ref[0:1, :]
    alog_row = prow_ref[1:2, :]
    bias_col = pcol_ref[:, 0:1]
    alog_col = pcol_ref[:, 1:2]
    rowq = lax.broadcasted_iota(jnp.int32, (Q, LANES), 0)
    laneq = lax.broadcasted_iota(jnp.int32, (8, Q), 1)
    bf = lambda a: a.astype(jnp.bfloat16)
    for j in range(tc // Q):
        rows = slice(j * Q, (j + 1) * Q)
        xq = xbc_f[rows, :]
        xt = xq.T
        dcol = dt_ref[0, rows, 0:LANES]
        drow = dcol.T[0:8, :]
        dt_col = _softplus(dcol + bias_row)
        dt_row = _softplus(drow + bias_col)
        if n_valid < tc:
            dt_col = jnp.where(rowq + j * Q < n_valid, dt_col, 0.0)
            dt_row = jnp.where(laneq + j * Q < n_valid, dt_row, 0.0)
        da_col = dt_col * (-jnp.exp(alog_row))
        da_row = dt_row * (-jnp.exp(alog_col))
        cs_col = jnp.dot(tri_l, da_col, precision=_HI, preferred_element_type=jnp.float32)
        cs_row = jnp.dot(da_row, tri_u, precision=_HI, preferred_element_type=jnp.float32)
        for g in range(SSD_GROUPS):
            bg = xq[:, G + g * N:G + (g + 1) * N]
            cg = xq[:, G + GN + g * N:G + GN + (g + 1) * N]
            bgt = xt[G + g * N:G + (g + 1) * N, :]
            cb = lax.dot_general(bf(cg), bf(bg), (((1,), (1,)), ((), ())), preferred_element_type=jnp.float32)
            for h in range(g * (SSD_HEADS // SSD_GROUPS), (g + 1) * (SSD_HEADS // SSD_GROUPS)):
                ci = cs_col[:, h:h + 1]
                cj = cs_row[h:h + 1, :]
                c_last = cs_row[h:h + 1, Q - 1:Q]
                lm = jnp.exp(jnp.where(lower, ci - cj, NEG_INF))
                x_h = xq[:, h * P:(h + 1) * P]
                xdt = x_h * dt_col[:, h:h + 1]
                st = st_scr[h]
                y_h = jnp.dot(bf(cb * lm), bf(xdt), preferred_element_type=jnp.float32)
                y_h = y_h + jnp.dot(bf(cg * jnp.exp(ci)), bf(st), preferred_element_type=jnp.float32)
                y_h = y_h + dvec_ref[:, h * P:(h + 1) * P] * x_h
                y_scr[rows, h * P:(h + 1) * P] = y_h
                st_scr[h] = st * jnp.exp(c_last) + jnp.dot(bf(bgt), bf(xdt * jnp.exp(c_last - ci)),
                                                            preferred_element_type=jnp.float32)
    y = y_scr[...] * _silu(z_ref[0])
    y_ref[0] = y * lax.rsqrt(jnp.mean(y * y, axis=-1, keepdims=True) + NORM_EPS) * nw_ref[...]

    @pl.when(c == pl.num_programs(1) - 1)
    def _():
        cbo_ref[0] = conv_scr[...]
        so_ref[0] = st_scr[...]


def _ssd_mixer(proj3, conv0, s0, conv_w, conv_b, dt_bias, a_log, d_skip, norm_w, *, n_valid, tc):
    B, Tp, _ = proj3.shape
    G = GROUP_WIDTH
    H, P, N = SSD_HEADS, SSD_HEADDIM, SSD_STATE
    C = SSD_CONV_CH
    f32 = jnp.float32
    nchunk = Tp // tc
    assert Tp % tc == 0 and tc % SSD_CHUNK == 0 and (n_valid == tc or nchunk == 1)
    cb0 = jnp.pad(conv0, ((0, 0), (0, 8 - (CONV_W - 1)), (0, 0)))
    s0t = jnp.swapaxes(s0, -1, -2)
    prow = jnp.zeros((8, LANES), f32).at[0, :H].set(dt_bias).at[1, :H].set(a_log)
    pcol = jnp.zeros((8, LANES), f32).at[:H, 0].set(dt_bias).at[:H, 1].set(a_log)
    dvec = jnp.repeat(d_skip, P)[None, :]
    full2 = lambda shp: pl.BlockSpec(shp, lambda b, c: (0, 0))
    y, cbo, so = pl.pallas_call(
        functools.partial(_ssd_kernel, n_valid=n_valid),
        grid=(B, nchunk),
        in_specs=[pl.BlockSpec((1, tc, G), lambda b, c: (b, c, COL_Z)),
                  pl.BlockSpec((1, tc, C), lambda b, c: (b, c, COL_XBC)),
                  pl.BlockSpec((1, tc, G), lambda b, c: (b, c, COL_DT)),
                  pl.BlockSpec((1, 8, C), lambda b, c: (b, 0, 0)),
                  pl.BlockSpec((1, H, N, P), lambda b, c: (b, 0, 0, 0)),
                  full2((CONV_W, C)), full2((1, C)), full2((8, LANES)), full2((8, LANES)), full2((1, G)),
                  full2((1, G))],
        out_specs=[pl.BlockSpec((1, tc, G), lambda b, c: (b, c, 0)),
                   pl.BlockSpec((1, 8, C), lambda b, c: (b, 0, 0)),
                   pl.BlockSpec((1, H, N, P), lambda b, c: (b, 0, 0, 0))],
        out_shape=[jax.ShapeDtypeStruct((B, Tp, G), f32), jax.ShapeDtypeStruct((B, 8, C), f32),
                   jax.ShapeDtypeStruct((B, H, N, P), f32)],
        scratch_shapes=[pltpu.VMEM((8, C), f32), pltpu.VMEM((H, N, P), f32), pltpu.VMEM((tc, G), f32)],
        compiler_params=pltpu.CompilerParams(dimension_semantics=("parallel", "arbitrary"),
                                             vmem_limit_bytes=_VMEM_LIMIT),
        name="ssd_mixer",
    )(proj3, proj3, proj3, cb0, s0t, conv_w, conv_b[None, :], prow, pcol, dvec, norm_w[None, :])
    return y, cbo[:, :CONV_W - 1, :], jnp.swapaxes(so, -1, -2)


def _rwkv_pad_cols(a):
    G = GROUP_WIDTH
    z = lambda w: jnp.zeros(a.shape[:-1] + (w,), a.dtype)
    o = 3 * G
    wd = a[..., o:o + RWKV_W_RANK]
    ad = a[..., o + RWKV_W_RANK:o + RWKV_W_RANK + RWKV_A_RANK]
    gd = a[..., o + RWKV_W_RANK + RWKV_A_RANK:]
    return jnp.concatenate([a[..., :o], wd, z(LANES - RWKV_W_RANK), ad, z(LANES - RWKV_A_RANK),
                            gd, z(LANES - RWKV_G_RANK)], axis=-1)


def _rwkv_unpad_cols(a):
    o = 3 * GROUP_WIDTH
    return jnp.concatenate([a[..., :o], a[..., o:o + RWKV_W_RANK], a[..., o + LANES:o + LANES + RWKV_A_RANK],
                            a[..., o + 2 * LANES:o + 2 * LANES + RWKV_G_RANK]], axis=-1)


def _rwkv_kernel(pd_ref, sh0_ref, s0_ref, mu_ref, vec_ref, wup_ref, aup_ref, gup_ref, seg_ref,
                 y_ref, sout_ref, shout_ref,
                 prev_scr, s_scr, r_scr, k_scr, d_scr, kk_scr, bb_scr, g_scr, bonus_scr, vt_scr, yt_scr, *, n_steps):
    c = pl.program_id(1)
    tc = pd_ref.shape[1]
    G = GROUP_WIDTH
    hd = RWKV_HEADDIM

    @pl.when(c == 0)
    def _():
        prev_scr[...] = sh0_ref[...]
        s_scr[...] = s0_ref[...]

    nb = pd_ref.shape[0]
    w0 = vec_ref[0:1, :]
    a0 = vec_ref[1:2, :]
    k_k = vec_ref[2:3, :]
    k_a = vec_ref[3:4, :]
    ln_w = vec_ref[4:5, :]
    ln_b = vec_ref[5:6, :]
    r_k = vec_ref[6:7, :]
    seg = seg_ref[...]
    hdot = lambda x, y: jnp.dot(x, y, precision=_HI, preferred_element_type=jnp.float32)
    for ib in range(nb):
        cur = pd_ref[ib]
        row = lax.broadcasted_iota(jnp.int32, cur.shape, 0)
        prev = jnp.where(row == 0, prev_scr[ib], pltpu.roll(cur, 1, axis=0))
        prev_scr[ib] = cur[n_steps - 1:n_steps, :] if n_steps < tc else cur[tc - 1:tc, :]
        m = cur + (prev - cur) * mu_ref[...]
        r = m[:, 0:G]
        k = m[:, G:2 * G]
        v = m[:, 2 * G:3 * G]
        wd = m[:, 3 * G:3 * G + LANES]
        ad = m[:, 3 * G + LANES:3 * G + 2 * LANES]
        gd = m[:, 3 * G + 2 * LANES:3 * G + 3 * LANES]
        w = -_softplus(-(w0 + hdot(jnp.tanh(wd), wup_ref[...]))) - 0.5
        a = jax.nn.sigmoid(a0 + hdot(ad, aup_ref[...]))
        kkr = k * k_k
        kk = kkr / jnp.maximum(jnp.sqrt(hdot(kkr * kkr, seg)), 1e-12)
        k2 = k * (1.0 + (a - 1.0) * k_a)
        g_scr[ib] = hdot(jax.nn.sigmoid(gd), gup_ref[...])
        bonus_scr[ib] = hdot(r * k2 * r_k, seg) * v
        r_scr[ib] = r
        k_scr[ib] = k2
        d_scr[ib] = jnp.exp(-jnp.exp(w))
        kk_scr[ib] = -kk
        bb_scr[ib] = kk * a
        vt_scr[ib] = v.T
    yt_scr[...] = jnp.zeros_like(yt_scr)

    lane = lax.broadcasted_iota(jnp.int32, (hd, LANES), 1)
    lo_half = lane < hd

    def half_sums(p):
        lo = jnp.sum(jnp.where(lo_half, p, 0.0), axis=1, keepdims=True)
        hi = jnp.sum(jnp.where(lo_half, 0.0, p), axis=1, keepdims=True)
        return lo, hi

    npair = RWKV_HEADS // 2

    def step8(i, states):
        t0 = pl.multiple_of(i * 8, 8)
        blk = pl.multiple_of((t0 // LANES) * LANES, LANES)
        rows = pl.ds(t0, 8)
        chains = [(ib, p) for ib in range(nb) for p in range(npair)]
        states = list(states)
        rowv = {}
        tiles = {}
        for ci, (ib, p) in enumerate(chains):
            cols = slice(p * LANES, (p + 1) * LANES)
            rowv[ci] = tuple(ref[ib, rows, cols] for ref in (kk_scr, d_scr, bb_scr, k_scr, r_scr))
            tile = lambda ref, hh: ref[ib, pl.ds(p * LANES + hh * hd, hd), pl.ds(blk, LANES)]
            tiles[ci] = [tile(vt_scr, 0), tile(vt_scr, 1), tile(yt_scr, 0), tile(yt_scr, 1)]

        def vcol_of(ci, j):
            hit = lane == (t0 + j - blk)
            vc0 = jnp.sum(jnp.where(hit, tiles[ci][0], 0.0), axis=1, keepdims=True)
            vc1 = jnp.sum(jnp.where(hit, tiles[ci][1], 0.0), axis=1, keepdims=True)
            return jnp.where(lo_half, vc0, vc1)

        def emit_y(ci, j):
            hit = lane == (t0 + j - blk)
            ylo, yhi = half_sums(states[ci] * rowv[ci][4][j:j + 1, :])
            tiles[ci][2] = jnp.where(hit, ylo, tiles[ci][2])
            tiles[ci][3] = jnp.where(hit, yhi, tiles[ci][3])

        nj = min(8, n_steps)
        vcols = [vcol_of(ci, 0) for ci in range(len(chains))]
        for j in range(nj):
            sas = [half_sums(states[ci] * rowv[ci][0][j:j + 1, :]) for ci in range(len(chains))]
            if j > 0:
                for ci in range(len(chains)):
                    emit_y(ci, j - 1)
            nxt = [vcol_of(ci, j + 1) for ci in range(len(chains))] if j + 1 < nj else None
            for ci in range(len(chains)):
                _, d8, bb8, k8, _ = rowv[ci]
                sa = jnp.where(lo_half, sas[ci][0], sas[ci][1])
                states[ci] = states[ci] * d8[j:j + 1, :] + sa * bb8[j:j + 1, :] + vcols[ci] * k8[j:j + 1, :]
            vcols = nxt
        for ci in range(len(chains)):
            emit_y(ci, nj - 1)
        for ci, (ib, p) in enumerate(chains):
            yt_scr[ib, pl.ds(p * LANES, hd), pl.ds(blk, LANES)] = tiles[ci][2]
            yt_scr[ib, pl.ds(p * LANES + hd, hd), pl.ds(blk, LANES)] = tiles[ci][3]
        return tuple(states)

    init = tuple(s_scr[ib, p] for ib in range(nb) for p in range(npair))
    states = lax.fori_loop(0, -(-n_steps // 8), step8, init)
    for ib in range(nb):
        for p in range(npair):
            s_scr[ib, p] = states[ib * npair + p]

    inv = 1.0 / hd
    for ib in range(nb):
        y = yt_scr[ib].T
        mean = hdot(y, seg) * inv
        yc = y - mean
        var = hdot(yc * yc, seg) * inv
        yn = yc * lax.rsqrt(var + RWKV_LN_EPS) * ln_w + ln_b
        y_ref[ib] = (yn + bonus_scr[ib]) * g_scr[ib]

    @pl.when(c == pl.num_programs(1) - 1)
    def _():
        sout_ref[...] = s_scr[...]
        shout_ref[...] = prev_scr[...]


def _rwkv_mixer(proj3, shift0, s0, mu, w0, w_up, a0, a_up, g_up, k_k, k_a, r_k, ln_w, ln_b, *, n_steps, tc, nb=1):
    B, Tp, _ = proj3.shape
    G = GROUP_WIDTH
    H, hd = RWKV_HEADS, RWKV_HEADDIM
    nchunk = Tp // tc
    assert Tp % tc == 0 and (n_steps == tc or nchunk == 1) and (n_steps % 8 == 0 or n_steps < 8)
    assert B % nb == 0
    f32 = jnp.float32
    sh0 = _rwkv_pad_cols(shift0)[:, None, :]
    s0p = s0.reshape(B, H // 2, 2, hd, hd).transpose(0, 1, 3, 2, 4).reshape(B, H // 2, hd, 2 * hd)
    mu_p = _rwkv_pad_cols(mu[None, :])
    vec = jnp.stack([w0, a0, k_k, k_a, ln_w, ln_b, r_k.reshape(G), jnp.zeros((G,), f32)])
    padk = lambda wgt: jnp.pad(wgt, ((0, LANES - wgt.shape[0]), (0, 0)))
    hid = jnp.arange(G) // hd
    seg = (hid[:, None] == hid[None, :]).astype(f32)
    full2 = lambda shp: pl.BlockSpec(shp, lambda b, c: (0, 0))
    tscr = lambda: pltpu.VMEM((nb, tc, G), f32)
    y, s_out, sh_out = pl.pallas_call(
        functools.partial(_rwkv_kernel, n_steps=n_steps),
        grid=(B // nb, nchunk),
        in_specs=[pl.BlockSpec((nb, tc, RWKV_PCOLS), lambda b, c: (b, c, COL_RWKV)),
                  pl.BlockSpec((nb, 1, RWKV_PCOLS), lambda b, c: (b, 0, 0)),
                  pl.BlockSpec((nb, H // 2, hd, 2 * hd), lambda b, c: (b, 0, 0, 0)),
                  full2((1, RWKV_PCOLS)), full2((8, G)), full2((LANES, G)), full2((LANES, G)), full2((LANES, G)),
                  full2((G, G))],
        out_specs=[pl.BlockSpec((nb, tc, G), lambda b, c: (b, c, 0)),
                   pl.BlockSpec((nb, H // 2, hd, 2 * hd), lambda b, c: (b, 0, 0, 0)),
                   pl.BlockSpec((nb, 1, RWKV_PCOLS), lambda b, c: (b, 0, 0))],
        out_shape=[jax.ShapeDtypeStruct((B, Tp, G), f32),
                   jax.ShapeDtypeStruct((B, H // 2, hd, 2 * hd), f32),
                   jax.ShapeDtypeStruct((B, 1, RWKV_PCOLS), f32)],
        scratch_shapes=[pltpu.VMEM((nb, 1, RWKV_PCOLS), f32), pltpu.VMEM((nb, H // 2, hd, 2 * hd), f32),
                        tscr(), tscr(), tscr(), tscr(), tscr(), tscr(), tscr(),
                        pltpu.VMEM((nb, G, tc), f32), pltpu.VMEM((nb, G, tc), f32)],
        compiler_params=pltpu.CompilerParams(dimension_semantics=("parallel", "arbitrary"),
                                             vmem_limit_bytes=_VMEM_LIMIT),
        name="rwkv_mixer",
    )(proj3, sh0, s0p, mu_p, vec, padk(w_up), padk(a_up), padk(g_up), seg)
    s_new = s_out.reshape(B, H // 2, hd, 2, hd).transpose(0, 1, 3, 2, 4).reshape(B, H, hd, hd)
    return y, _rwkv_unpad_cols(sh_out[:, 0, :]), s_new


def _xattn_kernel(x_ref, g_ref, wq_ref, wo_ref, mk_ref, mv_ref, o_ref, cat_scr):
    x = x_ref[...]
    hn = x * lax.rsqrt(jnp.mean(x * x, axis=-1, keepdims=True) + NORM_EPS) * g_ref[...]
    q = jnp.dot(hn.astype(jnp.bfloat16), wq_ref[...], preferred_element_type=jnp.float32)
    qb = q.astype(jnp.bfloat16)
    mk = mk_ref[0].astype(jnp.bfloat16)
    mv = mv_ref[0].astype(jnp.bfloat16)
    hd = X_HEADDIM
    scale = hd ** -0.5
    for h in range(X_HEADS):
        cols = slice(h * hd, (h + 1) * hd)
        s = lax.dot_general(qb[:, cols], mk[:, cols], (((1,), (1,)), ((), ())),
                            preferred_element_type=jnp.float32) * scale
        m = jnp.max(s, axis=-1, keepdims=True)
        p = jnp.exp(s - m)
        p = p / jnp.sum(p, axis=-1, keepdims=True)
        cat_scr[:, cols] = jnp.dot(p.astype(jnp.bfloat16), mv[:, cols], preferred_element_type=jnp.float32)
    o_ref[...] = x + jnp.dot(cat_scr[...].astype(jnp.bfloat16), wo_ref[...], preferred_element_type=jnp.float32)


def _xattn_block(x, gain, wq, wo, mk, mv, *, tb, rows_per_mem):
    n = x.shape[0]
    M = mk.shape[1]
    assert n % tb == 0 and rows_per_mem % tb == 0
    per = rows_per_mem // tb
    return pl.pallas_call(
        _xattn_kernel,
        grid=(n // tb,),
        in_specs=[pl.BlockSpec((tb, D_MODEL), lambda i: (i, 0)),
                  pl.BlockSpec((1, D_MODEL), lambda i: (0, 0)),
                  pl.BlockSpec((D_MODEL, D_MODEL), lambda i: (0, 0)),
                  pl.BlockSpec((D_MODEL, D_MODEL), lambda i: (0, 0)),
                  pl.BlockSpec((1, M, D_MODEL), lambda i: (i // per, 0, 0)),
                  pl.BlockSpec((1, M, D_MODEL), lambda i: (i // per, 0, 0))],
        out_specs=pl.BlockSpec((tb, D_MODEL), lambda i: (i, 0)),
        out_shape=jax.ShapeDtypeStruct((n, D_MODEL), jnp.float32),
        scratch_shapes=[pltpu.VMEM((tb, D_MODEL), jnp.float32)],
        compiler_params=pltpu.CompilerParams(dimension_semantics=("parallel",), vmem_limit_bytes=_VMEM_LIMIT),
        name="xattn_block",
    )(x, gain.reshape(1, -1), wq.astype(jnp.bfloat16), wo.astype(jnp.bfloat16), mk, mv)


def _top_values(s, count, with_rank=False):
    vals = []
    cur = s
    rank = jnp.full(s.shape, float(count), jnp.float32) if with_rank else None
    for r in range(count):
        m = jnp.max(cur, axis=0, keepdims=True)
        vals.append(m)
        hit = cur == m
        if with_rank:
            rank = jnp.where(hit, float(r), rank)
        cur = jnp.where(hit, NEG_INF, cur)
    return (vals, rank) if with_rank else vals


def _peer_route_kernel(x_ref, g_ref, wqt_ref, sk_ref, hn_ref, cnt_ref, e1_ref, r2_ref, e2_ref, a1_scr, a2_scr):
    tb = x_ref.shape[0]
    x = x_ref[...]
    hn = x * lax.rsqrt(jnp.mean(x * x, axis=-1, keepdims=True) + NORM_EPS) * g_ref[...]
    hb = hn.astype(jnp.bfloat16)
    hn_ref[...] = hb
    qt = lax.dot_general(wqt_ref[...], hb, (((1,), (1,)), ((), ())), preferred_element_type=jnp.float32)
    n_top = PEER_TOPK
    row8 = lax.broadcasted_iota(jnp.int32, (8, tb), 0)
    for h in range(PEER_HEADS):
        q1 = qt[(2 * h) * PEER_HALF:(2 * h + 1) * PEER_HALF, :].astype(jnp.bfloat16)
        q2 = qt[(2 * h + 1) * PEER_HALF:(2 * h + 2) * PEER_HALF, :].astype(jnp.bfloat16)
        s1 = jnp.dot(sk_ref[2 * h], q1, preferred_element_type=jnp.float32)
        s2 = jnp.dot(sk_ref[2 * h + 1], q2, preferred_element_type=jnp.float32)
        v1 = _top_values(s1, n_top)
        v2, rank2 = _top_values(s2, n_top, with_rank=True)
        for r in range(n_top):
            a1_scr[r:r + 1, :] = v1[r]
            a2_scr[r:r + 1, :] = v2[r]
        a1 = a1_scr[...]
        a2 = a2_scr[...]
        pieces = [a1 + a2[0:1, :]]
        for q in range(1, 8):
            lim = n_top // (q + 1)
            pieces.append(jnp.where(row8 < lim, a1[0:8, :] + a2[q:q + 1, :], NEG_INF))
        pieces.append(a1[0:1, :] + a2[8:16, :])
        cand = jnp.concatenate(pieces, axis=0)
        c = _top_values(cand, n_top)
        tau = c[PEER_TOPK - 1]
        mx = c[0]
        z = jnp.sum(jnp.where(cand >= tau, jnp.exp(cand - mx), 0.0), axis=0, keepdims=True)
        cnt = jnp.zeros(s1.shape, jnp.float32)
        for q in range(n_top):
            cnt = cnt + jnp.where(s1 + v2[q] >= tau, 1.0, 0.0)
        cnt_ref[h] = cnt
        e1_ref[h] = jnp.exp(s1 - v1[0]) / z
        r2_ref[h] = rank2.astype(jnp.bfloat16)
        e2_ref[h] = jnp.exp(s2 - v2[0]).astype(jnp.bfloat16)


def _peer_route(x, gain, wqt_b, sk_b, tb=256):
    n = x.shape[0]
    assert n % tb == 0
    hk = PEER_HEADS
    blk3 = pl.BlockSpec((hk, PEER_NKEYS, tb), lambda i: (0, 0, i))
    shp_f = jax.ShapeDtypeStruct((hk, PEER_NKEYS, n), jnp.float32)
    shp_b = jax.ShapeDtypeStruct((hk, PEER_NKEYS, n), jnp.bfloat16)
    return pl.pallas_call(
        _peer_route_kernel,
        grid=(n // tb,),
        in_specs=[pl.BlockSpec((tb, D_MODEL), lambda i: (i, 0)),
                  pl.BlockSpec((1, D_MODEL), lambda i: (0, 0)),
                  pl.BlockSpec((2 * hk * PEER_HALF, D_MODEL), lambda i: (0, 0)),
                  pl.BlockSpec((2 * hk, PEER_NKEYS, PEER_HALF), lambda i: (0, 0, 0))],
        out_specs=[pl.BlockSpec((tb, D_MODEL), lambda i: (i, 0)), blk3, blk3, blk3, blk3],
        out_shape=[jax.ShapeDtypeStruct((n, D_MODEL), jnp.bfloat16), shp_f, shp_f, shp_b, shp_b],
        scratch_shapes=[pltpu.VMEM((PEER_TOPK, tb), jnp.float32), pltpu.VMEM((PEER_TOPK, tb), jnp.float32)],
        compiler_params=pltpu.CompilerParams(dimension_semantics=("parallel",), vmem_limit_bytes=_VMEM_LIMIT),
        name="peer_route",
    )(x, gain, wqt_b, sk_b)


def _peer_dense_kernel(x_ref, hn_ref, u_ref, v_ref, cnt_ref, e1_ref, r2_ref, e2_ref, fg_ref, o_ref, acc_ref, w_ref,
                       *, ic, final_norm):
    e = pl.program_id(1)
    tb = x_ref.shape[0]

    @pl.when(e == 0)
    def _():
        acc_ref[...] = jnp.zeros_like(acc_ref)

    act = lax.dot_general(u_ref[...], hn_ref[...], (((1,), (1,)), ((), ())), preferred_element_type=jnp.float32)
    for il in range(ic):
        g = None
        for h in range(PEER_HEADS):
            cb = jnp.broadcast_to(cnt_ref[h, il:il + 1, :], (16, tb)).astype(jnp.bfloat16)[None]
            eb = jnp.broadcast_to(e1_ref[h, il:il + 1, :], (16, tb)).astype(jnp.bfloat16)[None]
            r2 = r2_ref[h].reshape(PEER_NKEYS // 16, 16, tb)
            e2 = e2_ref[h].reshape(PEER_NKEYS // 16, 16, tb)
            t = jnp.where(r2 < cb, e2 * eb, jnp.zeros_like(e2))
            g = t if g is None else g + t
        rows = slice(il * PEER_NKEYS, (il + 1) * PEER_NKEYS)
        a = act[rows, :].astype(jnp.bfloat16).reshape(PEER_NKEYS // 16, 16, tb)
        w_ref[rows, :] = (g * _gelu_tanh(a)).reshape(PEER_NKEYS, tb)
    acc_ref[...] += lax.dot_general(w_ref[...], v_ref[...], (((0,), (0,)), ((), ())),
                                    preferred_element_type=jnp.float32)

    @pl.when(e == pl.num_programs(1) - 1)
    def _():
        y = x_ref[...] + acc_ref[...]
        if final_norm:
            y = y * lax.rsqrt(jnp.mean(y * y, axis=-1, keepdims=True) + NORM_EPS) * fg_ref[...]
        o_ref[...] = y


def _peer_dense(x, hn_b, u_b, v_b, cnt, e1, r2, e2, final_gain, tb, ic=16):
    n = x.shape[0]
    ne = u_b.shape[0]
    ec = ic * PEER_NKEYS
    assert n % tb == 0 and ne % ec == 0
    hk = PEER_HEADS
    row_blk = pl.BlockSpec((hk, ic, tb), lambda i, e: (0, e, i))
    full_blk = pl.BlockSpec((hk, PEER_NKEYS, tb), lambda i, e: (0, 0, i))
    fg = jnp.ones((1, D_MODEL), jnp.float32) if final_gain is None else final_gain.reshape(1, D_MODEL)
    return pl.pallas_call(
        functools.partial(_peer_dense_kernel, ic=ic, final_norm=final_gain is not None),
        grid=(n // tb, ne // ec),
        in_specs=[pl.BlockSpec((tb, D_MODEL), lambda i, e: (i, 0)),
                  pl.BlockSpec((tb, D_MODEL), lambda i, e: (i, 0)),
                  pl.BlockSpec((ec, D_MODEL), lambda i, e: (e, 0)),
                  pl.BlockSpec((ec, D_MODEL), lambda i, e: (e, 0)),
                  row_blk, row_blk, full_blk, full_blk,
                  pl.BlockSpec((1, D_MODEL), lambda i, e: (0, 0))],
        out_specs=pl.BlockSpec((tb, D_MODEL), lambda i, e: (i, 0)),
        out_shape=jax.ShapeDtypeStruct((n, D_MODEL), jnp.float32),
        scratch_shapes=[pltpu.VMEM((tb, D_MODEL), jnp.float32), pltpu.VMEM((ec, tb), jnp.bfloat16)],
        compiler_params=pltpu.CompilerParams(dimension_semantics=("parallel", "arbitrary"),
                                             vmem_limit_bytes=_VMEM_LIMIT),
        name="peer_dense",
    )(x, hn_b, u_b, v_b, cnt, e1, r2, e2, fg)


def _peer_weights(wq, subkeys, u_tab, v_tab):
    return (wq.T.astype(jnp.bfloat16),
            subkeys.reshape(2 * PEER_HEADS, PEER_NKEYS, PEER_HALF).astype(jnp.bfloat16),
            u_tab.astype(jnp.bfloat16), v_tab.astype(jnp.bfloat16))


def _peer_block(x, gain, weights, tb_route, tb_dense, final_gain=None):
    wqt_b, sk_b, u_b, v_b = weights
    hn_b, cnt, e1, r2, e2 = _peer_route(x, gain.reshape(1, -1), wqt_b, sk_b, tb=tb_route)
    return _peer_dense(x, hn_b, u_b, v_b, cnt, e1, r2, e2, final_gain, tb=tb_dense)


def _in_weight(w_in):
    abc = w_in[:, :D_OFF]
    pad = jnp.zeros((w_in.shape[0], ABC_PCOLS - D_OFF), w_in.dtype)
    return jnp.concatenate([abc, pad, _rwkv_pad_cols(w_in[:, D_OFF:])], axis=1)


def _layer_pre_peer(x2, bsz, T, lp, st, prompt):
    G = GROUP_WIDTH
    n = bsz * T
    proj = _mm(x2, lp['w_in_p'], gain=lp['norm_mix'])
    if prompt:
        n_valid, tc = SEQ_CHUNK, SEQ_CHUNK
        proj3 = proj.reshape(bsz, T, -1)
        ya, k_t, v_t = _moba_prompt(proj3, jnp.arange(T, dtype=jnp.int32))
        heads = lambda t: t.reshape(bsz, A_HEADS, HEAD_DIM, T).transpose(0, 3, 1, 2)
        k_new, v_new = heads(k_t), heads(v_t)
    else:
        assert T == 1
        n_valid, tc = T, DEC_ROWS
        ya, k_r = _moba_decode(proj[:, 0:G], proj[:, G:2 * G], proj[:, 2 * G:3 * G],
                               st['k_cache'], st['v_cache'], st['page_table'], PAST_LEN)
        heads = lambda t: t.reshape(bsz, T, A_HEADS, HEAD_DIM)
        k_new, v_new = heads(k_r), heads(proj[:, 2 * G:3 * G])
        proj3 = jnp.pad(proj[:, None, :], ((0, 0), (0, tc - T), (0, 0)))
    yb, lru_conv, lru_h = _lru_mixer(proj3, st['lru_conv'], st['lru_h'], lp['lru_conv_w'], lp['lru_conv_b'],
                                     lp['lru_wa'], lp['lru_ba'], lp['lru_wx'], lp['lru_bx'], lp['lru_lambda'],
                                     n_valid=n_valid, tc=tc)
    yc, ssd_conv, ssd_s = _ssd_mixer(proj3, st['ssd_conv'], st['ssd'], lp['ssd_conv_w'], lp['ssd_conv_b'],
                                     lp['ssd_dt_bias'], lp['ssd_a_log'], lp['ssd_d'], lp['ssd_norm'],
                                     n_valid=n_valid, tc=tc)
    yd, rwkv_shift, rwkv_s = _rwkv_mixer(proj3, st['rwkv_shift'], st['rwkv'],
                                         lp['rwkv_mu'], lp['rwkv_w0'], lp['rwkv_w_up'], lp['rwkv_a0'],
                                         lp['rwkv_a_up'], lp['rwkv_g_up'], lp['rwkv_k_k'], lp['rwkv_k_a'],
                                         lp['rwkv_r_k'], lp['rwkv_ln_w'], lp['rwkv_ln_b'], n_steps=n_valid, tc=tc)
    rows = lambda y: y[:, :T].reshape(n, G)
    x2 = _mm([ya.reshape(n, G), rows(yb), rows(yc), rows(yd)], lp['w_out'], residual=x2)
    if prompt:
        x2 = _xattn_block(x2, lp['norm_x'], lp['x_wq'], lp['x_wo'], st['mem_k'], st['mem_v'],
                          tb=SEQ_CHUNK, rows_per_mem=T)
    else:
        x8 = jnp.pad(x2[:, None, :], ((0, 0), (0, 7), (0, 0))).reshape(n * 8, D_MODEL)
        x8 = _xattn_block(x8, lp['norm_x'], lp['x_wq'], lp['x_wo'], st['mem_k'], st['mem_v'], tb=8, rows_per_mem=8)
        x2 = x8.reshape(n, 8, D_MODEL)[:, 0, :]
    new = {'k': k_new, 'v': v_new, 'lru_h': lru_h, 'lru_conv': lru_conv, 'ssd': ssd_s, 'ssd_conv': ssd_conv,
           'rwkv': rwkv_s, 'rwkv_shift': rwkv_shift}
    return x2, new


def kernel(x_prompt, x_sample, mem_prompt, cache_moba_k, cache_moba_v, page_table, state_lru_h, state_lru_conv, state_ssd, state_ssd_conv, state_rwkv, state_rwkv_shift, cache_mem_k, cache_mem_v, norm_mix, w_in, w_out, lru_conv_w, lru_conv_b, lru_wa, lru_ba, lru_wx, lru_bx, lru_lambda, ssd_conv_w, ssd_conv_b, ssd_dt_bias, ssd_a_log, ssd_d, ssd_norm, rwkv_mu, rwkv_w0, rwkv_w_up, rwkv_a0, rwkv_a_up, rwkv_g_up, rwkv_k_k, rwkv_k_a, rwkv_r_k, rwkv_ln_w, rwkv_ln_b, norm_x, x_wq, x_wk, x_wv, x_wo, norm_ffn, peer_wq, peer_subkeys, peer_u, peer_v, final_norm):
    bp, tp, _ = x_prompt.shape
    bd, td, _ = x_sample.shape
    n_p, n_s = bp * tp, bd * td
    n_s_pad = -(-n_s // LANES) * LANES
    assert n_p % PEER_TOK == 0
    f32 = jnp.float32
    xp2 = x_prompt.reshape(n_p, D_MODEL)
    xs2 = x_sample.reshape(n_s, D_MODEL)
    names = ('k', 'v', 'lru_h', 'lru_conv', 'ssd', 'ssd_conv', 'rwkv', 'rwkv_shift')
    p_new = {n: [] for n in names + ('mem_k', 'mem_v')}
    s_new = {n: [] for n in names}
    mem2 = mem_prompt.reshape(bp * MEM_LEN, D_MODEL)
    pool = lambda c: jnp.transpose(c, (0, 1, 3, 4, 2)).reshape((-1, A_HEADS, HEAD_DIM, PAGE_SIZE))
    cache_kt, cache_vt = pool(cache_moba_k), pool(cache_moba_v)
    for l in range(DEPTH):
        lp = {
            'norm_mix': norm_mix[l], 'w_in_p': _in_weight(w_in[l]), 'w_out': w_out[l],
            'lru_conv_w': lru_conv_w[l], 'lru_conv_b': lru_conv_b[l], 'lru_wa': lru_wa[l], 'lru_ba': lru_ba[l],
            'lru_wx': lru_wx[l], 'lru_bx': lru_bx[l], 'lru_lambda': lru_lambda[l],
            'ssd_conv_w': ssd_conv_w[l], 'ssd_conv_b': ssd_conv_b[l], 'ssd_dt_bias': ssd_dt_bias[l],
            'ssd_a_log': ssd_a_log[l], 'ssd_d': ssd_d[l], 'ssd_norm': ssd_norm[l],
            'rwkv_mu': rwkv_mu[l], 'rwkv_w0': rwkv_w0[l], 'rwkv_w_up': rwkv_w_up[l], 'rwkv_a0': rwkv_a0[l],
            'rwkv_a_up': rwkv_a_up[l], 'rwkv_g_up': rwkv_g_up[l], 'rwkv_k_k': rwkv_k_k[l], 'rwkv_k_a': rwkv_k_a[l],
            'rwkv_r_k': rwkv_r_k[l], 'rwkv_ln_w': rwkv_ln_w[l], 'rwkv_ln_b': rwkv_ln_b[l],
            'norm_x': norm_x[l], 'x_wq': x_wq[l], 'x_wo': x_wo[l],
        }
        mk = _mm(mem2, x_wk[l]).reshape(bp, MEM_LEN, D_MODEL)
        mv = _mm(mem2, x_wv[l]).reshape(bp, MEM_LEN, D_MODEL)
        st_p = {
            'lru_conv': jnp.zeros((bp, CONV_W - 1, LRU_WIDTH), f32),
            'lru_h': jnp.zeros((bp, LRU_WIDTH), f32),
            'ssd_conv': jnp.zeros((bp, CONV_W - 1, SSD_CONV_CH), f32),
            'ssd': jnp.zeros((bp, SSD_HEADS, SSD_HEADDIM, SSD_STATE), f32),
            'rwkv_shift': jnp.zeros((bp, RWKV_COLS), f32),
            'rwkv': jnp.zeros((bp, RWKV_HEADS, RWKV_HEADDIM, RWKV_HEADDIM), f32),
            'mem_k': mk, 'mem_v': mv,
        }
        xp2, npl = _layer_pre_peer(xp2, bp, tp, lp, st_p, True)
        for n in names:
            p_new[n].append(npl[n])
        p_new['mem_k'].append(mk.reshape(bp, MEM_LEN, X_HEADS, X_HEADDIM))
        p_new['mem_v'].append(mv.reshape(bp, MEM_LEN, X_HEADS, X_HEADDIM))
        st_s = {
            'k_cache': cache_kt, 'v_cache': cache_vt, 'page_table': page_table + l * cache_moba_k.shape[1],
            'lru_conv': state_lru_conv[l], 'lru_h': state_lru_h[l],
            'ssd_conv': state_ssd_conv[l], 'ssd': state_ssd[l],
            'rwkv_shift': state_rwkv_shift[l], 'rwkv': state_rwkv[l],
            'mem_k': cache_mem_k[l].reshape(bd, MEM_LEN, D_MODEL), 'mem_v': cache_mem_v[l].reshape(bd, MEM_LEN, D_MODEL),
        }
        xs2, nsl = _layer_pre_peer(xs2, bd, td, lp, st_s, False)
        for n in names:
            s_new[n].append(nsl[n])
        fg = final_norm if l == DEPTH - 1 else None
        pw = _peer_weights(peer_wq[l], peer_subkeys[l], peer_u[l], peer_v[l])
        xp2 = _peer_block(xp2, norm_ffn[l], pw, PEER_ROUTE_TOK, PEER_TOK, fg)
        xs_pad = jnp.pad(xs2, ((0, n_s_pad - n_s), (0, 0)))
        xs2 = _peer_block(xs_pad, norm_ffn[l], pw, LANES, LANES, fg)[:n_s]
    y_prompt = xp2.reshape(bp, tp, D_MODEL)
    y_sample = xs2.reshape(bd, td, D_MODEL)
    return (y_prompt, y_sample,
            jnp.stack(p_new['k']), jnp.stack(p_new['v']), jnp.stack(p_new['lru_h']), jnp.stack(p_new['lru_conv']),
            jnp.stack(p_new['ssd']), jnp.stack(p_new['ssd_conv']), jnp.stack(p_new['rwkv']), jnp.stack(p_new['rwkv_shift']),
            jnp.stack(p_new['mem_k']), jnp.stack(p_new['mem_v']),
            jnp.stack(s_new['k']), jnp.stack(s_new['v']), jnp.stack(s_new['lru_h']), jnp.stack(s_new['lru_conv']),
            jnp.stack(s_new['ssd']), jnp.stack(s_new['ssd_conv']), jnp.stack(s_new['rwkv']), jnp.stack(s_new['rwkv_shift']))
```

```python
import functools

import jax
import jax.numpy as jnp
from jax import lax
from jax.experimental import pallas as pl
from jax.experimental.pallas import tpu as pltpu

D_MODEL = 1024
DEPTH = 2
PAST_LEN = 16384
PAGE_SIZE = 128
GROUP_WIDTH = 256
HEAD_DIM = 64
A_HEADS = GROUP_WIDTH // HEAD_DIM
MOBA_BLOCK = 256
MOBA_TOPK = 3
ROPE_THETA = 10000.0
LRU_WIDTH = GROUP_WIDTH
LRU_C = 8.0
CONV_W = 4
SSD_HEADS = 4
SSD_HEADDIM = GROUP_WIDTH // SSD_HEADS
SSD_GROUPS = 2
SSD_STATE = 64
SSD_CHUNK = 128
SSD_CONV_CH = GROUP_WIDTH + 2 * SSD_GROUPS * SSD_STATE
RWKV_HEADS = 4
RWKV_HEADDIM = GROUP_WIDTH // RWKV_HEADS
RWKV_W_RANK = 32
RWKV_A_RANK = 32
RWKV_G_RANK = 64
RWKV_COLS = 3 * GROUP_WIDTH + RWKV_W_RANK + RWKV_A_RANK + RWKV_G_RANK
RWKV_LN_EPS = 64e-5
D_OFF = 3 * GROUP_WIDTH + 2 * LRU_WIDTH + GROUP_WIDTH + SSD_CONV_CH + SSD_HEADS
MEM_LEN = 256
X_HEADS = 4
X_HEADDIM = D_MODEL // X_HEADS
PEER_HEADS = 8
PEER_NKEYS = 128
PEER_HALF = 128
PEER_TOPK = 16
NORM_EPS = 1e-6
NEG_INF = -1e30

LANES = 128
RWKV_PCOLS = 3 * GROUP_WIDTH + 3 * LANES
ABC_PCOLS = 2 * RWKV_PCOLS
COL_Q, COL_K, COL_V, COL_U, COL_GATE, COL_Z = 0, 1, 2, 3, 4, 5
COL_XBC = 3
COL_DT = 8
COL_RWKV = 2
PAGES_PER_BLOCK = MOBA_BLOCK // PAGE_SIZE
KMEAN_PAGES = 32
PEER_TOK = 512
PEER_ROUTE_TOK = 256
SEQ_CHUNK = 512
DEC_ROWS = 128
_VMEM_LIMIT = 56 * 1024 * 1024
_HI = lax.Precision.HIGHEST


def _softplus(x):
    return jnp.maximum(x, 0.0) + jnp.log1p(jnp.exp(-jnp.abs(x)))


def _silu(x):
    return x * jax.nn.sigmoid(x)


def _gelu_tanh(x):
    return 0.5 * x * (1.0 + jnp.tanh(0.7978845608028654 * (x + 0.044715 * (x * x * x))))


def _mm_kernel(*refs, n_x, has_norm, has_res):
    x_refs = refs[:n_x]
    w_ref = refs[n_x]
    pos = n_x + 1
    g_ref = r_ref = None
    if has_norm:
        g_ref = refs[pos]
        pos += 1
    if has_res:
        r_ref = refs[pos]
        pos += 1
    o_ref = refs[pos]
    y = None
    off = 0
    for x_ref in x_refs:
        x = x_ref[...]
        kx = x.shape[1]
        if has_norm:
            x = x * lax.rsqrt(jnp.mean(x * x, axis=-1, keepdims=True) + NORM_EPS) * g_ref[...]
        t = jnp.dot(x.astype(jnp.bfloat16), w_ref[off:off + kx, :], preferred_element_type=jnp.float32)
        y = t if y is None else y + t
        off += kx
    if has_res:
        y = y + r_ref[...]
    o_ref[...] = y


def _mm(xs, w, gain=None, residual=None, tm=512):
    if not isinstance(xs, (list, tuple)):
        xs = [xs]
    assert gain is None or len(xs) == 1
    M = xs[0].shape[0]
    K, N = w.shape
    tm = min(tm, M)
    assert M % tm == 0 and sum(x.shape[1] for x in xs) == K
    args = list(xs) + [w.astype(jnp.bfloat16)]
    in_specs = [pl.BlockSpec((tm, x.shape[1]), lambda i: (i, 0)) for x in xs]
    in_specs.append(pl.BlockSpec((K, N), lambda i: (0, 0)))
    if gain is not None:
        args.append(gain.reshape(1, K).astype(jnp.float32))
        in_specs.append(pl.BlockSpec((1, K), lambda i: (0, 0)))
    if residual is not None:
        args.append(residual)
        in_specs.append(pl.BlockSpec((tm, N), lambda i: (i, 0)))
    return pl.pallas_call(
        functools.partial(_mm_kernel, n_x=len(xs), has_norm=gain is not None, has_res=residual is not None),
        grid=(M // tm,),
        in_specs=in_specs,
        out_specs=pl.BlockSpec((tm, N), lambda i: (i, 0)),
        out_shape=jax.ShapeDtypeStruct((M, N), jnp.float32),
        compiler_params=pltpu.CompilerParams(dimension_semantics=("parallel",), vmem_limit_bytes=_VMEM_LIMIT),
        name="mm",
    )(*args)


def _rmsnorm(x, g):
    return x * lax.rsqrt(jnp.mean(x * x, axis=-1, keepdims=True) + NORM_EPS) * g


def _rope_tables(pos):
    half = HEAD_DIM // 2
    freq = 1.0 / (ROPE_THETA ** (jnp.arange(half, dtype=jnp.float32) / half))
    ang = pos.astype(jnp.float32)[:, None] * freq[None, :]
    cos = jnp.cos(ang)
    sin = jnp.sin(ang)
    cos_t = jnp.tile(jnp.concatenate([cos, cos], axis=-1), (1, A_HEADS))
    sin_t = jnp.tile(jnp.concatenate([-sin, sin], axis=-1), (1, A_HEADS))
    return cos_t, sin_t


def _rope_apply(x, cos_t, sin_t):
    half = HEAD_DIM // 2
    lane = lax.broadcasted_iota(jnp.int32, x.shape, 1)
    first = (lane % HEAD_DIM) < half
    w = x.shape[1]
    swapped = jnp.where(first, pltpu.roll(x, w - half, axis=1), pltpu.roll(x, half, axis=1))
    return x * cos_t + swapped * sin_t


def _moba_prompt_kernel(q_ref, k_ref, v_ref, cos_ref, sin_ref, ya_ref, kr_ref, vo_ref, ot_scr):
    T = q_ref.shape[1]
    G = GROUP_WIDTH
    hd = HEAD_DIM
    blk = MOBA_BLOCK
    nb = T // blk
    cos_t = cos_ref[...]
    sin_t = sin_ref[...]
    q = _rope_apply(q_ref[0], cos_t, sin_t)
    k = _rope_apply(k_ref[0], cos_t, sin_t)
    v = v_ref[0]
    v_t = v.T
    kr_ref[0] = k.T
    vo_ref[0] = v_t
    kmean = jnp.concatenate([jnp.sum(k[n * blk:(n + 1) * blk, :], axis=0, keepdims=True) for n in range(nb)],
                            axis=0) * (1.0 / blk)
    qt = q.T.astype(jnp.bfloat16)
    vt = v_t.astype(jnp.bfloat16)
    kb = k.astype(jnp.bfloat16)
    lane_g = lax.broadcasted_iota(jnp.int32, (nb, G), 1)
    rown = lax.broadcasted_iota(jnp.int32, (nb, T), 0)
    cur = lax.broadcasted_iota(jnp.int32, (nb, T), 1) // blk
    kpos = lax.broadcasted_iota(jnp.int32, (blk, blk), 0)
    qpos = lax.broadcasted_iota(jnp.int32, (blk, blk), 1)
    causal = kpos <= qpos
    scale = hd ** -0.5
    for h in range(A_HEADS):
        km_h = jnp.where(lane_g // hd == h, kmean, 0.0)
        gate = lax.dot_general(km_h, q, (((1,), (1,)), ((), ())), precision=_HI,
                               preferred_element_type=jnp.float32)
        gate = jnp.where(rown < cur, gate, NEG_INF)
        sel_rows = []
        for n in range(nb):
            gn = gate[n:n + 1, :]
            beats = (gate > gn) | ((gate == gn) & (rown < n))
            rank = jnp.sum(beats.astype(jnp.float32), axis=0, keepdims=True)
            sel_rows.append(rank < MOBA_TOPK)
        k_h = kb[:, h * hd:(h + 1) * hd]
        qt_h = qt[h * hd:(h + 1) * hd, :]
        vt_h = vt[h * hd:(h + 1) * hd, :]
        for qi in range(nb):
            qs = slice(qi * blk, (qi + 1) * blk)
            q_blk = qt_h[:, qs]
            s = jnp.dot(k_h[qs, :], q_blk, preferred_element_type=jnp.float32) * scale
            s = jnp.where(causal, s, NEG_INF)
            m = jnp.max(s, axis=0, keepdims=True)
            p = jnp.exp(s - m)
            l = jnp.sum(p, axis=0, keepdims=True)
            acc = jnp.dot(vt_h[:, qs], p.astype(jnp.bfloat16), preferred_element_type=jnp.float32)
            for n in range(qi):
                ks = slice(n * blk, (n + 1) * blk)
                s = jnp.dot(k_h[ks, :], q_blk, preferred_element_type=jnp.float32) * scale
                s = jnp.where(sel_rows[n][:, qs], s, NEG_INF)
                m_new = jnp.maximum(m, jnp.max(s, axis=0, keepdims=True))
                alpha = jnp.exp(m - m_new)
                p = jnp.exp(s - m_new)
                l = l * alpha + jnp.sum(p, axis=0, keepdims=True)
                acc = acc * alpha + jnp.dot(vt_h[:, ks], p.astype(jnp.bfloat16), preferred_element_type=jnp.float32)
                m = m_new
            ot_scr[h * hd:(h + 1) * hd, qs] = acc / l
    ya_ref[0] = ot_scr[...].T


def _moba_prompt(proj3, pos):
    B, T, _ = proj3.shape
    G = GROUP_WIDTH
    cos_t, sin_t = _rope_tables(pos)
    col = lambda j: pl.BlockSpec((1, T, G), lambda b: (b, 0, j))
    tab = pl.BlockSpec((T, G), lambda b: (0, 0))
    out = pl.BlockSpec((1, T, G), lambda b: (b, 0, 0))
    out_t = pl.BlockSpec((1, G, T), lambda b: (b, 0, 0))
    shp = jax.ShapeDtypeStruct((B, T, G), jnp.float32)
    shp_t = jax.ShapeDtypeStruct((B, G, T), jnp.float32)
    return pl.pallas_call(
        _moba_prompt_kernel,
        grid=(B,),
        in_specs=[col(COL_Q), col(COL_K), col(COL_V), tab, tab],
        out_specs=[out, out_t, out_t],
        out_shape=[shp, shp_t, shp_t],
        scratch_shapes=[pltpu.VMEM((G, T), jnp.float32)],
        compiler_params=pltpu.CompilerParams(dimension_semantics=("parallel",), vmem_limit_bytes=_VMEM_LIMIT),
        name="moba_prompt",
    )(proj3, proj3, proj3, cos_t, sin_t)


def _kmean_kernel(pt_ref, *refs):
    pages, o_ref = refs[:-1], refs[-1]
    j = pl.program_id(1)
    nblk = len(pages) // PAGES_PER_BLOCK

    @pl.when(j == 0)
    def _():
        o_ref[...] = jnp.zeros_like(o_ref)

    lane = lax.broadcasted_iota(jnp.int32, o_ref.shape[1:], 2)
    acc = o_ref[0]
    for blk in range(nblk):
        s = None
        for i in range(PAGES_PER_BLOCK):
            part = pages[blk * PAGES_PER_BLOCK + i][0]
            s = part if s is None else s + part
        mean = jnp.sum(s, axis=-1, keepdims=True) * (1.0 / MOBA_BLOCK)
        acc = jnp.where(lane == j * nblk + blk, mean, acc)
    o_ref[0] = acc


def _moba_block_means(cache_kt, page_table):
    B, n_pages = page_table.shape
    H, hd = A_HEADS, HEAD_DIM
    assert n_pages % KMEAN_PAGES == 0 and n_pages // PAGES_PER_BLOCK <= LANES
    steps = n_pages // KMEAN_PAGES

    def page_spec(i):
        return pl.BlockSpec((1, H, hd, PAGE_SIZE), lambda b, j, pt: (pt[b * n_pages + j * KMEAN_PAGES + i], 0, 0, 0))

    return pl.pallas_call(
        _kmean_kernel,
        grid_spec=pltpu.PrefetchScalarGridSpec(
            num_scalar_prefetch=1,
            grid=(B, steps),
            in_specs=[page_spec(i) for i in range(KMEAN_PAGES)],
            out_specs=pl.BlockSpec((1, H, hd, LANES), lambda b, j, pt: (b, 0, 0, 0)),
        ),
        out_shape=jax.ShapeDtypeStruct((B, H, hd, LANES), jnp.float32),
        compiler_params=pltpu.CompilerParams(dimension_semantics=("parallel", "arbitrary"),
                                             vmem_limit_bytes=_VMEM_LIMIT),
        name="moba_block_means",
    )(page_table.reshape(-1), *([cache_kt] * KMEAN_PAGES))


def _moba_select_kernel(q_ref, k_ref, cos_ref, sin_ref, km_ref, qr_ref, kr_ref, top_ref, *, nb):
    q = _rope_apply(q_ref[0], cos_ref[...], sin_ref[...])
    k = _rope_apply(k_ref[0], cos_ref[...], sin_ref[...])
    qr_ref[0] = q
    kr_ref[0] = k
    qcol = q.T[:, 0:1]
    lane = lax.broadcasted_iota(jnp.int32, (1, LANES), 1)
    row8 = lax.broadcasted_iota(jnp.int32, (8, LANES), 0)
    lane8 = lax.broadcasted_iota(jnp.int32, (8, LANES), 1)
    top = jnp.zeros((8, LANES), jnp.int32)
    for h in range(A_HEADS):
        gate = jnp.sum(km_ref[0, h] * qcol[h * HEAD_DIM:(h + 1) * HEAD_DIM, :], axis=0, keepdims=True)
        gate = jnp.where(lane < nb, gate, NEG_INF)
        for r in range(MOBA_TOPK):
            m = jnp.max(gate, axis=1, keepdims=True)
            idx = jnp.min(jnp.where(gate == m, lane, LANES), axis=1, keepdims=True)
            top = jnp.where((row8 == r) & (lane8 == h), idx, top)
            gate = jnp.where(lane == idx, NEG_INF, gate)
    top_ref[0] = top


def _moba_select(q8, k8, pos, kmean_t, nb):
    B = q8.shape[0]
    G = GROUP_WIDTH
    cos_t, sin_t = _rope_tables(pos)
    row = pl.BlockSpec((1, 8, G), lambda b: (b, 0, 0))
    tab = pl.BlockSpec((1, G), lambda b: (0, 0))
    return pl.pallas_call(
        functools.partial(_moba_select_kernel, nb=nb),
        grid=(B,),
        in_specs=[row, row, tab, tab, pl.BlockSpec((1, A_HEADS, HEAD_DIM, LANES), lambda b: (b, 0, 0, 0))],
        out_specs=[row, row, pl.BlockSpec((1, 8, LANES), lambda b: (b, 0, 0))],
        out_shape=[jax.ShapeDtypeStruct((B, 8, G), jnp.float32), jax.ShapeDtypeStruct((B, 8, G), jnp.float32),
                   jax.ShapeDtypeStruct((B, 8, LANES), jnp.int32)],
        compiler_params=pltpu.CompilerParams(dimension_semantics=("parallel",)),
        name="moba_select",
    )(q8, k8, cos_t, sin_t, kmean_t)


def _moba_decode_attn_kernel(pp_ref, q_ref, kn_ref, vn_ref, *refs):
    npg = MOBA_TOPK * PAGES_PER_BLOCK
    kp, vp, o_ref = refs[:npg], refs[npg:2 * npg], refs[2 * npg]
    bf = lambda a: a.astype(jnp.bfloat16)
    qh = bf(q_ref[0, 0])
    scale = HEAD_DIM ** -0.5
    s_pages = [jnp.dot(qh, bf(kp[i][0, 0]), preferred_element_type=jnp.float32) * scale for i in range(npg)]
    s_own = jnp.sum(qh.astype(jnp.float32) * bf(kn_ref[0, 0]).astype(jnp.float32), axis=-1, keepdims=True) * scale
    m = s_own
    for s in s_pages:
        m = jnp.maximum(m, jnp.max(s, axis=-1, keepdims=True))
    p_own = jnp.exp(s_own - m)
    l = p_own
    acc = p_own * bf(vn_ref[0, 0]).astype(jnp.float32)
    for i in range(npg):
        p = jnp.exp(s_pages[i] - m)
        l = l + jnp.sum(p, axis=-1, keepdims=True)
        acc = acc + lax.dot_general(bf(p), bf(vp[i][0, 0]), (((1,), (1,)), ((), ())),
                                    preferred_element_type=jnp.float32)
    o_ref[0, 0] = acc / l


def _moba_decode_attn(qh, kh, vh, cache_kt, cache_vt, phys):
    B = qh.shape[0]
    H, hd = A_HEADS, HEAD_DIM
    npg = MOBA_TOPK * PAGES_PER_BLOCK
    row = pl.BlockSpec((1, 1, 8, hd), lambda b, h, pp: (b, h, 0, 0))

    def page_spec(i):
        return pl.BlockSpec((1, 1, hd, PAGE_SIZE), lambda b, h, pp: (pp[(b * H + h) * npg + i], h, 0, 0))

    return pl.pallas_call(
        _moba_decode_attn_kernel,
        grid_spec=pltpu.PrefetchScalarGridSpec(
            num_scalar_prefetch=1,
            grid=(B, H),
            in_specs=[row, row, row] + [page_spec(i) for i in range(npg)] * 2,
            out_specs=row,
        ),
        out_shape=jax.ShapeDtypeStruct((B, H, 8, hd), jnp.float32),
        compiler_params=pltpu.CompilerParams(dimension_semantics=("parallel", "arbitrary")),
        name="moba_decode_attn",
    )(phys.reshape(-1), qh, kh, vh, *([cache_kt] * npg), *([cache_vt] * npg))


def _moba_decode(q, k, v, cache_kt, cache_vt, page_table, pos0):
    B = q.shape[0]
    H, hd = A_HEADS, HEAD_DIM
    nb = page_table.shape[1] // PAGES_PER_BLOCK
    pad8 = lambda a: jnp.pad(a[:, None, :], ((0, 0), (0, 7), (0, 0)))
    kmean_t = _moba_block_means(cache_kt, page_table)
    q8, k8, top = _moba_select(pad8(q), pad8(k), jnp.full((1,), pos0, jnp.int32), kmean_t, nb)
    blocks = jnp.swapaxes(top[:, :MOBA_TOPK, :H], 1, 2)
    pages = (blocks[..., None] * PAGES_PER_BLOCK + jnp.arange(PAGES_PER_BLOCK)).reshape(B, -1)
    phys = jnp.take_along_axis(page_table, pages, axis=1)
    by_head = lambda a8: a8.reshape(B, 8, H, hd).transpose(0, 2, 1, 3)
    out = _moba_decode_attn(by_head(q8), by_head(k8), by_head(pad8(v)), cache_kt, cache_vt, phys)
    return out[:, :, 0, :].reshape(B, H * hd), k8[:, 0, :]


def _causal_conv_tile(xs, prev3, w_ref, b_ref):
    row = lax.broadcasted_iota(jnp.int32, xs.shape, 0)
    y = b_ref[...] + xs * w_ref[CONV_W - 1:CONV_W, :]
    for k in range(1, CONV_W):
        sh = pltpu.roll(xs, k, axis=0)
        for r in range(k):
            src = r + (CONV_W - 1) - k
            sh = jnp.where(row == r, prev3[src:src + 1, :], sh)
        y = y + sh * w_ref[CONV_W - 1 - k:CONV_W - k, :]
    return y


def _conv_tail(xs, prev3, n_valid):
    rows = []
    for i in range(CONV_W - 1):
        idx = n_valid - (CONV_W - 1) + i
        rows.append(xs[idx:idx + 1, :] if idx >= 0 else prev3[idx + CONV_W - 1:idx + CONV_W, :])
    return rows


def _lru_kernel(u_ref, gate_ref, cb0_ref, h0_ref, cw_ref, cbias_ref, wa_ref, wx_ref, vec_ref,
                y_ref, cbo_ref, ho_ref, conv_scr, h_scr, a_scr, b_scr, hs_scr, *, n_valid):
    c = pl.program_id(1)
    tc = u_ref.shape[1]

    @pl.when(c == 0)
    def _():
        conv_scr[...] = cb0_ref[0]
        h_scr[...] = h0_ref[0]

    us = u_ref[0]
    prev3 = conv_scr[...]
    xc = _causal_conv_tile(us, prev3, cw_ref, cbias_ref)
    tail = _conv_tail(us, prev3, n_valid)
    for i in range(CONV_W - 1):
        conv_scr[i:i + 1, :] = tail[i]
    xb = xc.astype(jnp.bfloat16)
    r = jax.nn.sigmoid(jnp.dot(xb, wa_ref[...], preferred_element_type=jnp.float32) + vec_ref[0:1, :])
    i_g = jax.nn.sigmoid(jnp.dot(xb, wx_ref[...], preferred_element_type=jnp.float32) + vec_ref[1:2, :])
    log_a = -LRU_C * r * _softplus(-vec_ref[2:3, :])
    a_scr[...] = jnp.exp(log_a)
    b_scr[...] = jnp.sqrt(-jnp.tanh(log_a) * (jnp.exp(2.0 * log_a) + 1.0)) * (i_g * xc)

    row8 = lax.broadcasted_iota(jnp.int32, (8, LRU_WIDTH), 0)

    def step8(i, h):
        t0 = pl.multiple_of(i * 8, 8)
        a8 = a_scr[pl.ds(t0, 8), :]
        b8 = b_scr[pl.ds(t0, 8), :]
        out = jnp.zeros((8, LRU_WIDTH), jnp.float32)
        for j in range(min(8, n_valid)):
            h = a8[j:j + 1, :] * h + b8[j:j + 1, :]
            out = jnp.where(row8 == j, h, out)
        hs_scr[pl.ds(t0, 8), :] = out
        return h

    if n_valid < tc:
        hs_scr[...] = jnp.zeros_like(hs_scr)
    h = lax.fori_loop(0, -(-n_valid // 8), step8, h_scr[...])
    h_scr[...] = h
    y_ref[0] = hs_scr[...] * _gelu_tanh(gate_ref[0])

    @pl.when(c == pl.num_programs(1) - 1)
    def _():
        cbo_ref[0] = conv_scr[...]
        ho_ref[0] = h_scr[...]


def _block_diag(w):
    n, d, e = w.shape
    eye = jnp.eye(n, dtype=w.dtype)
    return (eye[:, None, :, None] * w[:, :, None, :]).reshape(n * d, n * e)


def _lru_mixer(proj3, conv0, h0, conv_w, conv_b, wa, ba, wx, bx, lam, *, n_valid, tc):
    B, Tp, _ = proj3.shape
    G = GROUP_WIDTH
    f32 = jnp.float32
    nchunk = Tp // tc
    assert Tp % tc == 0 and (n_valid == tc or nchunk == 1) and (n_valid % 8 == 0 or n_valid < 8)
    cb0 = jnp.pad(conv0, ((0, 0), (0, 8 - (CONV_W - 1)), (0, 0)))
    vec = jnp.zeros((8, G), f32).at[0].set(ba).at[1].set(bx).at[2].set(lam)
    full2 = lambda shp: pl.BlockSpec(shp, lambda b, c: (0, 0))
    y, cbo, ho = pl.pallas_call(
        functools.partial(_lru_kernel, n_valid=n_valid),
        grid=(B, nchunk),
        in_specs=[pl.BlockSpec((1, tc, G), lambda b, c: (b, c, COL_U)),
                  pl.BlockSpec((1, tc, G), lambda b, c: (b, c, COL_GATE)),
                  pl.BlockSpec((1, 8, G), lambda b, c: (b, 0, 0)),
                  pl.BlockSpec((1, 1, G), lambda b, c: (b, 0, 0)),
                  full2((CONV_W, G)), full2((1, G)), full2((G, G)), full2((G, G)), full2((8, G))],
        out_specs=[pl.BlockSpec((1, tc, G), lambda b, c: (b, c, 0)),
                   pl.BlockSpec((1, 8, G), lambda b, c: (b, 0, 0)),
                   pl.BlockSpec((1, 1, G), lambda b, c: (b, 0, 0))],
        out_shape=[jax.ShapeDtypeStruct((B, Tp, G), f32), jax.ShapeDtypeStruct((B, 8, G), f32),
                   jax.ShapeDtypeStruct((B, 1, G), f32)],
        scratch_shapes=[pltpu.VMEM((8, G), f32), pltpu.VMEM((1, G), f32),
                        pltpu.VMEM((tc, G), f32), pltpu.VMEM((tc, G), f32), pltpu.VMEM((tc, G), f32)],
        compiler_params=pltpu.CompilerParams(dimension_semantics=("parallel", "arbitrary"),
                                             vmem_limit_bytes=_VMEM_LIMIT),
        name="lru_mixer",
    )(proj3, proj3, cb0, h0[:, None, :], conv_w, conv_b[None, :],
      _block_diag(wa).astype(jnp.bfloat16), _block_diag(wx).astype(jnp.bfloat16), vec)
    return y, cbo[:, :CONV_W - 1, :], ho[:, 0, :]


def _ssd_kernel(z_ref, xbc_ref, dt_ref, cb0_ref, s0_ref, cw_ref, cbias_ref, prow_ref, pcol_ref, dvec_ref, nw_ref,
                y_ref, cbo_ref, so_ref, conv_scr, st_scr, y_scr, *, n_valid):
    c = pl.program_id(1)
    tc = xbc_ref.shape[1]
    Q = SSD_CHUNK
    G = GROUP_WIDTH
    P = SSD_HEADDIM
    N = SSD_STATE
    GN = SSD_GROUPS * SSD_STATE

    @pl.when(c == 0)
    def _():
        conv_scr[...] = cb0_ref[0]
        st_scr[...] = s0_ref[0]

    xs = xbc_ref[0]
    prev3 = conv_scr[...]
    xbc_f = _silu(_causal_conv_tile(xs, prev3, cw_ref, cbias_ref))
    tail = _conv_tail(xs, prev3, n_valid)
    for i in range(CONV_W - 1):
        conv_scr[i:i + 1, :] = tail[i]

    ii = lax.broadcasted_iota(jnp.int32, (Q, Q), 0)
    jj = lax.broadcasted_iota(jnp.int32, (Q, Q), 1)
    lower = ii >= jj
    tri_l = lower.astype(jnp.float32)
    tri_u = (ii <= jj).astype(jnp.float32)
    bias_row = prow_ref[0:1, :]
    alog_row = prow_ref[1:2, :]
    bias_col = pcol_ref[:, 0:1]
    alog_col = pcol_ref[:, 1:2]
    rowq = lax.broadcasted_iota(jnp.int32, (Q, LANES), 0)
    laneq = lax.broadcasted_iota(jnp.int32, (8, Q), 1)
    bf = lambda a: a.astype(jnp.bfloat16)
    for j in range(tc // Q):
        rows = slice(j * Q, (j + 1) * Q)
        xq = xbc_f[rows, :]
        xt = xq.T
        dcol = dt_ref[0, rows, 0:LANES]
        drow = dcol.T[0:8, :]
        dt_col = _softplus(dcol + bias_row)
        dt_row = _softplus(drow + bias_col)
        if n_valid < tc:
            dt_col = jnp.where(rowq + j * Q < n_valid, dt_col, 0.0)
            dt_row = jnp.where(laneq + j * Q < n_valid, dt_row, 0.0)
        da_col = dt_col * (-jnp.exp(alog_row))
        da_row = dt_row * (-jnp.exp(alog_col))
        cs_col = jnp.dot(tri_l, da_col, precision=_HI, preferred_element_type=jnp.float32)
        cs_row = jnp.dot(da_row, tri_u, precision=_HI, preferred_element_type=jnp.float32)
        for g in range(SSD_GROUPS):
            bg = xq[:, G + g * N:G + (g + 1) * N]
            cg = xq[:, G + GN + g * N:G + GN + (g + 1) * N]
            bgt = xt[G + g * N:G + (g + 1) * N, :]
            cb = lax.dot_general(bf(cg), bf(bg), (((1,), (1,)), ((), ())), preferred_element_type=jnp.float32)
            for h in range(g * (SSD_HEADS // SSD_GROUPS), (g + 1) * (SSD_HEADS // SSD_GROUPS)):
                ci = cs_col[:, h:h + 1]
                cj = cs_row[h:h + 1, :]
                c_last = cs_row[h:h + 1, Q - 1:Q]
                lm = jnp.exp(jnp.where(lower, ci - cj, NEG_INF))
                x_h = xq[:, h * P:(h + 1) * P]
                xdt = x_h * dt_col[:, h:h + 1]
                st = st_scr[h]
                y_h = jnp.dot(bf(cb * lm), bf(xdt), preferred_element_type=jnp.float32)
                y_h = y_h + jnp.dot(bf(cg * jnp.exp(ci)), bf(st), preferred_element_type=jnp.float32)
                y_h = y_h + dvec_ref[:, h * P:(h + 1) * P] * x_h
                y_scr[rows, h * P:(h + 1) * P] = y_h
                st_scr[h] = st * jnp.exp(c_last) + jnp.dot(bf(bgt), bf(xdt * jnp.exp(c_last - ci)),
                                                            preferred_element_type=jnp.float32)
    y = y_scr[...] * _silu(z_ref[0])
    y_ref[0] = y * lax.rsqrt(jnp.mean(y * y, axis=-1, keepdims=True) + NORM_EPS) * nw_ref[...]

    @pl.when(c == pl.num_programs(1) - 1)
    def _():
        cbo_ref[0] = conv_scr[...]
        so_ref[0] = st_scr[...]


def _ssd_mixer(proj3, conv0, s0, conv_w, conv_b, dt_bias, a_log, d_skip, norm_w, *, n_valid, tc):
    B, Tp, _ = proj3.shape
    G = GROUP_WIDTH
    H, P, N = SSD_HEADS, SSD_HEADDIM, SSD_STATE
    C = SSD_CONV_CH
    f32 = jnp.float32
    nchunk = Tp // tc
    assert Tp % tc == 0 and tc % SSD_CHUNK == 0 and (n_valid == tc or nchunk == 1)
    cb0 = jnp.pad(conv0, ((0, 0), (0, 8 - (CONV_W - 1)), (0, 0)))
    s0t = jnp.swapaxes(s0, -1, -2)
    prow = jnp.zeros((8, LANES), f32).at[0, :H].set(dt_bias).at[1, :H].set(a_log)
    pcol = jnp.zeros((8, LANES), f32).at[:H, 0].set(dt_bias).at[:H, 1].set(a_log)
    dvec = jnp.repeat(d_skip, P)[None, :]
    full2 = lambda shp: pl.BlockSpec(shp, lambda b, c: (0, 0))
    y, cbo, so = pl.pallas_call(
        functools.partial(_ssd_kernel, n_valid=n_valid),
        grid=(B, nchunk),
        in_specs=[pl.BlockSpec((1, tc, G), lambda b, c: (b, c, COL_Z)),
                  pl.BlockSpec((1, tc, C), lambda b, c: (b, c, COL_XBC)),
                  pl.BlockSpec((1, tc, G), lambda b, c: (b, c, COL_DT)),
                  pl.BlockSpec((1, 8, C), lambda b, c: (b, 0, 0)),
                  pl.BlockSpec((1, H, N, P), lambda b, c: (b, 0, 0, 0)),
                  full2((CONV_W, C)), full2((1, C)), full2((8, LANES)), full2((8, LANES)), full2((1, G)),
                  full2((1, G))],
        out_specs=[pl.BlockSpec((1, tc, G), lambda b, c: (b, c, 0)),
                   pl.BlockSpec((1, 8, C), lambda b, c: (b, 0, 0)),
                   pl.BlockSpec((1, H, N, P), lambda b, c: (b, 0, 0, 0))],
        out_shape=[jax.ShapeDtypeStruct((B, Tp, G), f32), jax.ShapeDtypeStruct((B, 8, C), f32),
                   jax.ShapeDtypeStruct((B, H, N, P), f32)],
        scratch_shapes=[pltpu.VMEM((8, C), f32), pltpu.VMEM((H, N, P), f32), pltpu.VMEM((tc, G), f32)],
        compiler_params=pltpu.CompilerParams(dimension_semantics=("parallel", "arbitrary"),
                                             vmem_limit_bytes=_VMEM_LIMIT),
        name="ssd_mixer",
    )(proj3, proj3, proj3, cb0, s0t, conv_w, conv_b[None, :], prow, pcol, dvec, norm_w[None, :])
    return y, cbo[:, :CONV_W - 1, :], jnp.swapaxes(so, -1, -2)


def _rwkv_pad_cols(a):
    G = GROUP_WIDTH
    z = lambda w: jnp.zeros(a.shape[:-1] + (w,), a.dtype)
    o = 3 * G
    wd = a[..., o:o + RWKV_W_RANK]
    ad = a[..., o + RWKV_W_RANK:o + RWKV_W_RANK + RWKV_A_RANK]
    gd = a[..., o + RWKV_W_RANK + RWKV_A_RANK:]
    return jnp.concatenate([a[..., :o], wd, z(LANES - RWKV_W_RANK), ad, z(LANES - RWKV_A_RANK),
                            gd, z(LANES - RWKV_G_RANK)], axis=-1)


def _rwkv_unpad_cols(a):
    o = 3 * GROUP_WIDTH
    return jnp.concatenate([a[..., :o], a[..., o:o + RWKV_W_RANK], a[..., o + LANES:o + LANES + RWKV_A_RANK],
                            a[..., o + 2 * LANES:o + 2 * LANES + RWKV_G_RANK]], axis=-1)


def _rwkv_kernel(pd_ref, sh0_ref, s0_ref, mu_ref, vec_ref, wup_ref, aup_ref, gup_ref, seg_ref,
                 y_ref, sout_ref, shout_ref,
                 prev_scr, s_scr, r_scr, k_scr, d_scr, kk_scr, bb_scr, g_scr, bonus_scr, vt_scr, yt_scr, *, n_steps):
    c = pl.program_id(1)
    tc = pd_ref.shape[1]
    G = GROUP_WIDTH
    hd = RWKV_HEADDIM

    @pl.when(c == 0)
    def _():
        prev_scr[...] = sh0_ref[...]
        s_scr[...] = s0_ref[...]

    nb = pd_ref.shape[0]
    w0 = vec_ref[0:1, :]
    a0 = vec_ref[1:2, :]
    k_k = vec_ref[2:3, :]
    k_a = vec_ref[3:4, :]
    ln_w = vec_ref[4:5, :]
    ln_b = vec_ref[5:6, :]
    r_k = vec_ref[6:7, :]
    seg = seg_ref[...]
    hdot = lambda x, y: jnp.dot(x, y, precision=_HI, preferred_element_type=jnp.float32)
    for ib in range(nb):
        cur = pd_ref[ib]
        row = lax.broadcasted_iota(jnp.int32, cur.shape, 0)
        prev = jnp.where(row == 0, prev_scr[ib], pltpu.roll(cur, 1, axis=0))
        prev_scr[ib] = cur[n_steps - 1:n_steps, :] if n_steps < tc else cur[tc - 1:tc, :]
        m = cur + (prev - cur) * mu_ref[...]
        r = m[:, 0:G]
        k = m[:, G:2 * G]
        v = m[:, 2 * G:3 * G]
        wd = m[:, 3 * G:3 * G + LANES]
        ad = m[:, 3 * G + LANES:3 * G + 2 * LANES]
        gd = m[:, 3 * G + 2 * LANES:3 * G + 3 * LANES]
        w = -_softplus(-(w0 + hdot(jnp.tanh(wd), wup_ref[...]))) - 0.5
        a = jax.nn.sigmoid(a0 + hdot(ad, aup_ref[...]))
        kkr = k * k_k
        kk = kkr / jnp.maximum(jnp.sqrt(hdot(kkr * kkr, seg)), 1e-12)
        k2 = k * (1.0 + (a - 1.0) * k_a)
        g_scr[ib] = hdot(jax.nn.sigmoid(gd), gup_ref[...])
        bonus_scr[ib] = hdot(r * k2 * r_k, seg) * v
        r_scr[ib] = r
        k_scr[ib] = k2
        d_scr[ib] = jnp.exp(-jnp.exp(w))
        kk_scr[ib] = -kk
        bb_scr[ib] = kk * a
        vt_scr[ib] = v.T
    yt_scr[...] = jnp.zeros_like(yt_scr)

    lane = lax.broadcasted_iota(jnp.int32, (hd, LANES), 1)
    lo_half = lane < hd

    def half_sums(p):
        lo = jnp.sum(jnp.where(lo_half, p, 0.0), axis=1, keepdims=True)
        hi = jnp.sum(jnp.where(lo_half, 0.0, p), axis=1, keepdims=True)
        return lo, hi

    npair = RWKV_HEADS // 2

    def step8(i, states):
        t0 = pl.multiple_of(i * 8, 8)
        blk = pl.multiple_of((t0 // LANES) * LANES, LANES)
        rows = pl.ds(t0, 8)
        chains = [(ib, p) for ib in range(nb) for p in range(npair)]
        states = list(states)
        rowv = {}
        tiles = {}
        for ci, (ib, p) in enumerate(chains):
            cols = slice(p * LANES, (p + 1) * LANES)
            rowv[ci] = tuple(ref[ib, rows, cols] for ref in (kk_scr, d_scr, bb_scr, k_scr, r_scr))
            tile = lambda ref, hh: ref[ib, pl.ds(p * LANES + hh * hd, hd), pl.ds(blk, LANES)]
            tiles[ci] = [tile(vt_scr, 0), tile(vt_scr, 1), tile(yt_scr, 0), tile(yt_scr, 1)]

        def vcol_of(ci, j):
            hit = lane == (t0 + j - blk)
            vc0 = jnp.sum(jnp.where(hit, tiles[ci][0], 0.0), axis=1, keepdims=True)
            vc1 = jnp.sum(jnp.where(hit, tiles[ci][1], 0.0), axis=1, keepdims=True)
            return jnp.where(lo_half, vc0, vc1)

        def emit_y(ci, j):
            hit = lane == (t0 + j - blk)
            ylo, yhi = half_sums(states[ci] * rowv[ci][4][j:j + 1, :])
            tiles[ci][2] = jnp.where(hit, ylo, tiles[ci][2])
            tiles[ci][3] = jnp.where(hit, yhi, tiles[ci][3])

        nj = min(8, n_steps)
        vcols = [vcol_of(ci, 0) for ci in range(len(chains))]
        for j in range(nj):
            sas = [half_sums(states[ci] * rowv[ci][0][j:j + 1, :]) for ci in range(len(chains))]
            if j > 0:
                for ci in range(len(chains)):
                    emit_y(ci, j - 1)
            nxt = [vcol_of(ci, j + 1) for ci in range(len(chains))] if j + 1 < nj else None
            for ci in range(len(chains)):
                _, d8, bb8, k8, _ = rowv[ci]
                sa = jnp.where(lo_half, sas[ci][0], sas[ci][1])
                states[ci] = states[ci] * d8[j:j + 1, :] + sa * bb8[j:j + 1, :] + vcols[ci] * k8[j:j + 1, :]
            vcols = nxt
        for ci in range(len(chains)):
            emit_y(ci, nj - 1)
        for ci, (ib, p) in enumerate(chains):
            yt_scr[ib, pl.ds(p * LANES, hd), pl.ds(blk, LANES)] = tiles[ci][2]
            yt_scr[ib, pl.ds(p * LANES + hd, hd), pl.ds(blk, LANES)] = tiles[ci][3]
        return tuple(states)

    init = tuple(s_scr[ib, p] for ib in range(nb) for p in range(npair))
    states = lax.fori_loop(0, -(-n_steps // 8), step8, init)
    for ib in range(nb):
        for p in range(npair):
            s_scr[ib, p] = states[ib * npair + p]

    inv = 1.0 / hd
    for ib in range(nb):
        y = yt_scr[ib].T
        mean = hdot(y, seg) * inv
        yc = y - mean
        var = hdot(yc * yc, seg) * inv
        yn = yc * lax.rsqrt(var + RWKV_LN_EPS) * ln_w + ln_b
        y_ref[ib] = (yn + bonus_scr[ib]) * g_scr[ib]

    @pl.when(c == pl.num_programs(1) - 1)
    def _():
        sout_ref[...] = s_scr[...]
        shout_ref[...] = prev_scr[...]


def _rwkv_mixer(proj3, shift0, s0, mu, w0, w_up, a0, a_up, g_up, k_k, k_a, r_k, ln_w, ln_b, *, n_steps, tc, nb=1):
    B, Tp, _ = proj3.shape
    G = GROUP_WIDTH
    H, hd = RWKV_HEADS, RWKV_HEADDIM
    nchunk = Tp // tc
    assert Tp % tc == 0 and (n_steps == tc or nchunk == 1) and (n_steps % 8 == 0 or n_steps < 8)
    assert B % nb == 0
    f32 = jnp.float32
    sh0 = _rwkv_pad_cols(shift0)[:, None, :]
    s0p = s0.reshape(B, H // 2, 2, hd, hd).transpose(0, 1, 3, 2, 4).reshape(B, H // 2, hd, 2 * hd)
    mu_p = _rwkv_pad_cols(mu[None, :])
    vec = jnp.stack([w0, a0, k_k, k_a, ln_w, ln_b, r_k.reshape(G), jnp.zeros((G,), f32)])
    padk = lambda wgt: jnp.pad(wgt, ((0, LANES - wgt.shape[0]), (0, 0)))
    hid = jnp.arange(G) // hd
    seg = (hid[:, None] == hid[None, :]).astype(f32)
    full2 = lambda shp: pl.BlockSpec(shp, lambda b, c: (0, 0))
    tscr = lambda: pltpu.VMEM((nb, tc, G), f32)
    y, s_out, sh_out = pl.pallas_call(
        functools.partial(_rwkv_kernel, n_steps=n_steps),
        grid=(B // nb, nchunk),
        in_specs=[pl.BlockSpec((nb, tc, RWKV_PCOLS), lambda b, c: (b, c, COL_RWKV)),
                  pl.BlockSpec((nb, 1, RWKV_PCOLS), lambda b, c: (b, 0, 0)),
                  pl.BlockSpec((nb, H // 2, hd, 2 * hd), lambda b, c: (b, 0, 0, 0)),
                  full2((1, RWKV_PCOLS)), full2((8, G)), full2((LANES, G)), full2((LANES, G)), full2((LANES, G)),
                  full2((G, G))],
        out_specs=[pl.BlockSpec((nb, tc, G), lambda b, c: (b, c, 0)),
                   pl.BlockSpec((nb, H // 2, hd, 2 * hd), lambda b, c: (b, 0, 0, 0)),
                   pl.BlockSpec((nb, 1, RWKV_PCOLS), lambda b, c: (b, 0, 0))],
        out_shape=[jax.ShapeDtypeStruct((B, Tp, G), f32),
                   jax.ShapeDtypeStruct((B, H // 2, hd, 2 * hd), f32),
                   jax.ShapeDtypeStruct((B, 1, RWKV_PCOLS), f32)],
        scratch_shapes=[pltpu.VMEM((nb, 1, RWKV_PCOLS), f32), pltpu.VMEM((nb, H // 2, hd, 2 * hd), f32),
                        tscr(), tscr(), tscr(), tscr(), tscr(), tscr(), tscr(),
                        pltpu.VMEM((nb, G, tc), f32), pltpu.VMEM((nb, G, tc), f32)],
        compiler_params=pltpu.CompilerParams(dimension_semantics=("parallel", "arbitrary"),
                                             vmem_limit_bytes=_VMEM_LIMIT),
        name="rwkv_mixer",
    )(proj3, sh0, s0p, mu_p, vec, padk(w_up), padk(a_up), padk(g_up), seg)
    s_new = s_out.reshape(B, H // 2, hd, 2, hd).transpose(0, 1, 3, 2, 4).reshape(B, H, hd, hd)
    return y, _rwkv_unpad_cols(sh_out[:, 0, :]), s_new


def _xattn_kernel(x_ref, g_ref, wq_ref, wo_ref, mk_ref, mv_ref, o_ref, cat_scr):
    x = x_ref[...]
    hn = x * lax.rsqrt(jnp.mean(x * x, axis=-1, keepdims=True) + NORM_EPS) * g_ref[...]
    q = jnp.dot(hn.astype(jnp.bfloat16), wq_ref[...], preferred_element_type=jnp.float32)
    qb = q.astype(jnp.bfloat16)
    mk = mk_ref[0].astype(jnp.bfloat16)
    mv = mv_ref[0].astype(jnp.bfloat16)
    hd = X_HEADDIM
    scale = hd ** -0.5
    for h in range(X_HEADS):
        cols = slice(h * hd, (h + 1) * hd)
        s = lax.dot_general(qb[:, cols], mk[:, cols], (((1,), (1,)), ((), ())),
                            preferred_element_type=jnp.float32) * scale
        m = jnp.max(s, axis=-1, keepdims=True)
        p = jnp.exp(s - m)
        p = p / jnp.sum(p, axis=-1, keepdims=True)
        cat_scr[:, cols] = jnp.dot(p.astype(jnp.bfloat16), mv[:, cols], preferred_element_type=jnp.float32)
    o_ref[...] = x + jnp.dot(cat_scr[...].astype(jnp.bfloat16), wo_ref[...], preferred_element_type=jnp.float32)


def _xattn_block(x, gain, wq, wo, mk, mv, *, tb, rows_per_mem):
    n = x.shape[0]
    M = mk.shape[1]
    assert n % tb == 0 and rows_per_mem % tb == 0
    per = rows_per_mem // tb
    return pl.pallas_call(
        _xattn_kernel,
        grid=(n // tb,),
        in_specs=[pl.BlockSpec((tb, D_MODEL), lambda i: (i, 0)),
                  pl.BlockSpec((1, D_MODEL), lambda i: (0, 0)),
                  pl.BlockSpec((D_MODEL, D_MODEL), lambda i: (0, 0)),
                  pl.BlockSpec((D_MODEL, D_MODEL), lambda i: (0, 0)),
                  pl.BlockSpec((1, M, D_MODEL), lambda i: (i // per, 0, 0)),
                  pl.BlockSpec((1, M, D_MODEL), lambda i: (i // per, 0, 0))],
        out_specs=pl.BlockSpec((tb, D_MODEL), lambda i: (i, 0)),
        out_shape=jax.ShapeDtypeStruct((n, D_MODEL), jnp.float32),
        scratch_shapes=[pltpu.VMEM((tb, D_MODEL), jnp.float32)],
        compiler_params=pltpu.CompilerParams(dimension_semantics=("parallel",), vmem_limit_bytes=_VMEM_LIMIT),
        name="xattn_block",
    )(x, gain.reshape(1, -1), wq.astype(jnp.bfloat16), wo.astype(jnp.bfloat16), mk, mv)


def _top_values(s, count, with_rank=False):
    vals = []
    cur = s
    rank = jnp.full(s.shape, float(count), jnp.float32) if with_rank else None
    for r in range(count):
        m = jnp.max(cur, axis=0, keepdims=True)
        vals.append(m)
        hit = cur == m
        if with_rank:
            rank = jnp.where(hit, float(r), rank)
        cur = jnp.where(hit, NEG_INF, cur)
    return (vals, rank) if with_rank else vals


def _peer_route_kernel(x_ref, g_ref, wqt_ref, sk_ref, hn_ref, cnt_ref, e1_ref, r2_ref, e2_ref, a1_scr, a2_scr):
    tb = x_ref.shape[0]
    x = x_ref[...]
    hn = x * lax.rsqrt(jnp.mean(x * x, axis=-1, keepdims=True) + NORM_EPS) * g_ref[...]
    hb = hn.astype(jnp.bfloat16)
    hn_ref[...] = hb
    qt = lax.dot_general(wqt_ref[...], hb, (((1,), (1,)), ((), ())), preferred_element_type=jnp.float32)
    n_top = PEER_TOPK
    row8 = lax.broadcasted_iota(jnp.int32, (8, tb), 0)
    for h in range(PEER_HEADS):
        q1 = qt[(2 * h) * PEER_HALF:(2 * h + 1) * PEER_HALF, :].astype(jnp.bfloat16)
        q2 = qt[(2 * h + 1) * PEER_HALF:(2 * h + 2) * PEER_HALF, :].astype(jnp.bfloat16)
        s1 = jnp.dot(sk_ref[2 * h], q1, preferred_element_type=jnp.float32)
        s2 = jnp.dot(sk_ref[2 * h + 1], q2, preferred_element_type=jnp.float32)
        v1 = _top_values(s1, n_top)
        v2, rank2 = _top_values(s2, n_top, with_rank=True)
        for r in range(n_top):
            a1_scr[r:r + 1, :] = v1[r]
            a2_scr[r:r + 1, :] = v2[r]
        a1 = a1_scr[...]
        a2 = a2_scr[...]
        pieces = [a1 + a2[0:1, :]]
        for q in range(1, 8):
            lim = n_top // (q + 1)
            pieces.append(jnp.where(row8 < lim, a1[0:8, :] + a2[q:q + 1, :], NEG_INF))
        pieces.append(a1[0:1, :] + a2[8:16, :])
        cand = jnp.concatenate(pieces, axis=0)
        c = _top_values(cand, n_top)
        tau = c[PEER_TOPK - 1]
        mx = c[0]
        z = jnp.sum(jnp.where(cand >= tau, jnp.exp(cand - mx), 0.0), axis=0, keepdims=True)
        cnt = jnp.zeros(s1.shape, jnp.float32)
        for q in range(n_top):
            cnt = cnt + jnp.where(s1 + v2[q] >= tau, 1.0, 0.0)
        cnt_ref[h] = cnt
        e1_ref[h] = jnp.exp(s1 - v1[0]) / z
        r2_ref[h] = rank2.astype(jnp.bfloat16)
        e2_ref[h] = jnp.exp(s2 - v2[0]).astype(jnp.bfloat16)


def _peer_route(x, gain, wqt_b, sk_b, tb=256):
    n = x.shape[0]
    assert n % tb == 0
    hk = PEER_HEADS
    blk3 = pl.BlockSpec((hk, PEER_NKEYS, tb), lambda i: (0, 0, i))
    shp_f = jax.ShapeDtypeStruct((hk, PEER_NKEYS, n), jnp.float32)
    shp_b = jax.ShapeDtypeStruct((hk, PEER_NKEYS, n), jnp.bfloat16)
    return pl.pallas_call(
        _peer_route_kernel,
        grid=(n // tb,),
        in_specs=[pl.BlockSpec((tb, D_MODEL), lambda i: (i, 0)),
                  pl.BlockSpec((1, D_MODEL), lambda i: (0, 0)),
                  pl.BlockSpec((2 * hk * PEER_HALF, D_MODEL), lambda i: (0, 0)),
                  pl.BlockSpec((2 * hk, PEER_NKEYS, PEER_HALF), lambda i: (0, 0, 0))],
        out_specs=[pl.BlockSpec((tb, D_MODEL), lambda i: (i, 0)), blk3, blk3, blk3, blk3],
        out_shape=[jax.ShapeDtypeStruct((n, D_MODEL), jnp.bfloat16), shp_f, shp_f, shp_b, shp_b],
        scratch_shapes=[pltpu.VMEM((PEER_TOPK, tb), jnp.float32), pltpu.VMEM((PEER_TOPK, tb), jnp.float32)],
        compiler_params=pltpu.CompilerParams(dimension_semantics=("parallel",), vmem_limit_bytes=_VMEM_LIMIT),
        name="peer_route",
    )(x, gain, wqt_b, sk_b)


def _peer_dense_kernel(x_ref, hn_ref, u_ref, v_ref, cnt_ref, e1_ref, r2_ref, e2_ref, fg_ref, o_ref, acc_ref, w_ref,
                       *, ic, final_norm):
    e = pl.program_id(1)
    tb = x_ref.shape[0]

    @pl.when(e == 0)
    def _():
        acc_ref[...] = jnp.zeros_like(acc_ref)

    act = lax.dot_general(u_ref[0], hn_ref[...], (((1,), (1,)), ((), ())), preferred_element_type=jnp.float32)
    for il in range(ic):
        g = None
        for h in range(PEER_HEADS):
            cb = jnp.broadcast_to(cnt_ref[h, il:il + 1, :], (16, tb)).astype(jnp.bfloat16)[None]
            eb = jnp.broadcast_to(e1_ref[h, il:il + 1, :], (16, tb)).astype(jnp.bfloat16)[None]
            r2 = r2_ref[h].reshape(PEER_NKEYS // 16, 16, tb)
            e2 = e2_ref[h].reshape(PEER_NKEYS // 16, 16, tb)
            t = jnp.where(r2 < cb, e2 * eb, jnp.zeros_like(e2))
            g = t if g is None else g + t
        rows = slice(il * PEER_NKEYS, (il + 1) * PEER_NKEYS)
        a = act[rows, :].astype(jnp.bfloat16).reshape(PEER_NKEYS // 16, 16, tb)
        w_ref[rows, :] = (g * _gelu_tanh(a)).reshape(PEER_NKEYS, tb)
    acc_ref[...] += lax.dot_general(w_ref[...], v_ref[0], (((0,), (0,)), ((), ())),
                                    preferred_element_type=jnp.float32)

    @pl.when(e == pl.num_programs(1) - 1)
    def _():
        y = x_ref[...] + acc_ref[...]
        if final_norm:
            y = y * lax.rsqrt(jnp.mean(y * y, axis=-1, keepdims=True) + NORM_EPS) * fg_ref[...]
        o_ref[...] = y


def _peer_dense(x, hn_b, u_b, v_b, layer, cnt, e1, r2, e2, final_gain, tb, ic=16):
    n = x.shape[0]
    ne = u_b.shape[1]
    ec = ic * PEER_NKEYS
    assert n % tb == 0 and ne % ec == 0
    hk = PEER_HEADS
    row_blk = pl.BlockSpec((hk, ic, tb), lambda i, e: (0, e, i))
    full_blk = pl.BlockSpec((hk, PEER_NKEYS, tb), lambda i, e: (0, 0, i))
    fg = jnp.ones((1, D_MODEL), jnp.float32) if final_gain is None else final_gain.reshape(1, D_MODEL)
    return pl.pallas_call(
        functools.partial(_peer_dense_kernel, ic=ic, final_norm=final_gain is not None),
        grid=(n // tb, ne // ec),
        in_specs=[pl.BlockSpec((tb, D_MODEL), lambda i, e: (i, 0)),
                  pl.BlockSpec((tb, D_MODEL), lambda i, e: (i, 0)),
                  pl.BlockSpec((1, ec, D_MODEL), lambda i, e: (layer, e, 0)),
                  pl.BlockSpec((1, ec, D_MODEL), lambda i, e: (layer, e, 0)),
                  row_blk, row_blk, full_blk, full_blk,
                  pl.BlockSpec((1, D_MODEL), lambda i, e: (0, 0))],
        out_specs=pl.BlockSpec((tb, D_MODEL), lambda i, e: (i, 0)),
        out_shape=jax.ShapeDtypeStruct((n, D_MODEL), jnp.float32),
        scratch_shapes=[pltpu.VMEM((tb, D_MODEL), jnp.float32), pltpu.VMEM((ec, tb), jnp.bfloat16)],
        compiler_params=pltpu.CompilerParams(dimension_semantics=("parallel", "arbitrary"),
                                             vmem_limit_bytes=_VMEM_LIMIT),
        name="peer_dense",
    )(x, hn_b, u_b, v_b, cnt, e1, r2, e2, fg)


def _peer_weights(wq, subkeys, u_all_b, v_all_b, layer):
    return (wq.T.astype(jnp.bfloat16),
            subkeys.reshape(2 * PEER_HEADS, PEER_NKEYS, PEER_HALF).astype(jnp.bfloat16),
            u_all_b, v_all_b, layer)


def _peer_block(x, gain, weights, tb_route, tb_dense, final_gain=None):
    wqt_b, sk_b, u_b, v_b, layer = weights
    hn_b, cnt, e1, r2, e2 = _peer_route(x, gain.reshape(1, -1), wqt_b, sk_b, tb=tb_route)
    return _peer_dense(x, hn_b, u_b, v_b, layer, cnt, e1, r2, e2, final_gain, tb=tb_dense)


def _in_weight(w_in):
    abc = w_in[:, :D_OFF]
    pad = jnp.zeros((w_in.shape[0], ABC_PCOLS - D_OFF), w_in.dtype)
    return jnp.concatenate([abc, pad, _rwkv_pad_cols(w_in[:, D_OFF:])], axis=1)


def _layer_pre_peer(x2, bsz, T, lp, st, prompt):
    G = GROUP_WIDTH
    n = bsz * T
    proj = _mm(x2, lp['w_in_p'], gain=lp['norm_mix'])
    if prompt:
        n_valid, tc = SEQ_CHUNK, SEQ_CHUNK
        proj3 = proj.reshape(bsz, T, -1)
        ya, k_t, v_t = _moba_prompt(proj3, jnp.arange(T, dtype=jnp.int32))
        heads = lambda t: t.reshape(bsz, A_HEADS, HEAD_DIM, T).transpose(0, 3, 1, 2)
        k_new, v_new = heads(k_t), heads(v_t)
    else:
        assert T == 1
        n_valid, tc = T, DEC_ROWS
        ya, k_r = _moba_decode(proj[:, 0:G], proj[:, G:2 * G], proj[:, 2 * G:3 * G],
                               st['k_cache'], st['v_cache'], st['page_table'], PAST_LEN)
        heads = lambda t: t.reshape(bsz, T, A_HEADS, HEAD_DIM)
        k_new, v_new = heads(k_r), heads(proj[:, 2 * G:3 * G])
        proj3 = jnp.pad(proj[:, None, :], ((0, 0), (0, tc - T), (0, 0)))
    yb, lru_conv, lru_h = _lru_mixer(proj3, st['lru_conv'], st['lru_h'], lp['lru_conv_w'], lp['lru_conv_b'],
                                     lp['lru_wa'], lp['lru_ba'], lp['lru_wx'], lp['lru_bx'], lp['lru_lambda'],
                                     n_valid=n_valid, tc=tc)
    yc, ssd_conv, ssd_s = _ssd_mixer(proj3, st['ssd_conv'], st['ssd'], lp['ssd_conv_w'], lp['ssd_conv_b'],
                                     lp['ssd_dt_bias'], lp['ssd_a_log'], lp['ssd_d'], lp['ssd_norm'],
                                     n_valid=n_valid, tc=tc)
    yd, rwkv_shift, rwkv_s = _rwkv_mixer(proj3, st['rwkv_shift'], st['rwkv'],
                                         lp['rwkv_mu'], lp['rwkv_w0'], lp['rwkv_w_up'], lp['rwkv_a0'],
                                         lp['rwkv_a_up'], lp['rwkv_g_up'], lp['rwkv_k_k'], lp['rwkv_k_a'],
                                         lp['rwkv_r_k'], lp['rwkv_ln_w'], lp['rwkv_ln_b'], n_steps=n_valid, tc=tc)
    rows = lambda y: y[:, :T].reshape(n, G)
    x2 = _mm([ya.reshape(n, G), rows(yb), rows(yc), rows(yd)], lp['w_out'], residual=x2)
    if prompt:
        x2 = _xattn_block(x2, lp['norm_x'], lp['x_wq'], lp['x_wo'], st['mem_k'], st['mem_v'],
                          tb=SEQ_CHUNK, rows_per_mem=T)
    else:
        x8 = jnp.pad(x2[:, None, :], ((0, 0), (0, 7), (0, 0))).reshape(n * 8, D_MODEL)
        x8 = _xattn_block(x8, lp['norm_x'], lp['x_wq'], lp['x_wo'], st['mem_k'], st['mem_v'], tb=8, rows_per_mem=8)
        x2 = x8.reshape(n, 8, D_MODEL)[:, 0, :]
    new = {'k': k_new, 'v': v_new, 'lru_h': lru_h, 'lru_conv': lru_conv, 'ssd': ssd_s, 'ssd_conv': ssd_conv,
           'rwkv': rwkv_s, 'rwkv_shift': rwkv_shift}
    return x2, new


def kernel(x_prompt, x_sample, mem_prompt, cache_moba_k, cache_moba_v, page_table, state_lru_h, state_lru_conv, state_ssd, state_ssd_conv, state_rwkv, state_rwkv_shift, cache_mem_k, cache_mem_v, norm_mix, w_in, w_out, lru_conv_w, lru_conv_b, lru_wa, lru_ba, lru_wx, lru_bx, lru_lambda, ssd_conv_w, ssd_conv_b, ssd_dt_bias, ssd_a_log, ssd_d, ssd_norm, rwkv_mu, rwkv_w0, rwkv_w_up, rwkv_a0, rwkv_a_up, rwkv_g_up, rwkv_k_k, rwkv_k_a, rwkv_r_k, rwkv_ln_w, rwkv_ln_b, norm_x, x_wq, x_wk, x_wv, x_wo, norm_ffn, peer_wq, peer_subkeys, peer_u, peer_v, final_norm):
    bp, tp, _ = x_prompt.shape
    bd, td, _ = x_sample.shape
    n_p, n_s = bp * tp, bd * td
    n_s_pad = -(-n_s // LANES) * LANES
    assert n_p % PEER_TOK == 0
    f32 = jnp.float32
    xp2 = x_prompt.reshape(n_p, D_MODEL)
    xs2 = x_sample.reshape(n_s, D_MODEL)
    names = ('k', 'v', 'lru_h', 'lru_conv', 'ssd', 'ssd_conv', 'rwkv', 'rwkv_shift')
    p_new = {n: [] for n in names + ('mem_k', 'mem_v')}
    s_new = {n: [] for n in names}
    mem2 = mem_prompt.reshape(bp * MEM_LEN, D_MODEL)
    pool = lambda c: jnp.transpose(c, (0, 1, 3, 4, 2)).reshape((-1, A_HEADS, HEAD_DIM, PAGE_SIZE))
    cache_kt, cache_vt = pool(cache_moba_k), pool(cache_moba_v)
    peer_u_b, peer_v_b = peer_u.astype(jnp.bfloat16), peer_v.astype(jnp.bfloat16)
    for l in range(DEPTH):
        lp = {
            'norm_mix': norm_mix[l], 'w_in_p': _in_weight(w_in[l]), 'w_out': w_out[l],
            'lru_conv_w': lru_conv_w[l], 'lru_conv_b': lru_conv_b[l], 'lru_wa': lru_wa[l], 'lru_ba': lru_ba[l],
            'lru_wx': lru_wx[l], 'lru_bx': lru_bx[l], 'lru_lambda': lru_lambda[l],
            'ssd_conv_w': ssd_conv_w[l], 'ssd_conv_b': ssd_conv_b[l], 'ssd_dt_bias': ssd_dt_bias[l],
            'ssd_a_log': ssd_a_log[l], 'ssd_d': ssd_d[l], 'ssd_norm': ssd_norm[l],
            'rwkv_mu': rwkv_mu[l], 'rwkv_w0': rwkv_w0[l], 'rwkv_w_up': rwkv_w_up[l], 'rwkv_a0': rwkv_a0[l],
            'rwkv_a_up': rwkv_a_up[l], 'rwkv_g_up': rwkv_g_up[l], 'rwkv_k_k': rwkv_k_k[l], 'rwkv_k_a': rwkv_k_a[l],
            'rwkv_r_k': rwkv_r_k[l], 'rwkv_ln_w': rwkv_ln_w[l], 'rwkv_ln_b': rwkv_ln_b[l],
            'norm_x': norm_x[l], 'x_wq': x_wq[l], 'x_wo': x_wo[l],
        }
        mk = _mm(mem2, x_wk[l]).reshape(bp, MEM_LEN, D_MODEL)
        mv = _mm(mem2, x_wv[l]).reshape(bp, MEM_LEN, D_MODEL)
        st_p = {
            'lru_conv': jnp.zeros((bp, CONV_W - 1, LRU_WIDTH), f32),
            'lru_h': jnp.zeros((bp, LRU_WIDTH), f32),
            'ssd_conv': jnp.zeros((bp, CONV_W - 1, SSD_CONV_CH), f32),
            'ssd': jnp.zeros((bp, SSD_HEADS, SSD_HEADDIM, SSD_STATE), f32),
            'rwkv_shift': jnp.zeros((bp, RWKV_COLS), f32),
            'rwkv': jnp.zeros((bp, RWKV_HEADS, RWKV_HEADDIM, RWKV_HEADDIM), f32),
            'mem_k': mk, 'mem_v': mv,
        }
        xp2, npl = _layer_pre_peer(xp2, bp, tp, lp, st_p, True)
        for n in names:
            p_new[n].append(npl[n])
        p_new['mem_k'].append(mk.reshape(bp, MEM_LEN, X_HEADS, X_HEADDIM))
        p_new['mem_v'].append(mv.reshape(bp, MEM_LEN, X_HEADS, X_HEADDIM))
        st_s = {
            'k_cache': cache_kt, 'v_cache': cache_vt, 'page_table': page_table + l * cache_moba_k.shape[1],
            'lru_conv': state_lru_conv[l], 'lru_h': state_lru_h[l],
            'ssd_conv': state_ssd_conv[l], 'ssd': state_ssd[l],
            'rwkv_shift': state_rwkv_shift[l], 'rwkv': state_rwkv[l],
            'mem_k': cache_mem_k[l].reshape(bd, MEM_LEN, D_MODEL), 'mem_v': cache_mem_v[l].reshape(bd, MEM_LEN, D_MODEL),
        }
        xs2, nsl = _layer_pre_peer(xs2, bd, td, lp, st_s, False)
        for n in names:
            s_new[n].append(nsl[n])
        fg = final_norm if l == DEPTH - 1 else None
        pw = _peer_weights(peer_wq[l], peer_subkeys[l], peer_u_b, peer_v_b, l)
        xp2 = _peer_block(xp2, norm_ffn[l], pw, PEER_ROUTE_TOK, PEER_TOK, fg)
        xs_pad = jnp.pad(xs2, ((0, n_s_pad - n_s), (0, 0)))
        xs2 = _peer_block(xs_pad, norm_ffn[l], pw, LANES, LANES, fg)[:n_s]
    y_prompt = xp2.reshape(bp, tp, D_MODEL)
    y_sample = xs2.reshape(bd, td, D_MODEL)
    return (y_prompt, y_sample,
            jnp.stack(p_new['k']), jnp.stack(p_new['v']), jnp.stack(p_new['lru_h']), jnp.stack(p_new['lru_conv']),
            jnp.stack(p_new['ssd']), jnp.stack(p_new['ssd_conv']), jnp.stack(p_new['rwkv']), jnp.stack(p_new['rwkv_shift']),
            jnp.stack(p_new['mem_k']), jnp.stack(p_new['mem_v']),
            jnp.stack(s_new['k']), jnp.stack(s_new['v']), jnp.stack(s_new['lru_h']), jnp.stack(s_new['lru_conv']),
            jnp.stack(s_new['ssd']), jnp.stack(s_new['ssd_conv']), jnp.stack(s_new['rwkv']), jnp.stack(s_new['rwkv_shift']))
```

```python
import functools

import jax
import jax.numpy as jnp
from jax import lax
from jax.experimental import pallas as pl
from jax.experimental.pallas import tpu as pltpu

D_MODEL = 1024
DEPTH = 2
PAST_LEN = 16384
PAGE_SIZE = 128
GROUP_WIDTH = 256
HEAD_DIM = 64
A_HEADS = GROUP_WIDTH // HEAD_DIM
MOBA_BLOCK = 256
MOBA_TOPK = 3
ROPE_THETA = 10000.0
LRU_WIDTH = GROUP_WIDTH
LRU_C = 8.0
CONV_W = 4
SSD_HEADS = 4
SSD_HEADDIM = GROUP_WIDTH // SSD_HEADS
SSD_GROUPS = 2
SSD_STATE = 64
SSD_CHUNK = 128
SSD_CONV_CH = GROUP_WIDTH + 2 * SSD_GROUPS * SSD_STATE
RWKV_HEADS = 4
RWKV_HEADDIM = GROUP_WIDTH // RWKV_HEADS
RWKV_W_RANK = 32
RWKV_A_RANK = 32
RWKV_G_RANK = 64
RWKV_COLS = 3 * GROUP_WIDTH + RWKV_W_RANK + RWKV_A_RANK + RWKV_G_RANK
RWKV_LN_EPS = 64e-5
D_OFF = 3 * GROUP_WIDTH + 2 * LRU_WIDTH + GROUP_WIDTH + SSD_CONV_CH + SSD_HEADS
MEM_LEN = 256
X_HEADS = 4
X_HEADDIM = D_MODEL // X_HEADS
PEER_HEADS = 8
PEER_NKEYS = 128
PEER_HALF = 128
PEER_TOPK = 16
NORM_EPS = 1e-6
NEG_INF = -1e30

LANES = 128
RWKV_PCOLS = 3 * GROUP_WIDTH + 3 * LANES
ABC_PCOLS = 2 * RWKV_PCOLS
COL_Q, COL_K, COL_V, COL_U, COL_GATE, COL_Z = 0, 1, 2, 3, 4, 5
COL_XBC = 3
COL_DT = 8
COL_RWKV = 2
PAGES_PER_BLOCK = MOBA_BLOCK // PAGE_SIZE
KMEAN_PAGES = 32
PEER_TOK = 512
PEER_ROUTE_TOK = 256
SEQ_CHUNK = 512
DEC_ROWS = 128
_VMEM_LIMIT = 56 * 1024 * 1024
_HI = lax.Precision.HIGHEST


def _softplus(x):
    return jnp.maximum(x, 0.0) + jnp.log1p(jnp.exp(-jnp.abs(x)))


def _silu(x):
    return x * jax.nn.sigmoid(x)


def _gelu_tanh(x):
    return 0.5 * x * (1.0 + jnp.tanh(0.7978845608028654 * (x + 0.044715 * (x * x * x))))


def _mm_kernel(*refs, n_x, has_norm, has_res):
    x_refs = refs[:n_x]
    w_ref = refs[n_x]
    pos = n_x + 1
    g_ref = r_ref = None
    if has_norm:
        g_ref = refs[pos]
        pos += 1
    if has_res:
        r_ref = refs[pos]
        pos += 1
    o_ref = refs[pos]
    y = None
    off = 0
    for x_ref in x_refs:
        x = x_ref[...]
        kx = x.shape[1]
        if has_norm:
            x = x * lax.rsqrt(jnp.mean(x * x, axis=-1, keepdims=True) + NORM_EPS) * g_ref[...]
        t = jnp.dot(x.astype(jnp.bfloat16), w_ref[off:off + kx, :], preferred_element_type=jnp.float32)
        y = t if y is None else y + t
        off += kx
    if has_res:
        y = y + r_ref[...]
    o_ref[...] = y


def _mm(xs, w, gain=None, residual=None, tm=512):
    if not isinstance(xs, (list, tuple)):
        xs = [xs]
    assert gain is None or len(xs) == 1
    M = xs[0].shape[0]
    K, N = w.shape
    tm = min(tm, M)
    assert M % tm == 0 and sum(x.shape[1] for x in xs) == K
    args = list(xs) + [w.astype(jnp.bfloat16)]
    in_specs = [pl.BlockSpec((tm, x.shape[1]), lambda i: (i, 0)) for x in xs]
    in_specs.append(pl.BlockSpec((K, N), lambda i: (0, 0)))
    if gain is not None:
        args.append(gain.reshape(1, K).astype(jnp.float32))
        in_specs.append(pl.BlockSpec((1, K), lambda i: (0, 0)))
    if residual is not None:
        args.append(residual)
        in_specs.append(pl.BlockSpec((tm, N), lambda i: (i, 0)))
    return pl.pallas_call(
        functools.partial(_mm_kernel, n_x=len(xs), has_norm=gain is not None, has_res=residual is not None),
        grid=(M // tm,),
        in_specs=in_specs,
        out_specs=pl.BlockSpec((tm, N), lambda i: (i, 0)),
        out_shape=jax.ShapeDtypeStruct((M, N), jnp.float32),
        compiler_params=pltpu.CompilerParams(dimension_semantics=("parallel",), vmem_limit_bytes=_VMEM_LIMIT),
        name="mm",
    )(*args)


def _rmsnorm(x, g):
    return x * lax.rsqrt(jnp.mean(x * x, axis=-1, keepdims=True) + NORM_EPS) * g


def _rope_tables(pos):
    half = HEAD_DIM // 2
    freq = 1.0 / (ROPE_THETA ** (jnp.arange(half, dtype=jnp.float32) / half))
    ang = pos.astype(jnp.float32)[:, None] * freq[None, :]
    cos = jnp.cos(ang)
    sin = jnp.sin(ang)
    cos_t = jnp.tile(jnp.concatenate([cos, cos], axis=-1), (1, A_HEADS))
    sin_t = jnp.tile(jnp.concatenate([-sin, sin], axis=-1), (1, A_HEADS))
    return cos_t, sin_t


def _rope_apply(x, cos_t, sin_t):
    half = HEAD_DIM // 2
    lane = lax.broadcasted_iota(jnp.int32, x.shape, 1)
    first = (lane % HEAD_DIM) < half
    w = x.shape[1]
    swapped = jnp.where(first, pltpu.roll(x, w - half, axis=1), pltpu.roll(x, half, axis=1))
    return x * cos_t + swapped * sin_t


def _moba_prompt_kernel(q_ref, k_ref, v_ref, cos_ref, sin_ref, ya_ref, kr_ref, vo_ref, ot_scr):
    T = q_ref.shape[1]
    G = GROUP_WIDTH
    hd = HEAD_DIM
    blk = MOBA_BLOCK
    nb = T // blk
    cos_t = cos_ref[...]
    sin_t = sin_ref[...]
    q = _rope_apply(q_ref[0], cos_t, sin_t)
    k = _rope_apply(k_ref[0], cos_t, sin_t)
    v = v_ref[0]
    v_t = v.T
    kr_ref[0] = k.T
    vo_ref[0] = v_t
    kmean = jnp.concatenate([jnp.sum(k[n * blk:(n + 1) * blk, :], axis=0, keepdims=True) for n in range(nb)],
                            axis=0) * (1.0 / blk)
    qt = q.T.astype(jnp.bfloat16)
    vt = v_t.astype(jnp.bfloat16)
    kb = k.astype(jnp.bfloat16)
    lane_g = lax.broadcasted_iota(jnp.int32, (nb, G), 1)
    rown = lax.broadcasted_iota(jnp.int32, (nb, T), 0)
    cur = lax.broadcasted_iota(jnp.int32, (nb, T), 1) // blk
    kpos = lax.broadcasted_iota(jnp.int32, (blk, blk), 0)
    qpos = lax.broadcasted_iota(jnp.int32, (blk, blk), 1)
    causal = kpos <= qpos
    scale = hd ** -0.5
    for h in range(A_HEADS):
        km_h = jnp.where(lane_g // hd == h, kmean, 0.0)
        gate = lax.dot_general(km_h, q, (((1,), (1,)), ((), ())), precision=_HI,
                               preferred_element_type=jnp.float32)
        gate = jnp.where(rown < cur, gate, NEG_INF)
        sel_rows = []
        for n in range(nb):
            gn = gate[n:n + 1, :]
            beats = (gate > gn) | ((gate == gn) & (rown < n))
            rank = jnp.sum(beats.astype(jnp.float32), axis=0, keepdims=True)
            sel_rows.append(rank < MOBA_TOPK)
        k_h = kb[:, h * hd:(h + 1) * hd]
        qt_h = qt[h * hd:(h + 1) * hd, :]
        vt_h = vt[h * hd:(h + 1) * hd, :]
        for qi in range(nb):
            qs = slice(qi * blk, (qi + 1) * blk)
            q_blk = qt_h[:, qs]
            nk = (qi + 1) * blk
            s = jnp.dot(k_h[0:nk, :], q_blk, preferred_element_type=jnp.float32) * scale
            parts = [jnp.where(sel_rows[n][:, qs], s[n * blk:(n + 1) * blk, :], NEG_INF) for n in range(qi)]
            parts.append(jnp.where(causal, s[qi * blk:nk, :], NEG_INF))
            s = jnp.concatenate(parts, axis=0) if qi else parts[0]
            m = jnp.max(s, axis=0, keepdims=True)
            p = jnp.exp(s - m)
            l = jnp.sum(p, axis=0, keepdims=True)
            acc = jnp.dot(vt_h[:, 0:nk], p.astype(jnp.bfloat16), preferred_element_type=jnp.float32)
            ot_scr[h * hd:(h + 1) * hd, qs] = acc / l
    ya_ref[0] = ot_scr[...].T


def _moba_prompt(proj3, pos):
    B, T, _ = proj3.shape
    G = GROUP_WIDTH
    cos_t, sin_t = _rope_tables(pos)
    col = lambda j: pl.BlockSpec((1, T, G), lambda b: (b, 0, j))
    tab = pl.BlockSpec((T, G), lambda b: (0, 0))
    out = pl.BlockSpec((1, T, G), lambda b: (b, 0, 0))
    out_t = pl.BlockSpec((1, G, T), lambda b: (b, 0, 0))
    shp = jax.ShapeDtypeStruct((B, T, G), jnp.float32)
    shp_t = jax.ShapeDtypeStruct((B, G, T), jnp.float32)
    return pl.pallas_call(
        _moba_prompt_kernel,
        grid=(B,),
        in_specs=[col(COL_Q), col(COL_K), col(COL_V), tab, tab],
        out_specs=[out, out_t, out_t],
        out_shape=[shp, shp_t, shp_t],
        scratch_shapes=[pltpu.VMEM((G, T), jnp.float32)],
        compiler_params=pltpu.CompilerParams(dimension_semantics=("parallel",), vmem_limit_bytes=_VMEM_LIMIT),
        name="moba_prompt",
    )(proj3, proj3, proj3, cos_t, sin_t)


def _kmean_kernel(pt_ref, *refs):
    pages, o_ref = refs[:-1], refs[-1]
    j = pl.program_id(1)
    nblk = len(pages) // PAGES_PER_BLOCK

    @pl.when(j == 0)
    def _():
        o_ref[...] = jnp.zeros_like(o_ref)

    lane = lax.broadcasted_iota(jnp.int32, o_ref.shape[1:], 2)
    acc = o_ref[0]
    for blk in range(nblk):
        s = None
        for i in range(PAGES_PER_BLOCK):
            part = pages[blk * PAGES_PER_BLOCK + i][0]
            s = part if s is None else s + part
        mean = jnp.sum(s, axis=-1, keepdims=True) * (1.0 / MOBA_BLOCK)
        acc = jnp.where(lane == j * nblk + blk, mean, acc)
    o_ref[0] = acc


def _moba_block_means(cache_kt, page_table):
    B, n_pages = page_table.shape
    H, hd = A_HEADS, HEAD_DIM
    assert n_pages % KMEAN_PAGES == 0 and n_pages // PAGES_PER_BLOCK <= LANES
    steps = n_pages // KMEAN_PAGES

    def page_spec(i):
        return pl.BlockSpec((1, H, hd, PAGE_SIZE), lambda b, j, pt: (pt[b * n_pages + j * KMEAN_PAGES + i], 0, 0, 0))

    return pl.pallas_call(
        _kmean_kernel,
        grid_spec=pltpu.PrefetchScalarGridSpec(
            num_scalar_prefetch=1,
            grid=(B, steps),
            in_specs=[page_spec(i) for i in range(KMEAN_PAGES)],
            out_specs=pl.BlockSpec((1, H, hd, LANES), lambda b, j, pt: (b, 0, 0, 0)),
        ),
        out_shape=jax.ShapeDtypeStruct((B, H, hd, LANES), jnp.float32),
        compiler_params=pltpu.CompilerParams(dimension_semantics=("parallel", "arbitrary"),
                                             vmem_limit_bytes=_VMEM_LIMIT),
        name="moba_block_means",
    )(page_table.reshape(-1), *([cache_kt] * KMEAN_PAGES))


def _moba_select_kernel(q_ref, k_ref, cos_ref, sin_ref, km_ref, qr_ref, kr_ref, top_ref, *, nb):
    q = _rope_apply(q_ref[0], cos_ref[...], sin_ref[...])
    k = _rope_apply(k_ref[0], cos_ref[...], sin_ref[...])
    qr_ref[0] = q
    kr_ref[0] = k
    qcol = q.T[:, 0:1]
    lane = lax.broadcasted_iota(jnp.int32, (1, LANES), 1)
    row8 = lax.broadcasted_iota(jnp.int32, (8, LANES), 0)
    lane8 = lax.broadcasted_iota(jnp.int32, (8, LANES), 1)
    top = jnp.zeros((8, LANES), jnp.int32)
    for h in range(A_HEADS):
        gate = jnp.sum(km_ref[0, h] * qcol[h * HEAD_DIM:(h + 1) * HEAD_DIM, :], axis=0, keepdims=True)
        gate = jnp.where(lane < nb, gate, NEG_INF)
        for r in range(MOBA_TOPK):
            m = jnp.max(gate, axis=1, keepdims=True)
            idx = jnp.min(jnp.where(gate == m, lane, LANES), axis=1, keepdims=True)
            top = jnp.where((row8 == r) & (lane8 == h), idx, top)
            gate = jnp.where(lane == idx, NEG_INF, gate)
    top_ref[0] = top


def _moba_select(q8, k8, pos, kmean_t, nb):
    B = q8.shape[0]
    G = GROUP_WIDTH
    cos_t, sin_t = _rope_tables(pos)
    row = pl.BlockSpec((1, 8, G), lambda b: (b, 0, 0))
    tab = pl.BlockSpec((1, G), lambda b: (0, 0))
    return pl.pallas_call(
        functools.partial(_moba_select_kernel, nb=nb),
        grid=(B,),
        in_specs=[row, row, tab, tab, pl.BlockSpec((1, A_HEADS, HEAD_DIM, LANES), lambda b: (b, 0, 0, 0))],
        out_specs=[row, row, pl.BlockSpec((1, 8, LANES), lambda b: (b, 0, 0))],
        out_shape=[jax.ShapeDtypeStruct((B, 8, G), jnp.float32), jax.ShapeDtypeStruct((B, 8, G), jnp.float32),
                   jax.ShapeDtypeStruct((B, 8, LANES), jnp.int32)],
        compiler_params=pltpu.CompilerParams(dimension_semantics=("parallel",)),
        name="moba_select",
    )(q8, k8, cos_t, sin_t, kmean_t)


def _moba_decode_attn_kernel(pp_ref, q_ref, kn_ref, vn_ref, *refs):
    npg = MOBA_TOPK * PAGES_PER_BLOCK
    kp, vp, o_ref = refs[:npg], refs[npg:2 * npg], refs[2 * npg]
    bf = lambda a: a.astype(jnp.bfloat16)
    qh = bf(q_ref[0, 0])
    scale = HEAD_DIM ** -0.5
    s_pages = [jnp.dot(qh, bf(kp[i][0, 0]), preferred_element_type=jnp.float32) * scale for i in range(npg)]
    s_own = jnp.sum(qh.astype(jnp.float32) * bf(kn_ref[0, 0]).astype(jnp.float32), axis=-1, keepdims=True) * scale
    m = s_own
    for s in s_pages:
        m = jnp.maximum(m, jnp.max(s, axis=-1, keepdims=True))
    p_own = jnp.exp(s_own - m)
    l = p_own
    acc = p_own * bf(vn_ref[0, 0]).astype(jnp.float32)
    for i in range(npg):
        p = jnp.exp(s_pages[i] - m)
        l = l + jnp.sum(p, axis=-1, keepdims=True)
        acc = acc + lax.dot_general(bf(p), bf(vp[i][0, 0]), (((1,), (1,)), ((), ())),
                                    preferred_element_type=jnp.float32)
    o_ref[0, 0] = acc / l


def _moba_decode_attn(qh, kh, vh, cache_kt, cache_vt, phys):
    B = qh.shape[0]
    H, hd = A_HEADS, HEAD_DIM
    npg = MOBA_TOPK * PAGES_PER_BLOCK
    row = pl.BlockSpec((1, 1, 8, hd), lambda b, h, pp: (b, h, 0, 0))

    def page_spec(i):
        return pl.BlockSpec((1, 1, hd, PAGE_SIZE), lambda b, h, pp: (pp[(b * H + h) * npg + i], h, 0, 0))

    return pl.pallas_call(
        _moba_decode_attn_kernel,
        grid_spec=pltpu.PrefetchScalarGridSpec(
            num_scalar_prefetch=1,
            grid=(B, H),
            in_specs=[row, row, row] + [page_spec(i) for i in range(npg)] * 2,
            out_specs=row,
        ),
        out_shape=jax.ShapeDtypeStruct((B, H, 8, hd), jnp.float32),
        compiler_params=pltpu.CompilerParams(dimension_semantics=("parallel", "arbitrary")),
        name="moba_decode_attn",
    )(phys.reshape(-1), qh, kh, vh, *([cache_kt] * npg), *([cache_vt] * npg))


def _moba_decode(q, k, v, cache_kt, cache_vt, page_table, pos0):
    B = q.shape[0]
    H, hd = A_HEADS, HEAD_DIM
    nb = page_table.shape[1] // PAGES_PER_BLOCK
    pad8 = lambda a: jnp.pad(a[:, None, :], ((0, 0), (0, 7), (0, 0)))
    kmean_t = _moba_block_means(cache_kt, page_table)
    q8, k8, top = _moba_select(pad8(q), pad8(k), jnp.full((1,), pos0, jnp.int32), kmean_t, nb)
    blocks = jnp.swapaxes(top[:, :MOBA_TOPK, :H], 1, 2)
    pages = (blocks[..., None] * PAGES_PER_BLOCK + jnp.arange(PAGES_PER_BLOCK)).reshape(B, -1)
    phys = jnp.take_along_axis(page_table, pages, axis=1)
    by_head = lambda a8: a8.reshape(B, 8, H, hd).transpose(0, 2, 1, 3)
    out = _moba_decode_attn(by_head(q8), by_head(k8), by_head(pad8(v)), cache_kt, cache_vt, phys)
    return out[:, :, 0, :].reshape(B, H * hd), k8[:, 0, :]


def _causal_conv_tile(xs, prev3, w_ref, b_ref):
    row = lax.broadcasted_iota(jnp.int32, xs.shape, 0)
    y = b_ref[...] + xs * w_ref[CONV_W - 1:CONV_W, :]
    for k in range(1, CONV_W):
        sh = pltpu.roll(xs, k, axis=0)
        for r in range(k):
            src = r + (CONV_W - 1) - k
            sh = jnp.where(row == r, prev3[src:src + 1, :], sh)
        y = y + sh * w_ref[CONV_W - 1 - k:CONV_W - k, :]
    return y


def _conv_tail(xs, prev3, n_valid):
    rows = []
    for i in range(CONV_W - 1):
        idx = n_valid - (CONV_W - 1) + i
        rows.append(xs[idx:idx + 1, :] if idx >= 0 else prev3[idx + CONV_W - 1:idx + CONV_W, :])
    return rows


def _lru_kernel(u_ref, gate_ref, cb0_ref, h0_ref, cw_ref, cbias_ref, wa_ref, wx_ref, vec_ref,
                y_ref, cbo_ref, ho_ref, conv_scr, h_scr, a_scr, b_scr, hs_scr, *, n_valid):
    c = pl.program_id(1)
    tc = u_ref.shape[1]

    @pl.when(c == 0)
    def _():
        conv_scr[...] = cb0_ref[0]
        h_scr[...] = h0_ref[0]

    us = u_ref[0]
    prev3 = conv_scr[...]
    xc = _causal_conv_tile(us, prev3, cw_ref, cbias_ref)
    tail = _conv_tail(us, prev3, n_valid)
    for i in range(CONV_W - 1):
        conv_scr[i:i + 1, :] = tail[i]
    xb = xc.astype(jnp.bfloat16)
    r = jax.nn.sigmoid(jnp.dot(xb, wa_ref[...], preferred_element_type=jnp.float32) + vec_ref[0:1, :])
    i_g = jax.nn.sigmoid(jnp.dot(xb, wx_ref[...], preferred_element_type=jnp.float32) + vec_ref[1:2, :])
    log_a = -LRU_C * r * _softplus(-vec_ref[2:3, :])
    a_scr[...] = jnp.exp(log_a)
    b_scr[...] = jnp.sqrt(-jnp.tanh(log_a) * (jnp.exp(2.0 * log_a) + 1.0)) * (i_g * xc)

    row8 = lax.broadcasted_iota(jnp.int32, (8, LRU_WIDTH), 0)

    def step8(i, h):
        t0 = pl.multiple_of(i * 8, 8)
        a8 = a_scr[pl.ds(t0, 8), :]
        b8 = b_scr[pl.ds(t0, 8), :]
        out = jnp.zeros((8, LRU_WIDTH), jnp.float32)
        for j in range(min(8, n_valid)):
            h = a8[j:j + 1, :] * h + b8[j:j + 1, :]
            out = jnp.where(row8 == j, h, out)
        hs_scr[pl.ds(t0, 8), :] = out
        return h

    if n_valid < tc:
        hs_scr[...] = jnp.zeros_like(hs_scr)
    h = lax.fori_loop(0, -(-n_valid // 8), step8, h_scr[...])
    h_scr[...] = h
    y_ref[0] = hs_scr[...] * _gelu_tanh(gate_ref[0])

    @pl.when(c == pl.num_programs(1) - 1)
    def _():
        cbo_ref[0] = conv_scr[...]
        ho_ref[0] = h_scr[...]


def _block_diag(w):
    n, d, e = w.shape
    eye = jnp.eye(n, dtype=w.dtype)
    return (eye[:, None, :, None] * w[:, :, None, :]).reshape(n * d, n * e)


def _lru_mixer(proj3, conv0, h0, conv_w, conv_b, wa, ba, wx, bx, lam, *, n_valid, tc):
    B, Tp, _ = proj3.shape
    G = GROUP_WIDTH
    f32 = jnp.float32
    nchunk = Tp // tc
    assert Tp % tc == 0 and (n_valid == tc or nchunk == 1) and (n_valid % 8 == 0 or n_valid < 8)
    cb0 = jnp.pad(conv0, ((0, 0), (0, 8 - (CONV_W - 1)), (0, 0)))
    vec = jnp.zeros((8, G), f32).at[0].set(ba).at[1].set(bx).at[2].set(lam)
    full2 = lambda shp: pl.BlockSpec(shp, lambda b, c: (0, 0))
    y, cbo, ho = pl.pallas_call(
        functools.partial(_lru_kernel, n_valid=n_valid),
        grid=(B, nchunk),
        in_specs=[pl.BlockSpec((1, tc, G), lambda b, c: (b, c, COL_U)),
                  pl.BlockSpec((1, tc, G), lambda b, c: (b, c, COL_GATE)),
                  pl.BlockSpec((1, 8, G), lambda b, c: (b, 0, 0)),
                  pl.BlockSpec((1, 1, G), lambda b, c: (b, 0, 0)),
                  full2((CONV_W, G)), full2((1, G)), full2((G, G)), full2((G, G)), full2((8, G))],
        out_specs=[pl.BlockSpec((1, tc, G), lambda b, c: (b, c, 0)),
                   pl.BlockSpec((1, 8, G), lambda b, c: (b, 0, 0)),
                   pl.BlockSpec((1, 1, G), lambda b, c: (b, 0, 0))],
        out_shape=[jax.ShapeDtypeStruct((B, Tp, G), f32), jax.ShapeDtypeStruct((B, 8, G), f32),
                   jax.ShapeDtypeStruct((B, 1, G), f32)],
        scratch_shapes=[pltpu.VMEM((8, G), f32), pltpu.VMEM((1, G), f32),
                        pltpu.VMEM((tc, G), f32), pltpu.VMEM((tc, G), f32), pltpu.VMEM((tc, G), f32)],
        compiler_params=pltpu.CompilerParams(dimension_semantics=("parallel", "arbitrary"),
                                             vmem_limit_bytes=_VMEM_LIMIT),
        name="lru_mixer",
    )(proj3, proj3, cb0, h0[:, None, :], conv_w, conv_b[None, :],
      _block_diag(wa).astype(jnp.bfloat16), _block_diag(wx).astype(jnp.bfloat16), vec)
    return y, cbo[:, :CONV_W - 1, :], ho[:, 0, :]


def _ssd_kernel(z_ref, xbc_ref, dt_ref, cb0_ref, s0_ref, cw_ref, cbias_ref, prow_ref, pcol_ref, dvec_ref, nw_ref,
                y_ref, cbo_ref, so_ref, conv_scr, st_scr, y_scr, *, n_valid):
    c = pl.program_id(1)
    tc = xbc_ref.shape[1]
    Q = SSD_CHUNK
    G = GROUP_WIDTH
    P = SSD_HEADDIM
    N = SSD_STATE
    GN = SSD_GROUPS * SSD_STATE

    @pl.when(c == 0)
    def _():
        conv_scr[...] = cb0_ref[0]
        st_scr[...] = s0_ref[0]

    xs = xbc_ref[0]
    prev3 = conv_scr[...]
    xbc_f = _silu(_causal_conv_tile(xs, prev3, cw_ref, cbias_ref))
    tail = _conv_tail(xs, prev3, n_valid)
    for i in range(CONV_W - 1):
        conv_scr[i:i + 1, :] = tail[i]

    ii = lax.broadcasted_iota(jnp.int32, (Q, Q), 0)
    jj = lax.broadcasted_iota(jnp.int32, (Q, Q), 1)
    lower = ii >= jj
    tri_l = lower.astype(jnp.float32)
    tri_u = (ii <= jj).astype(jnp.float32)
    bias_row = prow_ref[0:1, :]
    alog_row = prow_ref[1:2, :]
    bias_col = pcol_ref[:, 0:1]
    alog_col = pcol_ref[:, 1:2]
    rowq = lax.broadcasted_iota(jnp.int32, (Q, LANES), 0)
    laneq = lax.broadcasted_iota(jnp.int32, (8, Q), 1)
    bf = lambda a: a.astype(jnp.bfloat16)
    for j in range(tc // Q):
        rows = slice(j * Q, (j + 1) * Q)
        xq = xbc_f[rows, :]
        xt = xq.T
        dcol = dt_ref[0, rows, 0:LANES]
        drow = dcol.T[0:8, :]
        dt_col = _softplus(dcol + bias_row)
        dt_row = _softplus(drow + bias_col)
        if n_valid < tc:
            dt_col = jnp.where(rowq + j * Q < n_valid, dt_col, 0.0)
            dt_row = jnp.where(laneq + j * Q < n_valid, dt_row, 0.0)
        da_col = dt_col * (-jnp.exp(alog_row))
        da_row = dt_row * (-jnp.exp(alog_col))
        cs_col = jnp.dot(tri_l, da_col, precision=_HI, preferred_element_type=jnp.float32)
        cs_row = jnp.dot(da_row, tri_u, precision=_HI, preferred_element_type=jnp.float32)
        for g in range(SSD_GROUPS):
            bg = xq[:, G + g * N:G + (g + 1) * N]
            cg = xq[:, G + GN + g * N:G + GN + (g + 1) * N]
            bgt = xt[G + g * N:G + (g + 1) * N, :]
            cb = lax.dot_general(bf(cg), bf(bg), (((1,), (1,)), ((), ())), preferred_element_type=jnp.float32)
            for h in range(g * (SSD_HEADS // SSD_GROUPS), (g + 1) * (SSD_HEADS // SSD_GROUPS)):
                ci = cs_col[:, h:h + 1]
                cj = cs_row[h:h + 1, :]
                c_last = cs_row[h:h + 1, Q - 1:Q]
                lm = jnp.exp(jnp.where(lower, ci - cj, NEG_INF))
                x_h = xq[:, h * P:(h + 1) * P]
                xdt = x_h * dt_col[:, h:h + 1]
                st = st_scr[h]
                y_h = jnp.dot(bf(cb * lm), bf(xdt), preferred_element_type=jnp.float32)
                y_h = y_h + jnp.dot(bf(cg * jnp.exp(ci)), bf(st), preferred_element_type=jnp.float32)
                y_h = y_h + dvec_ref[:, h * P:(h + 1) * P] * x_h
                y_scr[rows, h * P:(h + 1) * P] = y_h
                st_scr[h] = st * jnp.exp(c_last) + jnp.dot(bf(bgt), bf(xdt * jnp.exp(c_last - ci)),
                                                            preferred_element_type=jnp.float32)
    y = y_scr[...] * _silu(z_ref[0])
    y_ref[0] = y * lax.rsqrt(jnp.mean(y * y, axis=-1, keepdims=True) + NORM_EPS) * nw_ref[...]

    @pl.when(c == pl.num_programs(1) - 1)
    def _():
        cbo_ref[0] = conv_scr[...]
        so_ref[0] = st_scr[...]


def _ssd_mixer(proj3, conv0, s0, conv_w, conv_b, dt_bias, a_log, d_skip, norm_w, *, n_valid, tc):
    B, Tp, _ = proj3.shape
    G = GROUP_WIDTH
    H, P, N = SSD_HEADS, SSD_HEADDIM, SSD_STATE
    C = SSD_CONV_CH
    f32 = jnp.float32
    nchunk = Tp // tc
    assert Tp % tc == 0 and tc % SSD_CHUNK == 0 and (n_valid == tc or nchunk == 1)
    cb0 = jnp.pad(conv0, ((0, 0), (0, 8 - (CONV_W - 1)), (0, 0)))
    s0t = jnp.swapaxes(s0, -1, -2)
    prow = jnp.zeros((8, LANES), f32).at[0, :H].set(dt_bias).at[1, :H].set(a_log)
    pcol = jnp.zeros((8, LANES), f32).at[:H, 0].set(dt_bias).at[:H, 1].set(a_log)
    dvec = jnp.repeat(d_skip, P)[None, :]
    full2 = lambda shp: pl.BlockSpec(shp, lambda b, c: (0, 0))
    y, cbo, so = pl.pallas_call(
        functools.partial(_ssd_kernel, n_valid=n_valid),
        grid=(B, nchunk),
        in_specs=[pl.BlockSpec((1, tc, G), lambda b, c: (b, c, COL_Z)),
                  pl.BlockSpec((1, tc, C), lambda b, c: (b, c, COL_XBC)),
                  pl.BlockSpec((1, tc, G), lambda b, c: (b, c, COL_DT)),
                  pl.BlockSpec((1, 8, C), lambda b, c: (b, 0, 0)),
                  pl.BlockSpec((1, H, N, P), lambda b, c: (b, 0, 0, 0)),
                  full2((CONV_W, C)), full2((1, C)), full2((8, LANES)), full2((8, LANES)), full2((1, G)),
                  full2((1, G))],
        out_specs=[pl.BlockSpec((1, tc, G), lambda b, c: (b, c, 0)),
                   pl.BlockSpec((1, 8, C), lambda b, c: (b, 0, 0)),
                   pl.BlockSpec((1, H, N, P), lambda b, c: (b, 0, 0, 0))],
        out_shape=[jax.ShapeDtypeStruct((B, Tp, G), f32), jax.ShapeDtypeStruct((B, 8, C), f32),
                   jax.ShapeDtypeStruct((B, H, N, P), f32)],
        scratch_shapes=[pltpu.VMEM((8, C), f32), pltpu.VMEM((H, N, P), f32), pltpu.VMEM((tc, G), f32)],
        compiler_params=pltpu.CompilerParams(dimension_semantics=("parallel", "arbitrary"),
                                             vmem_limit_bytes=_VMEM_LIMIT),
        name="ssd_mixer",
    )(proj3, proj3, proj3, cb0, s0t, conv_w, conv_b[None, :], prow, pcol, dvec, norm_w[None, :])
    return y, cbo[:, :CONV_W - 1, :], jnp.swapaxes(so, -1, -2)


def _rwkv_pad_cols(a):
    G = GROUP_WIDTH
    z = lambda w: jnp.zeros(a.shape[:-1] + (w,), a.dtype)
    o = 3 * G
    wd = a[..., o:o + RWKV_W_RANK]
    ad = a[..., o + RWKV_W_RANK:o + RWKV_W_RANK + RWKV_A_RANK]
    gd = a[..., o + RWKV_W_RANK + RWKV_A_RANK:]
    return jnp.concatenate([a[..., :o], wd, z(LANES - RWKV_W_RANK), ad, z(LANES - RWKV_A_RANK),
                            gd, z(LANES - RWKV_G_RANK)], axis=-1)


def _rwkv_unpad_cols(a):
    o = 3 * GROUP_WIDTH
    return jnp.concatenate([a[..., :o], a[..., o:o + RWKV_W_RANK], a[..., o + LANES:o + LANES + RWKV_A_RANK],
                            a[..., o + 2 * LANES:o + 2 * LANES + RWKV_G_RANK]], axis=-1)


def _rwkv_kernel(pd_ref, sh0_ref, s0_ref, mu_ref, vec_ref, wup_ref, aup_ref, gup_ref, seg_ref,
                 y_ref, sout_ref, shout_ref,
                 prev_scr, s_scr, r_scr, k_scr, d_scr, kk_scr, bb_scr, g_scr, bonus_scr, vt_scr, yt_scr, *, n_steps):
    c = pl.program_id(1)
    tc = pd_ref.shape[1]
    G = GROUP_WIDTH
    hd = RWKV_HEADDIM

    @pl.when(c == 0)
    def _():
        prev_scr[...] = sh0_ref[...]
        s_scr[...] = s0_ref[...]

    nb = pd_ref.shape[0]
    w0 = vec_ref[0:1, :]
    a0 = vec_ref[1:2, :]
    k_k = vec_ref[2:3, :]
    k_a = vec_ref[3:4, :]
    ln_w = vec_ref[4:5, :]
    ln_b = vec_ref[5:6, :]
    r_k = vec_ref[6:7, :]
    seg = seg_ref[...]
    hdot = lambda x, y: jnp.dot(x, y, precision=_HI, preferred_element_type=jnp.float32)
    for ib in range(nb):
        cur = pd_ref[ib]
        row = lax.broadcasted_iota(jnp.int32, cur.shape, 0)
        prev = jnp.where(row == 0, prev_scr[ib], pltpu.roll(cur, 1, axis=0))
        prev_scr[ib] = cur[n_steps - 1:n_steps, :] if n_steps < tc else cur[tc - 1:tc, :]
        m = cur + (prev - cur) * mu_ref[...]
        r = m[:, 0:G]
        k = m[:, G:2 * G]
        v = m[:, 2 * G:3 * G]
        wd = m[:, 3 * G:3 * G + LANES]
        ad = m[:, 3 * G + LANES:3 * G + 2 * LANES]
        gd = m[:, 3 * G + 2 * LANES:3 * G + 3 * LANES]
        w = -_softplus(-(w0 + hdot(jnp.tanh(wd), wup_ref[...]))) - 0.5
        a = jax.nn.sigmoid(a0 + hdot(ad, aup_ref[...]))
        kkr = k * k_k
        kk = kkr / jnp.maximum(jnp.sqrt(hdot(kkr * kkr, seg)), 1e-12)
        k2 = k * (1.0 + (a - 1.0) * k_a)
        g_scr[ib] = hdot(jax.nn.sigmoid(gd), gup_ref[...])
        bonus_scr[ib] = hdot(r * k2 * r_k, seg) * v
        r_scr[ib] = r
        k_scr[ib] = k2
        d_scr[ib] = jnp.exp(-jnp.exp(w))
        kk_scr[ib] = -kk
        bb_scr[ib] = kk * a
        vt_scr[ib] = v.T
    yt_scr[...] = jnp.zeros_like(yt_scr)

    lane = lax.broadcasted_iota(jnp.int32, (hd, LANES), 1)
    lo_half = lane < hd

    def half_sums(p):
        lo = jnp.sum(jnp.where(lo_half, p, 0.0), axis=1, keepdims=True)
        hi = jnp.sum(jnp.where(lo_half, 0.0, p), axis=1, keepdims=True)
        return lo, hi

    npair = RWKV_HEADS // 2

    def step8(i, states):
        t0 = pl.multiple_of(i * 8, 8)
        blk = pl.multiple_of((t0 // LANES) * LANES, LANES)
        rows = pl.ds(t0, 8)
        chains = [(ib, p) for ib in range(nb) for p in range(npair)]
        states = list(states)
        rowv = {}
        tiles = {}
        for ci, (ib, p) in enumerate(chains):
            cols = slice(p * LANES, (p + 1) * LANES)
            rowv[ci] = tuple(ref[ib, rows, cols] for ref in (kk_scr, d_scr, bb_scr, k_scr, r_scr))
            tile = lambda ref, hh: ref[ib, pl.ds(p * LANES + hh * hd, hd), pl.ds(blk, LANES)]
            tiles[ci] = [tile(vt_scr, 0), tile(vt_scr, 1), tile(yt_scr, 0), tile(yt_scr, 1)]

        def vcol_of(ci, j):
            hit = lane == (t0 + j - blk)
            vc0 = jnp.sum(jnp.where(hit, tiles[ci][0], 0.0), axis=1, keepdims=True)
            vc1 = jnp.sum(jnp.where(hit, tiles[ci][1], 0.0), axis=1, keepdims=True)
            return jnp.where(lo_half, vc0, vc1)

        def emit_y(ci, j):
            hit = lane == (t0 + j - blk)
            ylo, yhi = half_sums(states[ci] * rowv[ci][4][j:j + 1, :])
            tiles[ci][2] = jnp.where(hit, ylo, tiles[ci][2])
            tiles[ci][3] = jnp.where(hit, yhi, tiles[ci][3])

        nj = min(8, n_steps)
        vcols = [vcol_of(ci, 0) for ci in range(len(chains))]
        for j in range(nj):
            sas = [half_sums(states[ci] * rowv[ci][0][j:j + 1, :]) for ci in range(len(chains))]
            if j > 0:
                for ci in range(len(chains)):
                    emit_y(ci, j - 1)
            nxt = [vcol_of(ci, j + 1) for ci in range(len(chains))] if j + 1 < nj else None
            for ci in range(len(chains)):
                _, d8, bb8, k8, _ = rowv[ci]
                sa = jnp.where(lo_half, sas[ci][0], sas[ci][1])
                states[ci] = states[ci] * d8[j:j + 1, :] + sa * bb8[j:j + 1, :] + vcols[ci] * k8[j:j + 1, :]
            vcols = nxt
        for ci in range(len(chains)):
            emit_y(ci, nj - 1)
        for ci, (ib, p) in enumerate(chains):
            yt_scr[ib, pl.ds(p * LANES, hd), pl.ds(blk, LANES)] = tiles[ci][2]
            yt_scr[ib, pl.ds(p * LANES + hd, hd), pl.ds(blk, LANES)] = tiles[ci][3]
        return tuple(states)

    init = tuple(s_scr[ib, p] for ib in range(nb) for p in range(npair))
    states = lax.fori_loop(0, -(-n_steps // 8), step8, init)
    for ib in range(nb):
        for p in range(npair):
            s_scr[ib, p] = states[ib * npair + p]

    inv = 1.0 / hd
    for ib in range(nb):
        y = yt_scr[ib].T
        mean = hdot(y, seg) * inv
        yc = y - mean
        var = hdot(yc * yc, seg) * inv
        yn = yc * lax.rsqrt(var + RWKV_LN_EPS) * ln_w + ln_b
        y_ref[ib] = (yn + bonus_scr[ib]) * g_scr[ib]

    @pl.when(c == pl.num_programs(1) - 1)
    def _():
        sout_ref[...] = s_scr[...]
        shout_ref[...] = prev_scr[...]


def _rwkv_mixer(proj3, shift0, s0, mu, w0, w_up, a0, a_up, g_up, k_k, k_a, r_k, ln_w, ln_b, *, n_steps, tc, nb=1):
    B, Tp, _ = proj3.shape
    G = GROUP_WIDTH
    H, hd = RWKV_HEADS, RWKV_HEADDIM
    nchunk = Tp // tc
    assert Tp % tc == 0 and (n_steps == tc or nchunk == 1) and (n_steps % 8 == 0 or n_steps < 8)
    assert B % nb == 0
    f32 = jnp.float32
    sh0 = _rwkv_pad_cols(shift0)[:, None, :]
    s0p = s0.reshape(B, H // 2, 2, hd, hd).transpose(0, 1, 3, 2, 4).reshape(B, H // 2, hd, 2 * hd)
    mu_p = _rwkv_pad_cols(mu[None, :])
    vec = jnp.stack([w0, a0, k_k, k_a, ln_w, ln_b, r_k.reshape(G), jnp.zeros((G,), f32)])
    padk = lambda wgt: jnp.pad(wgt, ((0, LANES - wgt.shape[0]), (0, 0)))
    hid = jnp.arange(G) // hd
    seg = (hid[:, None] == hid[None, :]).astype(f32)
    full2 = lambda shp: pl.BlockSpec(shp, lambda b, c: (0, 0))
    tscr = lambda: pltpu.VMEM((nb, tc, G), f32)
    y, s_out, sh_out = pl.pallas_call(
        functools.partial(_rwkv_kernel, n_steps=n_steps),
        grid=(B // nb, nchunk),
        in_specs=[pl.BlockSpec((nb, tc, RWKV_PCOLS), lambda b, c: (b, c, COL_RWKV)),
                  pl.BlockSpec((nb, 1, RWKV_PCOLS), lambda b, c: (b, 0, 0)),
                  pl.BlockSpec((nb, H // 2, hd, 2 * hd), lambda b, c: (b, 0, 0, 0)),
                  full2((1, RWKV_PCOLS)), full2((8, G)), full2((LANES, G)), full2((LANES, G)), full2((LANES, G)),
                  full2((G, G))],
        out_specs=[pl.BlockSpec((nb, tc, G), lambda b, c: (b, c, 0)),
                   pl.BlockSpec((nb, H // 2, hd, 2 * hd), lambda b, c: (b, 0, 0, 0)),
                   pl.BlockSpec((nb, 1, RWKV_PCOLS), lambda b, c: (b, 0, 0))],
        out_shape=[jax.ShapeDtypeStruct((B, Tp, G), f32),
                   jax.ShapeDtypeStruct((B, H // 2, hd, 2 * hd), f32),
                   jax.ShapeDtypeStruct((B, 1, RWKV_PCOLS), f32)],
        scratch_shapes=[pltpu.VMEM((nb, 1, RWKV_PCOLS), f32), pltpu.VMEM((nb, H // 2, hd, 2 * hd), f32),
                        tscr(), tscr(), tscr(), tscr(), tscr(), tscr(), tscr(),
                        pltpu.VMEM((nb, G, tc), f32), pltpu.VMEM((nb, G, tc), f32)],
        compiler_params=pltpu.CompilerParams(dimension_semantics=("parallel", "arbitrary"),
                                             vmem_limit_bytes=_VMEM_LIMIT),
        name="rwkv_mixer",
    )(proj3, sh0, s0p, mu_p, vec, padk(w_up), padk(a_up), padk(g_up), seg)
    s_new = s_out.reshape(B, H // 2, hd, 2, hd).transpose(0, 1, 3, 2, 4).reshape(B, H, hd, hd)
    return y, _rwkv_unpad_cols(sh_out[:, 0, :]), s_new


def _xattn_kernel(x_ref, g_ref, wq_ref, wo_ref, mk_ref, mv_ref, o_ref, cat_scr):
    x = x_ref[...]
    hn = x * lax.rsqrt(jnp.mean(x * x, axis=-1, keepdims=True) + NORM_EPS) * g_ref[...]
    q = jnp.dot(hn.astype(jnp.bfloat16), wq_ref[...], preferred_element_type=jnp.float32)
    qb = q.astype(jnp.bfloat16)
    mk = mk_ref[0].astype(jnp.bfloat16)
    mv = mv_ref[0].astype(jnp.bfloat16)
    hd = X_HEADDIM
    scale = hd ** -0.5
    for h in range(X_HEADS):
        cols = slice(h * hd, (h + 1) * hd)
        s = lax.dot_general(qb[:, cols], mk[:, cols], (((1,), (1,)), ((), ())),
                            preferred_element_type=jnp.float32) * scale
        m = jnp.max(s, axis=-1, keepdims=True)
        p = jnp.exp(s - m)
        p = p / jnp.sum(p, axis=-1, keepdims=True)
        cat_scr[:, cols] = jnp.dot(p.astype(jnp.bfloat16), mv[:, cols], preferred_element_type=jnp.float32)
    o_ref[...] = x + jnp.dot(cat_scr[...].astype(jnp.bfloat16), wo_ref[...], preferred_element_type=jnp.float32)


def _xattn_block(x, gain, wq, wo, mk, mv, *, tb, rows_per_mem):
    n = x.shape[0]
    M = mk.shape[1]
    assert n % tb == 0 and rows_per_mem % tb == 0
    per = rows_per_mem // tb
    return pl.pallas_call(
        _xattn_kernel,
        grid=(n // tb,),
        in_specs=[pl.BlockSpec((tb, D_MODEL), lambda i: (i, 0)),
                  pl.BlockSpec((1, D_MODEL), lambda i: (0, 0)),
                  pl.BlockSpec((D_MODEL, D_MODEL), lambda i: (0, 0)),
                  pl.BlockSpec((D_MODEL, D_MODEL), lambda i: (0, 0)),
                  pl.BlockSpec((1, M, D_MODEL), lambda i: (i // per, 0, 0)),
                  pl.BlockSpec((1, M, D_MODEL), lambda i: (i // per, 0, 0))],
        out_specs=pl.BlockSpec((tb, D_MODEL), lambda i: (i, 0)),
        out_shape=jax.ShapeDtypeStruct((n, D_MODEL), jnp.float32),
        scratch_shapes=[pltpu.VMEM((tb, D_MODEL), jnp.float32)],
        compiler_params=pltpu.CompilerParams(dimension_semantics=("parallel",), vmem_limit_bytes=_VMEM_LIMIT),
        name="xattn_block",
    )(x, gain.reshape(1, -1), wq.astype(jnp.bfloat16), wo.astype(jnp.bfloat16), mk, mv)


def _top_values(s, count, with_rank=False):
    vals = []
    cur = s
    rank = jnp.full(s.shape, float(count), jnp.float32) if with_rank else None
    for r in range(count):
        m = jnp.max(cur, axis=0, keepdims=True)
        vals.append(m)
        hit = cur == m
        if with_rank:
            rank = jnp.where(hit, float(r), rank)
        cur = jnp.where(hit, NEG_INF, cur)
    return (vals, rank) if with_rank else vals


def _peer_route_kernel(x_ref, g_ref, wqt_ref, sk_ref, hn_ref, cnt_ref, e1_ref, r2_ref, e2_ref, a1_scr, a2_scr):
    tb = x_ref.shape[0]
    x = x_ref[...]
    hn = x * lax.rsqrt(jnp.mean(x * x, axis=-1, keepdims=True) + NORM_EPS) * g_ref[...]
    hb = hn.astype(jnp.bfloat16)
    hn_ref[...] = hb
    qt = lax.dot_general(wqt_ref[...], hb, (((1,), (1,)), ((), ())), preferred_element_type=jnp.float32)
    n_top = PEER_TOPK
    row8 = lax.broadcasted_iota(jnp.int32, (8, tb), 0)
    for h in range(PEER_HEADS):
        q1 = qt[(2 * h) * PEER_HALF:(2 * h + 1) * PEER_HALF, :].astype(jnp.bfloat16)
        q2 = qt[(2 * h + 1) * PEER_HALF:(2 * h + 2) * PEER_HALF, :].astype(jnp.bfloat16)
        s1 = jnp.dot(sk_ref[2 * h], q1, preferred_element_type=jnp.float32)
        s2 = jnp.dot(sk_ref[2 * h + 1], q2, preferred_element_type=jnp.float32)
        v1 = _top_values(s1, n_top)
        v2, rank2 = _top_values(s2, n_top, with_rank=True)
        for r in range(n_top):
            a1_scr[r:r + 1, :] = v1[r]
            a2_scr[r:r + 1, :] = v2[r]
        a1 = a1_scr[...]
        a2 = a2_scr[...]
        pieces = [a1 + a2[0:1, :]]
        for q in range(1, 8):
            lim = n_top // (q + 1)
            pieces.append(jnp.where(row8 < lim, a1[0:8, :] + a2[q:q + 1, :], NEG_INF))
        pieces.append(a1[0:1, :] + a2[8:16, :])
        cand = jnp.concatenate(pieces, axis=0)
        c = _top_values(cand, n_top)
        tau = c[PEER_TOPK - 1]
        mx = c[0]
        z = jnp.sum(jnp.where(cand >= tau, jnp.exp(cand - mx), 0.0), axis=0, keepdims=True)
        cnt = jnp.zeros(s1.shape, jnp.float32)
        for q in range(n_top):
            cnt = cnt + jnp.where(s1 + v2[q] >= tau, 1.0, 0.0)
        cnt_ref[h] = cnt
        e1_ref[h] = jnp.exp(s1 - v1[0]) / z
        r2_ref[h] = rank2.astype(jnp.bfloat16)
        e2_ref[h] = jnp.exp(s2 - v2[0]).astype(jnp.bfloat16)


def _peer_route(x, gain, wqt_b, sk_b, tb=256):
    n = x.shape[0]
    assert n % tb == 0
    hk = PEER_HEADS
    blk3 = pl.BlockSpec((hk, PEER_NKEYS, tb), lambda i: (0, 0, i))
    shp_f = jax.ShapeDtypeStruct((hk, PEER_NKEYS, n), jnp.float32)
    shp_b = jax.ShapeDtypeStruct((hk, PEER_NKEYS, n), jnp.bfloat16)
    return pl.pallas_call(
        _peer_route_kernel,
        grid=(n // tb,),
        in_specs=[pl.BlockSpec((tb, D_MODEL), lambda i: (i, 0)),
                  pl.BlockSpec((1, D_MODEL), lambda i: (0, 0)),
                  pl.BlockSpec((2 * hk * PEER_HALF, D_MODEL), lambda i: (0, 0)),
                  pl.BlockSpec((2 * hk, PEER_NKEYS, PEER_HALF), lambda i: (0, 0, 0))],
        out_specs=[pl.BlockSpec((tb, D_MODEL), lambda i: (i, 0)), blk3, blk3, blk3, blk3],
        out_shape=[jax.ShapeDtypeStruct((n, D_MODEL), jnp.bfloat16), shp_f, shp_f, shp_b, shp_b],
        scratch_shapes=[pltpu.VMEM((PEER_TOPK, tb), jnp.float32), pltpu.VMEM((PEER_TOPK, tb), jnp.float32)],
        compiler_params=pltpu.CompilerParams(dimension_semantics=("parallel",), vmem_limit_bytes=_VMEM_LIMIT),
        name="peer_route",
    )(x, gain, wqt_b, sk_b)


def _peer_dense_kernel(x_ref, hn_ref, u_ref, v_ref, cnt_ref, e1_ref, r2_ref, e2_ref, fg_ref, o_ref, acc_ref, w_ref,
                       *, ic, final_norm):
    e = pl.program_id(1)
    tb = x_ref.shape[0]

    @pl.when(e == 0)
    def _():
        acc_ref[...] = jnp.zeros_like(acc_ref)

    act = lax.dot_general(u_ref[0], hn_ref[...], (((1,), (1,)), ((), ())), preferred_element_type=jnp.float32)
    for il in range(ic):
        g = None
        for h in range(PEER_HEADS):
            cb = jnp.broadcast_to(cnt_ref[h, il:il + 1, :], (16, tb)).astype(jnp.bfloat16)[None]
            eb = jnp.broadcast_to(e1_ref[h, il:il + 1, :], (16, tb)).astype(jnp.bfloat16)[None]
            r2 = r2_ref[h].reshape(PEER_NKEYS // 16, 16, tb)
            e2 = e2_ref[h].reshape(PEER_NKEYS // 16, 16, tb)
            t = jnp.where(r2 < cb, e2 * eb, jnp.zeros_like(e2))
            g = t if g is None else g + t
        rows = slice(il * PEER_NKEYS, (il + 1) * PEER_NKEYS)
        a = act[rows, :].astype(jnp.bfloat16).reshape(PEER_NKEYS // 16, 16, tb)
        w_ref[rows, :] = (g * _gelu_tanh(a)).reshape(PEER_NKEYS, tb)
    acc_ref[...] += lax.dot_general(w_ref[...], v_ref[0], (((0,), (0,)), ((), ())),
                                    preferred_element_type=jnp.float32)

    @pl.when(e == pl.num_programs(1) - 1)
    def _():
        y = x_ref[...] + acc_ref[...]
        if final_norm:
            y = y * lax.rsqrt(jnp.mean(y * y, axis=-1, keepdims=True) + NORM_EPS) * fg_ref[...]
        o_ref[...] = y


def _peer_dense(x, hn_b, u_b, v_b, layer, cnt, e1, r2, e2, final_gain, tb, ic=16):
    n = x.shape[0]
    ne = u_b.shape[1]
    ec = ic * PEER_NKEYS
    assert n % tb == 0 and ne % ec == 0
    hk = PEER_HEADS
    row_blk = pl.BlockSpec((hk, ic, tb), lambda i, e: (0, e, i))
    full_blk = pl.BlockSpec((hk, PEER_NKEYS, tb), lambda i, e: (0, 0, i))
    fg = jnp.ones((1, D_MODEL), jnp.float32) if final_gain is None else final_gain.reshape(1, D_MODEL)
    return pl.pallas_call(
        functools.partial(_peer_dense_kernel, ic=ic, final_norm=final_gain is not None),
        grid=(n // tb, ne // ec),
        in_specs=[pl.BlockSpec((tb, D_MODEL), lambda i, e: (i, 0)),
                  pl.BlockSpec((tb, D_MODEL), lambda i, e: (i, 0)),
                  pl.BlockSpec((1, ec, D_MODEL), lambda i, e: (layer, e, 0)),
                  pl.BlockSpec((1, ec, D_MODEL), lambda i, e: (layer, e, 0)),
                  row_blk, row_blk, full_blk, full_blk,
                  pl.BlockSpec((1, D_MODEL), lambda i, e: (0, 0))],
        out_specs=pl.BlockSpec((tb, D_MODEL), lambda i, e: (i, 0)),
        out_shape=jax.ShapeDtypeStruct((n, D_MODEL), jnp.float32),
        scratch_shapes=[pltpu.VMEM((tb, D_MODEL), jnp.float32), pltpu.VMEM((ec, tb), jnp.bfloat16)],
        compiler_params=pltpu.CompilerParams(dimension_semantics=("parallel", "arbitrary"),
                                             vmem_limit_bytes=_VMEM_LIMIT),
        name="peer_dense",
    )(x, hn_b, u_b, v_b, cnt, e1, r2, e2, fg)


def _peer_weights(wq, subkeys, u_all_b, v_all_b, layer):
    return (wq.T.astype(jnp.bfloat16),
            subkeys.reshape(2 * PEER_HEADS, PEER_NKEYS, PEER_HALF).astype(jnp.bfloat16),
            u_all_b, v_all_b, layer)


def _peer_block(x, gain, weights, tb_route, tb_dense, final_gain=None):
    wqt_b, sk_b, u_b, v_b, layer = weights
    hn_b, cnt, e1, r2, e2 = _peer_route(x, gain.reshape(1, -1), wqt_b, sk_b, tb=tb_route)
    return _peer_dense(x, hn_b, u_b, v_b, layer, cnt, e1, r2, e2, final_gain, tb=tb_dense)


def _in_weight(w_in):
    abc = w_in[:, :D_OFF]
    pad = jnp.zeros((w_in.shape[0], ABC_PCOLS - D_OFF), w_in.dtype)
    return jnp.concatenate([abc, pad, _rwkv_pad_cols(w_in[:, D_OFF:])], axis=1)


def _layer_pre_peer(x2, bsz, T, lp, st, prompt):
    G = GROUP_WIDTH
    n = bsz * T
    proj = _mm(x2, lp['w_in_p'], gain=lp['norm_mix'])
    if prompt:
        n_valid, tc = SEQ_CHUNK, SEQ_CHUNK
        proj3 = proj.reshape(bsz, T, -1)
        ya, k_t, v_t = _moba_prompt(proj3, jnp.arange(T, dtype=jnp.int32))
        heads = lambda t: t.reshape(bsz, A_HEADS, HEAD_DIM, T).transpose(0, 3, 1, 2)
        k_new, v_new = heads(k_t), heads(v_t)
    else:
        assert T == 1
        n_valid, tc = T, DEC_ROWS
        ya, k_r = _moba_decode(proj[:, 0:G], proj[:, G:2 * G], proj[:, 2 * G:3 * G],
                               st['k_cache'], st['v_cache'], st['page_table'], PAST_LEN)
        heads = lambda t: t.reshape(bsz, T, A_HEADS, HEAD_DIM)
        k_new, v_new = heads(k_r), heads(proj[:, 2 * G:3 * G])
        proj3 = jnp.pad(proj[:, None, :], ((0, 0), (0, tc - T), (0, 0)))
    yb, lru_conv, lru_h = _lru_mixer(proj3, st['lru_conv'], st['lru_h'], lp['lru_conv_w'], lp['lru_conv_b'],
                                     lp['lru_wa'], lp['lru_ba'], lp['lru_wx'], lp['lru_bx'], lp['lru_lambda'],
                                     n_valid=n_valid, tc=tc)
    yc, ssd_conv, ssd_s = _ssd_mixer(proj3, st['ssd_conv'], st['ssd'], lp['ssd_conv_w'], lp['ssd_conv_b'],
                                     lp['ssd_dt_bias'], lp['ssd_a_log'], lp['ssd_d'], lp['ssd_norm'],
                                     n_valid=n_valid, tc=tc)
    yd, rwkv_shift, rwkv_s = _rwkv_mixer(proj3, st['rwkv_shift'], st['rwkv'],
                                         lp['rwkv_mu'], lp['rwkv_w0'], lp['rwkv_w_up'], lp['rwkv_a0'],
                                         lp['rwkv_a_up'], lp['rwkv_g_up'], lp['rwkv_k_k'], lp['rwkv_k_a'],
                                         lp['rwkv_r_k'], lp['rwkv_ln_w'], lp['rwkv_ln_b'], n_steps=n_valid, tc=tc)
    rows = lambda y: y[:, :T].reshape(n, G)
    x2 = _mm([ya.reshape(n, G), rows(yb), rows(yc), rows(yd)], lp['w_out'], residual=x2)
    if prompt:
        x2 = _xattn_block(x2, lp['norm_x'], lp['x_wq'], lp['x_wo'], st['mem_k'], st['mem_v'],
                          tb=SEQ_CHUNK, rows_per_mem=T)
    else:
        x8 = jnp.pad(x2[:, None, :], ((0, 0), (0, 7), (0, 0))).reshape(n * 8, D_MODEL)
        x8 = _xattn_block(x8, lp['norm_x'], lp['x_wq'], lp['x_wo'], st['mem_k'], st['mem_v'], tb=8, rows_per_mem=8)
        x2 = x8.reshape(n, 8, D_MODEL)[:, 0, :]
    new = {'k': k_new, 'v': v_new, 'lru_h': lru_h, 'lru_conv': lru_conv, 'ssd': ssd_s, 'ssd_conv': ssd_conv,
           'rwkv': rwkv_s, 'rwkv_shift': rwkv_shift}
    return x2, new


def kernel(x_prompt, x_sample, mem_prompt, cache_moba_k, cache_moba_v, page_table, state_lru_h, state_lru_conv, state_ssd, state_ssd_conv, state_rwkv, state_rwkv_shift, cache_mem_k, cache_mem_v, norm_mix, w_in, w_out, lru_conv_w, lru_conv_b, lru_wa, lru_ba, lru_wx, lru_bx, lru_lambda, ssd_conv_w, ssd_conv_b, ssd_dt_bias, ssd_a_log, ssd_d, ssd_norm, rwkv_mu, rwkv_w0, rwkv_w_up, rwkv_a0, rwkv_a_up, rwkv_g_up, rwkv_k_k, rwkv_k_a, rwkv_r_k, rwkv_ln_w, rwkv_ln_b, norm_x, x_wq, x_wk, x_wv, x_wo, norm_ffn, peer_wq, peer_subkeys, peer_u, peer_v, final_norm):
    bp, tp, _ = x_prompt.shape
    bd, td, _ = x_sample.shape
    n_p, n_s = bp * tp, bd * td
    n_s_pad = -(-n_s // LANES) * LANES
    assert n_p % PEER_TOK == 0
    f32 = jnp.float32
    xp2 = x_prompt.reshape(n_p, D_MODEL)
    xs2 = x_sample.reshape(n_s, D_MODEL)
    names = ('k', 'v', 'lru_h', 'lru_conv', 'ssd', 'ssd_conv', 'rwkv', 'rwkv_shift')
    p_new = {n: [] for n in names + ('mem_k', 'mem_v')}
    s_new = {n: [] for n in names}
    mem2 = mem_prompt.reshape(bp * MEM_LEN, D_MODEL)
    pool = lambda c: jnp.transpose(c, (0, 1, 3, 4, 2)).reshape((-1, A_HEADS, HEAD_DIM, PAGE_SIZE))
    cache_kt, cache_vt = pool(cache_moba_k), pool(cache_moba_v)
    peer_u_b, peer_v_b = peer_u.astype(jnp.bfloat16), peer_v.astype(jnp.bfloat16)
    for l in range(DEPTH):
        lp = {
            'norm_mix': norm_mix[l], 'w_in_p': _in_weight(w_in[l]), 'w_out': w_out[l],
            'lru_conv_w': lru_conv_w[l], 'lru_conv_b': lru_conv_b[l], 'lru_wa': lru_wa[l], 'lru_ba': lru_ba[l],
            'lru_wx': lru_wx[l], 'lru_bx': lru_bx[l], 'lru_lambda': lru_lambda[l],
            'ssd_conv_w': ssd_conv_w[l], 'ssd_conv_b': ssd_conv_b[l], 'ssd_dt_bias': ssd_dt_bias[l],
            'ssd_a_log': ssd_a_log[l], 'ssd_d': ssd_d[l], 'ssd_norm': ssd_norm[l],
            'rwkv_mu': rwkv_mu[l], 'rwkv_w0': rwkv_w0[l], 'rwkv_w_up': rwkv_w_up[l], 'rwkv_a0': rwkv_a0[l],
            'rwkv_a_up': rwkv_a_up[l], 'rwkv_g_up': rwkv_g_up[l], 'rwkv_k_k': rwkv_k_k[l], 'rwkv_k_a': rwkv_k_a[l],
            'rwkv_r_k': rwkv_r_k[l], 'rwkv_ln_w': rwkv_ln_w[l], 'rwkv_ln_b': rwkv_ln_b[l],
            'norm_x': norm_x[l], 'x_wq': x_wq[l], 'x_wo': x_wo[l],
        }
        mk = _mm(mem2, x_wk[l]).reshape(bp, MEM_LEN, D_MODEL)
        mv = _mm(mem2, x_wv[l]).reshape(bp, MEM_LEN, D_MODEL)
        st_p = {
            'lru_conv': jnp.zeros((bp, CONV_W - 1, LRU_WIDTH), f32),
            'lru_h': jnp.zeros((bp, LRU_WIDTH), f32),
            'ssd_conv': jnp.zeros((bp, CONV_W - 1, SSD_CONV_CH), f32),
            'ssd': jnp.zeros((bp, SSD_HEADS, SSD_HEADDIM, SSD_STATE), f32),
            'rwkv_shift': jnp.zeros((bp, RWKV_COLS), f32),
            'rwkv': jnp.zeros((bp, RWKV_HEADS, RWKV_HEADDIM, RWKV_HEADDIM), f32),
            'mem_k': mk, 'mem_v': mv,
        }
        xp2, npl = _layer_pre_peer(xp2, bp, tp, lp, st_p, True)
        for n in names:
            p_new[n].append(npl[n])
        p_new['mem_k'].append(mk.reshape(bp, MEM_LEN, X_HEADS, X_HEADDIM))
        p_new['mem_v'].append(mv.reshape(bp, MEM_LEN, X_HEADS, X_HEADDIM))
        st_s = {
            'k_cache': cache_kt, 'v_cache': cache_vt, 'page_table': page_table + l * cache_moba_k.shape[1],
            'lru_conv': state_lru_conv[l], 'lru_h': state_lru_h[l],
            'ssd_conv': state_ssd_conv[l], 'ssd': state_ssd[l],
            'rwkv_shift': state_rwkv_shift[l], 'rwkv': state_rwkv[l],
            'mem_k': cache_mem_k[l].reshape(bd, MEM_LEN, D_MODEL), 'mem_v': cache_mem_v[l].reshape(bd, MEM_LEN, D_MODEL),
        }
        xs2, nsl = _layer_pre_peer(xs2, bd, td, lp, st_s, False)
        for n in names:
            s_new[n].append(nsl[n])
        fg = final_norm if l == DEPTH - 1 else None
        pw = _peer_weights(peer_wq[l], peer_subkeys[l], peer_u_b, peer_v_b, l)
        xp2 = _peer_block(xp2, norm_ffn[l], pw, PEER_ROUTE_TOK, PEER_TOK, fg)
        xs_pad = jnp.pad(xs2, ((0, n_s_pad - n_s), (0, 0)))
        xs2 = _peer_block(xs_pad, norm_ffn[l], pw, LANES, LANES, fg)[:n_s]
    y_prompt = xp2.reshape(bp, tp, D_MODEL)
    y_sample = xs2.reshape(bd, td, D_MODEL)
    return (y_prompt, y_sample,
            jnp.stack(p_new['k']), jnp.stack(p_new['v']), jnp.stack(p_new['lru_h']), jnp.stack(p_new['lru_conv']),
            jnp.stack(p_new['ssd']), jnp.stack(p_new['ssd_conv']), jnp.stack(p_new['rwkv']), jnp.stack(p_new['rwkv_shift']),
            jnp.stack(p_new['mem_k']), jnp.stack(p_new['mem_v']),
            jnp.stack(s_new['k']), jnp.stack(s_new['v']), jnp.stack(s_new['lru_h']), jnp.stack(s_new['lru_conv']),
            jnp.stack(s_new['ssd']), jnp.stack(s_new['ssd_conv']), jnp.stack(s_new['rwkv']), jnp.stack(s_new['rwkv_shift']))
```

```python
import functools

import jax
import jax.numpy as jnp
from jax import lax
from jax.experimental import pallas as pl
from jax.experimental.pallas import tpu as pltpu

D_MODEL = 1024
DEPTH = 2
PAST_LEN = 16384
PAGE_SIZE = 128
GROUP_WIDTH = 256
HEAD_DIM = 64
A_HEADS = GROUP_WIDTH // HEAD_DIM
MOBA_BLOCK = 256
MOBA_TOPK = 3
ROPE_THETA = 10000.0
LRU_WIDTH = GROUP_WIDTH
LRU_C = 8.0
CONV_W = 4
SSD_HEADS = 4
SSD_HEADDIM = GROUP_WIDTH // SSD_HEADS
SSD_GROUPS = 2
SSD_STATE = 64
SSD_CHUNK = 128
SSD_CONV_CH = GROUP_WIDTH + 2 * SSD_GROUPS * SSD_STATE
RWKV_HEADS = 4
RWKV_HEADDIM = GROUP_WIDTH // RWKV_HEADS
RWKV_W_RANK = 32
RWKV_A_RANK = 32
RWKV_G_RANK = 64
RWKV_COLS = 3 * GROUP_WIDTH + RWKV_W_RANK + RWKV_A_RANK + RWKV_G_RANK
RWKV_LN_EPS = 64e-5
D_OFF = 3 * GROUP_WIDTH + 2 * LRU_WIDTH + GROUP_WIDTH + SSD_CONV_CH + SSD_HEADS
MEM_LEN = 256
X_HEADS = 4
X_HEADDIM = D_MODEL // X_HEADS
PEER_HEADS = 8
PEER_NKEYS = 128
PEER_HALF = 128
PEER_TOPK = 16
NORM_EPS = 1e-6
NEG_INF = -1e30

LANES = 128
RWKV_PCOLS = 3 * GROUP_WIDTH + 3 * LANES
ABC_PCOLS = 2 * RWKV_PCOLS
COL_Q, COL_K, COL_V, COL_U, COL_GATE, COL_Z = 0, 1, 2, 3, 4, 5
COL_XBC = 3
COL_DT = 8
COL_RWKV = 2
PAGES_PER_BLOCK = MOBA_BLOCK // PAGE_SIZE
KMEAN_PAGES = 32
PEER_TOK = 512
PEER_ROUTE_TOK = 256
SEQ_CHUNK = 512
DEC_ROWS = 128
RWKV_SEQS_PER_STEP = 2
_VMEM_LIMIT = 56 * 1024 * 1024
_HI = lax.Precision.HIGHEST


def _softplus(x):
    return jnp.maximum(x, 0.0) + jnp.log1p(jnp.exp(-jnp.abs(x)))


def _silu(x):
    return x * jax.nn.sigmoid(x)


def _gelu_tanh(x):
    return 0.5 * x * (1.0 + jnp.tanh(0.7978845608028654 * (x + 0.044715 * (x * x * x))))


def _mm_kernel(*refs, n_x, has_norm, has_res):
    x_refs = refs[:n_x]
    w_ref = refs[n_x]
    pos = n_x + 1
    g_ref = r_ref = None
    if has_norm:
        g_ref = refs[pos]
        pos += 1
    if has_res:
        r_ref = refs[pos]
        pos += 1
    o_ref = refs[pos]
    y = None
    off = 0
    for x_ref in x_refs:
        x = x_ref[...]
        kx = x.shape[1]
        if has_norm:
            x = x * lax.rsqrt(jnp.mean(x * x, axis=-1, keepdims=True) + NORM_EPS) * g_ref[...]
        t = jnp.dot(x.astype(jnp.bfloat16), w_ref[off:off + kx, :], preferred_element_type=jnp.float32)
        y = t if y is None else y + t
        off += kx
    if has_res:
        y = y + r_ref[...]
    o_ref[...] = y


def _mm(xs, w, gain=None, residual=None, tm=512):
    if not isinstance(xs, (list, tuple)):
        xs = [xs]
    assert gain is None or len(xs) == 1
    M = xs[0].shape[0]
    K, N = w.shape
    tm = min(tm, M)
    assert M % tm == 0 and sum(x.shape[1] for x in xs) == K
    args = list(xs) + [w.astype(jnp.bfloat16)]
    in_specs = [pl.BlockSpec((tm, x.shape[1]), lambda i: (i, 0)) for x in xs]
    in_specs.append(pl.BlockSpec((K, N), lambda i: (0, 0)))
    if gain is not None:
        args.append(gain.reshape(1, K).astype(jnp.float32))
        in_specs.append(pl.BlockSpec((1, K), lambda i: (0, 0)))
    if residual is not None:
        args.append(residual)
        in_specs.append(pl.BlockSpec((tm, N), lambda i: (i, 0)))
    return pl.pallas_call(
        functools.partial(_mm_kernel, n_x=len(xs), has_norm=gain is not None, has_res=residual is not None),
        grid=(M // tm,),
        in_specs=in_specs,
        out_specs=pl.BlockSpec((tm, N), lambda i: (i, 0)),
        out_shape=jax.ShapeDtypeStruct((M, N), jnp.float32),
        compiler_params=pltpu.CompilerParams(dimension_semantics=("parallel",), vmem_limit_bytes=_VMEM_LIMIT),
        name="mm",
    )(*args)


def _rmsnorm(x, g):
    return x * lax.rsqrt(jnp.mean(x * x, axis=-1, keepdims=True) + NORM_EPS) * g


def _rope_tables(pos):
    half = HEAD_DIM // 2
    freq = 1.0 / (ROPE_THETA ** (jnp.arange(half, dtype=jnp.float32) / half))
    ang = pos.astype(jnp.float32)[:, None] * freq[None, :]
    cos = jnp.cos(ang)
    sin = jnp.sin(ang)
    cos_t = jnp.tile(jnp.concatenate([cos, cos], axis=-1), (1, A_HEADS))
    sin_t = jnp.tile(jnp.concatenate([-sin, sin], axis=-1), (1, A_HEADS))
    return cos_t, sin_t


def _rope_apply(x, cos_t, sin_t):
    half = HEAD_DIM // 2
    lane = lax.broadcasted_iota(jnp.int32, x.shape, 1)
    first = (lane % HEAD_DIM) < half
    w = x.shape[1]
    swapped = jnp.where(first, pltpu.roll(x, w - half, axis=1), pltpu.roll(x, half, axis=1))
    return x * cos_t + swapped * sin_t


def _moba_prompt_kernel(q_ref, k_ref, v_ref, cos_ref, sin_ref, ya_ref, kr_ref, vo_ref, ot_scr):
    T = q_ref.shape[1]
    G = GROUP_WIDTH
    hd = HEAD_DIM
    blk = MOBA_BLOCK
    nb = T // blk
    cos_t = cos_ref[...]
    sin_t = sin_ref[...]
    q = _rope_apply(q_ref[0], cos_t, sin_t)
    k = _rope_apply(k_ref[0], cos_t, sin_t)
    v = v_ref[0]
    v_t = v.T
    kr_ref[0] = k.T
    vo_ref[0] = v_t
    kmean = jnp.concatenate([jnp.sum(k[n * blk:(n + 1) * blk, :], axis=0, keepdims=True) for n in range(nb)],
                            axis=0) * (1.0 / blk)
    qt = q.T.astype(jnp.bfloat16)
    vt = v_t.astype(jnp.bfloat16)
    kb = k.astype(jnp.bfloat16)
    lane_g = lax.broadcasted_iota(jnp.int32, (nb, G), 1)
    rown = lax.broadcasted_iota(jnp.int32, (nb, T), 0)
    cur = lax.broadcasted_iota(jnp.int32, (nb, T), 1) // blk
    kpos = lax.broadcasted_iota(jnp.int32, (blk, blk), 0)
    qpos = lax.broadcasted_iota(jnp.int32, (blk, blk), 1)
    causal = kpos <= qpos
    scale = hd ** -0.5
    for h in range(A_HEADS):
        km_h = jnp.where(lane_g // hd == h, kmean, 0.0)
        gate = lax.dot_general(km_h, q, (((1,), (1,)), ((), ())), precision=_HI,
                               preferred_element_type=jnp.float32)
        gate = jnp.where(rown < cur, gate, NEG_INF)
        sel_rows = []
        for n in range(nb):
            gn = gate[n:n + 1, :]
            beats = (gate > gn) | ((gate == gn) & (rown < n))
            rank = jnp.sum(beats.astype(jnp.float32), axis=0, keepdims=True)
            sel_rows.append(rank < MOBA_TOPK)
        k_h = kb[:, h * hd:(h + 1) * hd]
        qt_h = qt[h * hd:(h + 1) * hd, :]
        vt_h = vt[h * hd:(h + 1) * hd, :]
        for qi in range(nb):
            qs = slice(qi * blk, (qi + 1) * blk)
            q_blk = qt_h[:, qs]
            nk = (qi + 1) * blk
            s = jnp.dot(k_h[0:nk, :], q_blk, preferred_element_type=jnp.float32) * scale
            parts = [jnp.where(sel_rows[n][:, qs], s[n * blk:(n + 1) * blk, :], NEG_INF) for n in range(qi)]
            parts.append(jnp.where(causal, s[qi * blk:nk, :], NEG_INF))
            s = jnp.concatenate(parts, axis=0) if qi else parts[0]
            m = jnp.max(s, axis=0, keepdims=True)
            p = jnp.exp(s - m)
            l = jnp.sum(p, axis=0, keepdims=True)
            acc = jnp.dot(vt_h[:, 0:nk], p.astype(jnp.bfloat16), preferred_element_type=jnp.float32)
            ot_scr[h * hd:(h + 1) * hd, qs] = acc / l
    ya_ref[0] = ot_scr[...].T


def _moba_prompt(proj3, pos):
    B, T, _ = proj3.shape
    G = GROUP_WIDTH
    cos_t, sin_t = _rope_tables(pos)
    col = lambda j: pl.BlockSpec((1, T, G), lambda b: (b, 0, j))
    tab = pl.BlockSpec((T, G), lambda b: (0, 0))
    out = pl.BlockSpec((1, T, G), lambda b: (b, 0, 0))
    out_t = pl.BlockSpec((1, G, T), lambda b: (b, 0, 0))
    shp = jax.ShapeDtypeStruct((B, T, G), jnp.float32)
    shp_t = jax.ShapeDtypeStruct((B, G, T), jnp.float32)
    return pl.pallas_call(
        _moba_prompt_kernel,
        grid=(B,),
        in_specs=[col(COL_Q), col(COL_K), col(COL_V), tab, tab],
        out_specs=[out, out_t, out_t],
        out_shape=[shp, shp_t, shp_t],
        scratch_shapes=[pltpu.VMEM((G, T), jnp.float32)],
        compiler_params=pltpu.CompilerParams(dimension_semantics=("parallel",), vmem_limit_bytes=_VMEM_LIMIT),
        name="moba_prompt",
    )(proj3, proj3, proj3, cos_t, sin_t)


def _kmean_kernel(pt_ref, *refs):
    pages, o_ref = refs[:-1], refs[-1]
    j = pl.program_id(1)
    nblk = len(pages) // PAGES_PER_BLOCK

    @pl.when(j == 0)
    def _():
        o_ref[...] = jnp.zeros_like(o_ref)

    lane = lax.broadcasted_iota(jnp.int32, o_ref.shape[1:], 2)
    acc = o_ref[0]
    for blk in range(nblk):
        s = None
        for i in range(PAGES_PER_BLOCK):
            part = pages[blk * PAGES_PER_BLOCK + i][0]
            s = part if s is None else s + part
        mean = jnp.sum(s, axis=-1, keepdims=True) * (1.0 / MOBA_BLOCK)
        acc = jnp.where(lane == j * nblk + blk, mean, acc)
    o_ref[0] = acc


def _moba_block_means(cache_kt, page_table):
    B, n_pages = page_table.shape
    H, hd = A_HEADS, HEAD_DIM
    assert n_pages % KMEAN_PAGES == 0 and n_pages // PAGES_PER_BLOCK <= LANES
    steps = n_pages // KMEAN_PAGES

    def page_spec(i):
        return pl.BlockSpec((1, H, hd, PAGE_SIZE), lambda b, j, pt: (pt[b * n_pages + j * KMEAN_PAGES + i], 0, 0, 0))

    return pl.pallas_call(
        _kmean_kernel,
        grid_spec=pltpu.PrefetchScalarGridSpec(
            num_scalar_prefetch=1,
            grid=(B, steps),
            in_specs=[page_spec(i) for i in range(KMEAN_PAGES)],
            out_specs=pl.BlockSpec((1, H, hd, LANES), lambda b, j, pt: (b, 0, 0, 0)),
        ),
        out_shape=jax.ShapeDtypeStruct((B, H, hd, LANES), jnp.float32),
        compiler_params=pltpu.CompilerParams(dimension_semantics=("parallel", "arbitrary"),
                                             vmem_limit_bytes=_VMEM_LIMIT),
        name="moba_block_means",
    )(page_table.reshape(-1), *([cache_kt] * KMEAN_PAGES))


def _moba_select_kernel(q_ref, k_ref, cos_ref, sin_ref, km_ref, qr_ref, kr_ref, top_ref, *, nb):
    q = _rope_apply(q_ref[0], cos_ref[...], sin_ref[...])
    k = _rope_apply(k_ref[0], cos_ref[...], sin_ref[...])
    qr_ref[0] = q
    kr_ref[0] = k
    qcol = q.T[:, 0:1]
    lane = lax.broadcasted_iota(jnp.int32, (1, LANES), 1)
    row8 = lax.broadcasted_iota(jnp.int32, (8, LANES), 0)
    lane8 = lax.broadcasted_iota(jnp.int32, (8, LANES), 1)
    top = jnp.zeros((8, LANES), jnp.int32)
    for h in range(A_HEADS):
        gate = jnp.sum(km_ref[0, h] * qcol[h * HEAD_DIM:(h + 1) * HEAD_DIM, :], axis=0, keepdims=True)
        gate = jnp.where(lane < nb, gate, NEG_INF)
        for r in range(MOBA_TOPK):
            m = jnp.max(gate, axis=1, keepdims=True)
            idx = jnp.min(jnp.where(gate == m, lane, LANES), axis=1, keepdims=True)
            top = jnp.where((row8 == r) & (lane8 == h), idx, top)
            gate = jnp.where(lane == idx, NEG_INF, gate)
    top_ref[0] = top


def _moba_select(q8, k8, pos, kmean_t, nb):
    B = q8.shape[0]
    G = GROUP_WIDTH
    cos_t, sin_t = _rope_tables(pos)
    row = pl.BlockSpec((1, 8, G), lambda b: (b, 0, 0))
    tab = pl.BlockSpec((1, G), lambda b: (0, 0))
    return pl.pallas_call(
        functools.partial(_moba_select_kernel, nb=nb),
        grid=(B,),
        in_specs=[row, row, tab, tab, pl.BlockSpec((1, A_HEADS, HEAD_DIM, LANES), lambda b: (b, 0, 0, 0))],
        out_specs=[row, row, pl.BlockSpec((1, 8, LANES), lambda b: (b, 0, 0))],
        out_shape=[jax.ShapeDtypeStruct((B, 8, G), jnp.float32), jax.ShapeDtypeStruct((B, 8, G), jnp.float32),
                   jax.ShapeDtypeStruct((B, 8, LANES), jnp.int32)],
        compiler_params=pltpu.CompilerParams(dimension_semantics=("parallel",)),
        name="moba_select",
    )(q8, k8, cos_t, sin_t, kmean_t)


def _moba_decode_attn_kernel(pp_ref, q_ref, kn_ref, vn_ref, *refs):
    npg = MOBA_TOPK * PAGES_PER_BLOCK
    kp, vp, o_ref = refs[:npg], refs[npg:2 * npg], refs[2 * npg]
    bf = lambda a: a.astype(jnp.bfloat16)
    qh = bf(q_ref[0, 0])
    scale = HEAD_DIM ** -0.5
    s_pages = [jnp.dot(qh, bf(kp[i][0, 0]), preferred_element_type=jnp.float32) * scale for i in range(npg)]
    s_own = jnp.sum(qh.astype(jnp.float32) * bf(kn_ref[0, 0]).astype(jnp.float32), axis=-1, keepdims=True) * scale
    m = s_own
    for s in s_pages:
        m = jnp.maximum(m, jnp.max(s, axis=-1, keepdims=True))
    p_own = jnp.exp(s_own - m)
    l = p_own
    acc = p_own * bf(vn_ref[0, 0]).astype(jnp.float32)
    for i in range(npg):
        p = jnp.exp(s_pages[i] - m)
        l = l + jnp.sum(p, axis=-1, keepdims=True)
        acc = acc + lax.dot_general(bf(p), bf(vp[i][0, 0]), (((1,), (1,)), ((), ())),
                                    preferred_element_type=jnp.float32)
    o_ref[0, 0] = acc / l


def _moba_decode_attn(qh, kh, vh, cache_kt, cache_vt, phys):
    B = qh.shape[0]
    H, hd = A_HEADS, HEAD_DIM
    npg = MOBA_TOPK * PAGES_PER_BLOCK
    row = pl.BlockSpec((1, 1, 8, hd), lambda b, h, pp: (b, h, 0, 0))

    def page_spec(i):
        return pl.BlockSpec((1, 1, hd, PAGE_SIZE), lambda b, h, pp: (pp[(b * H + h) * npg + i], h, 0, 0))

    return pl.pallas_call(
        _moba_decode_attn_kernel,
        grid_spec=pltpu.PrefetchScalarGridSpec(
            num_scalar_prefetch=1,
            grid=(B, H),
            in_specs=[row, row, row] + [page_spec(i) for i in range(npg)] * 2,
            out_specs=row,
        ),
        out_shape=jax.ShapeDtypeStruct((B, H, 8, hd), jnp.float32),
        compiler_params=pltpu.CompilerParams(dimension_semantics=("parallel", "arbitrary")),
        name="moba_decode_attn",
    )(phys.reshape(-1), qh, kh, vh, *([cache_kt] * npg), *([cache_vt] * npg))


def _moba_decode(q, k, v, cache_kt, cache_vt, page_table, pos0):
    B = q.shape[0]
    H, hd = A_HEADS, HEAD_DIM
    nb = page_table.shape[1] // PAGES_PER_BLOCK
    pad8 = lambda a: jnp.pad(a[:, None, :], ((0, 0), (0, 7), (0, 0)))
    kmean_t = _moba_block_means(cache_kt, page_table)
    q8, k8, top = _moba_select(pad8(q), pad8(k), jnp.full((1,), pos0, jnp.int32), kmean_t, nb)
    blocks = jnp.swapaxes(top[:, :MOBA_TOPK, :H], 1, 2)
    pages = (blocks[..., None] * PAGES_PER_BLOCK + jnp.arange(PAGES_PER_BLOCK)).reshape(B, -1)
    phys = jnp.take_along_axis(page_table, pages, axis=1)
    by_head = lambda a8: a8.reshape(B, 8, H, hd).transpose(0, 2, 1, 3)
    out = _moba_decode_attn(by_head(q8), by_head(k8), by_head(pad8(v)), cache_kt, cache_vt, phys)
    return out[:, :, 0, :].reshape(B, H * hd), k8[:, 0, :]


def _causal_conv_tile(xs, prev3, w_ref, b_ref):
    row = lax.broadcasted_iota(jnp.int32, xs.shape, 0)
    y = b_ref[...] + xs * w_ref[CONV_W - 1:CONV_W, :]
    for k in range(1, CONV_W):
        sh = pltpu.roll(xs, k, axis=0)
        for r in range(k):
            src = r + (CONV_W - 1) - k
            sh = jnp.where(row == r, prev3[src:src + 1, :], sh)
        y = y + sh * w_ref[CONV_W - 1 - k:CONV_W - k, :]
    return y


def _conv_tail(xs, prev3, n_valid):
    rows = []
    for i in range(CONV_W - 1):
        idx = n_valid - (CONV_W - 1) + i
        rows.append(xs[idx:idx + 1, :] if idx >= 0 else prev3[idx + CONV_W - 1:idx + CONV_W, :])
    return rows


def _lru_kernel(u_ref, gate_ref, cb0_ref, h0_ref, cw_ref, cbias_ref, wa_ref, wx_ref, vec_ref,
                y_ref, cbo_ref, ho_ref, conv_scr, h_scr, a_scr, b_scr, hs_scr, *, n_valid):
    c = pl.program_id(1)
    tc = u_ref.shape[1]

    @pl.when(c == 0)
    def _():
        conv_scr[...] = cb0_ref[0]
        h_scr[...] = h0_ref[0]

    us = u_ref[0]
    prev3 = conv_scr[...]
    xc = _causal_conv_tile(us, prev3, cw_ref, cbias_ref)
    tail = _conv_tail(us, prev3, n_valid)
    for i in range(CONV_W - 1):
        conv_scr[i:i + 1, :] = tail[i]
    xb = xc.astype(jnp.bfloat16)
    r = jax.nn.sigmoid(jnp.dot(xb, wa_ref[...], preferred_element_type=jnp.float32) + vec_ref[0:1, :])
    i_g = jax.nn.sigmoid(jnp.dot(xb, wx_ref[...], preferred_element_type=jnp.float32) + vec_ref[1:2, :])
    log_a = -LRU_C * r * _softplus(-vec_ref[2:3, :])
    a_scr[...] = jnp.exp(log_a)
    b_scr[...] = jnp.sqrt(-jnp.tanh(log_a) * (jnp.exp(2.0 * log_a) + 1.0)) * (i_g * xc)

    row8 = lax.broadcasted_iota(jnp.int32, (8, LRU_WIDTH), 0)

    def step8(i, h):
        t0 = pl.multiple_of(i * 8, 8)
        a8 = a_scr[pl.ds(t0, 8), :]
        b8 = b_scr[pl.ds(t0, 8), :]
        out = jnp.zeros((8, LRU_WIDTH), jnp.float32)
        for j in range(min(8, n_valid)):
            h = a8[j:j + 1, :] * h + b8[j:j + 1, :]
            out = jnp.where(row8 == j, h, out)
        hs_scr[pl.ds(t0, 8), :] = out
        return h

    if n_valid < tc:
        hs_scr[...] = jnp.zeros_like(hs_scr)
    h = lax.fori_loop(0, -(-n_valid // 8), step8, h_scr[...])
    h_scr[...] = h
    y_ref[0] = hs_scr[...] * _gelu_tanh(gate_ref[0])

    @pl.when(c == pl.num_programs(1) - 1)
    def _():
        cbo_ref[0] = conv_scr[...]
        ho_ref[0] = h_scr[...]


def _block_diag(w):
    n, d, e = w.shape
    eye = jnp.eye(n, dtype=w.dtype)
    return (eye[:, None, :, None] * w[:, :, None, :]).reshape(n * d, n * e)


def _lru_mixer(proj3, conv0, h0, conv_w, conv_b, wa, ba, wx, bx, lam, *, n_valid, tc):
    B, Tp, _ = proj3.shape
    G = GROUP_WIDTH
    f32 = jnp.float32
    nchunk = Tp // tc
    assert Tp % tc == 0 and (n_valid == tc or nchunk == 1) and (n_valid % 8 == 0 or n_valid < 8)
    cb0 = jnp.pad(conv0, ((0, 0), (0, 8 - (CONV_W - 1)), (0, 0)))
    vec = jnp.zeros((8, G), f32).at[0].set(ba).at[1].set(bx).at[2].set(lam)
    full2 = lambda shp: pl.BlockSpec(shp, lambda b, c: (0, 0))
    y, cbo, ho = pl.pallas_call(
        functools.partial(_lru_kernel, n_valid=n_valid),
        grid=(B, nchunk),
        in_specs=[pl.BlockSpec((1, tc, G), lambda b, c: (b, c, COL_U)),
                  pl.BlockSpec((1, tc, G), lambda b, c: (b, c, COL_GATE)),
                  pl.BlockSpec((1, 8, G), lambda b, c: (b, 0, 0)),
                  pl.BlockSpec((1, 1, G), lambda b, c: (b, 0, 0)),
                  full2((CONV_W, G)), full2((1, G)), full2((G, G)), full2((G, G)), full2((8, G))],
        out_specs=[pl.BlockSpec((1, tc, G), lambda b, c: (b, c, 0)),
                   pl.BlockSpec((1, 8, G), lambda b, c: (b, 0, 0)),
                   pl.BlockSpec((1, 1, G), lambda b, c: (b, 0, 0))],
        out_shape=[jax.ShapeDtypeStruct((B, Tp, G), f32), jax.ShapeDtypeStruct((B, 8, G), f32),
                   jax.ShapeDtypeStruct((B, 1, G), f32)],
        scratch_shapes=[pltpu.VMEM((8, G), f32), pltpu.VMEM((1, G), f32),
                        pltpu.VMEM((tc, G), f32), pltpu.VMEM((tc, G), f32), pltpu.VMEM((tc, G), f32)],
        compiler_params=pltpu.CompilerParams(dimension_semantics=("parallel", "arbitrary"),
                                             vmem_limit_bytes=_VMEM_LIMIT),
        name="lru_mixer",
    )(proj3, proj3, cb0, h0[:, None, :], conv_w, conv_b[None, :],
      _block_diag(wa).astype(jnp.bfloat16), _block_diag(wx).astype(jnp.bfloat16), vec)
    return y, cbo[:, :CONV_W - 1, :], ho[:, 0, :]


def _ssd_kernel(z_ref, xbc_ref, dt_ref, cb0_ref, s0_ref, cw_ref, cbias_ref, prow_ref, pcol_ref, dvec_ref, nw_ref,
                y_ref, cbo_ref, so_ref, conv_scr, st_scr, y_scr, *, n_valid):
    c = pl.program_id(1)
    tc = xbc_ref.shape[1]
    Q = SSD_CHUNK
    G = GROUP_WIDTH
    P = SSD_HEADDIM
    N = SSD_STATE
    GN = SSD_GROUPS * SSD_STATE

    @pl.when(c == 0)
    def _():
        conv_scr[...] = cb0_ref[0]
        st_scr[...] = s0_ref[0]

    xs = xbc_ref[0]
    prev3 = conv_scr[...]
    xbc_f = _silu(_causal_conv_tile(xs, prev3, cw_ref, cbias_ref))
    tail = _conv_tail(xs, prev3, n_valid)
    for i in range(CONV_W - 1):
        conv_scr[i:i + 1, :] = tail[i]

    ii = lax.broadcasted_iota(jnp.int32, (Q, Q), 0)
    jj = lax.broadcasted_iota(jnp.int32, (Q, Q), 1)
    lower = ii >= jj
    tri_l = lower.astype(jnp.float32)
    tri_u = (ii <= jj).astype(jnp.float32)
    bias_row = prow_ref[0:1, :]
    alog_row = prow_ref[1:2, :]
    bias_col = pcol_ref[:, 0:1]
    alog_col = pcol_ref[:, 1:2]
    rowq = lax.broadcasted_iota(jnp.int32, (Q, LANES), 0)
    laneq = lax.broadcasted_iota(jnp.int32, (8, Q), 1)
    bf = lambda a: a.astype(jnp.bfloat16)
    for j in range(tc // Q):
        rows = slice(j * Q, (j + 1) * Q)
        xq = xbc_f[rows, :]
        xt = xq.T
        dcol = dt_ref[0, rows, 0:LANES]
        drow = dcol.T[0:8, :]
        dt_col = _softplus(dcol + bias_row)
        dt_row = _softplus(drow + bias_col)
        if n_valid < tc:
            dt_col = jnp.where(rowq + j * Q < n_valid, dt_col, 0.0)
            dt_row = jnp.where(laneq + j * Q < n_valid, dt_row, 0.0)
        da_col = dt_col * (-jnp.exp(alog_row))
        da_row = dt_row * (-jnp.exp(alog_col))
        cs_col = jnp.dot(tri_l, da_col, precision=_HI, preferred_element_type=jnp.float32)
        cs_row = jnp.dot(da_row, tri_u, precision=_HI, preferred_element_type=jnp.float32)
        for g in range(SSD_GROUPS):
            bg = xq[:, G + g * N:G + (g + 1) * N]
            cg = xq[:, G + GN + g * N:G + GN + (g + 1) * N]
            bgt = xt[G + g * N:G + (g + 1) * N, :]
            cb = lax.dot_general(bf(cg), bf(bg), (((1,), (1,)), ((), ())), preferred_element_type=jnp.float32)
            for h in range(g * (SSD_HEADS // SSD_GROUPS), (g + 1) * (SSD_HEADS // SSD_GROUPS)):
                ci = cs_col[:, h:h + 1]
                cj = cs_row[h:h + 1, :]
                c_last = cs_row[h:h + 1, Q - 1:Q]
                lm = jnp.exp(jnp.where(lower, ci - cj, NEG_INF))
                x_h = xq[:, h * P:(h + 1) * P]
                xdt = x_h * dt_col[:, h:h + 1]
                st = st_scr[h]
                y_h = jnp.dot(bf(cb * lm), bf(xdt), preferred_element_type=jnp.float32)
                y_h = y_h + jnp.dot(bf(cg * jnp.exp(ci)), bf(st), preferred_element_type=jnp.float32)
                y_h = y_h + dvec_ref[:, h * P:(h + 1) * P] * x_h
                y_scr[rows, h * P:(h + 1) * P] = y_h
                st_scr[h] = st * jnp.exp(c_last) + jnp.dot(bf(bgt), bf(xdt * jnp.exp(c_last - ci)),
                                                            preferred_element_type=jnp.float32)
    y = y_scr[...] * _silu(z_ref[0])
    y_ref[0] = y * lax.rsqrt(jnp.mean(y * y, axis=-1, keepdims=True) + NORM_EPS) * nw_ref[...]

    @pl.when(c == pl.num_programs(1) - 1)
    def _():
        cbo_ref[0] = conv_scr[...]
        so_ref[0] = st_scr[...]


def _ssd_mixer(proj3, conv0, s0, conv_w, conv_b, dt_bias, a_log, d_skip, norm_w, *, n_valid, tc):
    B, Tp, _ = proj3.shape
    G = GROUP_WIDTH
    H, P, N = SSD_HEADS, SSD_HEADDIM, SSD_STATE
    C = SSD_CONV_CH
    f32 = jnp.float32
    nchunk = Tp // tc
    assert Tp % tc == 0 and tc % SSD_CHUNK == 0 and (n_valid == tc or nchunk == 1)
    cb0 = jnp.pad(conv0, ((0, 0), (0, 8 - (CONV_W - 1)), (0, 0)))
    s0t = jnp.swapaxes(s0, -1, -2)
    prow = jnp.zeros((8, LANES), f32).at[0, :H].set(dt_bias).at[1, :H].set(a_log)
    pcol = jnp.zeros((8, LANES), f32).at[:H, 0].set(dt_bias).at[:H, 1].set(a_log)
    dvec = jnp.repeat(d_skip, P)[None, :]
    full2 = lambda shp: pl.BlockSpec(shp, lambda b, c: (0, 0))
    y, cbo, so = pl.pallas_call(
        functools.partial(_ssd_kernel, n_valid=n_valid),
        grid=(B, nchunk),
        in_specs=[pl.BlockSpec((1, tc, G), lambda b, c: (b, c, COL_Z)),
                  pl.BlockSpec((1, tc, C), lambda b, c: (b, c, COL_XBC)),
                  pl.BlockSpec((1, tc, G), lambda b, c: (b, c, COL_DT)),
                  pl.BlockSpec((1, 8, C), lambda b, c: (b, 0, 0)),
                  pl.BlockSpec((1, H, N, P), lambda b, c: (b, 0, 0, 0)),
                  full2((CONV_W, C)), full2((1, C)), full2((8, LANES)), full2((8, LANES)), full2((1, G)),
                  full2((1, G))],
        out_specs=[pl.BlockSpec((1, tc, G), lambda b, c: (b, c, 0)),
                   pl.BlockSpec((1, 8, C), lambda b, c: (b, 0, 0)),
                   pl.BlockSpec((1, H, N, P), lambda b, c: (b, 0, 0, 0))],
        out_shape=[jax.ShapeDtypeStruct((B, Tp, G), f32), jax.ShapeDtypeStruct((B, 8, C), f32),
                   jax.ShapeDtypeStruct((B, H, N, P), f32)],
        scratch_shapes=[pltpu.VMEM((8, C), f32), pltpu.VMEM((H, N, P), f32), pltpu.VMEM((tc, G), f32)],
        compiler_params=pltpu.CompilerParams(dimension_semantics=("parallel", "arbitrary"),
                                             vmem_limit_bytes=_VMEM_LIMIT),
        name="ssd_mixer",
    )(proj3, proj3, proj3, cb0, s0t, conv_w, conv_b[None, :], prow, pcol, dvec, norm_w[None, :])
    return y, cbo[:, :CONV_W - 1, :], jnp.swapaxes(so, -1, -2)


def _rwkv_pad_cols(a):
    G = GROUP_WIDTH
    z = lambda w: jnp.zeros(a.shape[:-1] + (w,), a.dtype)
    o = 3 * G
    wd = a[..., o:o + RWKV_W_RANK]
    ad = a[..., o + RWKV_W_RANK:o + RWKV_W_RANK + RWKV_A_RANK]
    gd = a[..., o + RWKV_W_RANK + RWKV_A_RANK:]
    return jnp.concatenate([a[..., :o], wd, z(LANES - RWKV_W_RANK), ad, z(LANES - RWKV_A_RANK),
                            gd, z(LANES - RWKV_G_RANK)], axis=-1)


def _rwkv_unpad_cols(a):
    o = 3 * GROUP_WIDTH
    return jnp.concatenate([a[..., :o], a[..., o:o + RWKV_W_RANK], a[..., o + LANES:o + LANES + RWKV_A_RANK],
                            a[..., o + 2 * LANES:o + 2 * LANES + RWKV_G_RANK]], axis=-1)


def _rwkv_kernel(pd_ref, sh0_ref, s0_ref, mu_ref, vec_ref, wup_ref, aup_ref, gup_ref, seg_ref,
                 y_ref, sout_ref, shout_ref,
                 prev_scr, s_scr, r_scr, k_scr, d_scr, kk_scr, bb_scr, g_scr, bonus_scr, vt_scr, yt_scr, *, n_steps):
    c = pl.program_id(1)
    tc = pd_ref.shape[1]
    G = GROUP_WIDTH
    hd = RWKV_HEADDIM

    @pl.when(c == 0)
    def _():
        prev_scr[...] = sh0_ref[...]
        s_scr[...] = s0_ref[...]

    nb = pd_ref.shape[0]
    w0 = vec_ref[0:1, :]
    a0 = vec_ref[1:2, :]
    k_k = vec_ref[2:3, :]
    k_a = vec_ref[3:4, :]
    ln_w = vec_ref[4:5, :]
    ln_b = vec_ref[5:6, :]
    r_k = vec_ref[6:7, :]
    seg = seg_ref[...]
    hdot = lambda x, y: jnp.dot(x, y, precision=_HI, preferred_element_type=jnp.float32)
    for ib in range(nb):
        cur = pd_ref[ib]
        row = lax.broadcasted_iota(jnp.int32, cur.shape, 0)
        prev = jnp.where(row == 0, prev_scr[ib], pltpu.roll(cur, 1, axis=0))
        prev_scr[ib] = cur[n_steps - 1:n_steps, :] if n_steps < tc else cur[tc - 1:tc, :]
        m = cur + (prev - cur) * mu_ref[...]
        r = m[:, 0:G]
        k = m[:, G:2 * G]
        v = m[:, 2 * G:3 * G]
        wd = m[:, 3 * G:3 * G + LANES]
        ad = m[:, 3 * G + LANES:3 * G + 2 * LANES]
        gd = m[:, 3 * G + 2 * LANES:3 * G + 3 * LANES]
        w = -_softplus(-(w0 + hdot(jnp.tanh(wd), wup_ref[...]))) - 0.5
        a = jax.nn.sigmoid(a0 + hdot(ad, aup_ref[...]))
        kkr = k * k_k
        kk = kkr / jnp.maximum(jnp.sqrt(hdot(kkr * kkr, seg)), 1e-12)
        k2 = k * (1.0 + (a - 1.0) * k_a)
        g_scr[ib] = hdot(jax.nn.sigmoid(gd), gup_ref[...])
        bonus_scr[ib] = hdot(r * k2 * r_k, seg) * v
        r_scr[ib] = r
        k_scr[ib] = k2
        d_scr[ib] = jnp.exp(-jnp.exp(w))
        kk_scr[ib] = -kk
        bb_scr[ib] = kk * a
        vt_scr[ib] = v.T
    yt_scr[...] = jnp.zeros_like(yt_scr)

    lane = lax.broadcasted_iota(jnp.int32, (hd, LANES), 1)
    lo_half = lane < hd

    def half_sums(p):
        lo = jnp.sum(jnp.where(lo_half, p, 0.0), axis=1, keepdims=True)
        hi = jnp.sum(jnp.where(lo_half, 0.0, p), axis=1, keepdims=True)
        return lo, hi

    npair = RWKV_HEADS // 2

    def step8(i, states):
        t0 = pl.multiple_of(i * 8, 8)
        blk = pl.multiple_of((t0 // LANES) * LANES, LANES)
        rows = pl.ds(t0, 8)
        chains = [(ib, p) for ib in range(nb) for p in range(npair)]
        states = list(states)
        rowv = {}
        tiles = {}
        for ci, (ib, p) in enumerate(chains):
            cols = slice(p * LANES, (p + 1) * LANES)
            rowv[ci] = tuple(ref[ib, rows, cols] for ref in (kk_scr, d_scr, bb_scr, k_scr, r_scr))
            tile = lambda ref, hh: ref[ib, pl.ds(p * LANES + hh * hd, hd), pl.ds(blk, LANES)]
            tiles[ci] = [tile(vt_scr, 0), tile(vt_scr, 1), tile(yt_scr, 0), tile(yt_scr, 1)]

        def vcol_of(ci, j):
            hit = lane == (t0 + j - blk)
            vc0 = jnp.sum(jnp.where(hit, tiles[ci][0], 0.0), axis=1, keepdims=True)
            vc1 = jnp.sum(jnp.where(hit, tiles[ci][1], 0.0), axis=1, keepdims=True)
            return jnp.where(lo_half, vc0, vc1)

        def emit_y(ci, j):
            hit = lane == (t0 + j - blk)
            ylo, yhi = half_sums(states[ci] * rowv[ci][4][j:j + 1, :])
            tiles[ci][2] = jnp.where(hit, ylo, tiles[ci][2])
            tiles[ci][3] = jnp.where(hit, yhi, tiles[ci][3])

        nj = min(8, n_steps)
        vcols = [vcol_of(ci, 0) for ci in range(len(chains))]
        for j in range(nj):
            sas = [half_sums(states[ci] * rowv[ci][0][j:j + 1, :]) for ci in range(len(chains))]
            if j > 0:
                for ci in range(len(chains)):
                    emit_y(ci, j - 1)
            nxt = [vcol_of(ci, j + 1) for ci in range(len(chains))] if j + 1 < nj else None
            for ci in range(len(chains)):
                _, d8, bb8, k8, _ = rowv[ci]
                sa = jnp.where(lo_half, sas[ci][0], sas[ci][1])
                states[ci] = states[ci] * d8[j:j + 1, :] + sa * bb8[j:j + 1, :] + vcols[ci] * k8[j:j + 1, :]
            vcols = nxt
        for ci in range(len(chains)):
            emit_y(ci, nj - 1)
        for ci, (ib, p) in enumerate(chains):
            yt_scr[ib, pl.ds(p * LANES, hd), pl.ds(blk, LANES)] = tiles[ci][2]
            yt_scr[ib, pl.ds(p * LANES + hd, hd), pl.ds(blk, LANES)] = tiles[ci][3]
        return tuple(states)

    init = tuple(s_scr[ib, p] for ib in range(nb) for p in range(npair))
    states = lax.fori_loop(0, -(-n_steps // 8), step8, init)
    for ib in range(nb):
        for p in range(npair):
            s_scr[ib, p] = states[ib * npair + p]

    inv = 1.0 / hd
    for ib in range(nb):
        y = yt_scr[ib].T
        mean = hdot(y, seg) * inv
        yc = y - mean
        var = hdot(yc * yc, seg) * inv
        yn = yc * lax.rsqrt(var + RWKV_LN_EPS) * ln_w + ln_b
        y_ref[ib] = (yn + bonus_scr[ib]) * g_scr[ib]

    @pl.when(c == pl.num_programs(1) - 1)
    def _():
        sout_ref[...] = s_scr[...]
        shout_ref[...] = prev_scr[...]


def _rwkv_mixer(proj3, shift0, s0, mu, w0, w_up, a0, a_up, g_up, k_k, k_a, r_k, ln_w, ln_b, *, n_steps, tc, nb=1):
    B, Tp, _ = proj3.shape
    G = GROUP_WIDTH
    H, hd = RWKV_HEADS, RWKV_HEADDIM
    nchunk = Tp // tc
    assert Tp % tc == 0 and (n_steps == tc or nchunk == 1) and (n_steps % 8 == 0 or n_steps < 8)
    assert B % nb == 0
    f32 = jnp.float32
    sh0 = _rwkv_pad_cols(shift0)[:, None, :]
    s0p = s0.reshape(B, H // 2, 2, hd, hd).transpose(0, 1, 3, 2, 4).reshape(B, H // 2, hd, 2 * hd)
    mu_p = _rwkv_pad_cols(mu[None, :])
    vec = jnp.stack([w0, a0, k_k, k_a, ln_w, ln_b, r_k.reshape(G), jnp.zeros((G,), f32)])
    padk = lambda wgt: jnp.pad(wgt, ((0, LANES - wgt.shape[0]), (0, 0)))
    hid = jnp.arange(G) // hd
    seg = (hid[:, None] == hid[None, :]).astype(f32)
    full2 = lambda shp: pl.BlockSpec(shp, lambda b, c: (0, 0))
    tscr = lambda: pltpu.VMEM((nb, tc, G), f32)
    y, s_out, sh_out = pl.pallas_call(
        functools.partial(_rwkv_kernel, n_steps=n_steps),
        grid=(B // nb, nchunk),
        in_specs=[pl.BlockSpec((nb, tc, RWKV_PCOLS), lambda b, c: (b, c, COL_RWKV)),
                  pl.BlockSpec((nb, 1, RWKV_PCOLS), lambda b, c: (b, 0, 0)),
                  pl.BlockSpec((nb, H // 2, hd, 2 * hd), lambda b, c: (b, 0, 0, 0)),
                  full2((1, RWKV_PCOLS)), full2((8, G)), full2((LANES, G)), full2((LANES, G)), full2((LANES, G)),
                  full2((G, G))],
        out_specs=[pl.BlockSpec((nb, tc, G), lambda b, c: (b, c, 0)),
                   pl.BlockSpec((nb, H // 2, hd, 2 * hd), lambda b, c: (b, 0, 0, 0)),
                   pl.BlockSpec((nb, 1, RWKV_PCOLS), lambda b, c: (b, 0, 0))],
        out_shape=[jax.ShapeDtypeStruct((B, Tp, G), f32),
                   jax.ShapeDtypeStruct((B, H // 2, hd, 2 * hd), f32),
                   jax.ShapeDtypeStruct((B, 1, RWKV_PCOLS), f32)],
        scratch_shapes=[pltpu.VMEM((nb, 1, RWKV_PCOLS), f32), pltpu.VMEM((nb, H // 2, hd, 2 * hd), f32),
                        tscr(), tscr(), tscr(), tscr(), tscr(), tscr(), tscr(),
                        pltpu.VMEM((nb, G, tc), f32), pltpu.VMEM((nb, G, tc), f32)],
        compiler_params=pltpu.CompilerParams(dimension_semantics=("parallel", "arbitrary"),
                                             vmem_limit_bytes=_VMEM_LIMIT),
        name="rwkv_mixer",
    )(proj3, sh0, s0p, mu_p, vec, padk(w_up), padk(a_up), padk(g_up), seg)
    s_new = s_out.reshape(B, H // 2, hd, 2, hd).transpose(0, 1, 3, 2, 4).reshape(B, H, hd, hd)
    return y, _rwkv_unpad_cols(sh_out[:, 0, :]), s_new


def _xattn_kernel(x_ref, g_ref, wq_ref, wo_ref, mk_ref, mv_ref, o_ref, cat_scr):
    x = x_ref[...]
    hn = x * lax.rsqrt(jnp.mean(x * x, axis=-1, keepdims=True) + NORM_EPS) * g_ref[...]
    q = jnp.dot(hn.astype(jnp.bfloat16), wq_ref[...], preferred_element_type=jnp.float32)
    qb = q.astype(jnp.bfloat16)
    mk = mk_ref[0].astype(jnp.bfloat16)
    mv = mv_ref[0].astype(jnp.bfloat16)
    hd = X_HEADDIM
    scale = hd ** -0.5
    for h in range(X_HEADS):
        cols = slice(h * hd, (h + 1) * hd)
        s = lax.dot_general(qb[:, cols], mk[:, cols], (((1,), (1,)), ((), ())),
                            preferred_element_type=jnp.float32) * scale
        m = jnp.max(s, axis=-1, keepdims=True)
        p = jnp.exp(s - m)
        p = p / jnp.sum(p, axis=-1, keepdims=True)
        cat_scr[:, cols] = jnp.dot(p.astype(jnp.bfloat16), mv[:, cols], preferred_element_type=jnp.float32)
    o_ref[...] = x + jnp.dot(cat_scr[...].astype(jnp.bfloat16), wo_ref[...], preferred_element_type=jnp.float32)


def _xattn_block(x, gain, wq, wo, mk, mv, *, tb, rows_per_mem):
    n = x.shape[0]
    M = mk.shape[1]
    assert n % tb == 0 and rows_per_mem % tb == 0
    per = rows_per_mem // tb
    return pl.pallas_call(
        _xattn_kernel,
        grid=(n // tb,),
        in_specs=[pl.BlockSpec((tb, D_MODEL), lambda i: (i, 0)),
                  pl.BlockSpec((1, D_MODEL), lambda i: (0, 0)),
                  pl.BlockSpec((D_MODEL, D_MODEL), lambda i: (0, 0)),
                  pl.BlockSpec((D_MODEL, D_MODEL), lambda i: (0, 0)),
                  pl.BlockSpec((1, M, D_MODEL), lambda i: (i // per, 0, 0)),
                  pl.BlockSpec((1, M, D_MODEL), lambda i: (i // per, 0, 0))],
        out_specs=pl.BlockSpec((tb, D_MODEL), lambda i: (i, 0)),
        out_shape=jax.ShapeDtypeStruct((n, D_MODEL), jnp.float32),
        scratch_shapes=[pltpu.VMEM((tb, D_MODEL), jnp.float32)],
        compiler_params=pltpu.CompilerParams(dimension_semantics=("parallel",), vmem_limit_bytes=_VMEM_LIMIT),
        name="xattn_block",
    )(x, gain.reshape(1, -1), wq.astype(jnp.bfloat16), wo.astype(jnp.bfloat16), mk, mv)


def _top_values(s, count, with_rank=False):
    vals = []
    cur = s
    rank = jnp.full(s.shape, float(count), jnp.float32) if with_rank else None
    for r in range(count):
        m = jnp.max(cur, axis=0, keepdims=True)
        vals.append(m)
        hit = cur == m
        if with_rank:
            rank = jnp.where(hit, float(r), rank)
        cur = jnp.where(hit, NEG_INF, cur)
    return (vals, rank) if with_rank else vals


def _peer_route_kernel(x_ref, g_ref, wqt_ref, sk_ref, hn_ref, cnt_ref, e1_ref, r2_ref, e2_ref, a1_scr, a2_scr):
    tb = x_ref.shape[0]
    x = x_ref[...]
    hn = x * lax.rsqrt(jnp.mean(x * x, axis=-1, keepdims=True) + NORM_EPS) * g_ref[...]
    hb = hn.astype(jnp.bfloat16)
    hn_ref[...] = hb
    qt = lax.dot_general(wqt_ref[...], hb, (((1,), (1,)), ((), ())), preferred_element_type=jnp.float32)
    n_top = PEER_TOPK
    row8 = lax.broadcasted_iota(jnp.int32, (8, tb), 0)
    for h in range(PEER_HEADS):
        q1 = qt[(2 * h) * PEER_HALF:(2 * h + 1) * PEER_HALF, :].astype(jnp.bfloat16)
        q2 = qt[(2 * h + 1) * PEER_HALF:(2 * h + 2) * PEER_HALF, :].astype(jnp.bfloat16)
        s1 = jnp.dot(sk_ref[2 * h], q1, preferred_element_type=jnp.float32)
        s2 = jnp.dot(sk_ref[2 * h + 1], q2, preferred_element_type=jnp.float32)
        v1 = _top_values(s1, n_top)
        v2, rank2 = _top_values(s2, n_top, with_rank=True)
        for r in range(n_top):
            a1_scr[r:r + 1, :] = v1[r]
            a2_scr[r:r + 1, :] = v2[r]
        a1 = a1_scr[...]
        a2 = a2_scr[...]
        pieces = [a1 + a2[0:1, :]]
        for q in range(1, 8):
            lim = n_top // (q + 1)
            pieces.append(jnp.where(row8 < lim, a1[0:8, :] + a2[q:q + 1, :], NEG_INF))
        pieces.append(a1[0:1, :] + a2[8:16, :])
        cand = jnp.concatenate(pieces, axis=0)
        c = _top_values(cand, n_top)
        tau = c[PEER_TOPK - 1]
        mx = c[0]
        z = jnp.sum(jnp.where(cand >= tau, jnp.exp(cand - mx), 0.0), axis=0, keepdims=True)
        cnt = jnp.zeros(s1.shape, jnp.float32)
        for q in range(n_top):
            cnt = cnt + jnp.where(s1 + v2[q] >= tau, 1.0, 0.0)
        cnt_ref[h] = cnt
        e1_ref[h] = jnp.exp(s1 - v1[0]) / z
        r2_ref[h] = rank2.astype(jnp.bfloat16)
        e2_ref[h] = jnp.exp(s2 - v2[0]).astype(jnp.bfloat16)


def _peer_route(x, gain, wqt_b, sk_b, tb=256):
    n = x.shape[0]
    assert n % tb == 0
    hk = PEER_HEADS
    blk3 = pl.BlockSpec((hk, PEER_NKEYS, tb), lambda i: (0, 0, i))
    shp_f = jax.ShapeDtypeStruct((hk, PEER_NKEYS, n), jnp.float32)
    shp_b = jax.ShapeDtypeStruct((hk, PEER_NKEYS, n), jnp.bfloat16)
    return pl.pallas_call(
        _peer_route_kernel,
        grid=(n // tb,),
        in_specs=[pl.BlockSpec((tb, D_MODEL), lambda i: (i, 0)),
                  pl.BlockSpec((1, D_MODEL), lambda i: (0, 0)),
                  pl.BlockSpec((2 * hk * PEER_HALF, D_MODEL), lambda i: (0, 0)),
                  pl.BlockSpec((2 * hk, PEER_NKEYS, PEER_HALF), lambda i: (0, 0, 0))],
        out_specs=[pl.BlockSpec((tb, D_MODEL), lambda i: (i, 0)), blk3, blk3, blk3, blk3],
        out_shape=[jax.ShapeDtypeStruct((n, D_MODEL), jnp.bfloat16), shp_f, shp_f, shp_b, shp_b],
        scratch_shapes=[pltpu.VMEM((PEER_TOPK, tb), jnp.float32), pltpu.VMEM((PEER_TOPK, tb), jnp.float32)],
        compiler_params=pltpu.CompilerParams(dimension_semantics=("parallel",), vmem_limit_bytes=_VMEM_LIMIT),
        name="peer_route",
    )(x, gain, wqt_b, sk_b)


def _peer_dense_kernel(x_ref, hn_ref, u_ref, v_ref, cnt_ref, e1_ref, r2_ref, e2_ref, fg_ref, o_ref, acc_ref, w_ref,
                       *, ic, final_norm):
    e = pl.program_id(1)
    tb = x_ref.shape[0]

    @pl.when(e == 0)
    def _():
        acc_ref[...] = jnp.zeros_like(acc_ref)

    act = lax.dot_general(u_ref[0], hn_ref[...], (((1,), (1,)), ((), ())), preferred_element_type=jnp.float32)
    for il in range(ic):
        g = None
        for h in range(PEER_HEADS):
            cb = jnp.broadcast_to(cnt_ref[h, il:il + 1, :], (16, tb)).astype(jnp.bfloat16)[None]
            eb = jnp.broadcast_to(e1_ref[h, il:il + 1, :], (16, tb)).astype(jnp.bfloat16)[None]
            r2 = r2_ref[h].reshape(PEER_NKEYS // 16, 16, tb)
            e2 = e2_ref[h].reshape(PEER_NKEYS // 16, 16, tb)
            t = jnp.where(r2 < cb, e2 * eb, jnp.zeros_like(e2))
            g = t if g is None else g + t
        rows = slice(il * PEER_NKEYS, (il + 1) * PEER_NKEYS)
        a = act[rows, :].astype(jnp.bfloat16).reshape(PEER_NKEYS // 16, 16, tb)
        w_ref[rows, :] = (g * _gelu_tanh(a)).reshape(PEER_NKEYS, tb)
    acc_ref[...] += lax.dot_general(w_ref[...], v_ref[0], (((0,), (0,)), ((), ())),
                                    preferred_element_type=jnp.float32)

    @pl.when(e == pl.num_programs(1) - 1)
    def _():
        y = x_ref[...] + acc_ref[...]
        if final_norm:
            y = y * lax.rsqrt(jnp.mean(y * y, axis=-1, keepdims=True) + NORM_EPS) * fg_ref[...]
        o_ref[...] = y


def _peer_dense(x, hn_b, u_b, v_b, layer, cnt, e1, r2, e2, final_gain, tb, ic=16):
    n = x.shape[0]
    ne = u_b.shape[1]
    ec = ic * PEER_NKEYS
    assert n % tb == 0 and ne % ec == 0
    hk = PEER_HEADS
    row_blk = pl.BlockSpec((hk, ic, tb), lambda i, e: (0, e, i))
    full_blk = pl.BlockSpec((hk, PEER_NKEYS, tb), lambda i, e: (0, 0, i))
    fg = jnp.ones((1, D_MODEL), jnp.float32) if final_gain is None else final_gain.reshape(1, D_MODEL)
    return pl.pallas_call(
        functools.partial(_peer_dense_kernel, ic=ic, final_norm=final_gain is not None),
        grid=(n // tb, ne // ec),
        in_specs=[pl.BlockSpec((tb, D_MODEL), lambda i, e: (i, 0)),
                  pl.BlockSpec((tb, D_MODEL), lambda i, e: (i, 0)),
                  pl.BlockSpec((1, ec, D_MODEL), lambda i, e: (layer, e, 0)),
                  pl.BlockSpec((1, ec, D_MODEL), lambda i, e: (layer, e, 0)),
                  row_blk, row_blk, full_blk, full_blk,
                  pl.BlockSpec((1, D_MODEL), lambda i, e: (0, 0))],
        out_specs=pl.BlockSpec((tb, D_MODEL), lambda i, e: (i, 0)),
        out_shape=jax.ShapeDtypeStruct((n, D_MODEL), jnp.float32),
        scratch_shapes=[pltpu.VMEM((tb, D_MODEL), jnp.float32), pltpu.VMEM((ec, tb), jnp.bfloat16)],
        compiler_params=pltpu.CompilerParams(dimension_semantics=("parallel", "arbitrary"),
                                             vmem_limit_bytes=_VMEM_LIMIT),
        name="peer_dense",
    )(x, hn_b, u_b, v_b, cnt, e1, r2, e2, fg)


def _peer_weights(wq, subkeys, u_all_b, v_all_b, layer):
    return (wq.T.astype(jnp.bfloat16),
            subkeys.reshape(2 * PEER_HEADS, PEER_NKEYS, PEER_HALF).astype(jnp.bfloat16),
            u_all_b, v_all_b, layer)


def _peer_block(x, gain, weights, tb_route, tb_dense, final_gain=None):
    wqt_b, sk_b, u_b, v_b, layer = weights
    hn_b, cnt, e1, r2, e2 = _peer_route(x, gain.reshape(1, -1), wqt_b, sk_b, tb=tb_route)
    return _peer_dense(x, hn_b, u_b, v_b, layer, cnt, e1, r2, e2, final_gain, tb=tb_dense)


def _in_weight(w_in):
    abc = w_in[:, :D_OFF]
    pad = jnp.zeros((w_in.shape[0], ABC_PCOLS - D_OFF), w_in.dtype)
    return jnp.concatenate([abc, pad, _rwkv_pad_cols(w_in[:, D_OFF:])], axis=1)


def _layer_pre_peer(x2, bsz, T, lp, st, prompt):
    G = GROUP_WIDTH
    n = bsz * T
    proj = _mm(x2, lp['w_in_p'], gain=lp['norm_mix'])
    if prompt:
        n_valid, tc = SEQ_CHUNK, SEQ_CHUNK
        proj3 = proj.reshape(bsz, T, -1)
        ya, k_t, v_t = _moba_prompt(proj3, jnp.arange(T, dtype=jnp.int32))
        heads = lambda t: t.reshape(bsz, A_HEADS, HEAD_DIM, T).transpose(0, 3, 1, 2)
        k_new, v_new = heads(k_t), heads(v_t)
    else:
        assert T == 1
        n_valid, tc = T, DEC_ROWS
        ya, k_r = _moba_decode(proj[:, 0:G], proj[:, G:2 * G], proj[:, 2 * G:3 * G],
                               st['k_cache'], st['v_cache'], st['page_table'], PAST_LEN)
        heads = lambda t: t.reshape(bsz, T, A_HEADS, HEAD_DIM)
        k_new, v_new = heads(k_r), heads(proj[:, 2 * G:3 * G])
        proj3 = jnp.pad(proj[:, None, :], ((0, 0), (0, tc - T), (0, 0)))
    yb, lru_conv, lru_h = _lru_mixer(proj3, st['lru_conv'], st['lru_h'], lp['lru_conv_w'], lp['lru_conv_b'],
                                     lp['lru_wa'], lp['lru_ba'], lp['lru_wx'], lp['lru_bx'], lp['lru_lambda'],
                                     n_valid=n_valid, tc=tc)
    yc, ssd_conv, ssd_s = _ssd_mixer(proj3, st['ssd_conv'], st['ssd'], lp['ssd_conv_w'], lp['ssd_conv_b'],
                                     lp['ssd_dt_bias'], lp['ssd_a_log'], lp['ssd_d'], lp['ssd_norm'],
                                     n_valid=n_valid, tc=tc)
    yd, rwkv_shift, rwkv_s = _rwkv_mixer(proj3, st['rwkv_shift'], st['rwkv'],
                                         lp['rwkv_mu'], lp['rwkv_w0'], lp['rwkv_w_up'], lp['rwkv_a0'],
                                         lp['rwkv_a_up'], lp['rwkv_g_up'], lp['rwkv_k_k'], lp['rwkv_k_a'],
                                         lp['rwkv_r_k'], lp['rwkv_ln_w'], lp['rwkv_ln_b'], n_steps=n_valid, tc=tc,
                                         nb=RWKV_SEQS_PER_STEP if prompt else 1)
    rows = lambda y: y[:, :T].reshape(n, G)
    x2 = _mm([ya.reshape(n, G), rows(yb), rows(yc), rows(yd)], lp['w_out'], residual=x2)
    if prompt:
        x2 = _xattn_block(x2, lp['norm_x'], lp['x_wq'], lp['x_wo'], st['mem_k'], st['mem_v'],
                          tb=SEQ_CHUNK, rows_per_mem=T)
    else:
        x8 = jnp.pad(x2[:, None, :], ((0, 0), (0, 7), (0, 0))).reshape(n * 8, D_MODEL)
        x8 = _xattn_block(x8, lp['norm_x'], lp['x_wq'], lp['x_wo'], st['mem_k'], st['mem_v'], tb=8, rows_per_mem=8)
        x2 = x8.reshape(n, 8, D_MODEL)[:, 0, :]
    new = {'k': k_new, 'v': v_new, 'lru_h': lru_h, 'lru_conv': lru_conv, 'ssd': ssd_s, 'ssd_conv': ssd_conv,
           'rwkv': rwkv_s, 'rwkv_shift': rwkv_shift}
    return x2, new


def kernel(x_prompt, x_sample, mem_prompt, cache_moba_k, cache_moba_v, page_table, state_lru_h, state_lru_conv, state_ssd, state_ssd_conv, state_rwkv, state_rwkv_shift, cache_mem_k, cache_mem_v, norm_mix, w_in, w_out, lru_conv_w, lru_conv_b, lru_wa, lru_ba, lru_wx, lru_bx, lru_lambda, ssd_conv_w, ssd_conv_b, ssd_dt_bias, ssd_a_log, ssd_d, ssd_norm, rwkv_mu, rwkv_w0, rwkv_w_up, rwkv_a0, rwkv_a_up, rwkv_g_up, rwkv_k_k, rwkv_k_a, rwkv_r_k, rwkv_ln_w, rwkv_ln_b, norm_x, x_wq, x_wk, x_wv, x_wo, norm_ffn, peer_wq, peer_subkeys, peer_u, peer_v, final_norm):
    bp, tp, _ = x_prompt.shape
    bd, td, _ = x_sample.shape
    n_p, n_s = bp * tp, bd * td
    n_s_pad = -(-n_s // LANES) * LANES
    assert n_p % PEER_TOK == 0
    f32 = jnp.float32
    xp2 = x_prompt.reshape(n_p, D_MODEL)
    xs2 = x_sample.reshape(n_s, D_MODEL)
    names = ('k', 'v', 'lru_h', 'lru_conv', 'ssd', 'ssd_conv', 'rwkv', 'rwkv_shift')
    p_new = {n: [] for n in names + ('mem_k', 'mem_v')}
    s_new = {n: [] for n in names}
    mem2 = mem_prompt.reshape(bp * MEM_LEN, D_MODEL)
    pool = lambda c: jnp.transpose(c, (0, 1, 3, 4, 2)).reshape((-1, A_HEADS, HEAD_DIM, PAGE_SIZE))
    cache_kt, cache_vt = pool(cache_moba_k), pool(cache_moba_v)
    peer_u_b, peer_v_b = peer_u.astype(jnp.bfloat16), peer_v.astype(jnp.bfloat16)
    for l in range(DEPTH):
        lp = {
            'norm_mix': norm_mix[l], 'w_in_p': _in_weight(w_in[l]), 'w_out': w_out[l],
            'lru_conv_w': lru_conv_w[l], 'lru_conv_b': lru_conv_b[l], 'lru_wa': lru_wa[l], 'lru_ba': lru_ba[l],
            'lru_wx': lru_wx[l], 'lru_bx': lru_bx[l], 'lru_lambda': lru_lambda[l],
            'ssd_conv_w': ssd_conv_w[l], 'ssd_conv_b': ssd_conv_b[l], 'ssd_dt_bias': ssd_dt_bias[l],
            'ssd_a_log': ssd_a_log[l], 'ssd_d': ssd_d[l], 'ssd_norm': ssd_norm[l],
            'rwkv_mu': rwkv_mu[l], 'rwkv_w0': rwkv_w0[l], 'rwkv_w_up': rwkv_w_up[l], 'rwkv_a0': rwkv_a0[l],
            'rwkv_a_up': rwkv_a_up[l], 'rwkv_g_up': rwkv_g_up[l], 'rwkv_k_k': rwkv_k_k[l], 'rwkv_k_a': rwkv_k_a[l],
            'rwkv_r_k': rwkv_r_k[l], 'rwkv_ln_w': rwkv_ln_w[l], 'rwkv_ln_b': rwkv_ln_b[l],
            'norm_x': norm_x[l], 'x_wq': x_wq[l], 'x_wo': x_wo[l],
        }
        mk = _mm(mem2, x_wk[l]).reshape(bp, MEM_LEN, D_MODEL)
        mv = _mm(mem2, x_wv[l]).reshape(bp, MEM_LEN, D_MODEL)
        st_p = {
            'lru_conv': jnp.zeros((bp, CONV_W - 1, LRU_WIDTH), f32),
            'lru_h': jnp.zeros((bp, LRU_WIDTH), f32),
            'ssd_conv': jnp.zeros((bp, CONV_W - 1, SSD_CONV_CH), f32),
            'ssd': jnp.zeros((bp, SSD_HEADS, SSD_HEADDIM, SSD_STATE), f32),
            'rwkv_shift': jnp.zeros((bp, RWKV_COLS), f32),
            'rwkv': jnp.zeros((bp, RWKV_HEADS, RWKV_HEADDIM, RWKV_HEADDIM), f32),
            'mem_k': mk, 'mem_v': mv,
        }
        xp2, npl = _layer_pre_peer(xp2, bp, tp, lp, st_p, True)
        for n in names:
            p_new[n].append(npl[n])
        p_new['mem_k'].append(mk.reshape(bp, MEM_LEN, X_HEADS, X_HEADDIM))
        p_new['mem_v'].append(mv.reshape(bp, MEM_LEN, X_HEADS, X_HEADDIM))
        st_s = {
            'k_cache': cache_kt, 'v_cache': cache_vt, 'page_table': page_table + l * cache_moba_k.shape[1],
            'lru_conv': state_lru_conv[l], 'lru_h': state_lru_h[l],
            'ssd_conv': state_ssd_conv[l], 'ssd': state_ssd[l],
            'rwkv_shift': state_rwkv_shift[l], 'rwkv': state_rwkv[l],
            'mem_k': cache_mem_k[l].reshape(bd, MEM_LEN, D_MODEL), 'mem_v': cache_mem_v[l].reshape(bd, MEM_LEN, D_MODEL),
        }
        xs2, nsl = _layer_pre_peer(xs2, bd, td, lp, st_s, False)
        for n in names:
            s_new[n].append(nsl[n])
        fg = final_norm if l == DEPTH - 1 else None
        pw = _peer_weights(peer_wq[l], peer_subkeys[l], peer_u_b, peer_v_b, l)
        xp2 = _peer_block(xp2, norm_ffn[l], pw, PEER_ROUTE_TOK, PEER_TOK, fg)
        xs_pad = jnp.pad(xs2, ((0, n_s_pad - n_s), (0, 0)))
        xs2 = _peer_block(xs_pad, norm_ffn[l], pw, LANES, LANES, fg)[:n_s]
    y_prompt = xp2.reshape(bp, tp, D_MODEL)
    y_sample = xs2.reshape(bd, td, D_MODEL)
    return (y_prompt, y_sample,
            jnp.stack(p_new['k']), jnp.stack(p_new['v']), jnp.stack(p_new['lru_h']), jnp.stack(p_new['lru_conv']),
            jnp.stack(p_new['ssd']), jnp.stack(p_new['ssd_conv']), jnp.stack(p_new['rwkv']), jnp.stack(p_new['rwkv_shift']),
            jnp.stack(p_new['mem_k']), jnp.stack(p_new['mem_v']),
            jnp.stack(s_new['k']), jnp.stack(s_new['v']), jnp.stack(s_new['lru_h']), jnp.stack(s_new['lru_conv']),
            jnp.stack(s_new['ssd']), jnp.stack(s_new['ssd_conv']), jnp.stack(s_new['rwkv']), jnp.stack(s_new['rwkv_shift']))
```
